```python
import math
import jax
import jax.numpy as jnp
from jax import lax
import numpy as np

D_MODEL = 1024
BATCH = 4
SEQ = 4096
DEPTH = 1
DEC_BATCH = 32
DEC_SEQ = 1
PAST_LEN = 8192
PAGE_SIZE = 128

N_META = 16
N_HEADS = 8
HEAD_DIM = 64
ATTN_WIDTH = N_HEADS * HEAD_DIM
N_IDX_HEADS = 8
IDX_DIM = 64
TOPK_MAX = 256
N_BUCKETS = 32
MAX_DISTANCE = 128
POOL_WINDOWS = (2, 4, 8, 16)
POOL_GROUPS = 4
POOL_WIDTH = D_MODEL // 2
POOL_GROUP_DIM = POOL_WIDTH // POOL_GROUPS
POOL_STATE = 15
N_EXPERTS = 32
TOP_EXPERTS = 4
D_EXPERT = D_MODEL
SWIGLU_LIMIT = 7.0
SWIGLU_ALPHA = 1.702
MOE_BLOCK = 64
Q_BLOCK = 128
EPS = 1e-6
SEG_WIDTHS = (ATTN_WIDTH, ATTN_WIDTH, ATTN_WIDTH, N_IDX_HEADS * IDX_DIM, IDX_DIM, N_IDX_HEADS, POOL_WIDTH, D_MODEL, D_MODEL)
IN_WIDTH = sum(SEG_WIDTHS)

kernel_name = 'dsa_pool_moe_hybrid_step'


def rms_norm(x, g):
    xf = x.astype(jnp.float32)
    y = xf * lax.rsqrt(jnp.mean(xf * xf, axis=-1, keepdims=True) + EPS)
    return (y * g.astype(jnp.float32)).astype(x.dtype)


def t5_bucket(dist):
    n = jnp.maximum(dist, 0)
    max_exact = N_BUCKETS // 2
    nf = jnp.maximum(n, 1).astype(jnp.float32)
    large = max_exact + (jnp.log(nf / max_exact) / math.log(MAX_DISTANCE / max_exact) * (N_BUCKETS - max_exact)).astype(jnp.int32)
    large = jnp.minimum(large, N_BUCKETS - 1)
    return jnp.where(n < max_exact, n, large)


def project(h, w_in):
    z = h @ w_in
    points = [int(c) for c in np.cumsum(SEG_WIDTHS)[:-1]]
    q, k, v, qi, ki, wi, p, ga, gb = jnp.split(z, points, axis=-1)
    lead = h.shape[:-1]
    q = q.reshape(*lead, N_HEADS, HEAD_DIM)
    k = k.reshape(*lead, N_HEADS, HEAD_DIM)
    v = v.reshape(*lead, N_HEADS, HEAD_DIM)
    qi = qi.reshape(*lead, N_IDX_HEADS, IDX_DIM)
    return q, k, v, qi, ki, wi, p, ga, gb


def sparse_attend(q, qi, wi, q_pos, k_all, v_all, ki_all, rel_bias, k_sel):
    L = k_all.shape[1]
    k_pos = jnp.arange(L, dtype=jnp.int32)
    dots = jnp.einsum('bqhd,bld->bqhl', qi, ki_all).astype(jnp.float32)
    score = jnp.einsum('bqh,bqhl->bql', wi.astype(jnp.float32), jax.nn.relu(dots))
    causal = k_pos[None, :] <= q_pos[:, None]
    score = jnp.where(causal[None], score, -jnp.inf)
    _, idx = lax.top_k(score, k_sel)
    gather = jax.vmap(lambda rows, i: rows[i])
    kg = gather(k_all, idx)
    vg = gather(v_all, idx)
    dist = q_pos[None, :, None] - idx
    bias = rel_bias[t5_bucket(dist)].astype(jnp.float32)
    logits = jnp.einsum('bqhd,bqkhd->bhqk', q, kg).astype(jnp.float32) * (HEAD_DIM ** -0.5) + jnp.transpose(bias, (0, 3, 1, 2))
    logits = jnp.where((dist >= 0)[:, None], logits, -jnp.inf)
    p = jax.nn.softmax(logits, axis=-1)
    return jnp.einsum('bhqk,bqkhd->bqhd', p.astype(vg.dtype), vg)


def blocked_attention(q, qi, wi, pos0, k_all, v_all, ki_all, rel_bias, k_sel):
    B, Q = q.shape[:2]
    bq = min(Q_BLOCK, Q)
    nb = -(-Q // bq)
    pad = nb * bq - Q

    def blocks(a):
        a = jnp.pad(a, [(0, 0), (0, pad)] + [(0, 0)] * (a.ndim - 2))
        return jnp.swapaxes(a.reshape(B, nb, bq, *a.shape[2:]), 0, 1)

    q_pos = (pos0 + jnp.arange(nb * bq, dtype=jnp.int32)).reshape(nb, bq)

    def one_block(args):
        qb, qib, wib, pb = args
        return sparse_attend(qb, qib, wib, pb, k_all, v_all, ki_all, rel_bias, k_sel)

    out = lax.map(one_block, (blocks(q), blocks(qi), blocks(wi), q_pos))
    out = jnp.swapaxes(out, 0, 1).reshape(B, nb * bq, N_HEADS, HEAD_DIM)
    return out[:, :Q]


def multi_pool(p):
    B, L, C = p.shape
    pf = p.astype(jnp.float32)
    c0 = jnp.concatenate([jnp.zeros((B, 1, C), jnp.float32), jnp.cumsum(pf, axis=1)], axis=1)
    t = jnp.arange(L, dtype=jnp.int32)
    outs = []
    for g, w in enumerate(POOL_WINDOWS):
        sl = slice(g * POOL_GROUP_DIM, (g + 1) * POOL_GROUP_DIM)
        lo = jnp.maximum(t + 1 - w, 0)
        s = c0[:, 1:, sl] - c0[:, lo, sl]
        cnt = (t + 1 - lo).astype(jnp.float32)
        outs.append(s / cnt[None, :, None] - pf[:, :, sl])
    return jnp.concatenate(outs, axis=-1).astype(p.dtype)


def merge_branches(attn, pooled, ga, gb, w_pool_mix, pool_scale, w_br_attn, w_br_pool, w_out):
    lead = attn.shape[:-2]
    a = attn.reshape(*lead, ATTN_WIDTH) @ w_br_attn
    pg = pooled.reshape(*lead, POOL_GROUPS, POOL_GROUP_DIM)
    pm = jnp.einsum('...gc,gcd->...gd', pg, w_pool_mix).reshape(*lead, POOL_WIDTH) * pool_scale
    b = pm @ w_br_pool
    return (jax.nn.sigmoid(ga) * a + jax.nn.sigmoid(gb) * b) @ w_out


def moe_ffn(h, w_router, b_router, w_gu, b_gu, w_down, b_down):
    shp = h.shape
    x = h.reshape(-1, D_MODEL)
    N = x.shape[0]
    logits = (x @ w_router).astype(jnp.float32) + b_router.astype(jnp.float32)
    top_v, top_e = lax.top_k(logits, TOP_EXPERTS)
    gates = jax.nn.softmax(top_v, axis=-1)
    NK = N * TOP_EXPERTS
    flat_e = top_e.reshape(NK)
    flat_g = gates.reshape(NK)
    flat_t = jnp.arange(NK, dtype=jnp.int32) // TOP_EXPERTS
    order = jnp.argsort(flat_e)
    se = flat_e[order]
    counts = jnp.bincount(flat_e, length=N_EXPERTS)
    padded = (counts + MOE_BLOCK - 1) // MOE_BLOCK * MOE_BLOCK
    pend = jnp.cumsum(padded)
    pstart = pend - padded
    start = jnp.cumsum(counts) - counts
    dest = pstart[se] + jnp.arange(NK, dtype=jnp.int32) - start[se]
    n_blocks = -(-(NK + N_EXPERTS * (MOE_BLOCK - 1)) // MOE_BLOCK)
    n_slots = n_blocks * MOE_BLOCK
    slot_tok = jnp.full((n_slots,), N, jnp.int32).at[dest].set(flat_t[order])
    slot_gate = jnp.zeros((n_slots,), jnp.float32).at[dest].set(flat_g[order])
    block_e = jnp.minimum(jnp.searchsorted(pend, jnp.arange(n_blocks, dtype=jnp.int32) * MOE_BLOCK, side='right'), N_EXPERTS - 1)
    x_ext = jnp.concatenate([x, jnp.zeros((1, D_MODEL), x.dtype)], axis=0)
    xb = x_ext[slot_tok].reshape(n_blocks, MOE_BLOCK, D_MODEL)

    def expert_block(args):
        xe, e = args
        gu = xe @ w_gu[e] + b_gu[e]
        gate = jnp.minimum(gu[:, :D_EXPERT], SWIGLU_LIMIT)
        up = jnp.clip(gu[:, D_EXPERT:], -SWIGLU_LIMIT, SWIGLU_LIMIT)
        act = (up + 1.0) * gate * jax.nn.sigmoid(SWIGLU_ALPHA * gate)
        return act @ w_down[e] + b_down[e]

    yb = lax.map(expert_block, (xb, block_e)).reshape(n_slots, D_MODEL)
    y = jnp.zeros((N + 1, D_MODEL), jnp.float32).at[slot_tok].add(yb.astype(jnp.float32) * slot_gate[:, None])[:N]
    return y.astype(h.dtype).reshape(shp)


def paged_rows(pool, table):
    rows = pool[table]
    return rows.reshape(table.shape[0], table.shape[1] * pool.shape[1], *pool.shape[2:])


def setup_inputs(seed: int = 0) -> dict:
    key = jax.random.key(seed)
    ks = jax.random.split(key, 24)
    f32 = jnp.float32
    n_pages = PAST_LEN // PAGE_SIZE
    n_used = DEC_BATCH * n_pages
    n_pool = n_used + n_used // 4

    def nrm(k, shape, scale):
        return jax.random.normal(k, shape, f32) * scale

    return {
        'x_prompt': nrm(ks[0], (BATCH, SEQ, D_MODEL), 1.0),
        'x_sample': nrm(ks[1], (DEC_BATCH, DEC_SEQ, D_MODEL), 1.0),
        'cache_k': nrm(ks[2], (DEPTH, n_pool, PAGE_SIZE, N_HEADS, HEAD_DIM), 1.0),
        'cache_v': nrm(ks[3], (DEPTH, n_pool, PAGE_SIZE, N_HEADS, HEAD_DIM), 1.0),
        'cache_kidx': nrm(ks[4], (DEPTH, n_pool, PAGE_SIZE, IDX_DIM), 1.0),
        'state_pool': nrm(ks[5], (DEPTH, DEC_BATCH, POOL_STATE, POOL_WIDTH), 1.0),
        'page_table': jax.random.permutation(ks[6], n_pool)[:n_used].reshape(DEC_BATCH, n_pages).astype(jnp.int32),
        'meta_tokens': nrm(ks[7], (N_META, D_MODEL), 1.0),
        'rel_bias': nrm(ks[8], (N_BUCKETS, N_HEADS), 0.5),
        'g_mix': 1.0 + nrm(ks[9], (DEPTH, D_MODEL), 0.05),
        'w_in': nrm(ks[10], (DEPTH, D_MODEL, IN_WIDTH), D_MODEL ** -0.5),
        'w_pool_mix': nrm(ks[11], (DEPTH, POOL_GROUPS, POOL_GROUP_DIM, POOL_GROUP_DIM), POOL_GROUP_DIM ** -0.5),
        'pool_scale': 1.0 + nrm(ks[12], (DEPTH, POOL_WIDTH), 0.05),
        'w_br_attn': nrm(ks[13], (DEPTH, ATTN_WIDTH, D_MODEL), ATTN_WIDTH ** -0.5),
        'w_br_pool': nrm(ks[14], (DEPTH, POOL_WIDTH, D_MODEL), POOL_WIDTH ** -0.5),
        'w_out': nrm(ks[15], (DEPTH, D_MODEL, D_MODEL), D_MODEL ** -0.5),
        'g_ffn': 1.0 + nrm(ks[16], (DEPTH, D_MODEL), 0.05),
        'w_router': nrm(ks[17], (DEPTH, D_MODEL, N_EXPERTS), D_MODEL ** -0.5),
        'b_router': nrm(ks[18], (DEPTH, N_EXPERTS), 0.01),
        'w_gate_up': nrm(ks[19], (DEPTH, N_EXPERTS, D_MODEL, 2 * D_EXPERT), D_MODEL ** -0.5),
        'b_gate_up': nrm(ks[20], (DEPTH, N_EXPERTS, 2 * D_EXPERT), 0.01),
        'w_down': nrm(ks[21], (DEPTH, N_EXPERTS, D_EXPERT, D_MODEL), D_EXPERT ** -0.5),
        'b_down': nrm(ks[22], (DEPTH, N_EXPERTS, D_MODEL), 0.01),
        'g_final': 1.0 + nrm(ks[23], (D_MODEL,), 0.05),
    }


def reference(x_prompt, x_sample, cache_k, cache_v, cache_kidx, state_pool, page_table, meta_tokens, rel_bias, g_mix, w_in, w_pool_mix, pool_scale, w_br_attn, w_br_pool, w_out, g_ffn, w_router, b_router, w_gate_up, b_gate_up, w_down, b_down, g_final):
    b_p, s_p, _ = x_prompt.shape
    s_s = x_sample.shape[1]
    n_pages = page_table.shape[1]
    past_len = n_pages * PAGE_SIZE
    k_sel_p = min(TOPK_MAX, s_p // 4)
    k_sel_s = min(TOPK_MAX, (past_len + s_s) // 4)

    meta = jnp.broadcast_to(meta_tokens[None].astype(x_prompt.dtype), (b_p, N_META, D_MODEL))
    xp = jnp.concatenate([meta, x_prompt], axis=1)
    xs = x_sample
    nk_p, nv_p, nki_p, npool_p = [], [], [], []
    nk_s, nv_s, nki_s, npool_s = [], [], [], []

    for l in range(DEPTH):
        h = rms_norm(xp, g_mix[l])
        q, k, v, qi, ki, wi, p, ga, gb = project(h, w_in[l])
        attn = blocked_attention(q, qi, wi, 0, k, v, ki, rel_bias, k_sel_p)
        pooled = multi_pool(p)
        xp = xp + merge_branches(attn, pooled, ga, gb, w_pool_mix[l], pool_scale[l], w_br_attn[l], w_br_pool[l], w_out[l])
        xp = xp + moe_ffn(rms_norm(xp, g_ffn[l]), w_router[l], b_router[l], w_gate_up[l], b_gate_up[l], w_down[l], b_down[l])
        nk_p.append(k)
        nv_p.append(v)
        nki_p.append(ki)
        npool_p.append(p[:, -POOL_STATE:])

        h = rms_norm(xs, g_mix[l])
        q, k, v, qi, ki, wi, p, ga, gb = project(h, w_in[l])
        k_all = jnp.concatenate([paged_rows(cache_k[l], page_table).astype(k.dtype), k], axis=1)
        v_all = jnp.concatenate([paged_rows(cache_v[l], page_table).astype(v.dtype), v], axis=1)
        ki_all = jnp.concatenate([paged_rows(cache_kidx[l], page_table).astype(ki.dtype), ki], axis=1)
        attn = blocked_attention(q, qi, wi, past_len, k_all, v_all, ki_all, rel_bias, k_sel_s)
        p_ext = jnp.concatenate([state_pool[l].astype(p.dtype), p], axis=1)
        pooled = multi_pool(p_ext)[:, POOL_STATE:]
        xs = xs + merge_branches(attn, pooled, ga, gb, w_pool_mix[l], pool_scale[l], w_br_attn[l], w_br_pool[l], w_out[l])
        xs = xs + moe_ffn(rms_norm(xs, g_ffn[l]), w_router[l], b_router[l], w_gate_up[l], b_gate_up[l], w_down[l], b_down[l])
        nk_s.append(k)
        nv_s.append(v)
        nki_s.append(ki)
        npool_s.append(p_ext[:, -POOL_STATE:])

    y_prompt = rms_norm(xp, g_final)[:, N_META:]
    y_sample = rms_norm(xs, g_final)
    return (y_prompt, y_sample, jnp.stack(nk_p), jnp.stack(nv_p), jnp.stack(nki_p), jnp.stack(npool_p), jnp.stack(nk_s), jnp.stack(nv_s), jnp.stack(nki_s), jnp.stack(npool_s))
```

```python
import functools
import math

import jax
import jax.numpy as jnp
from jax import lax
from jax.experimental import pallas as pl
from jax.experimental.pallas import tpu as pltpu

F32 = jnp.float32
BF16 = jnp.bfloat16
I32 = jnp.int32
U32 = jnp.uint32

N_META = 16
N_HEADS = 8
HEAD_DIM = 64
ATTN_WIDTH = N_HEADS * HEAD_DIM
N_IDX_HEADS = 8
IDX_DIM = 64
TOPK_MAX = 256
N_BUCKETS = 32
MAX_DISTANCE = 128
POOL_WINDOWS = (2, 4, 8, 16)
POOL_GROUPS = 4
POOL_STATE = 15
N_EXPERTS = 32
TOP_EXPERTS = 4
SWIGLU_LIMIT = 7.0
SWIGLU_ALPHA = 1.702
PAGE_SIZE = 128
EPS = 1e-6

LANES = 128
SUBLANES = 8
MXU_DIM = 256
TQ = MXU_DIM
TM = MXU_DIM
TS = MXU_DIM
VMEM_LIMIT = 56 * 1024 * 1024

NEG = -1e30
INT_MIN = -(2 ** 31)

_MAX_EXACT = N_BUCKETS // 2
_BUCKET_THRESHOLDS = tuple(
    math.ceil(_MAX_EXACT * (MAX_DISTANCE / _MAX_EXACT) ** (j / (N_BUCKETS - _MAX_EXACT)))
    for j in range(1, N_BUCKETS - _MAX_EXACT))


def _round_up(a, m):
    return (a + m - 1) // m * m


def _rms(x, g):
    return x * lax.rsqrt(jnp.mean(x * x, axis=-1, keepdims=True) + EPS) * g


def _dot(a, b):
    return jnp.dot(a, b, preferred_element_type=F32)


def _dot_nt(a, b):
    return lax.dot_general(a, b, (((1,), (1,)), ((), ())), preferred_element_type=F32)


def _bucket(dist):
    large = jnp.full(dist.shape, _MAX_EXACT, I32)
    for thr in _BUCKET_THRESHOLDS:
        large = large + jnp.where(dist >= thr, 1, 0)
    return jnp.where(dist < _MAX_EXACT, dist, large)


def _sort_key(s):
    s = jnp.where(s == 0.0, 0.0, s)
    bits = lax.bitcast_convert_type(s, I32)
    return jnp.where(bits >= 0, bits, bits ^ jnp.int32(0x7FFFFFFF))


def _full(shape):
    return pl.BlockSpec(shape, lambda *_: (0,) * len(shape))


def _inproj_body(x_ref, g_ref, wa_ref, wki_ref, wwi_ref, wp_ref, wga_ref, wgb_ref,
                 qs_o, kf_o, vf_o, kb_o, vb_o, qib_o, ki2b_o, kif_o, wif_o, pf_o, ga_o, gb_o):
    h = _rms(x_ref[...], g_ref[...]).astype(BF16)
    za = _dot(h, wa_ref[...])
    w = ATTN_WIDTH
    qs_o[...] = (za[:, 0:w] * (HEAD_DIM ** -0.5)).astype(BF16)
    k = za[:, w:2 * w]
    v = za[:, 2 * w:3 * w]
    kf_o[...] = k
    vf_o[...] = v
    kb_o[...] = k.astype(BF16)
    vb_o[...] = v.astype(BF16)
    qib_o[...] = za[:, 3 * w:4 * w].astype(BF16)
    ki2 = _dot(h, wki_ref[...])
    kif_o[...] = ki2
    ki2b_o[...] = ki2.astype(BF16)
    wif_o[...] = _dot(h, wwi_ref[...])
    pf_o[...] = _dot(h, wp_ref[...])
    ga_o[...] = _dot(h, wga_ref[...])
    gb_o[...] = _dot(h, wgb_ref[...])


def _inproj(x_all, g, wa, wki2, wwi, wp, wga, wgb):
    nt, d = x_all.shape
    pw = wp.shape[1]
    row = lambda width: pl.BlockSpec((TM, width), lambda i: (i, 0))
    outs = [
        (ATTN_WIDTH, BF16), (ATTN_WIDTH, F32), (ATTN_WIDTH, F32), (ATTN_WIDTH, BF16), (ATTN_WIDTH, BF16),
        (ATTN_WIDTH, BF16), (LANES, BF16), (LANES, F32), (LANES, F32), (pw, F32), (d, F32), (d, F32)]
    return pl.pallas_call(
        _inproj_body,
        grid=(nt // TM,),
        in_specs=[row(d), _full((1, d)), _full(wa.shape), _full(wki2.shape), _full(wwi.shape),
                  _full(wp.shape), _full(wga.shape), _full(wgb.shape)],
        out_specs=[row(wd) for wd, _ in outs],
        out_shape=[jax.ShapeDtypeStruct((nt, wd), dt) for wd, dt in outs],
        compiler_params=pltpu.CompilerParams(dimension_semantics=("arbitrary",), vmem_limit_bytes=VMEM_LIMIT),
        name="inproj",
    )(x_all, g, wa, wki2, wwi, wp, wga, wgb)


def _attn_prompt_body(qs_ref, qib_ref, wif_ref, kb_ref, vb_ref, ki2b_ref, u2_ref, rb_ref,
                      o_ref, qis_scr, wb_scr, key_scr, l_scr, ntab_scr, *, ksel):
    b = pl.program_id(0)
    i = pl.program_id(1)
    nchunk = i + 1
    half = HEAD_DIM

    @pl.when((b == 0) & (i == 0))
    def _():
        def slab_rows(s, _):
            r0 = pl.multiple_of(s * SUBLANES, SUBLANES)
            r = r0 + lax.broadcasted_iota(I32, (SUBLANES, 2 * TQ), 0)
            x = lax.broadcasted_iota(I32, (SUBLANES, 2 * TQ), 1)
            bucket = _bucket(jnp.maximum(r + TQ - x, 0))
            accs = [jnp.zeros((SUBLANES, 2 * TQ), F32) for _ in range(N_HEADS)]
            for j in range(N_BUCKETS):
                m = bucket == j
                accs = [jnp.where(m, rb_ref[j, h], accs[h]) for h in range(N_HEADS)]
            for h in range(N_HEADS):
                ntab_scr[h, 0, pl.ds(r0, SUBLANES), :] = jnp.full((SUBLANES, TQ), rb_ref[N_BUCKETS - 1, h], F32)
                ntab_scr[h, 1, pl.ds(r0, SUBLANES), :] = accs[h][:, 0:TQ]
                ntab_scr[h, 2, pl.ds(r0, SUBLANES), :] = accs[h][:, TQ:2 * TQ]
            return 0
        lax.fori_loop(0, TQ // SUBLANES, slab_rows, 0)

    lane = lax.broadcasted_iota(I32, (TQ, LANES), 1)
    lo_half = lane < half

    for h in range(N_IDX_HEADS):
        pair = qib_ref[:, (h // 2) * LANES:(h // 2 + 1) * LANES].astype(F32)
        keep = lo_half if h % 2 == 0 else jnp.logical_not(lo_half)
        qis_scr[h * TQ:(h + 1) * TQ, :] = jnp.where(keep, pair, 0.0).astype(BF16)
        wb_scr[h] = jnp.broadcast_to(wif_ref[:, h:h + 1], (TQ, LANES))

    dmat = lax.broadcasted_iota(I32, (TQ, TQ), 1) - lax.broadcasted_iota(I32, (TQ, TQ), 0)

    def score_chunk(c, _):
        kc = ki2b_ref[pl.ds(pl.multiple_of(c * TQ, TQ), TQ), :]
        d = _dot_nt(qis_scr[...], kc)
        s = jnp.zeros((TQ, TQ), F32)
        for h in range(N_IDX_HEADS):
            w = wb_scr[h]
            s = s + jnp.concatenate([w, w], axis=1) * jnp.maximum(d[h * TQ:(h + 1) * TQ], 0.0)
        key = jnp.where(dmat <= (i - c) * TQ, _sort_key(s), INT_MIN)
        key_scr[c] = key
        return 0
    lax.fori_loop(0, nchunk, score_chunk, 0)

    def count(pred):
        def body(c, acc):
            hit = jnp.where(pred(key_scr[c]), 1.0, 0.0)
            return acc + (hit[:, 0:LANES] + hit[:, LANES:2 * LANES])
        acc = lax.fori_loop(0, nchunk, body, jnp.zeros((TQ, LANES), F32))
        return jnp.broadcast_to(jnp.sum(acc, axis=1, keepdims=True), (TQ, LANES))

    def two(x):
        return jnp.concatenate([x, x], axis=1)

    def bit_step(step, t):
        cand = t + lax.shift_left(jnp.int32(1), 31 - step)
        cand2 = two(cand)
        cnt = count(lambda k: k >= cand2)
        return jnp.where(cnt >= ksel, cand, t)
    t = lax.fori_loop(0, 32, bit_step, jnp.full((TQ, LANES), INT_MIN, I32))
    t2 = two(t)
    n_gt = count(lambda k: k > t2)
    need2 = two(jnp.where(t == INT_MIN, 0.0, ksel - n_gt))

    def mask_chunk(c, carry):
        kc = key_scr[c]
        eq = kc == t2
        pre = _dot(jnp.where(eq, 1.0, 0.0).astype(BF16), u2_ref[...])
        prefix = pre[:, 0:TQ] + two(carry)
        tie_ok = jnp.where(eq, prefix, 3e38) <= need2
        madd = jnp.where(kc > t2, 0.0, jnp.where(tie_ok, 0.0, NEG))
        key_scr[c] = lax.bitcast_convert_type(madd, I32)
        return carry + pre[:, TQ:TQ + LANES]
    lax.fori_loop(0, nchunk, mask_chunk, jnp.zeros((TQ, LANES), F32))

    for hp in range(N_HEADS // 2):
        cols = slice(hp * LANES, (hp + 1) * LANES)
        qpair = qs_ref[:, cols].astype(F32)
        out_pair = jnp.zeros((TQ, LANES), F32)
        for sub in range(2):
            h = 2 * hp + sub
            keep = lo_half if sub == 0 else jnp.logical_not(lo_half)
            qz = jnp.where(keep, qpair, 0.0).astype(BF16)

            def logits_chunk(c, mrun, h=h, qz=qz, cols=cols):
                kc = kb_ref[pl.ds(pl.multiple_of(c * TQ, TQ), TQ), cols]
                slab = jnp.maximum(c - i + 2, 0)
                l = _dot_nt(qz, kc) + lax.bitcast_convert_type(key_scr[c], F32) + ntab_scr[h, slab]
                l_scr[c] = l
                return jnp.maximum(mrun, jnp.maximum(l[:, 0:LANES], l[:, LANES:2 * LANES]))
            mrun = lax.fori_loop(0, nchunk, logits_chunk, jnp.full((TQ, LANES), NEG, F32))
            m = jnp.max(mrun, axis=1, keepdims=True)

            def pv_chunk(c, carry, m=m, cols=cols):
                acc, ssum = carry
                p = jnp.exp(l_scr[c] - m)
                vc = vb_ref[pl.ds(pl.multiple_of(c * TQ, TQ), TQ), cols]
                return acc + _dot(p.astype(BF16), vc), ssum + (p[:, 0:LANES] + p[:, LANES:2 * LANES])
            acc, ssum = lax.fori_loop(0, nchunk, pv_chunk,
                                      (jnp.zeros((TQ, LANES), F32), jnp.zeros((TQ, LANES), F32)))
            o = acc / jnp.sum(ssum, axis=1, keepdims=True)
            out_pair = jnp.where(keep, o, out_pair)
        o_ref[:, cols] = out_pair.astype(BF16)


def _attn_prompt(qs, qib, wif, kb, vb, ki2b, u2, rel_bias, *, nb, lp, ksel):
    nq = lp // TQ
    rowq = lambda width: pl.BlockSpec((TQ, width), lambda b, i: (b * nq + i, 0))
    seq = lambda width: pl.BlockSpec((lp, width), lambda b, i: (b, 0))
    return pl.pallas_call(
        functools.partial(_attn_prompt_body, ksel=float(ksel)),
        grid=(nb, nq),
        in_specs=[rowq(ATTN_WIDTH), rowq(ATTN_WIDTH), rowq(LANES), seq(ATTN_WIDTH), seq(ATTN_WIDTH), seq(LANES),
                  _full(u2.shape), pl.BlockSpec(memory_space=pltpu.SMEM)],
        out_specs=rowq(ATTN_WIDTH),
        out_shape=jax.ShapeDtypeStruct((nb * lp, ATTN_WIDTH), BF16),
        scratch_shapes=[
            pltpu.VMEM((N_IDX_HEADS * TQ, LANES), BF16),
            pltpu.VMEM((N_IDX_HEADS, TQ, LANES), F32),
            pltpu.VMEM((nq, TQ, TQ), I32),
            pltpu.VMEM((nq, TQ, TQ), F32),
            pltpu.VMEM((N_HEADS, 3, TQ, TQ), F32),
        ],
        compiler_params=pltpu.CompilerParams(dimension_semantics=("arbitrary", "arbitrary"),
                                             vmem_limit_bytes=VMEM_LIMIT),
        name="attn_prompt",
    )(qs, qib, wif, kb, vb, ki2b, u2, rel_bias)


def _pool_prompt_body(p_ref, halo_ref, o_ref, ext_scr, *, tiles_per_seq):
    j = pl.program_id(0) % tiles_per_seq
    halo = 2 * SUBLANES
    p = p_ref[...]
    ext_scr[0:halo, :] = jnp.where(j == 0, 0.0, halo_ref[...])
    ext_scr[halo:halo + TM, :] = p
    pos = j * TM + lax.broadcasted_iota(I32, (TM, 1), 0)
    gw = p.shape[1] // POOL_GROUPS
    for g, w in enumerate(POOL_WINDOWS):
        cols = slice(g * gw, (g + 1) * gw)
        s = p[:, cols]
        for back in range(1, w):
            s = s + ext_scr[halo - back:halo - back + TM, cols]
        cnt = jnp.minimum(pos + 1, w).astype(F32)
        o_ref[:, cols] = (s / cnt - p[:, cols]).astype(BF16)


def _pool_prompt(pf, *, n_rows, lp):
    pw = pf.shape[1]
    halo = 2 * SUBLANES
    return pl.pallas_call(
        functools.partial(_pool_prompt_body, tiles_per_seq=lp // TM),
        grid=(n_rows // TM,),
        in_specs=[pl.BlockSpec((TM, pw), lambda i: (i, 0)),
                  pl.BlockSpec((halo, pw), lambda i: (jnp.maximum(i * (TM // halo) - 1, 0), 0))],
        out_specs=pl.BlockSpec((TM, pw), lambda i: (i, 0)),
        out_shape=jax.ShapeDtypeStruct((n_rows, pw), BF16),
        scratch_shapes=[pltpu.VMEM((halo + TM, pw), F32)],
        compiler_params=pltpu.CompilerParams(dimension_semantics=("arbitrary",)),
        name="pool_prompt",
    )(pf, pf)


def _sample_index_body(pt_ref, qi_ref, kinew_ref, wi_ref, u2_ref, kv_ref, cache_hbm,
                       idx_o, kbuf, sem, *, n_pages, ksel):
    s = pl.program_id(0)
    past = n_pages * PAGE_SIZE
    next_w = past + LANES

    def page_copy(pg):
        return pltpu.make_async_copy(cache_hbm.at[pt_ref[s * n_pages + pg]],
                                     kbuf.at[pl.ds(pl.multiple_of(pg * PAGE_SIZE, PAGE_SIZE), PAGE_SIZE)], sem)

    def start(pg, _):
        page_copy(pg).start()
        return 0
    lax.fori_loop(0, n_pages, start, 0)

    def wait(pg, _):
        page_copy(pg).wait()
        return 0
    lax.fori_loop(0, n_pages, wait, 0)

    qi = qi_ref[0]
    wcol = wi_ref[0]
    d = _dot_nt(qi, kbuf[...].astype(BF16))
    sc = jnp.sum(wcol * jnp.maximum(d, 0.0), axis=0, keepdims=True)
    knew = kinew_ref[0].astype(BF16).astype(F32)
    dn = jnp.sum(qi.astype(F32) * knew, axis=1, keepdims=True)
    sn = jnp.sum(wcol * jnp.maximum(dn, 0.0), axis=0, keepdims=True)
    lane = lax.broadcasted_iota(I32, (1, LANES), 1)
    tail = jnp.where(lane == 0, _sort_key(jnp.broadcast_to(sn, (1, LANES))), INT_MIN)
    key = jnp.concatenate([_sort_key(sc), tail], axis=1)
    pos = lax.broadcasted_iota(I32, (1, next_w), 1)

    def cnt(pred):
        return jnp.sum(jnp.where(pred, 1.0, 0.0), axis=1, keepdims=True)

    def bit_step(step, t):
        cand = t + lax.shift_left(jnp.int32(1), 31 - step)
        return jnp.where(cnt(key >= cand) >= ksel, cand, t)
    t = lax.fori_loop(0, 32, bit_step, jnp.full((1, 1), INT_MIN, I32))
    gt = key > t
    eq = key == t
    need = ksel - cnt(gt)

    nbits = (next_w - 1).bit_length()

    def cut_step(step, c):
        cand = c - lax.shift_left(jnp.int32(1), nbits - 1 - step)
        ok = cnt(jnp.logical_and(eq, pos <= cand)) >= need
        return jnp.where(ok, cand, c)
    cut = lax.fori_loop(0, nbits, cut_step, jnp.full((1, 1), 2 ** nbits - 1, I32))
    sel = jnp.where(gt, 1.0, jnp.where(jnp.logical_and(eq, pos <= cut), 1.0, 0.0))
    sel = jnp.where(pos <= past, sel, 0.0)

    jrow = (lax.broadcasted_iota(I32, (TOPK_MAX, TQ), 0) + 1).astype(F32)
    carry = jnp.zeros((1, LANES), F32)
    acc = jnp.zeros((TOPK_MAX, LANES), F32)
    for c in range(next_w // TQ + (1 if next_w % TQ else 0)):
        width = min(TQ, next_w - c * TQ)
        sc_c = sel[:, c * TQ:c * TQ + width]
        if width < TQ:
            sc_c = jnp.concatenate([sc_c, jnp.zeros((1, TQ - width), F32)], axis=1)
        pre = _dot(sc_c.astype(BF16), u2_ref[...])
        rank = pre[:, 0:TQ] + jnp.concatenate([carry, carry], axis=1)
        onehot = jnp.where(jnp.logical_and(rank == jrow, sc_c > 0.0), 1.0, 0.0).astype(BF16)
        acc = acc + _dot(onehot, kv_ref[c])
        carry = carry + pre[:, TQ:TQ + LANES]
    idx_o[0] = (acc[:, 0:1] * LANES + acc[:, 1:2]).astype(I32) + jnp.zeros((TOPK_MAX, LANES), I32)


def _sample_index(page_table_flat, qi3, kinew3, wi3, u2, kvals, cache_kidx, *, n_pages, ksel):
    db = qi3.shape[0]
    past = n_pages * PAGE_SIZE
    grid_spec = pltpu.PrefetchScalarGridSpec(
        num_scalar_prefetch=1,
        grid=(db,),
        in_specs=[pl.BlockSpec((1, N_IDX_HEADS, IDX_DIM), lambda s, pt: (s, 0, 0)),
                  pl.BlockSpec((1, 1, IDX_DIM), lambda s, pt: (s, 0, 0)),
                  pl.BlockSpec((1, N_IDX_HEADS, 1), lambda s, pt: (s, 0, 0)),
                  pl.BlockSpec(u2.shape, lambda s, pt: (0, 0)),
                  pl.BlockSpec(kvals.shape, lambda s, pt: (0, 0, 0)),
                  pl.BlockSpec(memory_space=pl.ANY)],
        out_specs=pl.BlockSpec((1, TOPK_MAX, LANES), lambda s, pt: (s, 0, 0)),
        scratch_shapes=[pltpu.VMEM((past, IDX_DIM), F32), pltpu.SemaphoreType.DMA],
    )
    return pl.pallas_call(
        functools.partial(_sample_index_body, n_pages=n_pages, ksel=float(ksel)),
        grid_spec=grid_spec,
        out_shape=jax.ShapeDtypeStruct((db, TOPK_MAX, LANES), I32),
        compiler_params=pltpu.CompilerParams(dimension_semantics=("arbitrary",), vmem_limit_bytes=VMEM_LIMIT),
        name="sample_index",
    )(page_table_flat, qi3, kinew3, wi3, u2, kvals, cache_kidx)


def _sample_attn_body(row_ref, q_ref, knew_ref, vnew_ref, idx_ref, rb_ref, hsel_ref, state_ref, pnew_ref,
                      ck_hbm, cv_hbm, attn_o, pool_o, kbuf, vbuf, sem, *, past, ksel):
    s = pl.program_id(0)

    def copies(j):
        r = row_ref[s * TOPK_MAX + j]
        return (pltpu.make_async_copy(ck_hbm.at[pl.ds(r, 1)], kbuf.at[pl.ds(j, 1)], sem.at[0]),
                pltpu.make_async_copy(cv_hbm.at[pl.ds(r, 1)], vbuf.at[pl.ds(j, 1)], sem.at[1]))

    def start(j, _):
        ck, cv = copies(j)
        ck.start()
        cv.start()
        return 0
    lax.fori_loop(0, TOPK_MAX, start, 0)

    def wait(j, _):
        ck, cv = copies(j)
        ck.wait()
        cv.wait()
        return 0
    lax.fori_loop(0, TOPK_MAX, wait, 0)

    idx = idx_ref[0][:, 0:1]
    is_new = idx == past
    valid = lax.broadcasted_iota(I32, (TOPK_MAX, 1), 0) < ksel
    kg = jnp.where(is_new, knew_ref[0], kbuf[...]).astype(BF16).astype(F32)
    vg = jnp.where(is_new, vnew_ref[0], vbuf[...]).astype(BF16)
    prod = kg * q_ref[0].astype(F32)
    logits = jnp.dot(prod, hsel_ref[...], precision=lax.Precision.HIGHEST, preferred_element_type=F32)
    bucket = _bucket(jnp.maximum(past - idx, 0))
    bias = jnp.zeros((TOPK_MAX, LANES), F32)
    for j in range(N_BUCKETS):
        bias = jnp.where(bucket == j, rb_ref[j:j + 1, :], bias)
    logits = jnp.where(valid, logits + bias, NEG)
    m = jnp.max(logits, axis=0, keepdims=True)
    p = jnp.exp(logits - m)
    p = p / jnp.sum(p, axis=0, keepdims=True)
    pexp = _dot_nt(p.astype(BF16), hsel_ref[...].astype(BF16))
    attn_o[0] = jnp.sum(pexp.astype(BF16).astype(F32) * vg.astype(F32), axis=0, keepdims=True).astype(BF16)

    pnew = pnew_ref[0]
    st = state_ref[0]
    gw = pnew.shape[1] // POOL_GROUPS
    for g, w in enumerate(POOL_WINDOWS):
        cols = slice(g * gw, (g + 1) * gw)
        acc = pnew[:, cols]
        for back in range(1, w):
            acc = acc + st[POOL_STATE - back:POOL_STATE - back + 1, cols]
        pool_o[0, :, cols] = (acc / float(min(POOL_STATE + 1, w)) - pnew[:, cols]).astype(BF16)


def _sample_attn(rows_flat, q3, knew3, vnew3, idx3, rb_pad, hsel, state, pnew3, ck_rows, cv_rows, *, past, ksel):
    db = q3.shape[0]
    pw = pnew3.shape[2]
    per = lambda shape: pl.BlockSpec((1,) + shape, lambda s, r: (s,) + (0,) * len(shape))
    grid_spec = pltpu.PrefetchScalarGridSpec(
        num_scalar_prefetch=1,
        grid=(db,),
        in_specs=[per((1, ATTN_WIDTH)), per((1, ATTN_WIDTH)), per((1, ATTN_WIDTH)), per((TOPK_MAX, LANES)),
                  pl.BlockSpec(rb_pad.shape, lambda s, r: (0, 0)),
                  pl.BlockSpec(hsel.shape, lambda s, r: (0, 0)),
                  per((POOL_STATE, pw)), per((1, pw)),
                  pl.BlockSpec(memory_space=pl.ANY), pl.BlockSpec(memory_space=pl.ANY)],
        out_specs=[per((1, ATTN_WIDTH)), per((1, pw))],
        scratch_shapes=[pltpu.VMEM((TOPK_MAX, ATTN_WIDTH), F32), pltpu.VMEM((TOPK_MAX, ATTN_WIDTH), F32),
                        pltpu.SemaphoreType.DMA((2,))],
    )
    return pl.pallas_call(
        functools.partial(_sample_attn_body, past=past, ksel=ksel),
        grid_spec=grid_spec,
        out_shape=[jax.ShapeDtypeStruct((db, 1, ATTN_WIDTH), BF16), jax.ShapeDtypeStruct((db, 1, pw), BF16)],
        compiler_params=pltpu.CompilerParams(dimension_semantics=("arbitrary",)),
        name="sample_attn",
    )(rows_flat, q3, knew3, vnew3, idx3, rb_pad, hsel, state, pnew3, ck_rows, cv_rows)


def _merge_body(x_ref, attn_ref, pool_ref, ga_ref, gb_ref, wba_ref, wpm_ref, ps_ref, wbp_ref, wout_ref,
                gffn_ref, wr_ref, br_ref, ltri_ref,
                x1_o, h2p_o, ev_o, gv_o, rk_o, cnt_o, carry_scr):
    step = pl.program_id(0)

    @pl.when(step == 0)
    def _():
        carry_scr[...] = jnp.zeros_like(carry_scr)

    a = _dot(attn_ref[...], wba_ref[...])
    pooled = pool_ref[...]
    gw = pooled.shape[1] // POOL_GROUPS
    pm = jnp.concatenate([_dot(pooled[:, g * gw:(g + 1) * gw], wpm_ref[g]) for g in range(POOL_GROUPS)], axis=1)
    bb = _dot((pm * ps_ref[...]).astype(BF16), wbp_ref[...])
    mix = jax.nn.sigmoid(ga_ref[...]) * a + jax.nn.sigmoid(gb_ref[...]) * bb
    x1 = x_ref[...] + _dot(mix.astype(BF16), wout_ref[...])
    x1_o[...] = x1
    h2 = _rms(x1, gffn_ref[...]).astype(BF16)
    half = h2.shape[1] // 2
    lo = lax.shift_right_logical(lax.bitcast_convert_type(h2[:, 0:half].astype(F32), U32), jnp.uint32(16))
    hi = lax.bitcast_convert_type(h2[:, half:].astype(F32), U32) & jnp.uint32(0xFFFF0000)
    h2p_o[...] = hi | lo

    logits = _dot(h2, wr_ref[...]) + br_ref[...]
    lane = lax.broadcasted_iota(I32, (TM, LANES), 1)
    lanef = lane.astype(F32)
    tops, ids = [], []
    l = logits
    for _ in range(TOP_EXPERTS):
        mx = jnp.max(l, axis=1, keepdims=True)
        ix = jnp.min(jnp.where(l == mx, lanef, float(LANES)), axis=1, keepdims=True)
        tops.append(mx)
        ids.append(ix)
        l = jnp.where(lanef == ix, -3e38, l)
    es = [jnp.exp(tv - tops[0]) for tv in tops]
    den = es[0] + es[1] + es[2] + es[3]
    onehot = jnp.zeros((TM, LANES), F32)
    for ix in ids:
        onehot = onehot + jnp.where(lanef == ix, 1.0, 0.0)
    before = _dot(ltri_ref[...], onehot.astype(BF16)) + carry_scr[0:1, :]
    ev = jnp.zeros((TM, LANES), I32)
    gv = jnp.zeros((TM, LANES), F32)
    rk = jnp.zeros((TM, LANES), I32)
    for k in range(TOP_EXPERTS):
        rank = jnp.sum(jnp.where(lanef == ids[k], before, 0.0), axis=1, keepdims=True)
        ev = jnp.where(lane == k, ids[k].astype(I32), ev)
        gv = jnp.where(lane == k, es[k] / den, gv)
        rk = jnp.where(lane == k, rank.astype(I32), rk)
    ev_o[...] = ev
    gv_o[...] = gv
    rk_o[...] = rk
    carry_scr[...] = carry_scr[...] + jnp.sum(onehot, axis=0, keepdims=True)
    cnt_o[...] = carry_scr[...]


def _merge(x_all, attn, pooled, ga, gb, wba, wpm, ps, wbp, wout, gffn, wr, br, ltri):
    nt, d = x_all.shape
    row = lambda width: pl.BlockSpec((TM, width), lambda i: (i, 0))
    consts = [wba, wpm, ps, wbp, wout, gffn, wr, br, ltri]
    return pl.pallas_call(
        _merge_body,
        grid=(nt // TM,),
        in_specs=[row(d), row(attn.shape[1]), row(pooled.shape[1]), row(d), row(d)] + [_full(c.shape) for c in consts],
        out_specs=[row(d), row(d // 2), row(LANES), row(LANES), row(LANES), _full((SUBLANES, LANES))],
        out_shape=[jax.ShapeDtypeStruct((nt, d), F32), jax.ShapeDtypeStruct((nt, d // 2), U32),
                   jax.ShapeDtypeStruct((nt, LANES), I32), jax.ShapeDtypeStruct((nt, LANES), F32),
                   jax.ShapeDtypeStruct((nt, LANES), I32), jax.ShapeDtypeStruct((SUBLANES, LANES), F32)],
        scratch_shapes=[pltpu.VMEM((SUBLANES, LANES), F32)],
        compiler_params=pltpu.CompilerParams(dimension_semantics=("arbitrary",), vmem_limit_bytes=VMEM_LIMIT),
        name="merge_router",
    )(x_all, attn, pooled, ga, gb, *consts)


def _dispatch_body(dest_ref, gend_ref, h2p_ref, xs_hbm, zero_scr, sem, zsem):
    step = pl.program_id(0)

    @pl.when(step == 0)
    def _():
        zero_scr[...] = jnp.zeros_like(zero_scr)

        def fill(e):
            return pltpu.make_async_copy(zero_scr, xs_hbm.at[pl.ds(pl.multiple_of(gend_ref[e] - TS, TS), TS)], zsem)

        def start(e, _):
            @pl.when(gend_ref[e] > gend_ref[e + N_EXPERTS])
            def _():
                fill(e).start()
            return 0
        lax.fori_loop(0, N_EXPERTS, start, 0)

        def wait(e, _):
            @pl.when(gend_ref[e] > gend_ref[e + N_EXPERTS])
            def _():
                fill(e).wait()
            return 0
        lax.fori_loop(0, N_EXPERTS, wait, 0)

        def tail(j):
            return pltpu.make_async_copy(zero_scr, xs_hbm.at[pl.ds(pl.multiple_of(j * TS, TS), TS)], zsem)

        first_unused = gend_ref[N_EXPERTS - 1] // TS
        n_tiles = xs_hbm.shape[0] // TS

        def tail_start(j, _):
            tail(j).start()
            return 0
        lax.fori_loop(first_unused, n_tiles, tail_start, 0)

        def tail_wait(j, _):
            tail(j).wait()
            return 0
        lax.fori_loop(first_unused, n_tiles, tail_wait, 0)

    t0 = step * TM

    def row_copy(r, k):
        d = dest_ref[(t0 + r) * TOP_EXPERTS + k]
        return pltpu.make_async_copy(h2p_ref.at[pl.ds(r, 1)], xs_hbm.at[pl.ds(d, 1)], sem)

    def start(r, _):
        for k in range(TOP_EXPERTS):
            row_copy(r, k).start()
        return 0
    lax.fori_loop(0, TM, start, 0)

    def wait(r, _):
        for k in range(TOP_EXPERTS):
            row_copy(r, k).wait()
        return 0
    lax.fori_loop(0, TM, wait, 0)


def _dispatch(dest_flat, gend, h2p, *, n_slots):
    nt, hw = h2p.shape
    grid_spec = pltpu.PrefetchScalarGridSpec(
        num_scalar_prefetch=2,
        grid=(nt // TM,),
        in_specs=[pl.BlockSpec((TM, hw), lambda i, d, g: (i, 0))],
        out_specs=pl.BlockSpec(memory_space=pl.ANY),
        scratch_shapes=[pltpu.VMEM((TS, hw), U32), pltpu.SemaphoreType.DMA, pltpu.SemaphoreType.DMA],
    )
    return pl.pallas_call(
        _dispatch_body,
        grid_spec=grid_spec,
        out_shape=jax.ShapeDtypeStruct((n_slots, hw), U32),
        compiler_params=pltpu.CompilerParams(dimension_semantics=("arbitrary",)),
        name="dispatch",
    )(dest_flat, gend, h2p)


def _moe_body(te_ref, nu_ref, xs_ref, wgu_ref, bgu_ref, wdn_ref, bdn_ref, ys_o, wgu_b, wdn_b):
    j = pl.program_id(0)
    prev = te_ref[jnp.maximum(j - 1, 0)]
    live = j < nu_ref[0]

    @pl.when(jnp.logical_and(live, jnp.logical_or(j == 0, te_ref[j] != prev)))
    def _():
        wgu_b[...] = wgu_ref[0].astype(BF16)
        wdn_b[...] = wdn_ref[0].astype(BF16)

    @pl.when(live)
    def _():
        words = xs_ref[...]
        x_lo = lax.bitcast_convert_type(lax.shift_left(words, jnp.uint32(16)), F32).astype(BF16)
        x_hi = lax.bitcast_convert_type(words & jnp.uint32(0xFFFF0000), F32).astype(BF16)
        half = words.shape[1]
        gu = _dot(x_lo, wgu_b[0:half, :]) + _dot(x_hi, wgu_b[half:, :]) + bgu_ref[0]
        de = gu.shape[1] // 2
        gate = jnp.minimum(gu[:, 0:de], SWIGLU_LIMIT)
        up = jnp.clip(gu[:, de:], -SWIGLU_LIMIT, SWIGLU_LIMIT)
        act = (up + 1.0) * gate * jax.nn.sigmoid(SWIGLU_ALPHA * gate)
        ys_o[...] = _dot(act.astype(BF16), wdn_b[...]) + bdn_ref[0]

    @pl.when(jnp.logical_not(live))
    def _():
        ys_o[...] = jnp.zeros_like(ys_o)


def _moe(tile_expert, n_used, xs, wgu, bgu, wdn, bdn):
    n_slots, hw = xs.shape
    ne, d, de2 = wgu.shape
    last = lambda j, nu: jnp.minimum(j, nu[0] - 1)
    grid_spec = pltpu.PrefetchScalarGridSpec(
        num_scalar_prefetch=2,
        grid=(n_slots // TS,),
        in_specs=[pl.BlockSpec((TS, hw), lambda j, te, nu: (last(j, nu), 0)),
                  pl.BlockSpec((1, d, de2), lambda j, te, nu: (te[j], 0, 0)),
                  pl.BlockSpec((1, 1, de2), lambda j, te, nu: (te[j], 0, 0)),
                  pl.BlockSpec((1, de2 // 2, d), lambda j, te, nu: (te[j], 0, 0)),
                  pl.BlockSpec((1, 1, d), lambda j, te, nu: (te[j], 0, 0))],
        out_specs=pl.BlockSpec((TS, d), lambda j, te, nu: (j, 0)),
        scratch_shapes=[pltpu.VMEM((d, de2), BF16), pltpu.VMEM((de2 // 2, d), BF16)],
    )
    return pl.pallas_call(
        _moe_body,
        grid_spec=grid_spec,
        out_shape=jax.ShapeDtypeStruct((n_slots, d), F32),
        compiler_params=pltpu.CompilerParams(dimension_semantics=("arbitrary",), vmem_limit_bytes=VMEM_LIMIT),
        name="moe_experts",
    )(tile_expert, n_used, xs, wgu, bgu, wdn, bdn)


def _combine_body(dest_ref, x1_ref, gv_ref, gfin_ref, ys_hbm, y_o, ybuf, sem):
    t0 = pl.program_id(0) * TM

    def row_copy(r, k):
        d = dest_ref[(t0 + r) * TOP_EXPERTS + k]
        return pltpu.make_async_copy(ys_hbm.at[pl.ds(d, 1)], ybuf.at[k, pl.ds(r, 1)], sem)

    def start(r, _):
        for k in range(TOP_EXPERTS):
            row_copy(r, k).start()
        return 0
    lax.fori_loop(0, TM, start, 0)

    def wait(r, _):
        for k in range(TOP_EXPERTS):
            row_copy(r, k).wait()
        return 0
    lax.fori_loop(0, TM, wait, 0)

    gv = gv_ref[...]
    y = jnp.zeros(x1_ref.shape, F32)
    for k in range(TOP_EXPERTS):
        y = y + ybuf[k] * gv[:, k:k + 1]
    y_o[...] = _rms(x1_ref[...] + y, gfin_ref[...])


def _combine(dest_flat, x1, gv, gfin, ys):
    nt, d = x1.shape
    grid_spec = pltpu.PrefetchScalarGridSpec(
        num_scalar_prefetch=1,
        grid=(nt // TM,),
        in_specs=[pl.BlockSpec((TM, d), lambda i, dref: (i, 0)),
                  pl.BlockSpec((TM, LANES), lambda i, dref: (i, 0)),
                  pl.BlockSpec((1, d), lambda i, dref: (0, 0)),
                  pl.BlockSpec(memory_space=pl.ANY)],
        out_specs=pl.BlockSpec((TM, d), lambda i, dref: (i, 0)),
        scratch_shapes=[pltpu.VMEM((TOP_EXPERTS, TM, d), F32), pltpu.SemaphoreType.DMA],
    )
    return pl.pallas_call(
        _combine_body,
        grid_spec=grid_spec,
        out_shape=jax.ShapeDtypeStruct((nt, d), F32),
        compiler_params=pltpu.CompilerParams(dimension_semantics=("arbitrary",)),
        name="combine_norm",
    )(dest_flat, x1, gv, gfin, ys)


def _tri_constants():
    r = lax.broadcasted_iota(I32, (TQ, TQ), 0)
    c = lax.broadcasted_iota(I32, (TQ, TQ), 1)
    incl = (r <= c).astype(BF16)
    u2 = jnp.concatenate([incl, jnp.ones((TQ, LANES), BF16)], axis=1)
    ltri = (c < r).astype(BF16)
    return u2, ltri


def kernel(x_prompt, x_sample, cache_k, cache_v, cache_kidx, state_pool, page_table, meta_tokens, rel_bias, g_mix, w_in, w_pool_mix, pool_scale, w_br_attn, w_br_pool, w_out, g_ffn, w_router, b_router, w_gate_up, b_gate_up, w_down, b_down, g_final):
    nb, seq, d = x_prompt.shape
    db, dec_seq, _ = x_sample.shape
    assert dec_seq == 1, "one new token per sample"
    depth = w_in.shape[0]
    assert depth == 1, "single-layer stack: the combine kernel applies the final norm"
    n_pages = page_table.shape[1]
    past = n_pages * PAGE_SIZE
    pw = state_pool.shape[-1]
    l_seq = seq + N_META
    lp = _round_up(l_seq, TQ)
    n_prompt = nb * lp
    n_sample = _round_up(db, TM)
    nt = n_prompt + n_sample
    ksel_p = min(TOPK_MAX, seq // 4)
    ksel_s = min(TOPK_MAX, (past + dec_seq) // 4)
    n_tiles = (TOP_EXPERTS * nt) // TS + N_EXPERTS
    n_slots = n_tiles * TS

    meta = jnp.broadcast_to(meta_tokens[None].astype(x_prompt.dtype), (nb, N_META, d))
    xp = jnp.concatenate([meta, x_prompt, jnp.zeros((nb, lp - l_seq, d), x_prompt.dtype)], axis=1)
    xs_rows = jnp.concatenate([x_sample[:, 0, :], jnp.zeros((n_sample - db, d), x_sample.dtype)], axis=0)
    x_all = jnp.concatenate([xp.reshape(n_prompt, d), xs_rows], axis=0)

    u2, ltri = _tri_constants()
    kpos = jnp.arange(_round_up(past + LANES, TQ), dtype=I32)
    kvals = jnp.zeros((kpos.shape[0], LANES), F32).at[:, 0].set((kpos // LANES).astype(F32))
    kvals = kvals.at[:, 1].set((kpos % LANES).astype(F32)).astype(BF16).reshape(-1, TQ, LANES)
    hsel = (jnp.arange(ATTN_WIDTH, dtype=I32)[:, None] // HEAD_DIM == jnp.arange(LANES, dtype=I32)[None, :]).astype(F32)
    rb_pad = jnp.pad(rel_bias.astype(F32), ((0, 0), (0, LANES - N_HEADS)))
    page_flat = page_table.reshape(-1).astype(I32)

    outs = {name: [] for name in ("k_p", "v_p", "ki_p", "pool_p", "k_s", "v_s", "ki_s", "pool_s")}
    for l in range(depth):
        wl = w_in[l]
        aw = ATTN_WIDTH
        o_ki = 4 * aw
        o_wi = o_ki + IDX_DIM
        o_p = o_wi + N_IDX_HEADS
        o_ga = o_p + pw
        o_gb = o_ga + d
        wa = wl[:, 0:o_ki].astype(BF16)
        wki = wl[:, o_ki:o_wi]
        wki2 = jnp.concatenate([wki, wki], axis=1).astype(BF16)
        wwi = jnp.pad(wl[:, o_wi:o_p], ((0, 0), (0, LANES - N_IDX_HEADS))).astype(BF16)
        wp = wl[:, o_p:o_ga].astype(BF16)
        wga = wl[:, o_ga:o_gb].astype(BF16)
        wgb = wl[:, o_gb:o_gb + d].astype(BF16)

        (qs, kf, vf, kb, vb, qib, ki2b, kif, wif, pf, ga, gb) = _inproj(
            x_all, g_mix[l][None, :], wa, wki2, wwi, wp, wga, wgb)

        attn_p = _attn_prompt(qs, qib, wif, kb, vb, ki2b, u2, rel_bias.astype(F32), nb=nb, lp=lp, ksel=ksel_p)
        pooled_p = _pool_prompt(pf, n_rows=n_prompt, lp=lp)

        sl = slice(n_prompt, n_prompt + db)
        idx3 = _sample_index(page_flat, qib[sl].reshape(db, N_IDX_HEADS, IDX_DIM), kif[sl, 0:IDX_DIM][:, None, :],
                             wif[sl, 0:N_IDX_HEADS][:, :, None], u2, kvals, cache_kidx[l],
                             n_pages=n_pages, ksel=ksel_s)
        idx = idx3[:, :, 0]
        cached = jnp.minimum(idx, past - 1)
        rows = jnp.take_along_axis(page_table.astype(I32), cached // PAGE_SIZE, axis=1) * PAGE_SIZE + cached % PAGE_SIZE
        attn_s, pooled_s = _sample_attn(
            rows.reshape(-1), qs[sl][:, None, :], kf[sl][:, None, :], vf[sl][:, None, :], idx3, rb_pad, hsel,
            state_pool[l], pf[sl][:, None, :],
            cache_k[l].reshape(-1, ATTN_WIDTH), cache_v[l].reshape(-1, ATTN_WIDTH), past=past, ksel=ksel_s)

        pad_s = lambda a: jnp.concatenate([a[:, 0, :], jnp.zeros((n_sample - db, a.shape[2]), a.dtype)], axis=0)
        attn_all = jnp.concatenate([attn_p, pad_s(attn_s)], axis=0)
        pooled_all = jnp.concatenate([pooled_p, pad_s(pooled_s)], axis=0)

        wr = jnp.pad(w_router[l], ((0, 0), (0, LANES - N_EXPERTS))).astype(BF16)
        br = jnp.concatenate([b_router[l].astype(F32), jnp.full((LANES - N_EXPERTS,), NEG, F32)])[None, :]
        x1, h2p, ev, gv, rk, cnt = _merge(
            x_all, attn_all, pooled_all, ga, gb, w_br_attn[l].astype(BF16), w_pool_mix[l].astype(BF16),
            pool_scale[l][None, :], w_br_pool[l].astype(BF16), w_out[l].astype(BF16), g_ffn[l][None, :], wr, br, ltri)

        counts = cnt[0, 0:N_EXPERTS].astype(I32)
        padded = (counts + TS - 1) // TS * TS
        gend = jnp.cumsum(padded)
        gstart = gend - padded
        dest = (gstart[ev[:, 0:TOP_EXPERTS]] + rk[:, 0:TOP_EXPERTS]).reshape(-1)
        tile_expert = jnp.minimum(
            jnp.searchsorted(gend, jnp.arange(n_tiles, dtype=I32) * TS, side="right"), N_EXPERTS - 1).astype(I32)
        n_used = (gend[-1] // TS).astype(I32)[None]

        xs_sorted = _dispatch(dest, jnp.concatenate([gend, gstart + counts]).astype(I32), h2p, n_slots=n_slots)
        ys = _moe(tile_expert, n_used, xs_sorted, w_gate_up[l], b_gate_up[l][:, None, :], w_down[l],
                  b_down[l][:, None, :])
        y_all = _combine(dest, x1, gv, g_final[None, :], ys)

        kp = kf[:n_prompt].reshape(nb, lp, N_HEADS, HEAD_DIM)[:, :l_seq]
        vp = vf[:n_prompt].reshape(nb, lp, N_HEADS, HEAD_DIM)[:, :l_seq]
        kip = kif[:n_prompt, 0:IDX_DIM].reshape(nb, lp, IDX_DIM)[:, :l_seq]
        pp = pf[:n_prompt].reshape(nb, lp, pw)
        outs["k_p"].append(kp)
        outs["v_p"].append(vp)
        outs["ki_p"].append(kip)
        outs["pool_p"].append(pp[:, l_seq - POOL_STATE:l_seq])
        outs["k_s"].append(kf[sl].reshape(db, 1, N_HEADS, HEAD_DIM))
        outs["v_s"].append(vf[sl].reshape(db, 1, N_HEADS, HEAD_DIM))
        outs["ki_s"].append(kif[sl, 0:IDX_DIM].reshape(db, 1, IDX_DIM))
        outs["pool_s"].append(jnp.concatenate([state_pool[l][:, 1:], pf[sl][:, None, :]], axis=1))

    y_prompt = y_all[:n_prompt].reshape(nb, lp, d)[:, N_META:l_seq]
    y_sample = y_all[n_prompt:n_prompt + db].reshape(db, 1, d)
    st = lambda name: jnp.stack(outs[name])
    return (y_prompt, y_sample, st("k_p"), st("v_p"), st("ki_p"), st("pool_p"),
            st("k_s"), st("v_s"), st("ki_s"), st("pool_s"))
```

```python
import functools
import math

import jax
import jax.numpy as jnp
from jax import lax
from jax.experimental import pallas as pl
from jax.experimental.pallas import tpu as pltpu

F32 = jnp.float32
BF16 = jnp.bfloat16
I32 = jnp.int32
U32 = jnp.uint32

N_META = 16
N_HEADS = 8
HEAD_DIM = 64
ATTN_WIDTH = N_HEADS * HEAD_DIM
N_IDX_HEADS = 8
IDX_DIM = 64
TOPK_MAX = 256
N_BUCKETS = 32
MAX_DISTANCE = 128
POOL_WINDOWS = (2, 4, 8, 16)
POOL_GROUPS = 4
POOL_STATE = 15
N_EXPERTS = 32
TOP_EXPERTS = 4
SWIGLU_LIMIT = 7.0
SWIGLU_ALPHA = 1.702
PAGE_SIZE = 128
EPS = 1e-6

LANES = 128
SUBLANES = 8
MXU_DIM = 256
TQ = MXU_DIM
TM = MXU_DIM
TS = MXU_DIM
VMEM_LIMIT = 56 * 1024 * 1024

NEG = -1e30
INT_MIN = -(2 ** 31)

_MAX_EXACT = N_BUCKETS // 2
_BUCKET_THRESHOLDS = tuple(
    math.ceil(_MAX_EXACT * (MAX_DISTANCE / _MAX_EXACT) ** (j / (N_BUCKETS - _MAX_EXACT)))
    for j in range(1, N_BUCKETS - _MAX_EXACT))


def _round_up(a, m):
    return (a + m - 1) // m * m


def _rms(x, g):
    return x * lax.rsqrt(jnp.mean(x * x, axis=-1, keepdims=True) + EPS) * g


def _dot(a, b):
    return jnp.dot(a, b, preferred_element_type=F32)


def _dot_nt(a, b):
    return lax.dot_general(a, b, (((1,), (1,)), ((), ())), preferred_element_type=F32)


def _bucket(dist):
    large = jnp.full(dist.shape, _MAX_EXACT, I32)
    for thr in _BUCKET_THRESHOLDS:
        large = large + jnp.where(dist >= thr, 1, 0)
    return jnp.where(dist < _MAX_EXACT, dist, large)


def _sort_key(s):
    s = jnp.where(s == 0.0, 0.0, s)
    bits = lax.bitcast_convert_type(s, I32)
    return jnp.where(bits >= 0, bits, bits ^ jnp.int32(0x7FFFFFFF))


def _full(shape):
    return pl.BlockSpec(shape, lambda *_: (0,) * len(shape))


def _inproj_body(x_ref, g_ref, wa_ref, wki_ref, wwi_ref, wp_ref, wga_ref, wgb_ref,
                 qs_o, kf_o, vf_o, kb_o, vb_o, qib_o, ki2b_o, kif_o, wif_o, pf_o, ga_o, gb_o):
    h = _rms(x_ref[...], g_ref[...]).astype(BF16)
    za = _dot(h, wa_ref[...])
    w = ATTN_WIDTH
    qs_o[...] = (za[:, 0:w] * (HEAD_DIM ** -0.5)).astype(BF16)
    k = za[:, w:2 * w]
    v = za[:, 2 * w:3 * w]
    kf_o[...] = k
    vf_o[...] = v
    kb_o[...] = k.astype(BF16)
    vb_o[...] = v.astype(BF16)
    qib_o[...] = za[:, 3 * w:4 * w].astype(BF16)
    ki2 = _dot(h, wki_ref[...])
    kif_o[...] = ki2
    ki2b_o[...] = ki2.astype(BF16)
    wif_o[...] = _dot(h, wwi_ref[...])
    pf_o[...] = _dot(h, wp_ref[...])
    ga_o[...] = _dot(h, wga_ref[...])
    gb_o[...] = _dot(h, wgb_ref[...])


def _inproj(x_all, g, wa, wki2, wwi, wp, wga, wgb):
    nt, d = x_all.shape
    pw = wp.shape[1]
    row = lambda width: pl.BlockSpec((TM, width), lambda i: (i, 0))
    outs = [
        (ATTN_WIDTH, BF16), (ATTN_WIDTH, F32), (ATTN_WIDTH, F32), (ATTN_WIDTH, BF16), (ATTN_WIDTH, BF16),
        (ATTN_WIDTH, BF16), (LANES, BF16), (LANES, F32), (LANES, F32), (pw, F32), (d, F32), (d, F32)]
    return pl.pallas_call(
        _inproj_body,
        grid=(nt // TM,),
        in_specs=[row(d), _full((1, d)), _full(wa.shape), _full(wki2.shape), _full(wwi.shape),
                  _full(wp.shape), _full(wga.shape), _full(wgb.shape)],
        out_specs=[row(wd) for wd, _ in outs],
        out_shape=[jax.ShapeDtypeStruct((nt, wd), dt) for wd, dt in outs],
        compiler_params=pltpu.CompilerParams(dimension_semantics=("arbitrary",), vmem_limit_bytes=VMEM_LIMIT),
        name="inproj",
    )(x_all, g, wa, wki2, wwi, wp, wga, wgb)


def _attn_prompt_body(qs_ref, qib_ref, wif_ref, kb_ref, vb_ref, ki2b_ref, u2_ref, rb_ref,
                      o_ref, qis_scr, wb_scr, key_scr, ntab_scr, qz_scr, m_scr, acc_scr, sum_scr, *, ksel):
    b = pl.program_id(0)
    i = pl.program_id(1)
    nchunk = i + 1
    n_far = jnp.maximum(i - 1, 0)
    hr = TQ // 2

    @pl.when((b == 0) & (i == 0))
    def _():
        def slab_rows(s, _):
            r0 = pl.multiple_of(s * SUBLANES, SUBLANES)
            r = r0 + lax.broadcasted_iota(I32, (SUBLANES, 2 * TQ), 0)
            x = lax.broadcasted_iota(I32, (SUBLANES, 2 * TQ), 1)
            bucket = _bucket(jnp.maximum(r + TQ - x, 0))
            accs = [jnp.zeros((SUBLANES, 2 * TQ), F32) for _ in range(N_HEADS)]
            for j in range(N_BUCKETS - 1):
                m = bucket == j
                accs = [jnp.where(m, rb_ref[j, h] - rb_ref[N_BUCKETS - 1, h], accs[h]) for h in range(N_HEADS)]
            for h in range(N_HEADS):
                ntab_scr[h, 0, pl.ds(r0, SUBLANES), :] = accs[h][:, 0:TQ]
                ntab_scr[h, 1, pl.ds(r0, SUBLANES), :] = accs[h][:, TQ:2 * TQ]
            return 0
        lax.fori_loop(0, TQ // SUBLANES, slab_rows, 0)

    lane = lax.broadcasted_iota(I32, (TQ, LANES), 1)
    lo_half = lane < HEAD_DIM

    def two(x):
        return jnp.concatenate([x, x], axis=1)

    for h in range(N_HEADS):
        cols = slice((h // 2) * LANES, (h // 2 + 1) * LANES)
        keep = lo_half if h % 2 == 0 else jnp.logical_not(lo_half)
        qis_scr[h] = jnp.where(keep, qib_ref[:, cols].astype(F32), 0.0).astype(BF16)
        qz_scr[h] = jnp.where(keep, qs_ref[:, cols].astype(F32), 0.0).astype(BF16)
        wb_scr[h] = jnp.broadcast_to(wif_ref[:, h:h + 1], (TQ, LANES))
        m_scr[h] = jnp.full((TQ, LANES), NEG, F32)
        acc_scr[h] = jnp.zeros((TQ, LANES), F32)
        sum_scr[h] = jnp.zeros((TQ, LANES), F32)

    dmat = lax.broadcasted_iota(I32, (TQ, TQ), 1) - lax.broadcasted_iota(I32, (TQ, TQ), 0)

    def score_chunk(c, _):
        kc = ki2b_ref[pl.ds(pl.multiple_of(c * TQ, TQ), TQ), :]
        s = jnp.zeros((TQ, TQ), F32)
        for h in range(N_IDX_HEADS):
            s = s + two(wb_scr[h]) * jnp.maximum(_dot_nt(qis_scr[h], kc), 0.0)
        key_scr[c] = jnp.where(dmat <= (i - c) * TQ, _sort_key(s), INT_MIN)
        return 0
    lax.fori_loop(0, nchunk, score_chunk, 0)

    def count(r0, pred):
        def body(c, acc):
            hit = jnp.where(pred(key_scr[c, r0:r0 + hr, :]), 1.0, 0.0)
            return acc + (hit[:, 0:LANES] + hit[:, LANES:2 * LANES])
        acc = lax.fori_loop(0, nchunk, body, jnp.zeros((hr, LANES), F32))
        return jnp.broadcast_to(jnp.sum(acc, axis=1, keepdims=True), (hr, LANES))

    ts, needs = [], []
    for r0 in (0, hr):
        def bit_step(step, t, r0=r0):
            cand = t + lax.shift_left(jnp.int32(1), 31 - step)
            cand2 = two(cand)
            return jnp.where(count(r0, lambda k: k >= cand2) >= ksel, cand, t)
        t = lax.fori_loop(0, 32, bit_step, jnp.full((hr, LANES), INT_MIN, I32))
        th2 = two(t)
        n_gt = count(r0, lambda k: k > th2)
        ts.append(t)
        needs.append(jnp.where(t == INT_MIN, 0.0, ksel - n_gt))
    t2 = two(jnp.concatenate(ts, axis=0))
    need2 = two(jnp.concatenate(needs, axis=0))

    def mask_chunk(c, carry):
        kc = key_scr[c]
        eq = kc == t2
        pre = _dot(jnp.where(eq, 1.0, 0.0).astype(BF16), u2_ref[...])
        prefix = pre[:, 0:TQ] + two(carry)
        tie_ok = jnp.where(eq, prefix, 3e38) <= need2
        madd = jnp.where(kc > t2, 0.0, jnp.where(tie_ok, 0.0, NEG))
        key_scr[c] = lax.bitcast_convert_type(madd, I32)
        return carry + pre[:, TQ:TQ + LANES]
    lax.fori_loop(0, nchunk, mask_chunk, jnp.zeros((TQ, LANES), F32))

    def chunk_logits(c, slab):
        rows = pl.ds(pl.multiple_of(c * TQ, TQ), TQ)
        madd = lax.bitcast_convert_type(key_scr[c], F32)
        out = []
        for hp in range(N_HEADS // 2):
            kc = kb_ref[rows, hp * LANES:(hp + 1) * LANES]
            for h in (2 * hp, 2 * hp + 1):
                l = _dot_nt(qz_scr[h], kc) + madd
                out.append(l if slab is None else l + ntab_scr[h, slab])
        return rows, out

    def row_max(c, slab):
        _, ls = chunk_logits(c, slab)
        for h, l in enumerate(ls):
            m_scr[h] = jnp.maximum(m_scr[h], jnp.maximum(l[:, 0:LANES], l[:, LANES:2 * LANES]))

    def accumulate(c, slab):
        rows, ls = chunk_logits(c, slab)
        for h, l in enumerate(ls):
            p = jnp.exp(l - two(m_scr[h]))
            vc = vb_ref[rows, (h // 2) * LANES:(h // 2 + 1) * LANES]
            acc_scr[h] = acc_scr[h] + _dot(p.astype(BF16), vc)
            sum_scr[h] = sum_scr[h] + (p[:, 0:LANES] + p[:, LANES:2 * LANES])

    def sweep(step):
        def far(c, _):
            step(c, None)
            return 0
        lax.fori_loop(0, n_far, far, 0)

        @pl.when(i >= 1)
        def _():
            step(i - 1, 0)
        step(i, 1)

    sweep(row_max)
    for h in range(N_HEADS):
        m_scr[h] = jnp.broadcast_to(jnp.max(m_scr[h], axis=1, keepdims=True), (TQ, LANES))
    sweep(accumulate)
    for hp in range(N_HEADS // 2):
        outs = [acc_scr[h] / jnp.sum(sum_scr[h], axis=1, keepdims=True) for h in (2 * hp, 2 * hp + 1)]
        o_ref[:, hp * LANES:(hp + 1) * LANES] = jnp.where(lo_half, outs[0], outs[1]).astype(BF16)


def _attn_prompt(qs, qib, wif, kb, vb, ki2b, u2, rel_bias, *, nb, lp, ksel):
    nq = lp // TQ
    rowq = lambda width: pl.BlockSpec((TQ, width), lambda b, i: (b * nq + i, 0))
    seq = lambda width: pl.BlockSpec((lp, width), lambda b, i: (b, 0))
    per_head = lambda dt: pltpu.VMEM((N_HEADS, TQ, LANES), dt)
    return pl.pallas_call(
        functools.partial(_attn_prompt_body, ksel=float(ksel)),
        grid=(nb, nq),
        in_specs=[rowq(ATTN_WIDTH), rowq(ATTN_WIDTH), rowq(LANES), seq(ATTN_WIDTH), seq(ATTN_WIDTH), seq(LANES),
                  _full(u2.shape), pl.BlockSpec(memory_space=pltpu.SMEM)],
        out_specs=rowq(ATTN_WIDTH),
        out_shape=jax.ShapeDtypeStruct((nb * lp, ATTN_WIDTH), BF16),
        scratch_shapes=[
            per_head(BF16),
            per_head(F32),
            pltpu.VMEM((nq, TQ, TQ), I32),
            pltpu.VMEM((N_HEADS, 2, TQ, TQ), F32),
            per_head(BF16),
            per_head(F32), per_head(F32), per_head(F32),
        ],
        compiler_params=pltpu.CompilerParams(dimension_semantics=("arbitrary", "arbitrary"),
                                             vmem_limit_bytes=VMEM_LIMIT),
        name="attn_prompt",
    )(qs, qib, wif, kb, vb, ki2b, u2, rel_bias)


def _pool_prompt_body(p_ref, halo_ref, o_ref, ext_scr, *, tiles_per_seq):
    j = pl.program_id(0) % tiles_per_seq
    halo = 2 * SUBLANES
    p = p_ref[...]
    ext_scr[0:halo, :] = jnp.where(j == 0, 0.0, halo_ref[...])
    ext_scr[halo:halo + TM, :] = p
    pos = j * TM + lax.broadcasted_iota(I32, (TM, 1), 0)
    gw = p.shape[1] // POOL_GROUPS
    for g, w in enumerate(POOL_WINDOWS):
        cols = slice(g * gw, (g + 1) * gw)
        s = p[:, cols]
        for back in range(1, w):
            s = s + ext_scr[halo - back:halo - back + TM, cols]
        cnt = jnp.minimum(pos + 1, w).astype(F32)
        o_ref[:, cols] = (s / cnt - p[:, cols]).astype(BF16)


def _pool_prompt(pf, *, n_rows, lp):
    pw = pf.shape[1]
    halo = 2 * SUBLANES
    return pl.pallas_call(
        functools.partial(_pool_prompt_body, tiles_per_seq=lp // TM),
        grid=(n_rows // TM,),
        in_specs=[pl.BlockSpec((TM, pw), lambda i: (i, 0)),
                  pl.BlockSpec((halo, pw), lambda i: (jnp.maximum(i * (TM // halo) - 1, 0), 0))],
        out_specs=pl.BlockSpec((TM, pw), lambda i: (i, 0)),
        out_shape=jax.ShapeDtypeStruct((n_rows, pw), BF16),
        scratch_shapes=[pltpu.VMEM((halo + TM, pw), F32)],
        compiler_params=pltpu.CompilerParams(dimension_semantics=("arbitrary",)),
        name="pool_prompt",
    )(pf, pf)


def _sample_index_body(pt_ref, qi_ref, kinew_ref, wi_ref, u2_ref, kv_ref, cache_hbm,
                       idx_o, kbuf, sem, *, layer, n_pages, ksel):
    s = pl.program_id(0)
    past = n_pages * PAGE_SIZE
    next_w = past + LANES

    def page_copy(pg):
        return pltpu.make_async_copy(cache_hbm.at[layer, pt_ref[s * n_pages + pg]],
                                     kbuf.at[pl.ds(pl.multiple_of(pg * PAGE_SIZE, PAGE_SIZE), PAGE_SIZE)], sem)

    def start(pg, _):
        page_copy(pg).start()
        return 0
    lax.fori_loop(0, n_pages, start, 0)

    def wait(pg, _):
        page_copy(pg).wait()
        return 0
    lax.fori_loop(0, n_pages, wait, 0)

    qi = qi_ref[0]
    wcol = wi_ref[0]
    d = _dot_nt(qi, kbuf[...].astype(BF16))
    sc = jnp.sum(wcol * jnp.maximum(d, 0.0), axis=0, keepdims=True)
    knew = kinew_ref[0].astype(BF16).astype(F32)
    dn = jnp.sum(qi.astype(F32) * knew, axis=1, keepdims=True)
    sn = jnp.sum(wcol * jnp.maximum(dn, 0.0), axis=0, keepdims=True)
    lane = lax.broadcasted_iota(I32, (1, LANES), 1)
    tail = jnp.where(lane == 0, _sort_key(jnp.broadcast_to(sn, (1, LANES))), INT_MIN)
    key = jnp.concatenate([_sort_key(sc), tail], axis=1)
    pos = lax.broadcasted_iota(I32, (1, next_w), 1)

    def cnt(pred):
        return jnp.sum(jnp.where(pred, 1.0, 0.0), axis=1, keepdims=True)

    def bit_step(step, t):
        cand = t + lax.shift_left(jnp.int32(1), 31 - step)
        return jnp.where(cnt(key >= cand) >= ksel, cand, t)
    t = lax.fori_loop(0, 32, bit_step, jnp.full((1, 1), INT_MIN, I32))
    gt = key > t
    eq = key == t
    need = ksel - cnt(gt)

    nbits = (next_w - 1).bit_length()

    def cut_step(step, c):
        cand = c - lax.shift_left(jnp.int32(1), nbits - 1 - step)
        ok = cnt(jnp.logical_and(eq, pos <= cand)) >= need
        return jnp.where(ok, cand, c)
    cut = lax.fori_loop(0, nbits, cut_step, jnp.full((1, 1), 2 ** nbits - 1, I32))
    sel = jnp.where(gt, 1.0, jnp.where(jnp.logical_and(eq, pos <= cut), 1.0, 0.0))
    sel = jnp.where(pos <= past, sel, 0.0)

    jrow = (lax.broadcasted_iota(I32, (TOPK_MAX, TQ), 0) + 1).astype(F32)
    carry = jnp.zeros((1, LANES), F32)
    acc = jnp.zeros((TOPK_MAX, LANES), F32)
    for c in range(next_w // TQ + (1 if next_w % TQ else 0)):
        width = min(TQ, next_w - c * TQ)
        sc_c = sel[:, c * TQ:c * TQ + width]
        if width < TQ:
            sc_c = jnp.concatenate([sc_c, jnp.zeros((1, TQ - width), F32)], axis=1)
        pre = _dot(sc_c.astype(BF16), u2_ref[...])
        rank = pre[:, 0:TQ] + jnp.concatenate([carry, carry], axis=1)
        onehot = jnp.where(jnp.logical_and(rank == jrow, sc_c > 0.0), 1.0, 0.0).astype(BF16)
        acc = acc + _dot(onehot, kv_ref[c])
        carry = carry + pre[:, TQ:TQ + LANES]
    idx_o[0] = (acc[:, 0:1] * LANES + acc[:, 1:2]).astype(I32) + jnp.zeros((TOPK_MAX, LANES), I32)


def _sample_index(page_table_flat, qi3, kinew3, wi3, u2, kvals, cache_kidx, *, layer, n_pages, ksel):
    db = qi3.shape[0]
    past = n_pages * PAGE_SIZE
    grid_spec = pltpu.PrefetchScalarGridSpec(
        num_scalar_prefetch=1,
        grid=(db,),
        in_specs=[pl.BlockSpec((1, N_IDX_HEADS, IDX_DIM), lambda s, pt: (s, 0, 0)),
                  pl.BlockSpec((1, 1, IDX_DIM), lambda s, pt: (s, 0, 0)),
                  pl.BlockSpec((1, N_IDX_HEADS, 1), lambda s, pt: (s, 0, 0)),
                  pl.BlockSpec(u2.shape, lambda s, pt: (0, 0)),
                  pl.BlockSpec(kvals.shape, lambda s, pt: (0, 0, 0)),
                  pl.BlockSpec(memory_space=pl.ANY)],
        out_specs=pl.BlockSpec((1, TOPK_MAX, LANES), lambda s, pt: (s, 0, 0)),
        scratch_shapes=[pltpu.VMEM((past, IDX_DIM), F32), pltpu.SemaphoreType.DMA],
    )
    return pl.pallas_call(
        functools.partial(_sample_index_body, layer=layer, n_pages=n_pages, ksel=float(ksel)),
        grid_spec=grid_spec,
        out_shape=jax.ShapeDtypeStruct((db, TOPK_MAX, LANES), I32),
        compiler_params=pltpu.CompilerParams(dimension_semantics=("arbitrary",), vmem_limit_bytes=VMEM_LIMIT),
        name="sample_index",
    )(page_table_flat, qi3, kinew3, wi3, u2, kvals, cache_kidx)


def _sample_attn_body(row_ref, q_ref, knew_ref, vnew_ref, idx_ref, rb_ref, state_ref, pnew_ref,
                      ck_hbm, cv_hbm, attn_o, pool_o, kbuf, vbuf, sem, *, layer, past, ksel):
    s = pl.program_id(0)

    def copies(j):
        r = row_ref[s * TOPK_MAX + j]
        page = lax.shift_right_logical(r, PAGE_SIZE.bit_length() - 1)
        slot = r & (PAGE_SIZE - 1)
        return (pltpu.make_async_copy(ck_hbm.at[layer, page, slot], kbuf.at[j], sem.at[0]),
                pltpu.make_async_copy(cv_hbm.at[layer, page, slot], vbuf.at[j], sem.at[1]))

    def start(j, _):
        ck, cv = copies(j)
        ck.start()
        cv.start()
        return 0
    lax.fori_loop(0, TOPK_MAX, start, 0)

    def wait(j, _):
        ck, cv = copies(j)
        ck.wait()
        cv.wait()
        return 0
    lax.fori_loop(0, TOPK_MAX, wait, 0)

    idx = idx_ref[0][:, 0:1]
    is_new = idx == past
    valid = lax.broadcasted_iota(I32, (TOPK_MAX, 1), 0) < ksel
    lane = lax.broadcasted_iota(I32, (TOPK_MAX, LANES), 1)
    q = q_ref[0].astype(F32)
    logits = jnp.zeros((TOPK_MAX, LANES), F32)
    for h in range(N_HEADS):
        kh = jnp.where(is_new, knew_ref[0, h:h + 1, :], kbuf[:, h, :]).astype(BF16).astype(F32)
        logits = jnp.where(lane == h, jnp.sum(kh * q[h:h + 1, :], axis=1, keepdims=True), logits)
    bucket = _bucket(jnp.maximum(past - idx, 0))
    bias = jnp.zeros((TOPK_MAX, LANES), F32)
    for j in range(N_BUCKETS):
        bias = jnp.where(bucket == j, rb_ref[j:j + 1, :], bias)
    logits = jnp.where(valid, logits + bias, NEG)
    m = jnp.max(logits, axis=0, keepdims=True)
    p = jnp.exp(logits - m)
    p = (p / jnp.sum(p, axis=0, keepdims=True)).astype(BF16).astype(F32)
    rows = []
    for h in range(N_HEADS):
        vh = jnp.where(is_new, vnew_ref[0, h:h + 1, :], vbuf[:, h, :]).astype(BF16).astype(F32)
        rows.append(jnp.sum(p[:, h:h + 1] * vh, axis=0, keepdims=True))
    attn_o[0] = jnp.concatenate(rows, axis=0).astype(BF16)

    pnew = pnew_ref[0]
    st = state_ref[0]
    gw = pnew.shape[1] // POOL_GROUPS
    for g, w in enumerate(POOL_WINDOWS):
        cols = slice(g * gw, (g + 1) * gw)
        acc = pnew[:, cols]
        for back in range(1, w):
            acc = acc + st[POOL_STATE - back:POOL_STATE - back + 1, cols]
        pool_o[0, :, cols] = (acc / float(min(POOL_STATE + 1, w)) - pnew[:, cols]).astype(BF16)


def _sample_attn(rows_flat, q3, knew3, vnew3, idx3, rb_pad, state, pnew3, cache_k, cache_v, *, layer, past, ksel):
    db = q3.shape[0]
    pw = pnew3.shape[2]
    per = lambda shape: pl.BlockSpec((1,) + shape, lambda s, r: (s,) + (0,) * len(shape))
    slab = (N_HEADS, HEAD_DIM)
    grid_spec = pltpu.PrefetchScalarGridSpec(
        num_scalar_prefetch=1,
        grid=(db,),
        in_specs=[per(slab), per(slab), per(slab), per((TOPK_MAX, LANES)),
                  pl.BlockSpec(rb_pad.shape, lambda s, r: (0, 0)),
                  per((POOL_STATE, pw)), per((1, pw)),
                  pl.BlockSpec(memory_space=pl.ANY), pl.BlockSpec(memory_space=pl.ANY)],
        out_specs=[per(slab), per((1, pw))],
        scratch_shapes=[pltpu.VMEM((TOPK_MAX,) + slab, F32), pltpu.VMEM((TOPK_MAX,) + slab, F32),
                        pltpu.SemaphoreType.DMA((2,))],
    )
    return pl.pallas_call(
        functools.partial(_sample_attn_body, layer=layer, past=past, ksel=ksel),
        grid_spec=grid_spec,
        out_shape=[jax.ShapeDtypeStruct((db,) + slab, BF16), jax.ShapeDtypeStruct((db, 1, pw), BF16)],
        compiler_params=pltpu.CompilerParams(dimension_semantics=("arbitrary",)),
        name="sample_attn",
    )(rows_flat, q3, knew3, vnew3, idx3, rb_pad, state, pnew3, cache_k, cache_v)


def _merge_body(x_ref, attn_ref, pool_ref, ga_ref, gb_ref, wba_ref, wpm_ref, ps_ref, wbp_ref, wout_ref,
                gffn_ref, wr_ref, br_ref, ltri_ref,
                x1_o, h2p_o, ev_o, gv_o, rk_o, cnt_o, carry_scr):
    step = pl.program_id(0)

    @pl.when(step == 0)
    def _():
        carry_scr[...] = jnp.zeros_like(carry_scr)

    a = _dot(attn_ref[...], wba_ref[...])
    pooled = pool_ref[...]
    gw = pooled.shape[1] // POOL_GROUPS
    pm = jnp.concatenate([_dot(pooled[:, g * gw:(g + 1) * gw], wpm_ref[g]) for g in range(POOL_GROUPS)], axis=1)
    bb = _dot((pm * ps_ref[...]).astype(BF16), wbp_ref[...])
    mix = jax.nn.sigmoid(ga_ref[...]) * a + jax.nn.sigmoid(gb_ref[...]) * bb
    x1 = x_ref[...] + _dot(mix.astype(BF16), wout_ref[...])
    x1_o[...] = x1
    h2 = _rms(x1, gffn_ref[...]).astype(BF16)
    half = h2.shape[1] // 2
    lo = lax.shift_right_logical(lax.bitcast_convert_type(h2[:, 0:half].astype(F32), U32), jnp.uint32(16))
    hi = lax.bitcast_convert_type(h2[:, half:].astype(F32), U32) & jnp.uint32(0xFFFF0000)
    h2p_o[...] = hi | lo

    logits = _dot(h2, wr_ref[...]) + br_ref[...]
    lane = lax.broadcasted_iota(I32, (TM, LANES), 1)
    lanef = lane.astype(F32)
    tops, ids = [], []
    l = logits
    for _ in range(TOP_EXPERTS):
        mx = jnp.max(l, axis=1, keepdims=True)
        ix = jnp.min(jnp.where(l == mx, lanef, float(LANES)), axis=1, keepdims=True)
        tops.append(mx)
        ids.append(ix)
        l = jnp.where(lanef == ix, -3e38, l)
    es = [jnp.exp(tv - tops[0]) for tv in tops]
    den = es[0] + es[1] + es[2] + es[3]
    onehot = jnp.zeros((TM, LANES), F32)
    for ix in ids:
        onehot = onehot + jnp.where(lanef == ix, 1.0, 0.0)
    before = _dot(ltri_ref[...], onehot.astype(BF16)) + carry_scr[0:1, :]
    ev = jnp.zeros((TM, LANES), I32)
    gv = jnp.zeros((TM, LANES), F32)
    rk = jnp.zeros((TM, LANES), I32)
    for k in range(TOP_EXPERTS):
        rank = jnp.sum(jnp.where(lanef == ids[k], before, 0.0), axis=1, keepdims=True)
        ev = jnp.where(lane == k, ids[k].astype(I32), ev)
        gv = jnp.where(lane == k, es[k] / den, gv)
        rk = jnp.where(lane == k, rank.astype(I32), rk)
    ev_o[...] = ev
    gv_o[...] = gv
    rk_o[...] = rk
    carry_scr[...] = carry_scr[...] + jnp.sum(onehot, axis=0, keepdims=True)
    cnt_o[...] = carry_scr[...]


def _merge(x_all, attn, pooled, ga, gb, wba, wpm, ps, wbp, wout, gffn, wr, br, ltri):
    nt, d = x_all.shape
    row = lambda width: pl.BlockSpec((TM, width), lambda i: (i, 0))
    consts = [wba, wpm, ps, wbp, wout, gffn, wr, br, ltri]
    return pl.pallas_call(
        _merge_body,
        grid=(nt // TM,),
        in_specs=[row(d), row(attn.shape[1]), row(pooled.shape[1]), row(d), row(d)] + [_full(c.shape) for c in consts],
        out_specs=[row(d), row(d // 2), row(LANES), row(LANES), row(LANES), _full((SUBLANES, LANES))],
        out_shape=[jax.ShapeDtypeStruct((nt, d), F32), jax.ShapeDtypeStruct((nt, d // 2), U32),
                   jax.ShapeDtypeStruct((nt, LANES), I32), jax.ShapeDtypeStruct((nt, LANES), F32),
                   jax.ShapeDtypeStruct((nt, LANES), I32), jax.ShapeDtypeStruct((SUBLANES, LANES), F32)],
        scratch_shapes=[pltpu.VMEM((SUBLANES, LANES), F32)],
        compiler_params=pltpu.CompilerParams(dimension_semantics=("arbitrary",), vmem_limit_bytes=VMEM_LIMIT),
        name="merge_router",
    )(x_all, attn, pooled, ga, gb, *consts)


def _dispatch_body(dest_ref, gend_ref, h2p_ref, xs_hbm, zero_scr, sem, zsem):
    step = pl.program_id(0)

    @pl.when(step == 0)
    def _():
        zero_scr[...] = jnp.zeros_like(zero_scr)

        def fill(e):
            return pltpu.make_async_copy(zero_scr, xs_hbm.at[pl.ds(pl.multiple_of(gend_ref[e] - TS, TS), TS)], zsem)

        def start(e, _):
            @pl.when(gend_ref[e] > gend_ref[e + N_EXPERTS])
            def _():
                fill(e).start()
            return 0
        lax.fori_loop(0, N_EXPERTS, start, 0)

        def wait(e, _):
            @pl.when(gend_ref[e] > gend_ref[e + N_EXPERTS])
            def _():
                fill(e).wait()
            return 0
        lax.fori_loop(0, N_EXPERTS, wait, 0)

        def tail(j):
            return pltpu.make_async_copy(zero_scr, xs_hbm.at[pl.ds(pl.multiple_of(j * TS, TS), TS)], zsem)

        first_unused = gend_ref[N_EXPERTS - 1] // TS
        n_tiles = xs_hbm.shape[0] // TS

        def tail_start(j, _):
            tail(j).start()
            return 0
        lax.fori_loop(first_unused, n_tiles, tail_start, 0)

        def tail_wait(j, _):
            tail(j).wait()
            return 0
        lax.fori_loop(first_unused, n_tiles, tail_wait, 0)

    t0 = step * TM

    def row_copy(r, k):
        d = dest_ref[(t0 + r) * TOP_EXPERTS + k]
        return pltpu.make_async_copy(h2p_ref.at[pl.ds(r, 1)], xs_hbm.at[pl.ds(d, 1)], sem)

    def start(r, _):
        for k in range(TOP_EXPERTS):
            row_copy(r, k).start()
        return 0
    lax.fori_loop(0, TM, start, 0)

    def wait(r, _):
        for k in range(TOP_EXPERTS):
            row_copy(r, k).wait()
        return 0
    lax.fori_loop(0, TM, wait, 0)


def _dispatch(dest_flat, gend, h2p, *, n_slots):
    nt, hw = h2p.shape
    grid_spec = pltpu.PrefetchScalarGridSpec(
        num_scalar_prefetch=2,
        grid=(nt // TM,),
        in_specs=[pl.BlockSpec((TM, hw), lambda i, d, g: (i, 0))],
        out_specs=pl.BlockSpec(memory_space=pl.ANY),
        scratch_shapes=[pltpu.VMEM((TS, hw), U32), pltpu.SemaphoreType.DMA, pltpu.SemaphoreType.DMA],
    )
    return pl.pallas_call(
        _dispatch_body,
        grid_spec=grid_spec,
        out_shape=jax.ShapeDtypeStruct((n_slots, hw), U32),
        compiler_params=pltpu.CompilerParams(dimension_semantics=("arbitrary",)),
        name="dispatch",
    )(dest_flat, gend, h2p)


def _moe_body(te_ref, nu_ref, xs_ref, wgu_ref, bgu_ref, wdn_ref, bdn_ref, ys_o, wgu_b, wdn_b):
    j = pl.program_id(0)
    prev = te_ref[jnp.maximum(j - 1, 0)]
    live = j < nu_ref[0]

    @pl.when(jnp.logical_and(live, jnp.logical_or(j == 0, te_ref[j] != prev)))
    def _():
        wgu_b[...] = wgu_ref[0, 0].astype(BF16)
        wdn_b[...] = wdn_ref[0, 0].astype(BF16)

    @pl.when(live)
    def _():
        words = xs_ref[...]
        x_lo = lax.bitcast_convert_type(lax.shift_left(words, jnp.uint32(16)), F32).astype(BF16)
        x_hi = lax.bitcast_convert_type(words & jnp.uint32(0xFFFF0000), F32).astype(BF16)
        half = words.shape[1]
        gu = _dot(x_lo, wgu_b[0:half, :]) + _dot(x_hi, wgu_b[half:, :]) + bgu_ref[0, 0]
        de = gu.shape[1] // 2
        gate = jnp.minimum(gu[:, 0:de], SWIGLU_LIMIT)
        up = jnp.clip(gu[:, de:], -SWIGLU_LIMIT, SWIGLU_LIMIT)
        act = (up + 1.0) * gate * jax.nn.sigmoid(SWIGLU_ALPHA * gate)
        ys_o[...] = _dot(act.astype(BF16), wdn_b[...]) + bdn_ref[0, 0]

    @pl.when(jnp.logical_not(live))
    def _():
        ys_o[...] = jnp.zeros_like(ys_o)


def _moe(tile_expert, n_used, xs, wgu, bgu, wdn, bdn, *, layer):
    n_slots, hw = xs.shape
    _, ne, d, de2 = wgu.shape
    last = lambda j, nu: jnp.minimum(j, nu[0] - 1)
    per_expert = lambda rows, cols: pl.BlockSpec((1, 1, rows, cols), lambda j, te, nu: (layer, te[j], 0, 0))
    grid_spec = pltpu.PrefetchScalarGridSpec(
        num_scalar_prefetch=2,
        grid=(n_slots // TS,),
        in_specs=[pl.BlockSpec((TS, hw), lambda j, te, nu: (last(j, nu), 0)),
                  per_expert(d, de2), per_expert(1, de2), per_expert(de2 // 2, d), per_expert(1, d)],
        out_specs=pl.BlockSpec((TS, d), lambda j, te, nu: (j, 0)),
        scratch_shapes=[pltpu.VMEM((d, de2), BF16), pltpu.VMEM((de2 // 2, d), BF16)],
    )
    return pl.pallas_call(
        _moe_body,
        grid_spec=grid_spec,
        out_shape=jax.ShapeDtypeStruct((n_slots, d), F32),
        compiler_params=pltpu.CompilerParams(dimension_semantics=("arbitrary",), vmem_limit_bytes=VMEM_LIMIT),
        name="moe_experts",
    )(tile_expert, n_used, xs, wgu, bgu[:, :, None, :], wdn, bdn[:, :, None, :])


def _combine_body(dest_ref, x1_ref, gv_ref, gfin_ref, ys_hbm, y_o, ybuf, sem):
    t0 = pl.program_id(0) * TM

    def row_copy(r, k):
        d = dest_ref[(t0 + r) * TOP_EXPERTS + k]
        return pltpu.make_async_copy(ys_hbm.at[pl.ds(d, 1)], ybuf.at[k, pl.ds(r, 1)], sem)

    def start(r, _):
        for k in range(TOP_EXPERTS):
            row_copy(r, k).start()
        return 0
    lax.fori_loop(0, TM, start, 0)

    def wait(r, _):
        for k in range(TOP_EXPERTS):
            row_copy(r, k).wait()
        return 0
    lax.fori_loop(0, TM, wait, 0)

    gv = gv_ref[...]
    y = jnp.zeros(x1_ref.shape, F32)
    for k in range(TOP_EXPERTS):
        y = y + ybuf[k] * gv[:, k:k + 1]
    y_o[...] = _rms(x1_ref[...] + y, gfin_ref[...])


def _combine(dest_flat, x1, gv, gfin, ys):
    nt, d = x1.shape
    grid_spec = pltpu.PrefetchScalarGridSpec(
        num_scalar_prefetch=1,
        grid=(nt // TM,),
        in_specs=[pl.BlockSpec((TM, d), lambda i, dref: (i, 0)),
                  pl.BlockSpec((TM, LANES), lambda i, dref: (i, 0)),
                  pl.BlockSpec((1, d), lambda i, dref: (0, 0)),
                  pl.BlockSpec(memory_space=pl.ANY)],
        out_specs=pl.BlockSpec((TM, d), lambda i, dref: (i, 0)),
        scratch_shapes=[pltpu.VMEM((TOP_EXPERTS, TM, d), F32), pltpu.SemaphoreType.DMA],
    )
    return pl.pallas_call(
        _combine_body,
        grid_spec=grid_spec,
        out_shape=jax.ShapeDtypeStruct((nt, d), F32),
        compiler_params=pltpu.CompilerParams(dimension_semantics=("arbitrary",)),
        name="combine_norm",
    )(dest_flat, x1, gv, gfin, ys)


def _tri_constants():
    r = lax.broadcasted_iota(I32, (TQ, TQ), 0)
    c = lax.broadcasted_iota(I32, (TQ, TQ), 1)
    incl = (r <= c).astype(BF16)
    u2 = jnp.concatenate([incl, jnp.ones((TQ, LANES), BF16)], axis=1)
    ltri = (c < r).astype(BF16)
    return u2, ltri


def kernel(x_prompt, x_sample, cache_k, cache_v, cache_kidx, state_pool, page_table, meta_tokens, rel_bias, g_mix, w_in, w_pool_mix, pool_scale, w_br_attn, w_br_pool, w_out, g_ffn, w_router, b_router, w_gate_up, b_gate_up, w_down, b_down, g_final):
    nb, seq, d = x_prompt.shape
    db, dec_seq, _ = x_sample.shape
    assert dec_seq == 1, "one new token per sample"
    depth = w_in.shape[0]
    assert depth == 1, "single-layer stack: the combine kernel applies the final norm"
    n_pages = page_table.shape[1]
    past = n_pages * PAGE_SIZE
    pw = state_pool.shape[-1]
    l_seq = seq + N_META
    lp = _round_up(l_seq, TQ)
    n_prompt = nb * lp
    n_sample = _round_up(db, TM)
    nt = n_prompt + n_sample
    ksel_p = min(TOPK_MAX, seq // 4)
    ksel_s = min(TOPK_MAX, (past + dec_seq) // 4)
    n_tiles = (TOP_EXPERTS * nt) // TS + N_EXPERTS
    n_slots = n_tiles * TS

    meta = jnp.broadcast_to(meta_tokens[None].astype(x_prompt.dtype), (nb, N_META, d))
    xp = jnp.concatenate([meta, x_prompt, jnp.zeros((nb, lp - l_seq, d), x_prompt.dtype)], axis=1)
    xs_rows = jnp.concatenate([x_sample[:, 0, :], jnp.zeros((n_sample - db, d), x_sample.dtype)], axis=0)
    x_all = jnp.concatenate([xp.reshape(n_prompt, d), xs_rows], axis=0)

    u2, ltri = _tri_constants()
    kpos = lax.broadcasted_iota(I32, (_round_up(past + LANES, TQ), LANES), 0)
    kcol = lax.broadcasted_iota(I32, kpos.shape, 1)
    kvals = jnp.where(kcol == 0, kpos // LANES, jnp.where(kcol == 1, kpos % LANES, 0))
    kvals = kvals.astype(BF16).reshape(-1, TQ, LANES)
    rb_pad = jnp.pad(rel_bias.astype(F32), ((0, 0), (0, LANES - N_HEADS)))
    page_flat = page_table.reshape(-1).astype(I32)

    outs = {name: [] for name in ("k_p", "v_p", "ki_p", "pool_p", "k_s", "v_s", "ki_s", "pool_s")}
    for l in range(depth):
        wl = w_in[l]
        aw = ATTN_WIDTH
        o_ki = 4 * aw
        o_wi = o_ki + IDX_DIM
        o_p = o_wi + N_IDX_HEADS
        o_ga = o_p + pw
        o_gb = o_ga + d
        wa = wl[:, 0:o_ki].astype(BF16)
        wki = wl[:, o_ki:o_wi]
        wki2 = jnp.concatenate([wki, wki], axis=1).astype(BF16)
        wwi = jnp.pad(wl[:, o_wi:o_p], ((0, 0), (0, LANES - N_IDX_HEADS))).astype(BF16)
        wp = wl[:, o_p:o_ga].astype(BF16)
        wga = wl[:, o_ga:o_gb].astype(BF16)
        wgb = wl[:, o_gb:o_gb + d].astype(BF16)

        (qs, kf, vf, kb, vb, qib, ki2b, kif, wif, pf, ga, gb) = _inproj(
            x_all, g_mix[l][None, :], wa, wki2, wwi, wp, wga, wgb)

        attn_p = _attn_prompt(qs, qib, wif, kb, vb, ki2b, u2, rel_bias.astype(F32), nb=nb, lp=lp, ksel=ksel_p)
        pooled_p = _pool_prompt(pf, n_rows=n_prompt, lp=lp)

        sl = slice(n_prompt, n_prompt + db)
        idx3 = _sample_index(page_flat, qib[sl].reshape(db, N_IDX_HEADS, IDX_DIM), kif[sl, 0:IDX_DIM][:, None, :],
                             wif[sl, 0:N_IDX_HEADS][:, :, None], u2, kvals, cache_kidx,
                             layer=l, n_pages=n_pages, ksel=ksel_s)
        idx = idx3[:, :, 0]
        cached = jnp.minimum(idx, past - 1)
        rows = jnp.take_along_axis(page_table.astype(I32), cached // PAGE_SIZE, axis=1) * PAGE_SIZE + cached % PAGE_SIZE
        heads = lambda a: a[sl].reshape(db, N_HEADS, HEAD_DIM)
        attn_s, pooled_s = _sample_attn(
            rows.reshape(-1), heads(qs), heads(kf), heads(vf), idx3, rb_pad, state_pool[l], pf[sl][:, None, :],
            cache_k, cache_v, layer=l, past=past, ksel=ksel_s)

        pad_s = lambda a: jnp.concatenate([a, jnp.zeros((n_sample - db, a.shape[1]), a.dtype)], axis=0)
        attn_all = jnp.concatenate([attn_p, pad_s(attn_s.reshape(db, ATTN_WIDTH))], axis=0)
        pooled_all = jnp.concatenate([pooled_p, pad_s(pooled_s[:, 0, :])], axis=0)

        wr = jnp.pad(w_router[l], ((0, 0), (0, LANES - N_EXPERTS))).astype(BF16)
        br = jnp.concatenate([b_router[l].astype(F32), jnp.full((LANES - N_EXPERTS,), NEG, F32)])[None, :]
        x1, h2p, ev, gv, rk, cnt = _merge(
            x_all, attn_all, pooled_all, ga, gb, w_br_attn[l].astype(BF16), w_pool_mix[l].astype(BF16),
            pool_scale[l][None, :], w_br_pool[l].astype(BF16), w_out[l].astype(BF16), g_ffn[l][None, :], wr, br, ltri)

        counts = cnt[0, 0:N_EXPERTS].astype(I32)
        padded = (counts + TS - 1) // TS * TS
        gend = jnp.cumsum(padded)
        gstart = gend - padded
        experts = jnp.arange(N_EXPERTS, dtype=I32)
        ev4 = ev[:, 0:TOP_EXPERTS]
        dest = (jnp.sum(jnp.where(ev4[:, :, None] == experts, gstart, 0), axis=2) + rk[:, 0:TOP_EXPERTS]).reshape(-1)
        tile_first = jnp.arange(n_tiles, dtype=I32)[:, None] * TS
        tile_expert = jnp.minimum(jnp.sum((gend[None, :] <= tile_first).astype(I32), axis=1), N_EXPERTS - 1)
        n_used = (gend[-1] // TS).astype(I32)[None]

        xs_sorted = _dispatch(dest, jnp.concatenate([gend, gstart + counts]).astype(I32), h2p, n_slots=n_slots)
        ys = _moe(tile_expert, n_used, xs_sorted, w_gate_up, b_gate_up, w_down, b_down, layer=l)
        y_all = _combine(dest, x1, gv, g_final[None, :], ys)

        kp = kf[:n_prompt].reshape(nb, lp, N_HEADS, HEAD_DIM)[:, :l_seq]
        vp = vf[:n_prompt].reshape(nb, lp, N_HEADS, HEAD_DIM)[:, :l_seq]
        kip = kif[:n_prompt, 0:IDX_DIM].reshape(nb, lp, IDX_DIM)[:, :l_seq]
        pp = pf[:n_prompt].reshape(nb, lp, pw)
        outs["k_p"].append(kp)
        outs["v_p"].append(vp)
        outs["ki_p"].append(kip)
        outs["pool_p"].append(pp[:, l_seq - POOL_STATE:l_seq])
        outs["k_s"].append(kf[sl].reshape(db, 1, N_HEADS, HEAD_DIM))
        outs["v_s"].append(vf[sl].reshape(db, 1, N_HEADS, HEAD_DIM))
        outs["ki_s"].append(kif[sl, 0:IDX_DIM].reshape(db, 1, IDX_DIM))
        outs["pool_s"].append(jnp.concatenate([state_pool[l][:, 1:], pf[sl][:, None, :]], axis=1))

    y_prompt = y_all[:n_prompt].reshape(nb, lp, d)[:, N_META:l_seq]
    y_sample = y_all[n_prompt:n_prompt + db].reshape(db, 1, d)
    st = lambda name: jnp.stack(outs[name])
    return (y_prompt, y_sample, st("k_p"), st("v_p"), st("ki_p"), st("pool_p"),
            st("k_s"), st("v_s"), st("ki_s"), st("pool_s"))
```

```python
import functools
import math

import jax
import jax.numpy as jnp
from jax import lax
from jax.experimental import pallas as pl
from jax.experimental.pallas import tpu as pltpu

F32 = jnp.float32
BF16 = jnp.bfloat16
I32 = jnp.int32
U32 = jnp.uint32

N_META = 16
N_HEADS = 8
HEAD_DIM = 64
ATTN_WIDTH = N_HEADS * HEAD_DIM
N_IDX_HEADS = 8
IDX_DIM = 64
TOPK_MAX = 256
N_BUCKETS = 32
MAX_DISTANCE = 128
POOL_WINDOWS = (2, 4, 8, 16)
POOL_GROUPS = 4
POOL_STATE = 15
N_EXPERTS = 32
TOP_EXPERTS = 4
SWIGLU_LIMIT = 7.0
SWIGLU_ALPHA = 1.702
PAGE_SIZE = 128
EPS = 1e-6

LANES = 128
SUBLANES = 8
MXU_DIM = 256
TQ = MXU_DIM
TM = MXU_DIM
TS = MXU_DIM
VMEM_LIMIT = 56 * 1024 * 1024

NEG = -1e30
INT_MIN = -(2 ** 31)

_MAX_EXACT = N_BUCKETS // 2
_BUCKET_THRESHOLDS = tuple(
    math.ceil(_MAX_EXACT * (MAX_DISTANCE / _MAX_EXACT) ** (j / (N_BUCKETS - _MAX_EXACT)))
    for j in range(1, N_BUCKETS - _MAX_EXACT))


def _round_up(a, m):
    return (a + m - 1) // m * m


def _rms(x, g):
    return x * lax.rsqrt(jnp.mean(x * x, axis=-1, keepdims=True) + EPS) * g


def _dot(a, b):
    return jnp.dot(a, b, preferred_element_type=F32)


def _dot_nt(a, b):
    return lax.dot_general(a, b, (((1,), (1,)), ((), ())), preferred_element_type=F32)


def _bucket(dist):
    large = jnp.full(dist.shape, _MAX_EXACT, I32)
    for thr in _BUCKET_THRESHOLDS:
        large = large + jnp.where(dist >= thr, 1, 0)
    return jnp.where(dist < _MAX_EXACT, dist, large)


def _sort_key(s):
    s = jnp.where(s == 0.0, 0.0, s)
    bits = lax.bitcast_convert_type(s, I32)
    return jnp.where(bits >= 0, bits, bits ^ jnp.int32(0x7FFFFFFF))


def _full(shape):
    return pl.BlockSpec(shape, lambda *_: (0,) * len(shape))


def _inproj_body(x_ref, g_ref, wa_ref, wki_ref, wwi_ref, wp_ref, wga_ref, wgb_ref,
                 qs_o, kf_o, vf_o, kb_o, vb_o, qib_o, ki2b_o, kif_o, wif_o, pf_o, ga_o, gb_o):
    h = _rms(x_ref[...], g_ref[...]).astype(BF16)
    za = _dot(h, wa_ref[...])
    w = ATTN_WIDTH
    qs_o[...] = (za[:, 0:w] * (HEAD_DIM ** -0.5)).astype(BF16)
    k = za[:, w:2 * w]
    v = za[:, 2 * w:3 * w]
    kf_o[...] = k
    vf_o[...] = v
    kb_o[...] = k.astype(BF16)
    vb_o[...] = v.astype(BF16)
    qib_o[...] = za[:, 3 * w:4 * w].astype(BF16)
    ki2 = _dot(h, wki_ref[...])
    kif_o[...] = ki2
    ki2b_o[...] = ki2.astype(BF16)
    wif_o[...] = _dot(h, wwi_ref[...])
    pf_o[...] = _dot(h, wp_ref[...])
    ga_o[...] = _dot(h, wga_ref[...])
    gb_o[...] = _dot(h, wgb_ref[...])


def _inproj(x_all, g, wa, wki2, wwi, wp, wga, wgb):
    nt, d = x_all.shape
    pw = wp.shape[1]
    row = lambda width: pl.BlockSpec((TM, width), lambda i: (i, 0))
    outs = [
        (ATTN_WIDTH, BF16), (ATTN_WIDTH, F32), (ATTN_WIDTH, F32), (ATTN_WIDTH, BF16), (ATTN_WIDTH, BF16),
        (ATTN_WIDTH, BF16), (LANES, BF16), (LANES, F32), (LANES, F32), (pw, F32), (d, F32), (d, F32)]
    return pl.pallas_call(
        _inproj_body,
        grid=(nt // TM,),
        in_specs=[row(d), _full((1, d)), _full(wa.shape), _full(wki2.shape), _full(wwi.shape),
                  _full(wp.shape), _full(wga.shape), _full(wgb.shape)],
        out_specs=[row(wd) for wd, _ in outs],
        out_shape=[jax.ShapeDtypeStruct((nt, wd), dt) for wd, dt in outs],
        compiler_params=pltpu.CompilerParams(dimension_semantics=("arbitrary",), vmem_limit_bytes=VMEM_LIMIT),
        name="inproj",
    )(x_all, g, wa, wki2, wwi, wp, wga, wgb)


def _attn_prompt_body(qs_ref, qib_ref, wif_ref, kb_ref, vb_ref, ki2b_ref, u2_ref, rb_ref,
                      o_ref, qis_scr, wb_scr, key_scr, ntab_scr, qz_scr, m_scr, acc_scr, sum_scr, *, ksel):
    b = pl.program_id(0)
    i = pl.program_id(1)
    nchunk = i + 1
    n_far = jnp.maximum(i - 1, 0)
    hr = TQ // 2

    @pl.when((b == 0) & (i == 0))
    def _():
        def slab_rows(s, _):
            r0 = pl.multiple_of(s * SUBLANES, SUBLANES)
            r = r0 + lax.broadcasted_iota(I32, (SUBLANES, 2 * TQ), 0)
            x = lax.broadcasted_iota(I32, (SUBLANES, 2 * TQ), 1)
            bucket = _bucket(jnp.maximum(r + TQ - x, 0))
            accs = [jnp.zeros((SUBLANES, 2 * TQ), F32) for _ in range(N_HEADS)]
            for j in range(N_BUCKETS - 1):
                m = bucket == j
                accs = [jnp.where(m, rb_ref[j, h] - rb_ref[N_BUCKETS - 1, h], accs[h]) for h in range(N_HEADS)]
            for h in range(N_HEADS):
                ntab_scr[h, 0, pl.ds(r0, SUBLANES), :] = accs[h][:, 0:TQ]
                ntab_scr[h, 1, pl.ds(r0, SUBLANES), :] = accs[h][:, TQ:2 * TQ]
            return 0
        lax.fori_loop(0, TQ // SUBLANES, slab_rows, 0)

    lane = lax.broadcasted_iota(I32, (TQ, LANES), 1)
    lo_half = lane < HEAD_DIM

    def two(x):
        return jnp.concatenate([x, x], axis=1)

    for h in range(N_HEADS):
        cols = slice((h // 2) * LANES, (h // 2 + 1) * LANES)
        keep = lo_half if h % 2 == 0 else jnp.logical_not(lo_half)
        qis_scr[h] = jnp.where(keep, qib_ref[:, cols].astype(F32), 0.0).astype(BF16)
        qz_scr[h] = jnp.where(keep, qs_ref[:, cols].astype(F32), 0.0).astype(BF16)
        wb_scr[h] = jnp.broadcast_to(wif_ref[:, h:h + 1], (TQ, LANES))
        m_scr[h] = jnp.full((TQ, LANES), NEG, F32)
        acc_scr[h] = jnp.zeros((TQ, LANES), F32)
        sum_scr[h] = jnp.zeros((TQ, LANES), F32)

    dmat = lax.broadcasted_iota(I32, (TQ, TQ), 1) - lax.broadcasted_iota(I32, (TQ, TQ), 0)

    def score_chunk(c, _):
        kc = ki2b_ref[pl.ds(pl.multiple_of(c * TQ, TQ), TQ), :]
        s = jnp.zeros((TQ, TQ), F32)
        for h in range(N_IDX_HEADS):
            s = s + two(wb_scr[h]) * jnp.maximum(_dot_nt(qis_scr[h], kc), 0.0)
        key_scr[c] = jnp.where(dmat <= (i - c) * TQ, _sort_key(s), INT_MIN)
        return 0
    lax.fori_loop(0, nchunk, score_chunk, 0)

    def count(r0, pred):
        def body(c, acc):
            hit = jnp.where(pred(key_scr[c, r0:r0 + hr, :]), 1.0, 0.0)
            return acc + (hit[:, 0:LANES] + hit[:, LANES:2 * LANES])
        acc = lax.fori_loop(0, nchunk, body, jnp.zeros((hr, LANES), F32))
        return jnp.broadcast_to(jnp.sum(acc, axis=1, keepdims=True), (hr, LANES))

    ts, needs = [], []
    for r0 in (0, hr):
        def bit_step(step, t, r0=r0):
            cand = t + lax.shift_left(jnp.int32(1), 31 - step)
            cand2 = two(cand)
            return jnp.where(count(r0, lambda k: k >= cand2) >= ksel, cand, t)
        t = lax.fori_loop(0, 32, bit_step, jnp.full((hr, LANES), INT_MIN, I32))
        th2 = two(t)
        n_gt = count(r0, lambda k: k > th2)
        ts.append(t)
        needs.append(jnp.where(t == INT_MIN, 0.0, ksel - n_gt))
    t2 = two(jnp.concatenate(ts, axis=0))
    need2 = two(jnp.concatenate(needs, axis=0))

    def mask_chunk(c, carry):
        kc = key_scr[c]
        eq = kc == t2
        pre = _dot(jnp.where(eq, 1.0, 0.0).astype(BF16), u2_ref[...])
        prefix = pre[:, 0:TQ] + two(carry)
        tie_ok = jnp.where(eq, prefix, 3e38) <= need2
        madd = jnp.where(kc > t2, 0.0, jnp.where(tie_ok, 0.0, NEG))
        key_scr[c] = lax.bitcast_convert_type(madd, I32)
        return carry + pre[:, TQ:TQ + LANES]
    lax.fori_loop(0, nchunk, mask_chunk, jnp.zeros((TQ, LANES), F32))

    def chunk_logits(c, slab):
        rows = pl.ds(pl.multiple_of(c * TQ, TQ), TQ)
        madd = lax.bitcast_convert_type(key_scr[c], F32)
        out = []
        for hp in range(N_HEADS // 2):
            kc = kb_ref[rows, hp * LANES:(hp + 1) * LANES]
            for h in (2 * hp, 2 * hp + 1):
                l = _dot_nt(qz_scr[h], kc) + madd
                out.append(l if slab is None else l + ntab_scr[h, slab])
        return rows, out

    def row_max(c, slab):
        _, ls = chunk_logits(c, slab)
        for h, l in enumerate(ls):
            m_scr[h] = jnp.maximum(m_scr[h], jnp.maximum(l[:, 0:LANES], l[:, LANES:2 * LANES]))

    def accumulate(c, slab):
        rows, ls = chunk_logits(c, slab)
        for h, l in enumerate(ls):
            p = jnp.exp(l - two(m_scr[h]))
            vc = vb_ref[rows, (h // 2) * LANES:(h // 2 + 1) * LANES]
            acc_scr[h] = acc_scr[h] + _dot(p.astype(BF16), vc)
            sum_scr[h] = sum_scr[h] + (p[:, 0:LANES] + p[:, LANES:2 * LANES])

    def sweep(step):
        def far(c, _):
            step(c, None)
            return 0
        lax.fori_loop(0, n_far, far, 0)

        @pl.when(i >= 1)
        def _():
            step(i - 1, 0)
        step(i, 1)

    sweep(row_max)
    for h in range(N_HEADS):
        m_scr[h] = jnp.broadcast_to(jnp.max(m_scr[h], axis=1, keepdims=True), (TQ, LANES))
    sweep(accumulate)
    for hp in range(N_HEADS // 2):
        outs = [acc_scr[h] / jnp.sum(sum_scr[h], axis=1, keepdims=True) for h in (2 * hp, 2 * hp + 1)]
        o_ref[:, hp * LANES:(hp + 1) * LANES] = jnp.where(lo_half, outs[0], outs[1]).astype(BF16)


def _attn_prompt(qs, qib, wif, kb, vb, ki2b, u2, rel_bias, *, nb, lp, ksel):
    nq = lp // TQ
    rowq = lambda width: pl.BlockSpec((TQ, width), lambda b, i: (b * nq + i, 0))
    seq = lambda width: pl.BlockSpec((lp, width), lambda b, i: (b, 0))
    per_head = lambda dt: pltpu.VMEM((N_HEADS, TQ, LANES), dt)
    return pl.pallas_call(
        functools.partial(_attn_prompt_body, ksel=float(ksel)),
        grid=(nb, nq),
        in_specs=[rowq(ATTN_WIDTH), rowq(ATTN_WIDTH), rowq(LANES), seq(ATTN_WIDTH), seq(ATTN_WIDTH), seq(LANES),
                  _full(u2.shape), pl.BlockSpec(memory_space=pltpu.SMEM)],
        out_specs=rowq(ATTN_WIDTH),
        out_shape=jax.ShapeDtypeStruct((nb * lp, ATTN_WIDTH), BF16),
        scratch_shapes=[
            per_head(BF16),
            per_head(F32),
            pltpu.VMEM((nq, TQ, TQ), I32),
            pltpu.VMEM((N_HEADS, 2, TQ, TQ), F32),
            per_head(BF16),
            per_head(F32), per_head(F32), per_head(F32),
        ],
        compiler_params=pltpu.CompilerParams(dimension_semantics=("arbitrary", "arbitrary"),
                                             vmem_limit_bytes=VMEM_LIMIT),
        name="attn_prompt",
    )(qs, qib, wif, kb, vb, ki2b, u2, rel_bias)


def _pool_prompt_body(p_ref, halo_ref, o_ref, ext_scr, *, tiles_per_seq):
    j = pl.program_id(0) % tiles_per_seq
    halo = 2 * SUBLANES
    p = p_ref[...]
    ext_scr[0:halo, :] = jnp.where(j == 0, 0.0, halo_ref[...])
    ext_scr[halo:halo + TM, :] = p
    pos = j * TM + lax.broadcasted_iota(I32, (TM, 1), 0)
    gw = p.shape[1] // POOL_GROUPS
    for g, w in enumerate(POOL_WINDOWS):
        cols = slice(g * gw, (g + 1) * gw)
        s = p[:, cols]
        for back in range(1, w):
            s = s + ext_scr[halo - back:halo - back + TM, cols]
        cnt = jnp.minimum(pos + 1, w).astype(F32)
        o_ref[:, cols] = (s / cnt - p[:, cols]).astype(BF16)


def _pool_prompt(pf, *, n_rows, lp):
    pw = pf.shape[1]
    halo = 2 * SUBLANES
    return pl.pallas_call(
        functools.partial(_pool_prompt_body, tiles_per_seq=lp // TM),
        grid=(n_rows // TM,),
        in_specs=[pl.BlockSpec((TM, pw), lambda i: (i, 0)),
                  pl.BlockSpec((halo, pw), lambda i: (jnp.maximum(i * (TM // halo) - 1, 0), 0))],
        out_specs=pl.BlockSpec((TM, pw), lambda i: (i, 0)),
        out_shape=jax.ShapeDtypeStruct((n_rows, pw), BF16),
        scratch_shapes=[pltpu.VMEM((halo + TM, pw), F32)],
        compiler_params=pltpu.CompilerParams(dimension_semantics=("arbitrary",)),
        name="pool_prompt",
    )(pf, pf)


def _sample_keys_body(pt_ref, qi_ref, kinew_ref, wi_ref, cache_hbm, key_o, kibuf, sem, *, layer, n_pages):
    s = pl.program_id(0)
    past = n_pages * PAGE_SIZE
    seg_w = past // SUBLANES

    copies = [pltpu.make_async_copy(cache_hbm.at[layer, pt_ref[s * n_pages + pg]],
                                    kibuf.at[:, pg * PAGE_SIZE:(pg + 1) * PAGE_SIZE], sem) for pg in range(n_pages)]
    for cp in copies:
        cp.start()
    for cp in copies:
        cp.wait()

    qi = qi_ref[0]
    wcol = wi_ref[0]
    d = _dot(qi, kibuf[...].astype(BF16))
    sc = jnp.sum(wcol * jnp.maximum(d, 0.0), axis=0, keepdims=True)
    knew = kinew_ref[0].astype(BF16).astype(F32)
    dn = jnp.sum(qi.astype(F32) * knew, axis=1, keepdims=True)
    sn = jnp.sum(wcol * jnp.maximum(dn, 0.0), axis=0, keepdims=True)
    key = _sort_key(sc)
    lane = lax.broadcasted_iota(I32, (1, LANES), 1)
    new_tail = jnp.where(lane == 0, _sort_key(jnp.broadcast_to(sn, (1, LANES))), INT_MIN)
    for g in range(SUBLANES):
        key_o[0, g:g + 1, 0:seg_w] = key[:, g * seg_w:(g + 1) * seg_w]
        key_o[0, g:g + 1, seg_w:seg_w + LANES] = new_tail if g == 0 else jnp.full((1, LANES), INT_MIN, I32)


def _sample_keys(page_table_flat, qi3, kinew3, wi3, cache_ki_t, *, layer, n_pages):
    db = qi3.shape[0]
    past = n_pages * PAGE_SIZE
    kw = past // SUBLANES + LANES
    grid_spec = pltpu.PrefetchScalarGridSpec(
        num_scalar_prefetch=1,
        grid=(db,),
        in_specs=[pl.BlockSpec((1, N_IDX_HEADS, IDX_DIM), lambda s, pt: (s, 0, 0)),
                  pl.BlockSpec((1, 1, IDX_DIM), lambda s, pt: (s, 0, 0)),
                  pl.BlockSpec((1, N_IDX_HEADS, 1), lambda s, pt: (s, 0, 0)),
                  pl.BlockSpec(memory_space=pl.ANY)],
        out_specs=pl.BlockSpec((1, SUBLANES, kw), lambda s, pt: (s, 0, 0)),
        scratch_shapes=[pltpu.VMEM((IDX_DIM, past), F32), pltpu.SemaphoreType.DMA],
    )
    return pl.pallas_call(
        functools.partial(_sample_keys_body, layer=layer, n_pages=n_pages),
        grid_spec=grid_spec,
        out_shape=jax.ShapeDtypeStruct((db, SUBLANES, kw), I32),
        compiler_params=pltpu.CompilerParams(dimension_semantics=("arbitrary",)),
        name="sample_keys",
    )(page_table_flat, qi3, kinew3, wi3, cache_ki_t)


def _sample_select_body(key_ref, sel_o, *, past, ksel):
    keys = key_ref[...]
    kw = keys.shape[2]
    seg_w = kw - LANES
    row = lax.broadcasted_iota(I32, (SUBLANES, kw), 0)
    lane = lax.broadcasted_iota(I32, (SUBLANES, kw), 1)
    pos = jnp.where(lane < seg_w, row * seg_w + lane, past + (lane - seg_w) + row * LANES)[None]

    def cnt(pred):
        x = jnp.where(pred, 1.0, 0.0)
        return jnp.sum(jnp.sum(x, axis=2, keepdims=True), axis=1, keepdims=True)

    def bit_step(step, t):
        cand = t + lax.shift_left(jnp.int32(1), 31 - step)
        return jnp.where(cnt(keys >= cand) >= ksel, cand, t)
    t = lax.fori_loop(0, 32, bit_step, jnp.full((keys.shape[0], 1, 1), INT_MIN, I32))
    gt = keys > t
    eq = keys == t
    need = ksel - cnt(gt)

    nbits = (past + SUBLANES * LANES).bit_length()

    def cut_step(step, c):
        cand = c - lax.shift_left(jnp.int32(1), nbits - 1 - step)
        ok = cnt(jnp.logical_and(eq, pos <= cand)) >= need
        return jnp.where(ok, cand, c)
    cut = lax.fori_loop(0, nbits, cut_step, jnp.full((keys.shape[0], 1, 1), 2 ** nbits - 1, I32))
    sel = jnp.where(gt, 1.0, jnp.where(jnp.logical_and(eq, pos <= cut), 1.0, 0.0))
    sel_o[...] = jnp.where(pos <= past, sel, 0.0)


def _sample_select(keys, *, past, ksel):
    return pl.pallas_call(
        functools.partial(_sample_select_body, past=past, ksel=float(ksel)),
        out_shape=jax.ShapeDtypeStruct(keys.shape, F32),
        name="sample_select",
    )(keys)


def _sample_attend_body(pt_ref, sel_ref, q_ref, knew_ref, vnew_ref, rb_ref, state_ref, pnew_ref, ck_hbm, cv_hbm,
                        attn_o, pool_o, kbuf, vbuf, qb_scr, lg_scr, acc_scr, ksem, vsem, *, layer, n_pages):
    s = pl.program_id(0)
    past = n_pages * PAGE_SIZE
    seg_w = past // SUBLANES
    pages_per_seg = n_pages // SUBLANES
    last_bias = [rb_ref[N_BUCKETS - 1, h] for h in range(N_HEADS)]

    def kcopy(pg):
        return pltpu.make_async_copy(ck_hbm.at[layer, pt_ref[s * n_pages + pg]], kbuf.at[pg], ksem.at[pg])

    def vcopy(pg):
        return pltpu.make_async_copy(cv_hbm.at[layer, pt_ref[s * n_pages + pg]], vbuf.at[pg], vsem.at[pg])

    def start(pg, _):
        kcopy(pg).start()
        vcopy(pg).start()
        return 0
    lax.fori_loop(0, n_pages, start, 0)

    for h in range(N_HEADS):
        qb_scr[h] = jnp.broadcast_to(q_ref[0, h], (HEAD_DIM, LANES))
        acc_scr[h] = jnp.zeros((HEAD_DIM, LANES), F32)

    def k_page(pg, _):
        kcopy(pg).wait()
        rows = [jnp.sum(kbuf[pg, h].astype(BF16).astype(F32) * qb_scr[h], axis=0, keepdims=True)
                for h in range(N_HEADS)]
        lg_scr[pg] = jnp.concatenate(rows, axis=0)
        return 0
    lax.fori_loop(0, n_pages, k_page, 0)

    lane = lax.broadcasted_iota(I32, (1, LANES), 1)
    bucket = _bucket(past - ((n_pages - 1) * PAGE_SIZE + lane))
    near_rows = []
    for h in range(N_HEADS):
        r = jnp.zeros((1, LANES), F32)
        for j in range(N_BUCKETS - 1):
            r = jnp.where(bucket == j, rb_ref[j, h] - last_bias[h], r)
        near_rows.append(r)
    near = jnp.concatenate(near_rows, axis=0)

    sel = sel_ref[0]
    m = jnp.full((N_HEADS, LANES), NEG, F32)
    for pg in range(n_pages):
        g, j = divmod(pg, pages_per_seg)
        keep = sel[g:g + 1, j * PAGE_SIZE:(j + 1) * PAGE_SIZE] > 0.0
        l = lg_scr[pg] + jnp.where(keep, 0.0, NEG)
        if pg == n_pages - 1:
            l = l + near
        lg_scr[pg] = l
        m = jnp.maximum(m, l)
    new_rows = []
    for h in range(N_HEADS):
        kn = knew_ref[0, h].astype(BF16).astype(F32)
        new_rows.append(jnp.sum(q_ref[0, h] * kn, axis=0, keepdims=True) + (rb_ref[0, h] - last_bias[h]))
    l_new = jnp.where(sel[0:1, seg_w:seg_w + 1] > 0.0, jnp.concatenate(new_rows, axis=0), NEG)
    m = jnp.maximum(jnp.max(m, axis=1, keepdims=True), l_new)
    ssum = jnp.zeros((N_HEADS, LANES), F32)
    for pg in range(n_pages):
        p = jnp.exp(lg_scr[pg] - m)
        lg_scr[pg] = p
        ssum = ssum + p
    p_new = jnp.exp(l_new - m)
    den = jnp.sum(ssum, axis=1, keepdims=True) + p_new
    for pg in range(n_pages):
        lg_scr[pg] = (lg_scr[pg] / den).astype(BF16).astype(F32)
    p_new = (p_new / den).astype(BF16).astype(F32)

    def v_page(pg, _):
        vcopy(pg).wait()
        p = lg_scr[pg]
        for h in range(N_HEADS):
            acc_scr[h] = acc_scr[h] + vbuf[pg, h].astype(BF16).astype(F32) * p[h:h + 1, :]
        return 0
    lax.fori_loop(0, n_pages, v_page, 0)
    for h in range(N_HEADS):
        vn = vnew_ref[0, h].astype(BF16).astype(F32)
        attn_o[0, h] = jnp.sum(acc_scr[h], axis=1, keepdims=True) + p_new[h:h + 1, :] * vn

    pnew = pnew_ref[0]
    st = state_ref[0]
    gw = pnew.shape[1] // POOL_GROUPS
    for g, w in enumerate(POOL_WINDOWS):
        cols = slice(g * gw, (g + 1) * gw)
        acc = pnew[:, cols]
        for back in range(1, w):
            acc = acc + st[POOL_STATE - back:POOL_STATE - back + 1, cols]
        pool_o[0, :, cols] = (acc / float(min(POOL_STATE + 1, w)) - pnew[:, cols]).astype(BF16)


def _sample_attend(page_table_flat, sel, q4, knew4, vnew4, rel_bias, state, pnew3, cache_k_t, cache_v_t, *, layer, n_pages):
    db = q4.shape[0]
    pw = pnew3.shape[2]
    per = lambda shape: pl.BlockSpec((1,) + shape, lambda s, pt: (s,) + (0,) * len(shape))
    col = (N_HEADS, HEAD_DIM, 1)
    page = (n_pages, N_HEADS, HEAD_DIM, LANES)
    grid_spec = pltpu.PrefetchScalarGridSpec(
        num_scalar_prefetch=1,
        grid=(db,),
        in_specs=[per(sel.shape[1:]), per(col), per(col), per(col), pl.BlockSpec(memory_space=pltpu.SMEM),
                  per((POOL_STATE, pw)), per((1, pw)),
                  pl.BlockSpec(memory_space=pl.ANY), pl.BlockSpec(memory_space=pl.ANY)],
        out_specs=[per(col), per((1, pw))],
        scratch_shapes=[pltpu.VMEM(page, F32), pltpu.VMEM(page, F32),
                        pltpu.VMEM((N_HEADS, HEAD_DIM, LANES), F32),
                        pltpu.VMEM((n_pages, N_HEADS, LANES), F32),
                        pltpu.VMEM((N_HEADS, HEAD_DIM, LANES), F32),
                        pltpu.SemaphoreType.DMA((n_pages,)), pltpu.SemaphoreType.DMA((n_pages,))],
    )
    return pl.pallas_call(
        functools.partial(_sample_attend_body, layer=layer, n_pages=n_pages),
        grid_spec=grid_spec,
        out_shape=[jax.ShapeDtypeStruct((db,) + col, F32), jax.ShapeDtypeStruct((db, 1, pw), BF16)],
        compiler_params=pltpu.CompilerParams(dimension_semantics=("arbitrary",), vmem_limit_bytes=VMEM_LIMIT),
        name="sample_attend",
    )(page_table_flat, sel, q4, knew4, vnew4, rel_bias, state, pnew3, cache_k_t, cache_v_t)


def _merge_body(x_ref, attn_ref, pool_ref, ga_ref, gb_ref, wba_ref, wpm_ref, ps_ref, wbp_ref, wout_ref,
                gffn_ref, wr_ref, br_ref, ltri_ref,
                x1_o, h2p_o, ev_o, gv_o, rk_o, cnt_o, carry_scr):
    step = pl.program_id(0)

    @pl.when(step == 0)
    def _():
        carry_scr[...] = jnp.zeros_like(carry_scr)

    a = _dot(attn_ref[...], wba_ref[...])
    pooled = pool_ref[...]
    gw = pooled.shape[1] // POOL_GROUPS
    pm = jnp.concatenate([_dot(pooled[:, g * gw:(g + 1) * gw], wpm_ref[g]) for g in range(POOL_GROUPS)], axis=1)
    bb = _dot((pm * ps_ref[...]).astype(BF16), wbp_ref[...])
    mix = jax.nn.sigmoid(ga_ref[...]) * a + jax.nn.sigmoid(gb_ref[...]) * bb
    x1 = x_ref[...] + _dot(mix.astype(BF16), wout_ref[...])
    x1_o[...] = x1
    h2 = _rms(x1, gffn_ref[...]).astype(BF16)
    half = h2.shape[1] // 2
    lo = lax.shift_right_logical(lax.bitcast_convert_type(h2[:, 0:half].astype(F32), U32), jnp.uint32(16))
    hi = lax.bitcast_convert_type(h2[:, half:].astype(F32), U32) & jnp.uint32(0xFFFF0000)
    h2p_o[...] = hi | lo

    logits = _dot(h2, wr_ref[...]) + br_ref[...]
    lane = lax.broadcasted_iota(I32, (TM, LANES), 1)
    lanef = lane.astype(F32)
    tops, ids = [], []
    l = logits
    for _ in range(TOP_EXPERTS):
        mx = jnp.max(l, axis=1, keepdims=True)
        ix = jnp.min(jnp.where(l == mx, lanef, float(LANES)), axis=1, keepdims=True)
        tops.append(mx)
        ids.append(ix)
        l = jnp.where(lanef == ix, -3e38, l)
    es = [jnp.exp(tv - tops[0]) for tv in tops]
    den = es[0] + es[1] + es[2] + es[3]
    onehot = jnp.zeros((TM, LANES), F32)
    for ix in ids:
        onehot = onehot + jnp.where(lanef == ix, 1.0, 0.0)
    before = _dot(ltri_ref[...], onehot.astype(BF16)) + carry_scr[0:1, :]
    ev = jnp.zeros((TM, LANES), I32)
    gv = jnp.zeros((TM, LANES), F32)
    rk = jnp.zeros((TM, LANES), I32)
    for k in range(TOP_EXPERTS):
        rank = jnp.sum(jnp.where(lanef == ids[k], before, 0.0), axis=1, keepdims=True)
        ev = jnp.where(lane == k, ids[k].astype(I32), ev)
        gv = jnp.where(lane == k, es[k] / den, gv)
        rk = jnp.where(lane == k, rank.astype(I32), rk)
    ev_o[...] = ev
    gv_o[...] = gv
    rk_o[...] = rk
    carry_scr[...] = carry_scr[...] + jnp.sum(onehot, axis=0, keepdims=True)
    cnt_o[...] = carry_scr[...]


def _merge(x_all, attn, pooled, ga, gb, wba, wpm, ps, wbp, wout, gffn, wr, br, ltri):
    nt, d = x_all.shape
    row = lambda width: pl.BlockSpec((TM, width), lambda i: (i, 0))
    consts = [wba, wpm, ps, wbp, wout, gffn, wr, br, ltri]
    return pl.pallas_call(
        _merge_body,
        grid=(nt // TM,),
        in_specs=[row(d), row(attn.shape[1]), row(pooled.shape[1]), row(d), row(d)] + [_full(c.shape) for c in consts],
        out_specs=[row(d), row(d // 2), row(LANES), row(LANES), row(LANES), _full((SUBLANES, LANES))],
        out_shape=[jax.ShapeDtypeStruct((nt, d), F32), jax.ShapeDtypeStruct((nt, d // 2), U32),
                   jax.ShapeDtypeStruct((nt, LANES), I32), jax.ShapeDtypeStruct((nt, LANES), F32),
                   jax.ShapeDtypeStruct((nt, LANES), I32), jax.ShapeDtypeStruct((SUBLANES, LANES), F32)],
        scratch_shapes=[pltpu.VMEM((SUBLANES, LANES), F32)],
        compiler_params=pltpu.CompilerParams(dimension_semantics=("arbitrary",), vmem_limit_bytes=VMEM_LIMIT),
        name="merge_router",
    )(x_all, attn, pooled, ga, gb, *consts)


def _dispatch_body(dest_ref, gend_ref, h2p_ref, xs_hbm, zero_scr, sem, zsem):
    step = pl.program_id(0)

    @pl.when(step == 0)
    def _():
        zero_scr[...] = jnp.zeros_like(zero_scr)

        def fill(e):
            return pltpu.make_async_copy(zero_scr, xs_hbm.at[pl.ds(pl.multiple_of(gend_ref[e] - TS, TS), TS)], zsem)

        def start(e, _):
            @pl.when(gend_ref[e] > gend_ref[e + N_EXPERTS])
            def _():
                fill(e).start()
            return 0
        lax.fori_loop(0, N_EXPERTS, start, 0)

        def wait(e, _):
            @pl.when(gend_ref[e] > gend_ref[e + N_EXPERTS])
            def _():
                fill(e).wait()
            return 0
        lax.fori_loop(0, N_EXPERTS, wait, 0)

        def tail(j):
            return pltpu.make_async_copy(zero_scr, xs_hbm.at[pl.ds(pl.multiple_of(j * TS, TS), TS)], zsem)

        first_unused = gend_ref[N_EXPERTS - 1] // TS
        n_tiles = xs_hbm.shape[0] // TS

        def tail_start(j, _):
            tail(j).start()
            return 0
        lax.fori_loop(first_unused, n_tiles, tail_start, 0)

        def tail_wait(j, _):
            tail(j).wait()
            return 0
        lax.fori_loop(first_unused, n_tiles, tail_wait, 0)

    t0 = step * TM

    def row_copy(r, k):
        d = dest_ref[(t0 + r) * TOP_EXPERTS + k]
        return pltpu.make_async_copy(h2p_ref.at[pl.ds(r, 1)], xs_hbm.at[pl.ds(d, 1)], sem)

    def start(r, _):
        for k in range(TOP_EXPERTS):
            row_copy(r, k).start()
        return 0
    lax.fori_loop(0, TM, start, 0)

    def wait(r, _):
        for k in range(TOP_EXPERTS):
            row_copy(r, k).wait()
        return 0
    lax.fori_loop(0, TM, wait, 0)


def _dispatch(dest_flat, gend, h2p, *, n_slots):
    nt, hw = h2p.shape
    grid_spec = pltpu.PrefetchScalarGridSpec(
        num_scalar_prefetch=2,
        grid=(nt // TM,),
        in_specs=[pl.BlockSpec((TM, hw), lambda i, d, g: (i, 0))],
        out_specs=pl.BlockSpec(memory_space=pl.ANY),
        scratch_shapes=[pltpu.VMEM((TS, hw), U32), pltpu.SemaphoreType.DMA, pltpu.SemaphoreType.DMA],
    )
    return pl.pallas_call(
        _dispatch_body,
        grid_spec=grid_spec,
        out_shape=jax.ShapeDtypeStruct((n_slots, hw), U32),
        compiler_params=pltpu.CompilerParams(dimension_semantics=("arbitrary",)),
        name="dispatch",
    )(dest_flat, gend, h2p)


def _moe_body(te_ref, nu_ref, xs_ref, wgu_ref, bgu_ref, wdn_ref, bdn_ref, ys_o, wgu_b, wdn_b):
    j = pl.program_id(0)
    prev = te_ref[jnp.maximum(j - 1, 0)]
    live = j < nu_ref[0]

    @pl.when(jnp.logical_and(live, jnp.logical_or(j == 0, te_ref[j] != prev)))
    def _():
        wgu_b[...] = wgu_ref[0, 0].astype(BF16)
        wdn_b[...] = wdn_ref[0, 0].astype(BF16)

    @pl.when(live)
    def _():
        words = xs_ref[...]
        x_lo = lax.bitcast_convert_type(lax.shift_left(words, jnp.uint32(16)), F32).astype(BF16)
        x_hi = lax.bitcast_convert_type(words & jnp.uint32(0xFFFF0000), F32).astype(BF16)
        half = words.shape[1]
        gu = _dot(x_lo, wgu_b[0:half, :]) + _dot(x_hi, wgu_b[half:, :]) + bgu_ref[0, 0]
        de = gu.shape[1] // 2
        gate = jnp.minimum(gu[:, 0:de], SWIGLU_LIMIT)
        up = jnp.clip(gu[:, de:], -SWIGLU_LIMIT, SWIGLU_LIMIT)
        act = (up + 1.0) * gate * jax.nn.sigmoid(SWIGLU_ALPHA * gate)
        ys_o[...] = _dot(act.astype(BF16), wdn_b[...]) + bdn_ref[0, 0]

    @pl.when(jnp.logical_not(live))
    def _():
        ys_o[...] = jnp.zeros_like(ys_o)


def _moe(tile_expert, n_used, xs, wgu, bgu, wdn, bdn, *, layer):
    n_slots, hw = xs.shape
    _, ne, d, de2 = wgu.shape
    last = lambda j, nu: jnp.minimum(j, nu[0] - 1)
    per_expert = lambda rows, cols: pl.BlockSpec((1, 1, rows, cols), lambda j, te, nu: (layer, te[j], 0, 0))
    grid_spec = pltpu.PrefetchScalarGridSpec(
        num_scalar_prefetch=2,
        grid=(n_slots // TS,),
        in_specs=[pl.BlockSpec((TS, hw), lambda j, te, nu: (last(j, nu), 0)),
                  per_expert(d, de2), per_expert(1, de2), per_expert(de2 // 2, d), per_expert(1, d)],
        out_specs=pl.BlockSpec((TS, d), lambda j, te, nu: (j, 0)),
        scratch_shapes=[pltpu.VMEM((d, de2), BF16), pltpu.VMEM((de2 // 2, d), BF16)],
    )
    return pl.pallas_call(
        _moe_body,
        grid_spec=grid_spec,
        out_shape=jax.ShapeDtypeStruct((n_slots, d), F32),
        compiler_params=pltpu.CompilerParams(dimension_semantics=("arbitrary",), vmem_limit_bytes=VMEM_LIMIT),
        name="moe_experts",
    )(tile_expert, n_used, xs, wgu, bgu[:, :, None, :], wdn, bdn[:, :, None, :])


def _combine_body(dest_ref, x1_ref, gv_ref, gfin_ref, ys_hbm, y_o, ybuf, sem):
    t0 = pl.program_id(0) * TM

    def row_copy(r, k):
        d = dest_ref[(t0 + r) * TOP_EXPERTS + k]
        return pltpu.make_async_copy(ys_hbm.at[pl.ds(d, 1)], ybuf.at[k, pl.ds(r, 1)], sem)

    def start(r, _):
        for k in range(TOP_EXPERTS):
            row_copy(r, k).start()
        return 0
    lax.fori_loop(0, TM, start, 0)

    def wait(r, _):
        for k in range(TOP_EXPERTS):
            row_copy(r, k).wait()
        return 0
    lax.fori_loop(0, TM, wait, 0)

    gv = gv_ref[...]
    y = jnp.zeros(x1_ref.shape, F32)
    for k in range(TOP_EXPERTS):
        y = y + ybuf[k] * gv[:, k:k + 1]
    y_o[...] = _rms(x1_ref[...] + y, gfin_ref[...])


def _combine(dest_flat, x1, gv, gfin, ys):
    nt, d = x1.shape
    grid_spec = pltpu.PrefetchScalarGridSpec(
        num_scalar_prefetch=1,
        grid=(nt // TM,),
        in_specs=[pl.BlockSpec((TM, d), lambda i, dref: (i, 0)),
                  pl.BlockSpec((TM, LANES), lambda i, dref: (i, 0)),
                  pl.BlockSpec((1, d), lambda i, dref: (0, 0)),
                  pl.BlockSpec(memory_space=pl.ANY)],
        out_specs=pl.BlockSpec((TM, d), lambda i, dref: (i, 0)),
        scratch_shapes=[pltpu.VMEM((TOP_EXPERTS, TM, d), F32), pltpu.SemaphoreType.DMA],
    )
    return pl.pallas_call(
        _combine_body,
        grid_spec=grid_spec,
        out_shape=jax.ShapeDtypeStruct((nt, d), F32),
        compiler_params=pltpu.CompilerParams(dimension_semantics=("arbitrary",)),
        name="combine_norm",
    )(dest_flat, x1, gv, gfin, ys)


def _tri_constants():
    r = lax.broadcasted_iota(I32, (TQ, TQ), 0)
    c = lax.broadcasted_iota(I32, (TQ, TQ), 1)
    incl = (r <= c).astype(BF16)
    u2 = jnp.concatenate([incl, jnp.ones((TQ, LANES), BF16)], axis=1)
    ltri = (c < r).astype(BF16)
    return u2, ltri


def kernel(x_prompt, x_sample, cache_k, cache_v, cache_kidx, state_pool, page_table, meta_tokens, rel_bias, g_mix, w_in, w_pool_mix, pool_scale, w_br_attn, w_br_pool, w_out, g_ffn, w_router, b_router, w_gate_up, b_gate_up, w_down, b_down, g_final):
    nb, seq, d = x_prompt.shape
    db, dec_seq, _ = x_sample.shape
    assert dec_seq == 1, "one new token per sample"
    depth = w_in.shape[0]
    assert depth == 1, "single-layer stack: the combine kernel applies the final norm"
    n_pages = page_table.shape[1]
    assert n_pages % SUBLANES == 0, "the sample top-k lays the cached keys out as eight equal page segments"
    past = n_pages * PAGE_SIZE
    pw = state_pool.shape[-1]
    l_seq = seq + N_META
    lp = _round_up(l_seq, TQ)
    n_prompt = nb * lp
    n_sample = _round_up(db, TM)
    nt = n_prompt + n_sample
    ksel_p = min(TOPK_MAX, seq // 4)
    ksel_s = min(TOPK_MAX, (past + dec_seq) // 4)
    n_tiles = (TOP_EXPERTS * nt) // TS + N_EXPERTS
    n_slots = n_tiles * TS

    meta = jnp.broadcast_to(meta_tokens[None].astype(x_prompt.dtype), (nb, N_META, d))
    xp = jnp.concatenate([meta, x_prompt, jnp.zeros((nb, lp - l_seq, d), x_prompt.dtype)], axis=1)
    xs_rows = jnp.concatenate([x_sample[:, 0, :], jnp.zeros((n_sample - db, d), x_sample.dtype)], axis=0)
    x_all = jnp.concatenate([xp.reshape(n_prompt, d), xs_rows], axis=0)

    u2, ltri = _tri_constants()
    page_flat = page_table.reshape(-1).astype(I32)
    rel_bias = rel_bias.astype(F32)
    cache_k_t = jnp.transpose(cache_k, (0, 1, 3, 4, 2))
    cache_v_t = jnp.transpose(cache_v, (0, 1, 3, 4, 2))
    cache_ki_t = jnp.transpose(cache_kidx, (0, 1, 3, 2))

    outs = {name: [] for name in ("k_p", "v_p", "ki_p", "pool_p", "k_s", "v_s", "ki_s", "pool_s")}
    for l in range(depth):
        wl = w_in[l]
        aw = ATTN_WIDTH
        o_ki = 4 * aw
        o_wi = o_ki + IDX_DIM
        o_p = o_wi + N_IDX_HEADS
        o_ga = o_p + pw
        o_gb = o_ga + d
        wa = wl[:, 0:o_ki].astype(BF16)
        wki = wl[:, o_ki:o_wi]
        wki2 = jnp.concatenate([wki, wki], axis=1).astype(BF16)
        wwi = jnp.pad(wl[:, o_wi:o_p], ((0, 0), (0, LANES - N_IDX_HEADS))).astype(BF16)
        wp = wl[:, o_p:o_ga].astype(BF16)
        wga = wl[:, o_ga:o_gb].astype(BF16)
        wgb = wl[:, o_gb:o_gb + d].astype(BF16)

        (qs, kf, vf, kb, vb, qib, ki2b, kif, wif, pf, ga, gb) = _inproj(
            x_all, g_mix[l][None, :], wa, wki2, wwi, wp, wga, wgb)

        attn_p = _attn_prompt(qs, qib, wif, kb, vb, ki2b, u2, rel_bias, nb=nb, lp=lp, ksel=ksel_p)
        pooled_p = _pool_prompt(pf, n_rows=n_prompt, lp=lp)

        sl = slice(n_prompt, n_prompt + db)
        keys = _sample_keys(page_flat, qib[sl].reshape(db, N_IDX_HEADS, IDX_DIM), kif[sl, 0:IDX_DIM][:, None, :],
                            wif[sl, 0:N_IDX_HEADS][:, :, None], cache_ki_t, layer=l, n_pages=n_pages)
        sel = _sample_select(keys, past=past, ksel=ksel_s)
        cols = lambda a: a[sl].astype(F32).reshape(db, N_HEADS, HEAD_DIM, 1)
        attn_s, pooled_s = _sample_attend(
            page_flat, sel, cols(qs), cols(kf), cols(vf), rel_bias, state_pool[l], pf[sl][:, None, :],
            cache_k_t, cache_v_t, layer=l, n_pages=n_pages)

        pad_s = lambda a: jnp.concatenate([a, jnp.zeros((n_sample - db, a.shape[1]), a.dtype)], axis=0)
        attn_all = jnp.concatenate([attn_p, pad_s(attn_s.reshape(db, ATTN_WIDTH))], axis=0)
        pooled_all = jnp.concatenate([pooled_p, pad_s(pooled_s[:, 0, :])], axis=0)

        wr = jnp.pad(w_router[l], ((0, 0), (0, LANES - N_EXPERTS))).astype(BF16)
        br = jnp.concatenate([b_router[l].astype(F32), jnp.full((LANES - N_EXPERTS,), NEG, F32)])[None, :]
        x1, h2p, ev, gv, rk, cnt = _merge(
            x_all, attn_all, pooled_all, ga, gb, w_br_attn[l].astype(BF16), w_pool_mix[l].astype(BF16),
            pool_scale[l][None, :], w_br_pool[l].astype(BF16), w_out[l].astype(BF16), g_ffn[l][None, :], wr, br, ltri)

        counts = cnt[0, 0:N_EXPERTS].astype(I32)
        padded = (counts + TS - 1) // TS * TS
        gend = jnp.cumsum(padded)
        gstart = gend - padded
        experts = jnp.arange(N_EXPERTS, dtype=I32)
        ev4 = ev[:, 0:TOP_EXPERTS]
        dest = (jnp.sum(jnp.where(ev4[:, :, None] == experts, gstart, 0), axis=2) + rk[:, 0:TOP_EXPERTS]).reshape(-1)
        tile_first = jnp.arange(n_tiles, dtype=I32)[:, None] * TS
        tile_expert = jnp.minimum(jnp.sum((gend[None, :] <= tile_first).astype(I32), axis=1), N_EXPERTS - 1)
        n_used = (gend[-1] // TS).astype(I32)[None]

        xs_sorted = _dispatch(dest, jnp.concatenate([gend, gstart + counts]).astype(I32), h2p, n_slots=n_slots)
        ys = _moe(tile_expert, n_used, xs_sorted, w_gate_up, b_gate_up, w_down, b_down, layer=l)
        y_all = _combine(dest, x1, gv, g_final[None, :], ys)

        kp = kf[:n_prompt].reshape(nb, lp, N_HEADS, HEAD_DIM)[:, :l_seq]
        vp = vf[:n_prompt].reshape(nb, lp, N_HEADS, HEAD_DIM)[:, :l_seq]
        kip = kif[:n_prompt, 0:IDX_DIM].reshape(nb, lp, IDX_DIM)[:, :l_seq]
        pp = pf[:n_prompt].reshape(nb, lp, pw)
        outs["k_p"].append(kp)
        outs["v_p"].append(vp)
        outs["ki_p"].append(kip)
        outs["pool_p"].append(pp[:, l_seq - POOL_STATE:l_seq])
        outs["k_s"].append(kf[sl].reshape(db, 1, N_HEADS, HEAD_DIM))
        outs["v_s"].append(vf[sl].reshape(db, 1, N_HEADS, HEAD_DIM))
        outs["ki_s"].append(kif[sl, 0:IDX_DIM].reshape(db, 1, IDX_DIM))
        outs["pool_s"].append(jnp.concatenate([state_pool[l][:, 1:], pf[sl][:, None, :]], axis=1))

    y_prompt = y_all[:n_prompt].reshape(nb, lp, d)[:, N_META:l_seq]
    y_sample = y_all[n_prompt:n_prompt + db].reshape(db, 1, d)
    st = lambda name: jnp.stack(outs[name])
    return (y_prompt, y_sample, st("k_p"), st("v_p"), st("ki_p"), st("pool_p"),
            st("k_s"), st("v_s"), st("ki_s"), st("pool_s"))
```

```python
import functools
import math

import jax
import jax.numpy as jnp
from jax import lax
from jax.experimental import pallas as pl
from jax.experimental.pallas import tpu as pltpu

F32 = jnp.float32
BF16 = jnp.bfloat16
I32 = jnp.int32
U32 = jnp.uint32

N_META = 16
N_HEADS = 8
HEAD_DIM = 64
ATTN_WIDTH = N_HEADS * HEAD_DIM
N_IDX_HEADS = 8
IDX_DIM = 64
TOPK_MAX = 256
N_BUCKETS = 32
MAX_DISTANCE = 128
POOL_WINDOWS = (2, 4, 8, 16)
POOL_GROUPS = 4
POOL_STATE = 15
N_EXPERTS = 32
TOP_EXPERTS = 4
SWIGLU_LIMIT = 7.0
SWIGLU_ALPHA = 1.702
PAGE_SIZE = 128
EPS = 1e-6

LANES = 128
SUBLANES = 8
MXU_DIM = 256
TQ = MXU_DIM
TM = MXU_DIM
TS = MXU_DIM
VMEM_LIMIT = 56 * 1024 * 1024
DMA_UNROLL = 8

NEG = -1e30
INT_MIN = -(2 ** 31)

_MAX_EXACT = N_BUCKETS // 2
_BUCKET_THRESHOLDS = tuple(
    math.ceil(_MAX_EXACT * (MAX_DISTANCE / _MAX_EXACT) ** (j / (N_BUCKETS - _MAX_EXACT)))
    for j in range(1, N_BUCKETS - _MAX_EXACT))


def _round_up(a, m):
    return (a + m - 1) // m * m


def _rms(x, g):
    return x * lax.rsqrt(jnp.mean(x * x, axis=-1, keepdims=True) + EPS) * g


def _dot(a, b):
    return jnp.dot(a, b, preferred_element_type=F32)


def _dot_nt(a, b):
    return lax.dot_general(a, b, (((1,), (1,)), ((), ())), preferred_element_type=F32)


def _bucket(dist):
    large = jnp.full(dist.shape, _MAX_EXACT, I32)
    for thr in _BUCKET_THRESHOLDS:
        large = large + jnp.where(dist >= thr, 1, 0)
    return jnp.where(dist < _MAX_EXACT, dist, large)


def _sort_key(s):
    s = jnp.where(s == 0.0, 0.0, s)
    bits = lax.bitcast_convert_type(s, I32)
    return jnp.where(bits >= 0, bits, bits ^ jnp.int32(0x7FFFFFFF))


def _full(shape):
    return pl.BlockSpec(shape, lambda *_: (0,) * len(shape))


def _inproj_body(x_ref, g_ref, wa_ref, wki_ref, wwi_ref, wp_ref, wga_ref, wgb_ref,
                 qs_o, kf_o, vf_o, kb_o, vb_o, qib_o, ki2b_o, kif_o, wif_o, pf_o, ga_o, gb_o):
    h = _rms(x_ref[...], g_ref[...]).astype(BF16)
    za = _dot(h, wa_ref[...])
    w = ATTN_WIDTH
    qs_o[...] = (za[:, 0:w] * (HEAD_DIM ** -0.5)).astype(BF16)
    k = za[:, w:2 * w]
    v = za[:, 2 * w:3 * w]
    kf_o[...] = k
    vf_o[...] = v
    kb_o[...] = k.astype(BF16)
    vb_o[...] = v.astype(BF16)
    qib_o[...] = za[:, 3 * w:4 * w].astype(BF16)
    ki2 = _dot(h, wki_ref[...])
    kif_o[...] = ki2
    ki2b_o[...] = ki2.astype(BF16)
    wif_o[...] = _dot(h, wwi_ref[...])
    pf_o[...] = _dot(h, wp_ref[...])
    ga_o[...] = _dot(h, wga_ref[...])
    gb_o[...] = _dot(h, wgb_ref[...])


def _inproj(x_all, g, wa, wki2, wwi, wp, wga, wgb):
    nt, d = x_all.shape
    pw = wp.shape[1]
    row = lambda width: pl.BlockSpec((TM, width), lambda i: (i, 0))
    outs = [
        (ATTN_WIDTH, BF16), (ATTN_WIDTH, F32), (ATTN_WIDTH, F32), (ATTN_WIDTH, BF16), (ATTN_WIDTH, BF16),
        (ATTN_WIDTH, BF16), (LANES, BF16), (LANES, F32), (LANES, F32), (pw, F32), (d, F32), (d, F32)]
    return pl.pallas_call(
        _inproj_body,
        grid=(nt // TM,),
        in_specs=[row(d), _full((1, d)), _full(wa.shape), _full(wki2.shape), _full(wwi.shape),
                  _full(wp.shape), _full(wga.shape), _full(wgb.shape)],
        out_specs=[row(wd) for wd, _ in outs],
        out_shape=[jax.ShapeDtypeStruct((nt, wd), dt) for wd, dt in outs],
        compiler_params=pltpu.CompilerParams(dimension_semantics=("arbitrary",), vmem_limit_bytes=VMEM_LIMIT),
        name="inproj",
    )(x_all, g, wa, wki2, wwi, wp, wga, wgb)


def _attn_prompt_body(qs_ref, qib_ref, wif_ref, kb_ref, vb_ref, ki2b_ref, u2_ref, rb_ref,
                      o_ref, qis_scr, wb_scr, key_scr, ntab_scr, qz_scr, m_scr, acc_scr, sum_scr, *, ksel, tail_rows):
    b = pl.program_id(0)
    i = pl.program_id(1)
    nchunk = i + 1
    n_far = jnp.maximum(i - 1, 0)

    @pl.when((b == 0) & (i == 0))
    def _():
        def slab_rows(s, _):
            r0 = pl.multiple_of(s * SUBLANES, SUBLANES)
            r = r0 + lax.broadcasted_iota(I32, (SUBLANES, 2 * TQ), 0)
            x = lax.broadcasted_iota(I32, (SUBLANES, 2 * TQ), 1)
            bucket = _bucket(jnp.maximum(r + TQ - x, 0))
            accs = [jnp.zeros((SUBLANES, 2 * TQ), F32) for _ in range(N_HEADS)]
            for j in range(N_BUCKETS - 1):
                m = bucket == j
                accs = [jnp.where(m, rb_ref[j, h] - rb_ref[N_BUCKETS - 1, h], accs[h]) for h in range(N_HEADS)]
            for h in range(N_HEADS):
                ntab_scr[h, 0, pl.ds(r0, SUBLANES), :] = accs[h][:, 0:TQ]
                ntab_scr[h, 1, pl.ds(r0, SUBLANES), :] = accs[h][:, TQ:2 * TQ]
            return 0
        lax.fori_loop(0, TQ // SUBLANES, slab_rows, 0)

    def two(x):
        return jnp.concatenate([x, x], axis=1)

    def block(nr):
        hr = nr // 2
        lane = lax.broadcasted_iota(I32, (nr, LANES), 1)
        lo_half = lane < HEAD_DIM

        for h in range(N_HEADS):
            cols = slice((h // 2) * LANES, (h // 2 + 1) * LANES)
            keep = lo_half if h % 2 == 0 else jnp.logical_not(lo_half)
            qis_scr[h, 0:nr] = jnp.where(keep, qib_ref[0:nr, cols].astype(F32), 0.0).astype(BF16)
            qz_scr[h, 0:nr] = jnp.where(keep, qs_ref[0:nr, cols].astype(F32), 0.0).astype(BF16)
            wb_scr[h, 0:nr] = jnp.broadcast_to(wif_ref[0:nr, h:h + 1], (nr, LANES))
            m_scr[h, 0:nr] = jnp.full((nr, LANES), NEG, F32)
            acc_scr[h, 0:nr] = jnp.zeros((nr, LANES), F32)
            sum_scr[h, 0:nr] = jnp.zeros((nr, LANES), F32)

        dmat = lax.broadcasted_iota(I32, (nr, TQ), 1) - lax.broadcasted_iota(I32, (nr, TQ), 0)

        def score_chunk(c, _):
            kc = ki2b_ref[pl.ds(pl.multiple_of(c * TQ, TQ), TQ), :]
            s = jnp.zeros((nr, TQ), F32)
            for h in range(N_IDX_HEADS):
                s = s + two(wb_scr[h, 0:nr]) * jnp.maximum(_dot_nt(qis_scr[h, 0:nr], kc), 0.0)
            key_scr[c, 0:nr] = jnp.where(dmat <= (i - c) * TQ, _sort_key(s), INT_MIN)
            return 0
        lax.fori_loop(0, nchunk, score_chunk, 0)

        def count(r0, pred):
            def body(c, acc):
                hit = jnp.where(pred(key_scr[c, r0:r0 + hr, :]), 1.0, 0.0)
                return acc + (hit[:, 0:LANES] + hit[:, LANES:2 * LANES])
            acc = lax.fori_loop(0, nchunk, body, jnp.zeros((hr, LANES), F32))
            return jnp.broadcast_to(jnp.sum(acc, axis=1, keepdims=True), (hr, LANES))

        ts, needs = [], []
        for r0 in (0, hr):
            def bit_step(step, t, r0=r0):
                cand = t + lax.shift_left(jnp.int32(1), 31 - step)
                cand2 = two(cand)
                return jnp.where(count(r0, lambda k: k >= cand2) >= ksel, cand, t)
            t = lax.fori_loop(0, 32, bit_step, jnp.full((hr, LANES), INT_MIN, I32))
            th2 = two(t)
            n_gt = count(r0, lambda k: k > th2)
            ts.append(t)
            needs.append(jnp.where(t == INT_MIN, 0.0, ksel - n_gt))
        t2 = two(jnp.concatenate(ts, axis=0))
        need2 = two(jnp.concatenate(needs, axis=0))

        def mask_chunk(c, carry):
            kc = key_scr[c, 0:nr]
            eq = kc == t2
            pre = _dot(jnp.where(eq, 1.0, 0.0).astype(BF16), u2_ref[...])
            prefix = pre[:, 0:TQ] + two(carry)
            tie_ok = jnp.where(eq, prefix, 3e38) <= need2
            madd = jnp.where(kc > t2, 0.0, jnp.where(tie_ok, 0.0, NEG))
            key_scr[c, 0:nr] = lax.bitcast_convert_type(madd, I32)
            return carry + pre[:, TQ:TQ + LANES]
        lax.fori_loop(0, nchunk, mask_chunk, jnp.zeros((nr, LANES), F32))

        def chunk_logits(c, slab):
            rows = pl.ds(pl.multiple_of(c * TQ, TQ), TQ)
            madd = lax.bitcast_convert_type(key_scr[c, 0:nr], F32)
            out = []
            for hp in range(N_HEADS // 2):
                kc = kb_ref[rows, hp * LANES:(hp + 1) * LANES]
                for h in (2 * hp, 2 * hp + 1):
                    l = _dot_nt(qz_scr[h, 0:nr], kc) + madd
                    out.append(l if slab is None else l + ntab_scr[h, slab, 0:nr])
            return rows, out

        def row_max(c, slab):
            _, ls = chunk_logits(c, slab)
            for h, l in enumerate(ls):
                m_scr[h, 0:nr] = jnp.maximum(m_scr[h, 0:nr], jnp.maximum(l[:, 0:LANES], l[:, LANES:2 * LANES]))

        def accumulate(c, slab):
            rows, ls = chunk_logits(c, slab)
            for h, l in enumerate(ls):
                p = jnp.exp(l - two(m_scr[h, 0:nr]))
                vc = vb_ref[rows, (h // 2) * LANES:(h // 2 + 1) * LANES]
                acc_scr[h, 0:nr] = acc_scr[h, 0:nr] + _dot(p.astype(BF16), vc)
                sum_scr[h, 0:nr] = sum_scr[h, 0:nr] + (p[:, 0:LANES] + p[:, LANES:2 * LANES])

        def sweep(step):
            def far(c, _):
                step(c, None)
                return 0
            lax.fori_loop(0, n_far, far, 0)

            @pl.when(i >= 1)
            def _():
                step(i - 1, 0)
            step(i, 1)

        sweep(row_max)
        for h in range(N_HEADS):
            m_scr[h, 0:nr] = jnp.broadcast_to(jnp.max(m_scr[h, 0:nr], axis=1, keepdims=True), (nr, LANES))
        sweep(accumulate)
        for hp in range(N_HEADS // 2):
            outs = [acc_scr[h, 0:nr] / jnp.sum(sum_scr[h, 0:nr], axis=1, keepdims=True) for h in (2 * hp, 2 * hp + 1)]
            o_ref[0:nr, hp * LANES:(hp + 1) * LANES] = jnp.where(lo_half, outs[0], outs[1]).astype(BF16)
        if nr < TQ:
            o_ref[nr:TQ, :] = jnp.zeros((TQ - nr, ATTN_WIDTH), BF16)

    if tail_rows == TQ:
        block(TQ)
    else:
        last = pl.num_programs(1) - 1
        pl.when(i < last)(lambda: block(TQ))
        pl.when(i == last)(lambda: block(tail_rows))


def _attn_prompt(qs, qib, wif, kb, vb, ki2b, u2, rel_bias, *, nb, lp, ksel, tail_rows):
    nq = lp // TQ
    rowq = lambda width: pl.BlockSpec((TQ, width), lambda b, i: (b * nq + i, 0))
    seq = lambda width: pl.BlockSpec((lp, width), lambda b, i: (b, 0))
    per_head = lambda dt: pltpu.VMEM((N_HEADS, TQ, LANES), dt)
    return pl.pallas_call(
        functools.partial(_attn_prompt_body, ksel=float(ksel), tail_rows=tail_rows),
        grid=(nb, nq),
        in_specs=[rowq(ATTN_WIDTH), rowq(ATTN_WIDTH), rowq(LANES), seq(ATTN_WIDTH), seq(ATTN_WIDTH), seq(LANES),
                  _full(u2.shape), pl.BlockSpec(memory_space=pltpu.SMEM)],
        out_specs=rowq(ATTN_WIDTH),
        out_shape=jax.ShapeDtypeStruct((nb * lp, ATTN_WIDTH), BF16),
        scratch_shapes=[
            per_head(BF16),
            per_head(F32),
            pltpu.VMEM((nq, TQ, TQ), I32),
            pltpu.VMEM((N_HEADS, 2, TQ, TQ), F32),
            per_head(BF16),
            per_head(F32), per_head(F32), per_head(F32),
        ],
        compiler_params=pltpu.CompilerParams(dimension_semantics=("arbitrary", "arbitrary"),
                                             vmem_limit_bytes=VMEM_LIMIT),
        name="attn_prompt",
    )(qs, qib, wif, kb, vb, ki2b, u2, rel_bias)


def _pool_prompt_body(p_ref, halo_ref, o_ref, ext_scr, *, tiles_per_seq):
    j = pl.program_id(0) % tiles_per_seq
    halo = 2 * SUBLANES
    p = p_ref[...]
    ext_scr[0:halo, :] = jnp.where(j == 0, 0.0, halo_ref[...])
    ext_scr[halo:halo + TM, :] = p
    pos = j * TM + lax.broadcasted_iota(I32, (TM, 1), 0)
    gw = p.shape[1] // POOL_GROUPS
    for g, w in enumerate(POOL_WINDOWS):
        cols = slice(g * gw, (g + 1) * gw)
        s = p[:, cols]
        for back in range(1, w):
            s = s + ext_scr[halo - back:halo - back + TM, cols]
        cnt = jnp.minimum(pos + 1, w).astype(F32)
        o_ref[:, cols] = (s / cnt - p[:, cols]).astype(BF16)


def _pool_prompt(pf, *, n_rows, lp):
    pw = pf.shape[1]
    halo = 2 * SUBLANES
    return pl.pallas_call(
        functools.partial(_pool_prompt_body, tiles_per_seq=lp // TM),
        grid=(n_rows // TM,),
        in_specs=[pl.BlockSpec((TM, pw), lambda i: (i, 0)),
                  pl.BlockSpec((halo, pw), lambda i: (jnp.maximum(i * (TM // halo) - 1, 0), 0))],
        out_specs=pl.BlockSpec((TM, pw), lambda i: (i, 0)),
        out_shape=jax.ShapeDtypeStruct((n_rows, pw), BF16),
        scratch_shapes=[pltpu.VMEM((halo + TM, pw), F32)],
        compiler_params=pltpu.CompilerParams(dimension_semantics=("arbitrary",)),
        name="pool_prompt",
    )(pf, pf)


def _sample_keys_body(pt_ref, qi_ref, kinew_ref, wi_ref, cache_hbm, key_o, kibuf, sem, *, layer, n_pages):
    s = pl.program_id(0)
    past = n_pages * PAGE_SIZE
    seg_w = past // SUBLANES

    copies = [pltpu.make_async_copy(cache_hbm.at[layer, pt_ref[s * n_pages + pg]],
                                    kibuf.at[:, pg * PAGE_SIZE:(pg + 1) * PAGE_SIZE], sem) for pg in range(n_pages)]
    for cp in copies:
        cp.start()
    for cp in copies:
        cp.wait()

    qi = qi_ref[0]
    wcol = wi_ref[0]
    d = _dot(qi, kibuf[...].astype(BF16))
    sc = jnp.sum(wcol * jnp.maximum(d, 0.0), axis=0, keepdims=True)
    knew = kinew_ref[0].astype(BF16).astype(F32)
    dn = jnp.sum(qi.astype(F32) * knew, axis=1, keepdims=True)
    sn = jnp.sum(wcol * jnp.maximum(dn, 0.0), axis=0, keepdims=True)
    key = _sort_key(sc)
    lane = lax.broadcasted_iota(I32, (1, LANES), 1)
    new_tail = jnp.where(lane == 0, _sort_key(jnp.broadcast_to(sn, (1, LANES))), INT_MIN)
    for g in range(SUBLANES):
        key_o[0, g:g + 1, 0:seg_w] = key[:, g * seg_w:(g + 1) * seg_w]
        key_o[0, g:g + 1, seg_w:seg_w + LANES] = new_tail if g == 0 else jnp.full((1, LANES), INT_MIN, I32)


def _sample_keys(page_table_flat, qi3, kinew3, wi3, cache_ki_t, *, layer, n_pages):
    db = qi3.shape[0]
    past = n_pages * PAGE_SIZE
    kw = past // SUBLANES + LANES
    grid_spec = pltpu.PrefetchScalarGridSpec(
        num_scalar_prefetch=1,
        grid=(db,),
        in_specs=[pl.BlockSpec((1, N_IDX_HEADS, IDX_DIM), lambda s, pt: (s, 0, 0)),
                  pl.BlockSpec((1, 1, IDX_DIM), lambda s, pt: (s, 0, 0)),
                  pl.BlockSpec((1, N_IDX_HEADS, 1), lambda s, pt: (s, 0, 0)),
                  pl.BlockSpec(memory_space=pl.ANY)],
        out_specs=pl.BlockSpec((1, SUBLANES, kw), lambda s, pt: (s, 0, 0)),
        scratch_shapes=[pltpu.VMEM((IDX_DIM, past), F32), pltpu.SemaphoreType.DMA],
    )
    return pl.pallas_call(
        functools.partial(_sample_keys_body, layer=layer, n_pages=n_pages),
        grid_spec=grid_spec,
        out_shape=jax.ShapeDtypeStruct((db, SUBLANES, kw), I32),
        compiler_params=pltpu.CompilerParams(dimension_semantics=("arbitrary",)),
        name="sample_keys",
    )(page_table_flat, qi3, kinew3, wi3, cache_ki_t)


def _sample_select_body(key_ref, sel_o, *, past, ksel):
    keys = key_ref[...]
    kw = keys.shape[2]
    seg_w = kw - LANES
    row = lax.broadcasted_iota(I32, (SUBLANES, kw), 0)
    lane = lax.broadcasted_iota(I32, (SUBLANES, kw), 1)
    pos = jnp.where(lane < seg_w, row * seg_w + lane, past + (lane - seg_w) + row * LANES)[None]

    def cnt(pred):
        x = jnp.where(pred, 1.0, 0.0)
        return jnp.sum(jnp.sum(x, axis=2, keepdims=True), axis=1, keepdims=True)

    def bit_step(step, t):
        cand = t + lax.shift_left(jnp.int32(1), 31 - step)
        return jnp.where(cnt(keys >= cand) >= ksel, cand, t)
    t = lax.fori_loop(0, 32, bit_step, jnp.full((keys.shape[0], 1, 1), INT_MIN, I32))
    gt = keys > t
    eq = keys == t
    need = ksel - cnt(gt)

    nbits = (past + SUBLANES * LANES).bit_length()

    def cut_step(step, c):
        cand = c - lax.shift_left(jnp.int32(1), nbits - 1 - step)
        ok = cnt(jnp.logical_and(eq, pos <= cand)) >= need
        return jnp.where(ok, cand, c)
    cut = lax.fori_loop(0, nbits, cut_step, jnp.full((keys.shape[0], 1, 1), 2 ** nbits - 1, I32))
    sel = jnp.where(gt, 1.0, jnp.where(jnp.logical_and(eq, pos <= cut), 1.0, 0.0))
    sel_o[...] = jnp.where(pos <= past, sel, 0.0)


def _sample_select(keys, *, past, ksel):
    return pl.pallas_call(
        functools.partial(_sample_select_body, past=past, ksel=float(ksel)),
        out_shape=jax.ShapeDtypeStruct(keys.shape, F32),
        name="sample_select",
    )(keys)


def _sample_attend_body(pt_ref, sel_ref, q_ref, knew_ref, vnew_ref, rb_ref, state_ref, pnew_ref, ck_hbm, cv_hbm,
                        attn_o, pool_o, kbuf, vbuf, qb_scr, lg_scr, acc_scr, ksem, vsem, *, layer, n_pages):
    s = pl.program_id(0)
    past = n_pages * PAGE_SIZE
    seg_w = past // SUBLANES
    pages_per_seg = n_pages // SUBLANES
    last_bias = [rb_ref[N_BUCKETS - 1, h] for h in range(N_HEADS)]

    def kcopy(pg):
        return pltpu.make_async_copy(ck_hbm.at[layer, pt_ref[s * n_pages + pg]], kbuf.at[pg], ksem.at[pg])

    def vcopy(pg):
        return pltpu.make_async_copy(cv_hbm.at[layer, pt_ref[s * n_pages + pg]], vbuf.at[pg], vsem.at[pg])

    def start(pg, _):
        kcopy(pg).start()
        vcopy(pg).start()
        return 0
    lax.fori_loop(0, n_pages, start, 0)

    for h in range(N_HEADS):
        qb_scr[h] = jnp.broadcast_to(q_ref[0, h], (HEAD_DIM, LANES))
        acc_scr[h] = jnp.zeros((HEAD_DIM, LANES), F32)

    def k_page(pg, _):
        kcopy(pg).wait()
        rows = [jnp.sum(kbuf[pg, h].astype(BF16).astype(F32) * qb_scr[h], axis=0, keepdims=True)
                for h in range(N_HEADS)]
        lg_scr[pg] = jnp.concatenate(rows, axis=0)
        return 0
    lax.fori_loop(0, n_pages, k_page, 0)

    lane = lax.broadcasted_iota(I32, (1, LANES), 1)
    bucket = _bucket(past - ((n_pages - 1) * PAGE_SIZE + lane))
    near_rows = []
    for h in range(N_HEADS):
        r = jnp.zeros((1, LANES), F32)
        for j in range(N_BUCKETS - 1):
            r = jnp.where(bucket == j, rb_ref[j, h] - last_bias[h], r)
        near_rows.append(r)
    near = jnp.concatenate(near_rows, axis=0)

    sel = sel_ref[0]
    m = jnp.full((N_HEADS, LANES), NEG, F32)
    for pg in range(n_pages):
        g, j = divmod(pg, pages_per_seg)
        keep = sel[g:g + 1, j * PAGE_SIZE:(j + 1) * PAGE_SIZE] > 0.0
        l = lg_scr[pg] + jnp.where(keep, 0.0, NEG)
        if pg == n_pages - 1:
            l = l + near
        lg_scr[pg] = l
        m = jnp.maximum(m, l)
    new_rows = []
    for h in range(N_HEADS):
        kn = knew_ref[0, h].astype(BF16).astype(F32)
        new_rows.append(jnp.sum(q_ref[0, h] * kn, axis=0, keepdims=True) + (rb_ref[0, h] - last_bias[h]))
    l_new = jnp.where(sel[0:1, seg_w:seg_w + 1] > 0.0, jnp.concatenate(new_rows, axis=0), NEG)
    m = jnp.maximum(jnp.max(m, axis=1, keepdims=True), l_new)
    ssum = jnp.zeros((N_HEADS, LANES), F32)
    for pg in range(n_pages):
        p = jnp.exp(lg_scr[pg] - m)
        lg_scr[pg] = p
        ssum = ssum + p
    p_new = jnp.exp(l_new - m)
    den = jnp.sum(ssum, axis=1, keepdims=True) + p_new
    for pg in range(n_pages):
        lg_scr[pg] = (lg_scr[pg] / den).astype(BF16).astype(F32)
    p_new = (p_new / den).astype(BF16).astype(F32)

    def v_page(pg, _):
        vcopy(pg).wait()
        p = lg_scr[pg]
        for h in range(N_HEADS):
            acc_scr[h] = acc_scr[h] + vbuf[pg, h].astype(BF16).astype(F32) * p[h:h + 1, :]
        return 0
    lax.fori_loop(0, n_pages, v_page, 0)
    for h in range(N_HEADS):
        vn = vnew_ref[0, h].astype(BF16).astype(F32)
        attn_o[0, h] = jnp.sum(acc_scr[h], axis=1, keepdims=True) + p_new[h:h + 1, :] * vn

    pnew = pnew_ref[0]
    st = state_ref[0]
    gw = pnew.shape[1] // POOL_GROUPS
    for g, w in enumerate(POOL_WINDOWS):
        cols = slice(g * gw, (g + 1) * gw)
        acc = pnew[:, cols]
        for back in range(1, w):
            acc = acc + st[POOL_STATE - back:POOL_STATE - back + 1, cols]
        pool_o[0, :, cols] = (acc / float(min(POOL_STATE + 1, w)) - pnew[:, cols]).astype(BF16)


def _sample_attend(page_table_flat, sel, q4, knew4, vnew4, rel_bias, state, pnew3, cache_k_t, cache_v_t, *, layer, n_pages):
    db = q4.shape[0]
    pw = pnew3.shape[2]
    per = lambda shape: pl.BlockSpec((1,) + shape, lambda s, pt: (s,) + (0,) * len(shape))
    col = (N_HEADS, HEAD_DIM, 1)
    page = (n_pages, N_HEADS, HEAD_DIM, LANES)
    grid_spec = pltpu.PrefetchScalarGridSpec(
        num_scalar_prefetch=1,
        grid=(db,),
        in_specs=[per(sel.shape[1:]), per(col), per(col), per(col), pl.BlockSpec(memory_space=pltpu.SMEM),
                  per((POOL_STATE, pw)), per((1, pw)),
                  pl.BlockSpec(memory_space=pl.ANY), pl.BlockSpec(memory_space=pl.ANY)],
        out_specs=[per(col), per((1, pw))],
        scratch_shapes=[pltpu.VMEM(page, F32), pltpu.VMEM(page, F32),
                        pltpu.VMEM((N_HEADS, HEAD_DIM, LANES), F32),
                        pltpu.VMEM((n_pages, N_HEADS, LANES), F32),
                        pltpu.VMEM((N_HEADS, HEAD_DIM, LANES), F32),
                        pltpu.SemaphoreType.DMA((n_pages,)), pltpu.SemaphoreType.DMA((n_pages,))],
    )
    return pl.pallas_call(
        functools.partial(_sample_attend_body, layer=layer, n_pages=n_pages),
        grid_spec=grid_spec,
        out_shape=[jax.ShapeDtypeStruct((db,) + col, F32), jax.ShapeDtypeStruct((db, 1, pw), BF16)],
        compiler_params=pltpu.CompilerParams(dimension_semantics=("arbitrary",), vmem_limit_bytes=VMEM_LIMIT),
        name="sample_attend",
    )(page_table_flat, sel, q4, knew4, vnew4, rel_bias, state, pnew3, cache_k_t, cache_v_t)


def _merge_body(x_ref, attn_ref, pool_ref, ga_ref, gb_ref, wba_ref, wpm_ref, ps_ref, wbp_ref, wout_ref,
                gffn_ref, wr_ref, br_ref, ltri_ref,
                x1_o, h2p_o, ev_o, gv_o, rk_o, cnt_o, carry_scr):
    step = pl.program_id(0)

    @pl.when(step == 0)
    def _():
        carry_scr[...] = jnp.zeros_like(carry_scr)

    a = _dot(attn_ref[...], wba_ref[...])
    pooled = pool_ref[...]
    gw = pooled.shape[1] // POOL_GROUPS
    pm = jnp.concatenate([_dot(pooled[:, g * gw:(g + 1) * gw], wpm_ref[g]) for g in range(POOL_GROUPS)], axis=1)
    bb = _dot((pm * ps_ref[...]).astype(BF16), wbp_ref[...])
    mix = jax.nn.sigmoid(ga_ref[...]) * a + jax.nn.sigmoid(gb_ref[...]) * bb
    x1 = x_ref[...] + _dot(mix.astype(BF16), wout_ref[...])
    x1_o[...] = x1
    h2 = _rms(x1, gffn_ref[...]).astype(BF16)
    half = h2.shape[1] // 2
    lo = lax.shift_right_logical(lax.bitcast_convert_type(h2[:, 0:half].astype(F32), U32), jnp.uint32(16))
    hi = lax.bitcast_convert_type(h2[:, half:].astype(F32), U32) & jnp.uint32(0xFFFF0000)
    h2p_o[...] = hi | lo

    logits = _dot(h2, wr_ref[...]) + br_ref[...]
    lane = lax.broadcasted_iota(I32, (TM, LANES), 1)
    lanef = lane.astype(F32)
    tops, ids = [], []
    l = logits
    for _ in range(TOP_EXPERTS):
        mx = jnp.max(l, axis=1, keepdims=True)
        ix = jnp.min(jnp.where(l == mx, lanef, float(LANES)), axis=1, keepdims=True)
        tops.append(mx)
        ids.append(ix)
        l = jnp.where(lanef == ix, -3e38, l)
    es = [jnp.exp(tv - tops[0]) for tv in tops]
    den = es[0] + es[1] + es[2] + es[3]
    onehot = jnp.zeros((TM, LANES), F32)
    for ix in ids:
        onehot = onehot + jnp.where(lanef == ix, 1.0, 0.0)
    before = _dot(ltri_ref[...], onehot.astype(BF16)) + carry_scr[0:1, :]
    ev = jnp.zeros((TM, LANES), I32)
    gv = jnp.zeros((TM, LANES), F32)
    rk = jnp.zeros((TM, LANES), I32)
    for k in range(TOP_EXPERTS):
        rank = jnp.sum(jnp.where(lanef == ids[k], before, 0.0), axis=1, keepdims=True)
        ev = jnp.where(lane == k, ids[k].astype(I32), ev)
        gv = jnp.where(lane == k, es[k] / den, gv)
        rk = jnp.where(lane == k, rank.astype(I32), rk)
    ev_o[...] = ev
    gv_o[...] = gv
    rk_o[...] = rk
    carry_scr[...] = carry_scr[...] + jnp.sum(onehot, axis=0, keepdims=True)
    cnt_o[...] = carry_scr[...]


def _merge(x_all, attn, pooled, ga, gb, wba, wpm, ps, wbp, wout, gffn, wr, br, ltri):
    nt, d = x_all.shape
    row = lambda width: pl.BlockSpec((TM, width), lambda i: (i, 0))
    consts = [wba, wpm, ps, wbp, wout, gffn, wr, br, ltri]
    return pl.pallas_call(
        _merge_body,
        grid=(nt // TM,),
        in_specs=[row(d), row(attn.shape[1]), row(pooled.shape[1]), row(d), row(d)] + [_full(c.shape) for c in consts],
        out_specs=[row(d), row(d // 2), row(LANES), row(LANES), row(LANES), _full((SUBLANES, LANES))],
        out_shape=[jax.ShapeDtypeStruct((nt, d), F32), jax.ShapeDtypeStruct((nt, d // 2), U32),
                   jax.ShapeDtypeStruct((nt, LANES), I32), jax.ShapeDtypeStruct((nt, LANES), F32),
                   jax.ShapeDtypeStruct((nt, LANES), I32), jax.ShapeDtypeStruct((SUBLANES, LANES), F32)],
        scratch_shapes=[pltpu.VMEM((SUBLANES, LANES), F32)],
        compiler_params=pltpu.CompilerParams(dimension_semantics=("arbitrary",), vmem_limit_bytes=VMEM_LIMIT),
        name="merge_router",
    )(x_all, attn, pooled, ga, gb, *consts)


def _dispatch_body(dest_ref, gend_ref, h2p_ref, xs_hbm, zero_scr, sem, zsem):
    step = pl.program_id(0)

    @pl.when(step == 0)
    def _():
        zero_scr[...] = jnp.zeros_like(zero_scr)

        def fill(e):
            return pltpu.make_async_copy(zero_scr, xs_hbm.at[pl.ds(pl.multiple_of(gend_ref[e] - TS, TS), TS)], zsem)

        def start(e, _):
            @pl.when(gend_ref[e] > gend_ref[e + N_EXPERTS])
            def _():
                fill(e).start()
            return 0
        lax.fori_loop(0, N_EXPERTS, start, 0)

        def wait(e, _):
            @pl.when(gend_ref[e] > gend_ref[e + N_EXPERTS])
            def _():
                fill(e).wait()
            return 0
        lax.fori_loop(0, N_EXPERTS, wait, 0)

        def tail(j):
            return pltpu.make_async_copy(zero_scr, xs_hbm.at[pl.ds(pl.multiple_of(j * TS, TS), TS)], zsem)

        first_unused = gend_ref[N_EXPERTS - 1] // TS
        n_tiles = xs_hbm.shape[0] // TS

        def tail_start(j, _):
            tail(j).start()
            return 0
        lax.fori_loop(first_unused, n_tiles, tail_start, 0)

        def tail_wait(j, _):
            tail(j).wait()
            return 0
        lax.fori_loop(first_unused, n_tiles, tail_wait, 0)

    t0 = step * TM

    def row_copy(r, k):
        d = dest_ref[(t0 + r) * TOP_EXPERTS + k]
        return pltpu.make_async_copy(h2p_ref.at[pl.ds(r, 1)], xs_hbm.at[pl.ds(d, 1)], sem)

    def start(r, _):
        for k in range(TOP_EXPERTS):
            row_copy(r, k).start()
        return 0
    lax.fori_loop(0, TM, start, 0, unroll=DMA_UNROLL)

    for _ in range(TOP_EXPERTS):
        pltpu.make_async_copy(h2p_ref, xs_hbm.at[pl.ds(0, TM)], sem).wait()


def _dispatch(dest_flat, gend, h2p, *, n_slots):
    nt, hw = h2p.shape
    grid_spec = pltpu.PrefetchScalarGridSpec(
        num_scalar_prefetch=2,
        grid=(nt // TM,),
        in_specs=[pl.BlockSpec((TM, hw), lambda i, d, g: (i, 0))],
        out_specs=pl.BlockSpec(memory_space=pl.ANY),
        scratch_shapes=[pltpu.VMEM((TS, hw), U32), pltpu.SemaphoreType.DMA, pltpu.SemaphoreType.DMA],
    )
    return pl.pallas_call(
        _dispatch_body,
        grid_spec=grid_spec,
        out_shape=jax.ShapeDtypeStruct((n_slots, hw), U32),
        compiler_params=pltpu.CompilerParams(dimension_semantics=("arbitrary",)),
        name="dispatch",
    )(dest_flat, gend, h2p)


def _moe_body(te_ref, nu_ref, xs_ref, wgu_ref, bgu_ref, wdn_ref, bdn_ref, ys_o, wgu_b, wdn_b):
    j = pl.program_id(0)
    prev = te_ref[jnp.maximum(j - 1, 0)]
    live = j < nu_ref[0]

    @pl.when(jnp.logical_and(live, jnp.logical_or(j == 0, te_ref[j] != prev)))
    def _():
        wgu_b[...] = wgu_ref[0, 0].astype(BF16)
        wdn_b[...] = wdn_ref[0, 0].astype(BF16)

    @pl.when(live)
    def _():
        words = xs_ref[...]
        x_lo = lax.bitcast_convert_type(lax.shift_left(words, jnp.uint32(16)), F32).astype(BF16)
        x_hi = lax.bitcast_convert_type(words & jnp.uint32(0xFFFF0000), F32).astype(BF16)
        half = words.shape[1]
        gu = _dot(x_lo, wgu_b[0:half, :]) + _dot(x_hi, wgu_b[half:, :]) + bgu_ref[0, 0]
        de = gu.shape[1] // 2
        gate = jnp.minimum(gu[:, 0:de], SWIGLU_LIMIT)
        up = jnp.clip(gu[:, de:], -SWIGLU_LIMIT, SWIGLU_LIMIT)
        act = (up + 1.0) * gate * jax.nn.sigmoid(SWIGLU_ALPHA * gate)
        ys_o[...] = _dot(act.astype(BF16), wdn_b[...]) + bdn_ref[0, 0]

    @pl.when(jnp.logical_not(live))
    def _():
        ys_o[...] = jnp.zeros_like(ys_o)


def _moe(tile_expert, n_used, xs, wgu, bgu, wdn, bdn, *, layer):
    n_slots, hw = xs.shape
    _, ne, d, de2 = wgu.shape
    last = lambda j, nu: jnp.minimum(j, nu[0] - 1)
    per_expert = lambda rows, cols: pl.BlockSpec((1, 1, rows, cols), lambda j, te, nu: (layer, te[j], 0, 0))
    grid_spec = pltpu.PrefetchScalarGridSpec(
        num_scalar_prefetch=2,
        grid=(n_slots // TS,),
        in_specs=[pl.BlockSpec((TS, hw), lambda j, te, nu: (last(j, nu), 0)),
                  per_expert(d, de2), per_expert(1, de2), per_expert(de2 // 2, d), per_expert(1, d)],
        out_specs=pl.BlockSpec((TS, d), lambda j, te, nu: (j, 0)),
        scratch_shapes=[pltpu.VMEM((d, de2), BF16), pltpu.VMEM((de2 // 2, d), BF16)],
    )
    return pl.pallas_call(
        _moe_body,
        grid_spec=grid_spec,
        out_shape=jax.ShapeDtypeStruct((n_slots, d), F32),
        compiler_params=pltpu.CompilerParams(dimension_semantics=("arbitrary",), vmem_limit_bytes=VMEM_LIMIT),
        name="moe_experts",
    )(tile_expert, n_used, xs, wgu, bgu[:, :, None, :], wdn, bdn[:, :, None, :])


def _combine_body(dest_ref, x1_ref, gv_ref, gfin_ref, ys_hbm, y_o, ybuf, sem):
    t0 = pl.program_id(0) * TM

    def row_copy(r, k):
        d = dest_ref[(t0 + r) * TOP_EXPERTS + k]
        return pltpu.make_async_copy(ys_hbm.at[pl.ds(d, 1)], ybuf.at[k, pl.ds(r, 1)], sem)

    def start(r, _):
        for k in range(TOP_EXPERTS):
            row_copy(r, k).start()
        return 0
    lax.fori_loop(0, TM, start, 0, unroll=DMA_UNROLL)

    for k in range(TOP_EXPERTS):
        pltpu.make_async_copy(ys_hbm.at[pl.ds(0, TM)], ybuf.at[k], sem).wait()

    gv = gv_ref[...]
    y = jnp.zeros(x1_ref.shape, F32)
    for k in range(TOP_EXPERTS):
        y = y + ybuf[k] * gv[:, k:k + 1]
    y_o[...] = _rms(x1_ref[...] + y, gfin_ref[...])


def _combine(dest_flat, x1, gv, gfin, ys):
    nt, d = x1.shape
    grid_spec = pltpu.PrefetchScalarGridSpec(
        num_scalar_prefetch=1,
        grid=(nt // TM,),
        in_specs=[pl.BlockSpec((TM, d), lambda i, dref: (i, 0)),
                  pl.BlockSpec((TM, LANES), lambda i, dref: (i, 0)),
                  pl.BlockSpec((1, d), lambda i, dref: (0, 0)),
                  pl.BlockSpec(memory_space=pl.ANY)],
        out_specs=pl.BlockSpec((TM, d), lambda i, dref: (i, 0)),
        scratch_shapes=[pltpu.VMEM((TOP_EXPERTS, TM, d), F32), pltpu.SemaphoreType.DMA],
    )
    return pl.pallas_call(
        _combine_body,
        grid_spec=grid_spec,
        out_shape=jax.ShapeDtypeStruct((nt, d), F32),
        compiler_params=pltpu.CompilerParams(dimension_semantics=("arbitrary",)),
        name="combine_norm",
    )(dest_flat, x1, gv, gfin, ys)


def _tri_constants():
    r = lax.broadcasted_iota(I32, (TQ, TQ), 0)
    c = lax.broadcasted_iota(I32, (TQ, TQ), 1)
    incl = (r <= c).astype(BF16)
    u2 = jnp.concatenate([incl, jnp.ones((TQ, LANES), BF16)], axis=1)
    ltri = (c < r).astype(BF16)
    return u2, ltri


def kernel(x_prompt, x_sample, cache_k, cache_v, cache_kidx, state_pool, page_table, meta_tokens, rel_bias, g_mix, w_in, w_pool_mix, pool_scale, w_br_attn, w_br_pool, w_out, g_ffn, w_router, b_router, w_gate_up, b_gate_up, w_down, b_down, g_final):
    nb, seq, d = x_prompt.shape
    db, dec_seq, _ = x_sample.shape
    assert dec_seq == 1, "one new token per sample"
    depth = w_in.shape[0]
    assert depth == 1, "single-layer stack: the combine kernel applies the final norm"
    n_pages = page_table.shape[1]
    assert n_pages % SUBLANES == 0, "the sample top-k lays the cached keys out as eight equal page segments"
    past = n_pages * PAGE_SIZE
    pw = state_pool.shape[-1]
    l_seq = seq + N_META
    lp = _round_up(l_seq, TQ)
    n_prompt = nb * lp
    n_sample = _round_up(db, TM)
    nt = n_prompt + n_sample
    ksel_p = min(TOPK_MAX, seq // 4)
    ksel_s = min(TOPK_MAX, (past + dec_seq) // 4)
    n_tiles = (TOP_EXPERTS * nt) // TS + N_EXPERTS
    n_slots = n_tiles * TS

    meta = jnp.broadcast_to(meta_tokens[None].astype(x_prompt.dtype), (nb, N_META, d))
    xp = jnp.concatenate([meta, x_prompt, jnp.zeros((nb, lp - l_seq, d), x_prompt.dtype)], axis=1)
    xs_rows = jnp.concatenate([x_sample[:, 0, :], jnp.zeros((n_sample - db, d), x_sample.dtype)], axis=0)
    x_all = jnp.concatenate([xp.reshape(n_prompt, d), xs_rows], axis=0)

    u2, ltri = _tri_constants()
    page_flat = page_table.reshape(-1).astype(I32)
    rel_bias = rel_bias.astype(F32)
    cache_k_t = jnp.transpose(cache_k, (0, 1, 3, 4, 2))
    cache_v_t = jnp.transpose(cache_v, (0, 1, 3, 4, 2))
    cache_ki_t = jnp.transpose(cache_kidx, (0, 1, 3, 2))

    outs = {name: [] for name in ("k_p", "v_p", "ki_p", "pool_p", "k_s", "v_s", "ki_s", "pool_s")}
    for l in range(depth):
        wl = w_in[l]
        aw = ATTN_WIDTH
        o_ki = 4 * aw
        o_wi = o_ki + IDX_DIM
        o_p = o_wi + N_IDX_HEADS
        o_ga = o_p + pw
        o_gb = o_ga + d
        wa = wl[:, 0:o_ki].astype(BF16)
        wki = wl[:, o_ki:o_wi]
        wki2 = jnp.concatenate([wki, wki], axis=1).astype(BF16)
        wwi = jnp.pad(wl[:, o_wi:o_p], ((0, 0), (0, LANES - N_IDX_HEADS))).astype(BF16)
        wp = wl[:, o_p:o_ga].astype(BF16)
        wga = wl[:, o_ga:o_gb].astype(BF16)
        wgb = wl[:, o_gb:o_gb + d].astype(BF16)

        (qs, kf, vf, kb, vb, qib, ki2b, kif, wif, pf, ga, gb) = _inproj(
            x_all, g_mix[l][None, :], wa, wki2, wwi, wp, wga, wgb)

        tail_rows = _round_up(l_seq - (lp - TQ), 2 * SUBLANES)
        attn_p = _attn_prompt(qs, qib, wif, kb, vb, ki2b, u2, rel_bias, nb=nb, lp=lp, ksel=ksel_p,
                              tail_rows=tail_rows)
        pooled_p = _pool_prompt(pf, n_rows=n_prompt, lp=lp)

        sl = slice(n_prompt, n_prompt + db)
        keys = _sample_keys(page_flat, qib[sl].reshape(db, N_IDX_HEADS, IDX_DIM), kif[sl, 0:IDX_DIM][:, None, :],
                            wif[sl, 0:N_IDX_HEADS][:, :, None], cache_ki_t, layer=l, n_pages=n_pages)
        sel = _sample_select(keys, past=past, ksel=ksel_s)
        cols = lambda a: a[sl].astype(F32).reshape(db, N_HEADS, HEAD_DIM, 1)
        attn_s, pooled_s = _sample_attend(
            page_flat, sel, cols(qs), cols(kf), cols(vf), rel_bias, state_pool[l], pf[sl][:, None, :],
            cache_k_t, cache_v_t, layer=l, n_pages=n_pages)

        pad_s = lambda a: jnp.concatenate([a, jnp.zeros((n_sample - db, a.shape[1]), a.dtype)], axis=0)
        attn_all = jnp.concatenate([attn_p, pad_s(attn_s.reshape(db, ATTN_WIDTH))], axis=0)
        pooled_all = jnp.concatenate([pooled_p, pad_s(pooled_s[:, 0, :])], axis=0)

        wr = jnp.pad(w_router[l], ((0, 0), (0, LANES - N_EXPERTS))).astype(BF16)
        br = jnp.concatenate([b_router[l].astype(F32), jnp.full((LANES - N_EXPERTS,), NEG, F32)])[None, :]
        x1, h2p, ev, gv, rk, cnt = _merge(
            x_all, attn_all, pooled_all, ga, gb, w_br_attn[l].astype(BF16), w_pool_mix[l].astype(BF16),
            pool_scale[l][None, :], w_br_pool[l].astype(BF16), w_out[l].astype(BF16), g_ffn[l][None, :], wr, br, ltri)

        counts = cnt[0, 0:N_EXPERTS].astype(I32)
        padded = (counts + TS - 1) // TS * TS
        gend = jnp.cumsum(padded)
        gstart = gend - padded
        experts = jnp.arange(N_EXPERTS, dtype=I32)
        ev4 = ev[:, 0:TOP_EXPERTS]
        dest = (jnp.sum(jnp.where(ev4[:, :, None] == experts, gstart, 0), axis=2) + rk[:, 0:TOP_EXPERTS]).reshape(-1)
        tile_first = jnp.arange(n_tiles, dtype=I32)[:, None] * TS
        tile_expert = jnp.minimum(jnp.sum((gend[None, :] <= tile_first).astype(I32), axis=1), N_EXPERTS - 1)
        n_used = (gend[-1] // TS).astype(I32)[None]

        xs_sorted = _dispatch(dest, jnp.concatenate([gend, gstart + counts]).astype(I32), h2p, n_slots=n_slots)
        ys = _moe(tile_expert, n_used, xs_sorted, w_gate_up, b_gate_up, w_down, b_down, layer=l)
        y_all = _combine(dest, x1, gv, g_final[None, :], ys)

        kp = kf[:n_prompt].reshape(nb, lp, N_HEADS, HEAD_DIM)[:, :l_seq]
        vp = vf[:n_prompt].reshape(nb, lp, N_HEADS, HEAD_DIM)[:, :l_seq]
        kip = kif[:n_prompt, 0:IDX_DIM].reshape(nb, lp, IDX_DIM)[:, :l_seq]
        pp = pf[:n_prompt].reshape(nb, lp, pw)
        outs["k_p"].append(kp)
        outs["v_p"].append(vp)
        outs["ki_p"].append(kip)
        outs["pool_p"].append(pp[:, l_seq - POOL_STATE:l_seq])
        outs["k_s"].append(kf[sl].reshape(db, 1, N_HEADS, HEAD_DIM))
        outs["v_s"].append(vf[sl].reshape(db, 1, N_HEADS, HEAD_DIM))
        outs["ki_s"].append(kif[sl, 0:IDX_DIM].reshape(db, 1, IDX_DIM))
        outs["pool_s"].append(jnp.concatenate([state_pool[l][:, 1:], pf[sl][:, None, :]], axis=1))

    y_prompt = y_all[:n_prompt].reshape(nb, lp, d)[:, N_META:l_seq]
    y_sample = y_all[n_prompt:n_prompt + db].reshape(db, 1, d)
    st = lambda name: jnp.stack(outs[name])
    return (y_prompt, y_sample, st("k_p"), st("v_p"), st("ki_p"), st("pool_p"),
            st("k_s"), st("v_s"), st("ki_s"), st("pool_s"))
```

```python
import functools
import math

import jax
import jax.numpy as jnp
from jax import lax
from jax.experimental import pallas as pl
from jax.experimental.pallas import tpu as pltpu

F32 = jnp.float32
BF16 = jnp.bfloat16
I32 = jnp.int32
I16 = jnp.int16
U32 = jnp.uint32

N_META = 16
N_HEADS = 8
HEAD_DIM = 64
ATTN_WIDTH = N_HEADS * HEAD_DIM
N_IDX_HEADS = 8
IDX_DIM = 64
TOPK_MAX = 256
N_BUCKETS = 32
MAX_DISTANCE = 128
POOL_WINDOWS = (2, 4, 8, 16)
POOL_GROUPS = 4
POOL_STATE = 15
N_EXPERTS = 32
TOP_EXPERTS = 4
SWIGLU_LIMIT = 7.0
SWIGLU_ALPHA = 1.702
PAGE_SIZE = 128
EPS = 1e-6

LANES = 128
SUBLANES = 8
MXU_DIM = 256
TQ = MXU_DIM
TM = MXU_DIM
TS = MXU_DIM
VMEM_LIMIT = 56 * 1024 * 1024
DMA_UNROLL = 8

NEG = -1e30
INT_MIN = -(2 ** 31)
I16_MIN = -(2 ** 15)

_MAX_EXACT = N_BUCKETS // 2
_BUCKET_THRESHOLDS = tuple(
    math.ceil(_MAX_EXACT * (MAX_DISTANCE / _MAX_EXACT) ** (j / (N_BUCKETS - _MAX_EXACT)))
    for j in range(1, N_BUCKETS - _MAX_EXACT))


def _round_up(a, m):
    return (a + m - 1) // m * m


def _rms(x, g):
    return x * lax.rsqrt(jnp.mean(x * x, axis=-1, keepdims=True) + EPS) * g


def _dot(a, b):
    return jnp.dot(a, b, preferred_element_type=F32)


def _dot_nt(a, b):
    return lax.dot_general(a, b, (((1,), (1,)), ((), ())), preferred_element_type=F32)


def _bucket(dist):
    large = jnp.full(dist.shape, _MAX_EXACT, I32)
    for thr in _BUCKET_THRESHOLDS:
        large = large + jnp.where(dist >= thr, 1, 0)
    return jnp.where(dist < _MAX_EXACT, dist, large)


def _sort_key(s):
    s = jnp.where(s == 0.0, 0.0, s)
    bits = lax.bitcast_convert_type(s, I32)
    return jnp.where(bits >= 0, bits, bits ^ jnp.int32(0x7FFFFFFF))


def _full(shape):
    return pl.BlockSpec(shape, lambda *_: (0,) * len(shape))


def _inproj_body(x_ref, g_ref, wa_ref, wki_ref, wwi_ref, wp_ref, wga_ref, wgb_ref,
                 qs_o, kf_o, vf_o, kb_o, vb_o, qib_o, ki2b_o, kif_o, wif_o, pf_o, ga_o, gb_o):
    h = _rms(x_ref[...], g_ref[...]).astype(BF16)
    za = _dot(h, wa_ref[...])
    w = ATTN_WIDTH
    qs_o[...] = (za[:, 0:w] * (HEAD_DIM ** -0.5)).astype(BF16)
    k = za[:, w:2 * w]
    v = za[:, 2 * w:3 * w]
    kf_o[...] = k
    vf_o[...] = v
    kb_o[...] = k.astype(BF16)
    vb_o[...] = v.astype(BF16)
    qib_o[...] = za[:, 3 * w:4 * w].astype(BF16)
    ki2 = _dot(h, wki_ref[...])
    kif_o[...] = ki2
    ki2b_o[...] = ki2.astype(BF16)
    wif_o[...] = _dot(h, wwi_ref[...])
    pf_o[...] = _dot(h, wp_ref[...])
    ga_o[...] = _dot(h, wga_ref[...])
    gb_o[...] = _dot(h, wgb_ref[...])


def _inproj(x_all, g, wa, wki2, wwi, wp, wga, wgb):
    nt, d = x_all.shape
    pw = wp.shape[1]
    row = lambda width: pl.BlockSpec((TM, width), lambda i: (i, 0))
    outs = [
        (ATTN_WIDTH, BF16), (ATTN_WIDTH, F32), (ATTN_WIDTH, F32), (ATTN_WIDTH, BF16), (ATTN_WIDTH, BF16),
        (ATTN_WIDTH, BF16), (LANES, BF16), (LANES, F32), (LANES, F32), (pw, F32), (d, F32), (d, F32)]
    return pl.pallas_call(
        _inproj_body,
        grid=(nt // TM,),
        in_specs=[row(d), _full((1, d)), _full(wa.shape), _full(wki2.shape), _full(wwi.shape),
                  _full(wp.shape), _full(wga.shape), _full(wgb.shape)],
        out_specs=[row(wd) for wd, _ in outs],
        out_shape=[jax.ShapeDtypeStruct((nt, wd), dt) for wd, dt in outs],
        compiler_params=pltpu.CompilerParams(dimension_semantics=("arbitrary",), vmem_limit_bytes=VMEM_LIMIT),
        name="inproj",
    )(x_all, g, wa, wki2, wwi, wp, wga, wgb)


def _attn_prompt_body(qs_ref, qib_ref, wif_ref, kb_ref, vb_ref, ki2b_ref, u2_ref, rb_ref,
                      o_ref, qis_scr, wb_scr, key_scr, khi_scr, klo_scr, ntab_scr, qz_scr, m_scr, acc_scr, sum_scr,
                      *, ksel, tail_rows):
    b = pl.program_id(0)
    i = pl.program_id(1)
    nchunk = i + 1
    n_far = jnp.maximum(i - 1, 0)

    @pl.when((b == 0) & (i == 0))
    def _():
        def slab_rows(s, _):
            r0 = pl.multiple_of(s * SUBLANES, SUBLANES)
            r = r0 + lax.broadcasted_iota(I32, (SUBLANES, 2 * TQ), 0)
            x = lax.broadcasted_iota(I32, (SUBLANES, 2 * TQ), 1)
            bucket = _bucket(jnp.maximum(r + TQ - x, 0))
            accs = [jnp.zeros((SUBLANES, 2 * TQ), F32) for _ in range(N_HEADS)]
            for j in range(N_BUCKETS - 1):
                m = bucket == j
                accs = [jnp.where(m, rb_ref[j, h] - rb_ref[N_BUCKETS - 1, h], accs[h]) for h in range(N_HEADS)]
            for h in range(N_HEADS):
                ntab_scr[h, 0, pl.ds(r0, SUBLANES), :] = accs[h][:, 0:TQ]
                ntab_scr[h, 1, pl.ds(r0, SUBLANES), :] = accs[h][:, TQ:2 * TQ]
            return 0
        lax.fori_loop(0, TQ // SUBLANES, slab_rows, 0)

    def two(x):
        return jnp.concatenate([x, x], axis=1)

    def block(nr):
        hr = nr // 2
        lane = lax.broadcasted_iota(I32, (nr, LANES), 1)
        lo_half = lane < HEAD_DIM

        for h in range(N_HEADS):
            cols = slice((h // 2) * LANES, (h // 2 + 1) * LANES)
            keep = lo_half if h % 2 == 0 else jnp.logical_not(lo_half)
            qis_scr[h, 0:nr] = jnp.where(keep, qib_ref[0:nr, cols].astype(F32), 0.0).astype(BF16)
            qz_scr[h, 0:nr] = jnp.where(keep, qs_ref[0:nr, cols].astype(F32), 0.0).astype(BF16)
            wb_scr[h, 0:nr] = jnp.broadcast_to(wif_ref[0:nr, h:h + 1], (nr, LANES))
            m_scr[h, 0:nr] = jnp.full((nr, LANES), NEG, F32)
            acc_scr[h, 0:nr] = jnp.zeros((nr, LANES), F32)
            sum_scr[h, 0:nr] = jnp.zeros((nr, LANES), F32)

        dmat = lax.broadcasted_iota(I32, (nr, TQ), 1) - lax.broadcasted_iota(I32, (nr, TQ), 0)

        def score_chunk(c, _):
            kc = ki2b_ref[pl.ds(pl.multiple_of(c * TQ, TQ), TQ), :]
            s = jnp.zeros((nr, TQ), F32)
            for h in range(N_IDX_HEADS):
                s = s + two(wb_scr[h, 0:nr]) * jnp.maximum(_dot_nt(qis_scr[h, 0:nr], kc), 0.0)
            key = jnp.where(dmat <= (i - c) * TQ, _sort_key(s), INT_MIN)
            key_scr[c, 0:nr] = key
            khi_scr[c, 0:nr] = lax.shift_right_arithmetic(key, 16).astype(I16)
            klo_scr[c, 0:nr] = ((key & 0xFFFF) + I16_MIN).astype(I16)
            return 0
        lax.fori_loop(0, nchunk, score_chunk, 0)

        gr = hr if hr % (2 * SUBLANES) == 0 else nr
        one16 = jnp.ones((), I16)
        zero16 = jnp.zeros((), I16)

        def count16(ref, r0, pred):
            def body(c, acc):
                hit = jnp.where(pred(ref[c, r0:r0 + gr, :]), one16, zero16)
                return acc + (hit[:, 0:LANES] + hit[:, LANES:2 * LANES])
            acc = lax.fori_loop(0, nchunk, body, jnp.zeros((gr, LANES), I16))
            return jnp.broadcast_to(jnp.sum(acc.astype(F32), axis=1, keepdims=True), (gr, LANES))

        def search16(ref, r0, target):
            def bit_step(step, t):
                cand = t + lax.shift_left(jnp.int32(1), 15 - step)
                cand2 = two(cand.astype(I16))
                return jnp.where(count16(ref, r0, lambda k: k >= cand2) >= target, cand, t)
            return lax.fori_loop(0, 16, bit_step, jnp.full((gr, LANES), I16_MIN, I32))

        ts, needs = [], []
        for r0 in range(0, nr, gr):
            rows = slice(r0, r0 + gr)
            t_hi = search16(khi_scr, r0, ksel)
            th2 = two(t_hi.astype(I16))
            n_above = count16(khi_scr, r0, lambda k: k > th2)

            def keep_bucket(c, _, rows=rows, th2=th2):
                klo_scr[c, rows] = jnp.where(khi_scr[c, rows] == th2, klo_scr[c, rows], jnp.full((), I16_MIN, I16))
                return 0
            lax.fori_loop(0, nchunk, keep_bucket, 0)
            t_lo = search16(klo_scr, r0, ksel - n_above)
            tl2 = two(t_lo.astype(I16))
            n_gt = n_above + count16(klo_scr, r0, lambda k: k > tl2)
            t = lax.shift_left(t_hi, 16) | ((t_lo - I16_MIN) & 0xFFFF)
            ts.append(t)
            needs.append(jnp.where(t == INT_MIN, 0.0, ksel - n_gt))
        t2 = two(jnp.concatenate(ts, axis=0))
        need2 = two(jnp.concatenate(needs, axis=0))

        def mask_chunk(c, carry):
            kc = key_scr[c, 0:nr]
            eq = kc == t2
            pre = _dot(jnp.where(eq, 1.0, 0.0).astype(BF16), u2_ref[...])
            prefix = pre[:, 0:TQ] + two(carry)
            tie_ok = jnp.where(eq, prefix, 3e38) <= need2
            madd = jnp.where(kc > t2, 0.0, jnp.where(tie_ok, 0.0, NEG))
            key_scr[c, 0:nr] = lax.bitcast_convert_type(madd, I32)
            return carry + pre[:, TQ:TQ + LANES]
        lax.fori_loop(0, nchunk, mask_chunk, jnp.zeros((nr, LANES), F32))

        def chunk_logits(c, slab):
            rows = pl.ds(pl.multiple_of(c * TQ, TQ), TQ)
            madd = lax.bitcast_convert_type(key_scr[c, 0:nr], F32)
            out = []
            for hp in range(N_HEADS // 2):
                kc = kb_ref[rows, hp * LANES:(hp + 1) * LANES]
                for h in (2 * hp, 2 * hp + 1):
                    l = _dot_nt(qz_scr[h, 0:nr], kc) + madd
                    out.append(l if slab is None else l + ntab_scr[h, slab, 0:nr])
            return rows, out

        def row_max(c, slab):
            _, ls = chunk_logits(c, slab)
            for h, l in enumerate(ls):
                m_scr[h, 0:nr] = jnp.maximum(m_scr[h, 0:nr], jnp.maximum(l[:, 0:LANES], l[:, LANES:2 * LANES]))

        def accumulate(c, slab):
            rows, ls = chunk_logits(c, slab)
            for h, l in enumerate(ls):
                p = jnp.exp(l - two(m_scr[h, 0:nr]))
                vc = vb_ref[rows, (h // 2) * LANES:(h // 2 + 1) * LANES]
                acc_scr[h, 0:nr] = acc_scr[h, 0:nr] + _dot(p.astype(BF16), vc)
                sum_scr[h, 0:nr] = sum_scr[h, 0:nr] + (p[:, 0:LANES] + p[:, LANES:2 * LANES])

        def sweep(step):
            def far(c, _):
                step(c, None)
                return 0
            lax.fori_loop(0, n_far, far, 0)

            @pl.when(i >= 1)
            def _():
                step(i - 1, 0)
            step(i, 1)

        sweep(row_max)
        for h in range(N_HEADS):
            m_scr[h, 0:nr] = jnp.broadcast_to(jnp.max(m_scr[h, 0:nr], axis=1, keepdims=True), (nr, LANES))
        sweep(accumulate)
        for hp in range(N_HEADS // 2):
            outs = [acc_scr[h, 0:nr] / jnp.sum(sum_scr[h, 0:nr], axis=1, keepdims=True) for h in (2 * hp, 2 * hp + 1)]
            o_ref[0:nr, hp * LANES:(hp + 1) * LANES] = jnp.where(lo_half, outs[0], outs[1]).astype(BF16)
        if nr < TQ:
            o_ref[nr:TQ, :] = jnp.zeros((TQ - nr, ATTN_WIDTH), BF16)

    if tail_rows == TQ:
        block(TQ)
    else:
        last = pl.num_programs(1) - 1
        pl.when(i < last)(lambda: block(TQ))
        pl.when(i == last)(lambda: block(tail_rows))


def _attn_prompt(qs, qib, wif, kb, vb, ki2b, u2, rel_bias, *, nb, lp, ksel, tail_rows):
    nq = lp // TQ
    rowq = lambda width: pl.BlockSpec((TQ, width), lambda b, i: (b * nq + i, 0))
    seq = lambda width: pl.BlockSpec((lp, width), lambda b, i: (b, 0))
    per_head = lambda dt: pltpu.VMEM((N_HEADS, TQ, LANES), dt)
    return pl.pallas_call(
        functools.partial(_attn_prompt_body, ksel=float(ksel), tail_rows=tail_rows),
        grid=(nb, nq),
        in_specs=[rowq(ATTN_WIDTH), rowq(ATTN_WIDTH), rowq(LANES), seq(ATTN_WIDTH), seq(ATTN_WIDTH), seq(LANES),
                  _full(u2.shape), pl.BlockSpec(memory_space=pltpu.SMEM)],
        out_specs=rowq(ATTN_WIDTH),
        out_shape=jax.ShapeDtypeStruct((nb * lp, ATTN_WIDTH), BF16),
        scratch_shapes=[
            per_head(BF16),
            per_head(F32),
            pltpu.VMEM((nq, TQ, TQ), I32),
            pltpu.VMEM((nq, TQ, TQ), I16),
            pltpu.VMEM((nq, TQ, TQ), I16),
            pltpu.VMEM((N_HEADS, 2, TQ, TQ), F32),
            per_head(BF16),
            per_head(F32), per_head(F32), per_head(F32),
        ],
        compiler_params=pltpu.CompilerParams(dimension_semantics=("arbitrary", "arbitrary"),
                                             vmem_limit_bytes=VMEM_LIMIT),
        name="attn_prompt",
    )(qs, qib, wif, kb, vb, ki2b, u2, rel_bias)


def _pool_prompt_body(p_ref, halo_ref, o_ref, ext_scr, *, tiles_per_seq):
    j = pl.program_id(0) % tiles_per_seq
    halo = 2 * SUBLANES
    p = p_ref[...]
    ext_scr[0:halo, :] = jnp.where(j == 0, 0.0, halo_ref[...])
    ext_scr[halo:halo + TM, :] = p
    pos = j * TM + lax.broadcasted_iota(I32, (TM, 1), 0)
    gw = p.shape[1] // POOL_GROUPS
    for g, w in enumerate(POOL_WINDOWS):
        cols = slice(g * gw, (g + 1) * gw)
        s = p[:, cols]
        for back in range(1, w):
            s = s + ext_scr[halo - back:halo - back + TM, cols]
        cnt = jnp.minimum(pos + 1, w).astype(F32)
        o_ref[:, cols] = (s / cnt - p[:, cols]).astype(BF16)


def _pool_prompt(pf, *, n_rows, lp):
    pw = pf.shape[1]
    halo = 2 * SUBLANES
    return pl.pallas_call(
        functools.partial(_pool_prompt_body, tiles_per_seq=lp // TM),
        grid=(n_rows // TM,),
        in_specs=[pl.BlockSpec((TM, pw), lambda i: (i, 0)),
                  pl.BlockSpec((halo, pw), lambda i: (jnp.maximum(i * (TM // halo) - 1, 0), 0))],
        out_specs=pl.BlockSpec((TM, pw), lambda i: (i, 0)),
        out_shape=jax.ShapeDtypeStruct((n_rows, pw), BF16),
        scratch_shapes=[pltpu.VMEM((halo + TM, pw), F32)],
        compiler_params=pltpu.CompilerParams(dimension_semantics=("arbitrary",)),
        name="pool_prompt",
    )(pf, pf)


def _sample_keys_body(pt_ref, qi_ref, kinew_ref, wi_ref, cache_hbm, key_o, kibuf, sem, *, layer, n_pages):
    s = pl.program_id(0)
    past = n_pages * PAGE_SIZE
    seg_w = past // SUBLANES

    copies = [pltpu.make_async_copy(cache_hbm.at[layer, pt_ref[s * n_pages + pg]],
                                    kibuf.at[:, pg * PAGE_SIZE:(pg + 1) * PAGE_SIZE], sem) for pg in range(n_pages)]
    for cp in copies:
        cp.start()
    for cp in copies:
        cp.wait()

    qi = qi_ref[0]
    wcol = wi_ref[0]
    d = _dot(qi, kibuf[...].astype(BF16))
    sc = jnp.sum(wcol * jnp.maximum(d, 0.0), axis=0, keepdims=True)
    knew = kinew_ref[0].astype(BF16).astype(F32)
    dn = jnp.sum(qi.astype(F32) * knew, axis=1, keepdims=True)
    sn = jnp.sum(wcol * jnp.maximum(dn, 0.0), axis=0, keepdims=True)
    key = _sort_key(sc)
    lane = lax.broadcasted_iota(I32, (1, LANES), 1)
    new_tail = jnp.where(lane == 0, _sort_key(jnp.broadcast_to(sn, (1, LANES))), INT_MIN)
    for g in range(SUBLANES):
        key_o[0, g:g + 1, 0:seg_w] = key[:, g * seg_w:(g + 1) * seg_w]
        key_o[0, g:g + 1, seg_w:seg_w + LANES] = new_tail if g == 0 else jnp.full((1, LANES), INT_MIN, I32)


def _sample_keys(page_table_flat, qi3, kinew3, wi3, cache_ki_t, *, layer, n_pages):
    db = qi3.shape[0]
    past = n_pages * PAGE_SIZE
    kw = past // SUBLANES + LANES
    grid_spec = pltpu.PrefetchScalarGridSpec(
        num_scalar_prefetch=1,
        grid=(db,),
        in_specs=[pl.BlockSpec((1, N_IDX_HEADS, IDX_DIM), lambda s, pt: (s, 0, 0)),
                  pl.BlockSpec((1, 1, IDX_DIM), lambda s, pt: (s, 0, 0)),
                  pl.BlockSpec((1, N_IDX_HEADS, 1), lambda s, pt: (s, 0, 0)),
                  pl.BlockSpec(memory_space=pl.ANY)],
        out_specs=pl.BlockSpec((1, SUBLANES, kw), lambda s, pt: (s, 0, 0)),
        scratch_shapes=[pltpu.VMEM((IDX_DIM, past), F32), pltpu.SemaphoreType.DMA],
    )
    return pl.pallas_call(
        functools.partial(_sample_keys_body, layer=layer, n_pages=n_pages),
        grid_spec=grid_spec,
        out_shape=jax.ShapeDtypeStruct((db, SUBLANES, kw), I32),
        compiler_params=pltpu.CompilerParams(dimension_semantics=("arbitrary",)),
        name="sample_keys",
    )(page_table_flat, qi3, kinew3, wi3, cache_ki_t)


def _sample_select_body(key_ref, sel_o, *, past, ksel):
    keys = key_ref[...]
    kw = keys.shape[2]
    seg_w = kw - LANES
    row = lax.broadcasted_iota(I32, (SUBLANES, kw), 0)
    lane = lax.broadcasted_iota(I32, (SUBLANES, kw), 1)
    pos = jnp.where(lane < seg_w, row * seg_w + lane, past + (lane - seg_w) + row * LANES)[None]

    def cnt(pred):
        x = jnp.where(pred, 1.0, 0.0)
        return jnp.sum(jnp.sum(x, axis=2, keepdims=True), axis=1, keepdims=True)

    def bit_step(step, t):
        cand = t + lax.shift_left(jnp.int32(1), 31 - step)
        return jnp.where(cnt(keys >= cand) >= ksel, cand, t)
    t = lax.fori_loop(0, 32, bit_step, jnp.full((keys.shape[0], 1, 1), INT_MIN, I32))
    gt = keys > t
    eq = keys == t
    need = ksel - cnt(gt)

    nbits = (past + SUBLANES * LANES).bit_length()

    def cut_step(step, c):
        cand = c - lax.shift_left(jnp.int32(1), nbits - 1 - step)
        ok = cnt(jnp.logical_and(eq, pos <= cand)) >= need
        return jnp.where(ok, cand, c)
    cut = lax.fori_loop(0, nbits, cut_step, jnp.full((keys.shape[0], 1, 1), 2 ** nbits - 1, I32))
    sel = jnp.where(gt, 1.0, jnp.where(jnp.logical_and(eq, pos <= cut), 1.0, 0.0))
    sel_o[...] = jnp.where(pos <= past, sel, 0.0)


def _sample_select(keys, *, past, ksel):
    return pl.pallas_call(
        functools.partial(_sample_select_body, past=past, ksel=float(ksel)),
        out_shape=jax.ShapeDtypeStruct(keys.shape, F32),
        name="sample_select",
    )(keys)


def _sample_attend_body(pt_ref, sel_ref, q_ref, knew_ref, vnew_ref, rb_ref, state_ref, pnew_ref, ck_hbm, cv_hbm,
                        attn_o, pool_o, kbuf, vbuf, qb_scr, lg_scr, acc_scr, ksem, vsem, *, layer, n_pages):
    s = pl.program_id(0)
    past = n_pages * PAGE_SIZE
    seg_w = past // SUBLANES
    pages_per_seg = n_pages // SUBLANES
    last_bias = [rb_ref[N_BUCKETS - 1, h] for h in range(N_HEADS)]

    def kcopy(pg):
        return pltpu.make_async_copy(ck_hbm.at[layer, pt_ref[s * n_pages + pg]], kbuf.at[pg], ksem.at[pg])

    def vcopy(pg):
        return pltpu.make_async_copy(cv_hbm.at[layer, pt_ref[s * n_pages + pg]], vbuf.at[pg], vsem.at[pg])

    def start(pg, _):
        kcopy(pg).start()
        vcopy(pg).start()
        return 0
    lax.fori_loop(0, n_pages, start, 0)

    for h in range(N_HEADS):
        qb_scr[h] = jnp.broadcast_to(q_ref[0, h], (HEAD_DIM, LANES))
        acc_scr[h] = jnp.zeros((HEAD_DIM, LANES), F32)

    def k_page(pg, _):
        kcopy(pg).wait()
        rows = [jnp.sum(kbuf[pg, h].astype(BF16).astype(F32) * qb_scr[h], axis=0, keepdims=True)
                for h in range(N_HEADS)]
        lg_scr[pg] = jnp.concatenate(rows, axis=0)
        return 0
    lax.fori_loop(0, n_pages, k_page, 0)

    lane = lax.broadcasted_iota(I32, (1, LANES), 1)
    bucket = _bucket(past - ((n_pages - 1) * PAGE_SIZE + lane))
    near_rows = []
    for h in range(N_HEADS):
        r = jnp.zeros((1, LANES), F32)
        for j in range(N_BUCKETS - 1):
            r = jnp.where(bucket == j, rb_ref[j, h] - last_bias[h], r)
        near_rows.append(r)
    near = jnp.concatenate(near_rows, axis=0)

    sel = sel_ref[0]
    m = jnp.full((N_HEADS, LANES), NEG, F32)
    for pg in range(n_pages):
        g, j = divmod(pg, pages_per_seg)
        keep = sel[g:g + 1, j * PAGE_SIZE:(j + 1) * PAGE_SIZE] > 0.0
        l = lg_scr[pg] + jnp.where(keep, 0.0, NEG)
        if pg == n_pages - 1:
            l = l + near
        lg_scr[pg] = l
        m = jnp.maximum(m, l)
    new_rows = []
    for h in range(N_HEADS):
        kn = knew_ref[0, h].astype(BF16).astype(F32)
        new_rows.append(jnp.sum(q_ref[0, h] * kn, axis=0, keepdims=True) + (rb_ref[0, h] - last_bias[h]))
    l_new = jnp.where(sel[0:1, seg_w:seg_w + 1] > 0.0, jnp.concatenate(new_rows, axis=0), NEG)
    m = jnp.maximum(jnp.max(m, axis=1, keepdims=True), l_new)
    ssum = jnp.zeros((N_HEADS, LANES), F32)
    for pg in range(n_pages):
        p = jnp.exp(lg_scr[pg] - m)
        lg_scr[pg] = p
        ssum = ssum + p
    p_new = jnp.exp(l_new - m)
    den = jnp.sum(ssum, axis=1, keepdims=True) + p_new
    for pg in range(n_pages):
        lg_scr[pg] = (lg_scr[pg] / den).astype(BF16).astype(F32)
    p_new = (p_new / den).astype(BF16).astype(F32)

    def v_page(pg, _):
        vcopy(pg).wait()
        p = lg_scr[pg]
        for h in range(N_HEADS):
            acc_scr[h] = acc_scr[h] + vbuf[pg, h].astype(BF16).astype(F32) * p[h:h + 1, :]
        return 0
    lax.fori_loop(0, n_pages, v_page, 0)
    for h in range(N_HEADS):
        vn = vnew_ref[0, h].astype(BF16).astype(F32)
        attn_o[0, h] = jnp.sum(acc_scr[h], axis=1, keepdims=True) + p_new[h:h + 1, :] * vn

    pnew = pnew_ref[0]
    st = state_ref[0]
    gw = pnew.shape[1] // POOL_GROUPS
    for g, w in enumerate(POOL_WINDOWS):
        cols = slice(g * gw, (g + 1) * gw)
        acc = pnew[:, cols]
        for back in range(1, w):
            acc = acc + st[POOL_STATE - back:POOL_STATE - back + 1, cols]
        pool_o[0, :, cols] = (acc / float(min(POOL_STATE + 1, w)) - pnew[:, cols]).astype(BF16)


def _sample_attend(page_table_flat, sel, q4, knew4, vnew4, rel_bias, state, pnew3, cache_k_t, cache_v_t, *, layer, n_pages):
    db = q4.shape[0]
    pw = pnew3.shape[2]
    per = lambda shape: pl.BlockSpec((1,) + shape, lambda s, pt: (s,) + (0,) * len(shape))
    col = (N_HEADS, HEAD_DIM, 1)
    page = (n_pages, N_HEADS, HEAD_DIM, LANES)
    grid_spec = pltpu.PrefetchScalarGridSpec(
        num_scalar_prefetch=1,
        grid=(db,),
        in_specs=[per(sel.shape[1:]), per(col), per(col), per(col), pl.BlockSpec(memory_space=pltpu.SMEM),
                  per((POOL_STATE, pw)), per((1, pw)),
                  pl.BlockSpec(memory_space=pl.ANY), pl.BlockSpec(memory_space=pl.ANY)],
        out_specs=[per(col), per((1, pw))],
        scratch_shapes=[pltpu.VMEM(page, F32), pltpu.VMEM(page, F32),
                        pltpu.VMEM((N_HEADS, HEAD_DIM, LANES), F32),
                        pltpu.VMEM((n_pages, N_HEADS, LANES), F32),
                        pltpu.VMEM((N_HEADS, HEAD_DIM, LANES), F32),
                        pltpu.SemaphoreType.DMA((n_pages,)), pltpu.SemaphoreType.DMA((n_pages,))],
    )
    return pl.pallas_call(
        functools.partial(_sample_attend_body, layer=layer, n_pages=n_pages),
        grid_spec=grid_spec,
        out_shape=[jax.ShapeDtypeStruct((db,) + col, F32), jax.ShapeDtypeStruct((db, 1, pw), BF16)],
        compiler_params=pltpu.CompilerParams(dimension_semantics=("arbitrary",), vmem_limit_bytes=VMEM_LIMIT),
        name="sample_attend",
    )(page_table_flat, sel, q4, knew4, vnew4, rel_bias, state, pnew3, cache_k_t, cache_v_t)


def _merge_body(x_ref, attn_ref, pool_ref, ga_ref, gb_ref, wba_ref, wpm_ref, ps_ref, wbp_ref, wout_ref,
                gffn_ref, wr_ref, br_ref, ltri_ref,
                x1_o, h2p_o, ev_o, gv_o, rk_o, cnt_o, carry_scr):
    step = pl.program_id(0)

    @pl.when(step == 0)
    def _():
        carry_scr[...] = jnp.zeros_like(carry_scr)

    a = _dot(attn_ref[...], wba_ref[...])
    pooled = pool_ref[...]
    gw = pooled.shape[1] // POOL_GROUPS
    pm = jnp.concatenate([_dot(pooled[:, g * gw:(g + 1) * gw], wpm_ref[g]) for g in range(POOL_GROUPS)], axis=1)
    bb = _dot((pm * ps_ref[...]).astype(BF16), wbp_ref[...])
    mix = jax.nn.sigmoid(ga_ref[...]) * a + jax.nn.sigmoid(gb_ref[...]) * bb
    x1 = x_ref[...] + _dot(mix.astype(BF16), wout_ref[...])
    x1_o[...] = x1
    h2 = _rms(x1, gffn_ref[...]).astype(BF16)
    half = h2.shape[1] // 2
    lo = lax.shift_right_logical(lax.bitcast_convert_type(h2[:, 0:half].astype(F32), U32), jnp.uint32(16))
    hi = lax.bitcast_convert_type(h2[:, half:].astype(F32), U32) & jnp.uint32(0xFFFF0000)
    h2p_o[...] = hi | lo

    logits = _dot(h2, wr_ref[...]) + br_ref[...]
    lane = lax.broadcasted_iota(I32, (TM, LANES), 1)
    lanef = lane.astype(F32)
    tops, ids = [], []
    l = logits
    for _ in range(TOP_EXPERTS):
        mx = jnp.max(l, axis=1, keepdims=True)
        ix = jnp.min(jnp.where(l == mx, lanef, float(LANES)), axis=1, keepdims=True)
        tops.append(mx)
        ids.append(ix)
        l = jnp.where(lanef == ix, -3e38, l)
    es = [jnp.exp(tv - tops[0]) for tv in tops]
    den = es[0] + es[1] + es[2] + es[3]
    onehot = jnp.zeros((TM, LANES), F32)
    for ix in ids:
        onehot = onehot + jnp.where(lanef == ix, 1.0, 0.0)
    before = _dot(ltri_ref[...], onehot.astype(BF16)) + carry_scr[0:1, :]
    ev = jnp.zeros((TM, LANES), I32)
    gv = jnp.zeros((TM, LANES), F32)
    rk = jnp.zeros((TM, LANES), I32)
    for k in range(TOP_EXPERTS):
        rank = jnp.sum(jnp.where(lanef == ids[k], before, 0.0), axis=1, keepdims=True)
        ev = jnp.where(lane == k, ids[k].astype(I32), ev)
        gv = jnp.where(lane == k, es[k] / den, gv)
        rk = jnp.where(lane == k, rank.astype(I32), rk)
    ev_o[...] = ev
    gv_o[...] = gv
    rk_o[...] = rk
    carry_scr[...] = carry_scr[...] + jnp.sum(onehot, axis=0, keepdims=True)
    cnt_o[...] = carry_scr[...]


def _merge(x_all, attn, pooled, ga, gb, wba, wpm, ps, wbp, wout, gffn, wr, br, ltri):
    nt, d = x_all.shape
    row = lambda width: pl.BlockSpec((TM, width), lambda i: (i, 0))
    consts = [wba, wpm, ps, wbp, wout, gffn, wr, br, ltri]
    return pl.pallas_call(
        _merge_body,
        grid=(nt // TM,),
        in_specs=[row(d), row(attn.shape[1]), row(pooled.shape[1]), row(d), row(d)] + [_full(c.shape) for c in consts],
        out_specs=[row(d), row(d // 2), row(LANES), row(LANES), row(LANES), _full((SUBLANES, LANES))],
        out_shape=[jax.ShapeDtypeStruct((nt, d), F32), jax.ShapeDtypeStruct((nt, d // 2), U32),
                   jax.ShapeDtypeStruct((nt, LANES), I32), jax.ShapeDtypeStruct((nt, LANES), F32),
                   jax.ShapeDtypeStruct((nt, LANES), I32), jax.ShapeDtypeStruct((SUBLANES, LANES), F32)],
        scratch_shapes=[pltpu.VMEM((SUBLANES, LANES), F32)],
        compiler_params=pltpu.CompilerParams(dimension_semantics=("arbitrary",), vmem_limit_bytes=VMEM_LIMIT),
        name="merge_router",
    )(x_all, attn, pooled, ga, gb, *consts)


def _dispatch_body(dest_ref, gend_ref, h2p_ref, xs_hbm, zero_scr, sem, zsem):
    step = pl.program_id(0)

    @pl.when(step == 0)
    def _():
        zero_scr[...] = jnp.zeros_like(zero_scr)

        def fill(e):
            return pltpu.make_async_copy(zero_scr, xs_hbm.at[pl.ds(pl.multiple_of(gend_ref[e] - TS, TS), TS)], zsem)

        def start(e, _):
            @pl.when(gend_ref[e] > gend_ref[e + N_EXPERTS])
            def _():
                fill(e).start()
            return 0
        lax.fori_loop(0, N_EXPERTS, start, 0)

        def wait(e, _):
            @pl.when(gend_ref[e] > gend_ref[e + N_EXPERTS])
            def _():
                fill(e).wait()
            return 0
        lax.fori_loop(0, N_EXPERTS, wait, 0)

        def tail(j):
            return pltpu.make_async_copy(zero_scr, xs_hbm.at[pl.ds(pl.multiple_of(j * TS, TS), TS)], zsem)

        first_unused = gend_ref[N_EXPERTS - 1] // TS
        n_tiles = xs_hbm.shape[0] // TS

        def tail_start(j, _):
            tail(j).start()
            return 0
        lax.fori_loop(first_unused, n_tiles, tail_start, 0)

        def tail_wait(j, _):
            tail(j).wait()
            return 0
        lax.fori_loop(first_unused, n_tiles, tail_wait, 0)

    t0 = step * TM

    def row_copy(r, k):
        d = dest_ref[(t0 + r) * TOP_EXPERTS + k]
        return pltpu.make_async_copy(h2p_ref.at[pl.ds(r, 1)], xs_hbm.at[pl.ds(d, 1)], sem)

    def start(r, _):
        for k in range(TOP_EXPERTS):
            row_copy(r, k).start()
        return 0
    lax.fori_loop(0, TM, start, 0, unroll=DMA_UNROLL)

    for _ in range(TOP_EXPERTS):
        pltpu.make_async_copy(h2p_ref, xs_hbm.at[pl.ds(0, TM)], sem).wait()


def _dispatch(dest_flat, gend, h2p, *, n_slots):
    nt, hw = h2p.shape
    grid_spec = pltpu.PrefetchScalarGridSpec(
        num_scalar_prefetch=2,
        grid=(nt // TM,),
        in_specs=[pl.BlockSpec((TM, hw), lambda i, d, g: (i, 0))],
        out_specs=pl.BlockSpec(memory_space=pl.ANY),
        scratch_shapes=[pltpu.VMEM((TS, hw), U32), pltpu.SemaphoreType.DMA, pltpu.SemaphoreType.DMA],
    )
    return pl.pallas_call(
        _dispatch_body,
        grid_spec=grid_spec,
        out_shape=jax.ShapeDtypeStruct((n_slots, hw), U32),
        compiler_params=pltpu.CompilerParams(dimension_semantics=("arbitrary",)),
        name="dispatch",
    )(dest_flat, gend, h2p)


def _moe_body(te_ref, nu_ref, xs_ref, wgu_ref, bgu_ref, wdn_ref, bdn_ref, ys_o, wgu_b, wdn_b):
    j = pl.program_id(0)
    prev = te_ref[jnp.maximum(j - 1, 0)]
    live = j < nu_ref[0]

    @pl.when(jnp.logical_and(live, jnp.logical_or(j == 0, te_ref[j] != prev)))
    def _():
        wgu_b[...] = wgu_ref[0, 0].astype(BF16)
        wdn_b[...] = wdn_ref[0, 0].astype(BF16)

    @pl.when(live)
    def _():
        words = xs_ref[...]
        x_lo = lax.bitcast_convert_type(lax.shift_left(words, jnp.uint32(16)), F32).astype(BF16)
        x_hi = lax.bitcast_convert_type(words & jnp.uint32(0xFFFF0000), F32).astype(BF16)
        half = words.shape[1]
        gu = _dot(x_lo, wgu_b[0:half, :]) + _dot(x_hi, wgu_b[half:, :]) + bgu_ref[0, 0]
        de = gu.shape[1] // 2
        gate = jnp.minimum(gu[:, 0:de], SWIGLU_LIMIT)
        up = jnp.clip(gu[:, de:], -SWIGLU_LIMIT, SWIGLU_LIMIT)
        act = (up + 1.0) * gate * jax.nn.sigmoid(SWIGLU_ALPHA * gate)
        ys_o[...] = _dot(act.astype(BF16), wdn_b[...]) + bdn_ref[0, 0]

    @pl.when(jnp.logical_not(live))
    def _():
        ys_o[...] = jnp.zeros_like(ys_o)


def _moe(tile_expert, n_used, xs, wgu, bgu, wdn, bdn, *, layer):
    n_slots, hw = xs.shape
    _, ne, d, de2 = wgu.shape
    last = lambda j, nu: jnp.minimum(j, nu[0] - 1)
    per_expert = lambda rows, cols: pl.BlockSpec((1, 1, rows, cols), lambda j, te, nu: (layer, te[j], 0, 0))
    grid_spec = pltpu.PrefetchScalarGridSpec(
        num_scalar_prefetch=2,
        grid=(n_slots // TS,),
        in_specs=[pl.BlockSpec((TS, hw), lambda j, te, nu: (last(j, nu), 0)),
                  per_expert(d, de2), per_expert(1, de2), per_expert(de2 // 2, d), per_expert(1, d)],
        out_specs=pl.BlockSpec((TS, d), lambda j, te, nu: (j, 0)),
        scratch_shapes=[pltpu.VMEM((d, de2), BF16), pltpu.VMEM((de2 // 2, d), BF16)],
    )
    return pl.pallas_call(
        _moe_body,
        grid_spec=grid_spec,
        out_shape=jax.ShapeDtypeStruct((n_slots, d), F32),
        compiler_params=pltpu.CompilerParams(dimension_semantics=("arbitrary",), vmem_limit_bytes=VMEM_LIMIT),
        name="moe_experts",
    )(tile_expert, n_used, xs, wgu, bgu[:, :, None, :], wdn, bdn[:, :, None, :])


def _combine_body(dest_ref, x1_ref, gv_ref, gfin_ref, ys_hbm, y_o, ybuf, sem):
    t0 = pl.program_id(0) * TM

    def row_copy(r, k):
        d = dest_ref[(t0 + r) * TOP_EXPERTS + k]
        return pltpu.make_async_copy(ys_hbm.at[pl.ds(d, 1)], ybuf.at[k, pl.ds(r, 1)], sem)

    def start(r, _):
        for k in range(TOP_EXPERTS):
            row_copy(r, k).start()
        return 0
    lax.fori_loop(0, TM, start, 0, unroll=DMA_UNROLL)

    for k in range(TOP_EXPERTS):
        pltpu.make_async_copy(ys_hbm.at[pl.ds(0, TM)], ybuf.at[k], sem).wait()

    gv = gv_ref[...]
    y = jnp.zeros(x1_ref.shape, F32)
    for k in range(TOP_EXPERTS):
        y = y + ybuf[k] * gv[:, k:k + 1]
    y_o[...] = _rms(x1_ref[...] + y, gfin_ref[...])


def _combine(dest_flat, x1, gv, gfin, ys):
    nt, d = x1.shape
    grid_spec = pltpu.PrefetchScalarGridSpec(
        num_scalar_prefetch=1,
        grid=(nt // TM,),
        in_specs=[pl.BlockSpec((TM, d), lambda i, dref: (i, 0)),
                  pl.BlockSpec((TM, LANES), lambda i, dref: (i, 0)),
                  pl.BlockSpec((1, d), lambda i, dref: (0, 0)),
                  pl.BlockSpec(memory_space=pl.ANY)],
        out_specs=pl.BlockSpec((TM, d), lambda i, dref: (i, 0)),
        scratch_shapes=[pltpu.VMEM((TOP_EXPERTS, TM, d), F32), pltpu.SemaphoreType.DMA],
    )
    return pl.pallas_call(
        _combine_body,
        grid_spec=grid_spec,
        out_shape=jax.ShapeDtypeStruct((nt, d), F32),
        compiler_params=pltpu.CompilerParams(dimension_semantics=("arbitrary",)),
        name="combine_norm",
    )(dest_flat, x1, gv, gfin, ys)


def _tri_constants():
    r = lax.broadcasted_iota(I32, (TQ, TQ), 0)
    c = lax.broadcasted_iota(I32, (TQ, TQ), 1)
    incl = (r <= c).astype(BF16)
    u2 = jnp.concatenate([incl, jnp.ones((TQ, LANES), BF16)], axis=1)
    ltri = (c < r).astype(BF16)
    return u2, ltri


def kernel(x_prompt, x_sample, cache_k, cache_v, cache_kidx, state_pool, page_table, meta_tokens, rel_bias, g_mix, w_in, w_pool_mix, pool_scale, w_br_attn, w_br_pool, w_out, g_ffn, w_router, b_router, w_gate_up, b_gate_up, w_down, b_down, g_final):
    nb, seq, d = x_prompt.shape
    db, dec_seq, _ = x_sample.shape
    assert dec_seq == 1, "one new token per sample"
    depth = w_in.shape[0]
    assert depth == 1, "single-layer stack: the combine kernel applies the final norm"
    n_pages = page_table.shape[1]
    assert n_pages % SUBLANES == 0, "the sample top-k lays the cached keys out as eight equal page segments"
    past = n_pages * PAGE_SIZE
    pw = state_pool.shape[-1]
    l_seq = seq + N_META
    lp = _round_up(l_seq, TQ)
    n_prompt = nb * lp
    n_sample = _round_up(db, TM)
    nt = n_prompt + n_sample
    ksel_p = min(TOPK_MAX, seq // 4)
    ksel_s = min(TOPK_MAX, (past + dec_seq) // 4)
    n_tiles = (TOP_EXPERTS * nt) // TS + N_EXPERTS
    n_slots = n_tiles * TS

    meta = jnp.broadcast_to(meta_tokens[None].astype(x_prompt.dtype), (nb, N_META, d))
    xp = jnp.concatenate([meta, x_prompt, jnp.zeros((nb, lp - l_seq, d), x_prompt.dtype)], axis=1)
    xs_rows = jnp.concatenate([x_sample[:, 0, :], jnp.zeros((n_sample - db, d), x_sample.dtype)], axis=0)
    x_all = jnp.concatenate([xp.reshape(n_prompt, d), xs_rows], axis=0)

    u2, ltri = _tri_constants()
    page_flat = page_table.reshape(-1).astype(I32)
    rel_bias = rel_bias.astype(F32)
    cache_k_t = jnp.transpose(cache_k, (0, 1, 3, 4, 2))
    cache_v_t = jnp.transpose(cache_v, (0, 1, 3, 4, 2))
    cache_ki_t = jnp.transpose(cache_kidx, (0, 1, 3, 2))

    outs = {name: [] for name in ("k_p", "v_p", "ki_p", "pool_p", "k_s", "v_s", "ki_s", "pool_s")}
    for l in range(depth):
        wl = w_in[l]
        aw = ATTN_WIDTH
        o_ki = 4 * aw
        o_wi = o_ki + IDX_DIM
        o_p = o_wi + N_IDX_HEADS
        o_ga = o_p + pw
        o_gb = o_ga + d
        wa = wl[:, 0:o_ki].astype(BF16)
        wki = wl[:, o_ki:o_wi]
        wki2 = jnp.concatenate([wki, wki], axis=1).astype(BF16)
        wwi = jnp.pad(wl[:, o_wi:o_p], ((0, 0), (0, LANES - N_IDX_HEADS))).astype(BF16)
        wp = wl[:, o_p:o_ga].astype(BF16)
        wga = wl[:, o_ga:o_gb].astype(BF16)
        wgb = wl[:, o_gb:o_gb + d].astype(BF16)

        (qs, kf, vf, kb, vb, qib, ki2b, kif, wif, pf, ga, gb) = _inproj(
            x_all, g_mix[l][None, :], wa, wki2, wwi, wp, wga, wgb)

        tail_rows = _round_up(l_seq - (lp - TQ), 2 * SUBLANES)
        attn_p = _attn_prompt(qs, qib, wif, kb, vb, ki2b, u2, rel_bias, nb=nb, lp=lp, ksel=ksel_p,
                              tail_rows=tail_rows)
        pooled_p = _pool_prompt(pf, n_rows=n_prompt, lp=lp)

        sl = slice(n_prompt, n_prompt + db)
        keys = _sample_keys(page_flat, qib[sl].reshape(db, N_IDX_HEADS, IDX_DIM), kif[sl, 0:IDX_DIM][:, None, :],
                            wif[sl, 0:N_IDX_HEADS][:, :, None], cache_ki_t, layer=l, n_pages=n_pages)
        sel = _sample_select(keys, past=past, ksel=ksel_s)
        cols = lambda a: a[sl].astype(F32).reshape(db, N_HEADS, HEAD_DIM, 1)
        attn_s, pooled_s = _sample_attend(
            page_flat, sel, cols(qs), cols(kf), cols(vf), rel_bias, state_pool[l], pf[sl][:, None, :],
            cache_k_t, cache_v_t, layer=l, n_pages=n_pages)

        pad_s = lambda a: jnp.concatenate([a, jnp.zeros((n_sample - db, a.shape[1]), a.dtype)], axis=0)
        attn_all = jnp.concatenate([attn_p, pad_s(attn_s.reshape(db, ATTN_WIDTH))], axis=0)
        pooled_all = jnp.concatenate([pooled_p, pad_s(pooled_s[:, 0, :])], axis=0)

        wr = jnp.pad(w_router[l], ((0, 0), (0, LANES - N_EXPERTS))).astype(BF16)
        br = jnp.concatenate([b_router[l].astype(F32), jnp.full((LANES - N_EXPERTS,), NEG, F32)])[None, :]
        x1, h2p, ev, gv, rk, cnt = _merge(
            x_all, attn_all, pooled_all, ga, gb, w_br_attn[l].astype(BF16), w_pool_mix[l].astype(BF16),
            pool_scale[l][None, :], w_br_pool[l].astype(BF16), w_out[l].astype(BF16), g_ffn[l][None, :], wr, br, ltri)

        counts = cnt[0, 0:N_EXPERTS].astype(I32)
        padded = (counts + TS - 1) // TS * TS
        gend = jnp.cumsum(padded)
        gstart = gend - padded
        experts = jnp.arange(N_EXPERTS, dtype=I32)
        ev4 = ev[:, 0:TOP_EXPERTS]
        dest = (jnp.sum(jnp.where(ev4[:, :, None] == experts, gstart, 0), axis=2) + rk[:, 0:TOP_EXPERTS]).reshape(-1)
        tile_first = jnp.arange(n_tiles, dtype=I32)[:, None] * TS
        tile_expert = jnp.minimum(jnp.sum((gend[None, :] <= tile_first).astype(I32), axis=1), N_EXPERTS - 1)
        n_used = (gend[-1] // TS).astype(I32)[None]

        xs_sorted = _dispatch(dest, jnp.concatenate([gend, gstart + counts]).astype(I32), h2p, n_slots=n_slots)
        ys = _moe(tile_expert, n_used, xs_sorted, w_gate_up, b_gate_up, w_down, b_down, layer=l)
        y_all = _combine(dest, x1, gv, g_final[None, :], ys)

        kp = kf[:n_prompt].reshape(nb, lp, N_HEADS, HEAD_DIM)[:, :l_seq]
        vp = vf[:n_prompt].reshape(nb, lp, N_HEADS, HEAD_DIM)[:, :l_seq]
        kip = kif[:n_prompt, 0:IDX_DIM].reshape(nb, lp, IDX_DIM)[:, :l_seq]
        pp = pf[:n_prompt].reshape(nb, lp, pw)
        outs["k_p"].append(kp)
        outs["v_p"].append(vp)
        outs["ki_p"].append(kip)
        outs["pool_p"].append(pp[:, l_seq - POOL_STATE:l_seq])
        outs["k_s"].append(kf[sl].reshape(db, 1, N_HEADS, HEAD_DIM))
        outs["v_s"].append(vf[sl].reshape(db, 1, N_HEADS, HEAD_DIM))
        outs["ki_s"].append(kif[sl, 0:IDX_DIM].reshape(db, 1, IDX_DIM))
        outs["pool_s"].append(jnp.concatenate([state_pool[l][:, 1:], pf[sl][:, None, :]], axis=1))

    y_prompt = y_all[:n_prompt].reshape(nb, lp, d)[:, N_META:l_seq]
    y_sample = y_all[n_prompt:n_prompt + db].reshape(db, 1, d)
    st = lambda name: jnp.stack(outs[name])
    return (y_prompt, y_sample, st("k_p"), st("v_p"), st("ki_p"), st("pool_p"),
            st("k_s"), st("v_s"), st("ki_s"), st("pool_s"))
```

```python
import functools
import math

import jax
import jax.numpy as jnp
from jax import lax
from jax.experimental import pallas as pl
from jax.experimental.pallas import tpu as pltpu

F32 = jnp.float32
BF16 = jnp.bfloat16
I32 = jnp.int32
U32 = jnp.uint32

N_META = 16
N_HEADS = 8
HEAD_DIM = 64
ATTN_WIDTH = N_HEADS * HEAD_DIM
N_IDX_HEADS = 8
IDX_DIM = 64
TOPK_MAX = 256
N_BUCKETS = 32
MAX_DISTANCE = 128
POOL_WINDOWS = (2, 4, 8, 16)
POOL_GROUPS = 4
POOL_STATE = 15
N_EXPERTS = 32
TOP_EXPERTS = 4
SWIGLU_LIMIT = 7.0
SWIGLU_ALPHA = 1.702
PAGE_SIZE = 128
EPS = 1e-6

LANES = 128
SUBLANES = 8
MXU_DIM = 256
TQ = MXU_DIM
TM = MXU_DIM
TS = MXU_DIM
VMEM_LIMIT = 56 * 1024 * 1024
DMA_UNROLL = 8

NEG = -1e30
INT_MIN = -(2 ** 31)

_MAX_EXACT = N_BUCKETS // 2
_BUCKET_THRESHOLDS = tuple(
    math.ceil(_MAX_EXACT * (MAX_DISTANCE / _MAX_EXACT) ** (j / (N_BUCKETS - _MAX_EXACT)))
    for j in range(1, N_BUCKETS - _MAX_EXACT))


def _round_up(a, m):
    return (a + m - 1) // m * m


def _rms(x, g):
    return x * lax.rsqrt(jnp.mean(x * x, axis=-1, keepdims=True) + EPS) * g


def _dot(a, b):
    return jnp.dot(a, b, preferred_element_type=F32)


def _dot_nt(a, b):
    return lax.dot_general(a, b, (((1,), (1,)), ((), ())), preferred_element_type=F32)


def _bucket(dist):
    large = jnp.full(dist.shape, _MAX_EXACT, I32)
    for thr in _BUCKET_THRESHOLDS:
        large = large + jnp.where(dist >= thr, 1, 0)
    return jnp.where(dist < _MAX_EXACT, dist, large)


def _sort_key(s):
    s = jnp.where(s == 0.0, 0.0, s)
    bits = lax.bitcast_convert_type(s, I32)
    return jnp.where(bits >= 0, bits, bits ^ jnp.int32(0x7FFFFFFF))


def _full(shape):
    return pl.BlockSpec(shape, lambda *_: (0,) * len(shape))


def _inproj_body(x_ref, g_ref, wa_ref, wki_ref, wwi_ref, wp_ref, wga_ref, wgb_ref,
                 qs_o, kf_o, vf_o, kb_o, vb_o, qib_o, ki2b_o, kif_o, wif_o, pf_o, ga_o, gb_o):
    h = _rms(x_ref[...], g_ref[...]).astype(BF16)
    za = _dot(h, wa_ref[...])
    w = ATTN_WIDTH
    qs_o[...] = (za[:, 0:w] * (HEAD_DIM ** -0.5)).astype(BF16)
    k = za[:, w:2 * w]
    v = za[:, 2 * w:3 * w]
    kf_o[...] = k
    vf_o[...] = v
    kb_o[...] = k.astype(BF16)
    vb_o[...] = v.astype(BF16)
    qib_o[...] = za[:, 3 * w:4 * w].astype(BF16)
    ki2 = _dot(h, wki_ref[...])
    kif_o[...] = ki2
    ki2b_o[...] = ki2.astype(BF16)
    wif_o[...] = _dot(h, wwi_ref[...])
    pf_o[...] = _dot(h, wp_ref[...])
    ga_o[...] = _dot(h, wga_ref[...])
    gb_o[...] = _dot(h, wgb_ref[...])


def _inproj(x_all, g, wa, wki2, wwi, wp, wga, wgb):
    nt, d = x_all.shape
    pw = wp.shape[1]
    row = lambda width: pl.BlockSpec((TM, width), lambda i: (i, 0))
    outs = [
        (ATTN_WIDTH, BF16), (ATTN_WIDTH, F32), (ATTN_WIDTH, F32), (ATTN_WIDTH, BF16), (ATTN_WIDTH, BF16),
        (ATTN_WIDTH, BF16), (LANES, BF16), (LANES, F32), (LANES, F32), (pw, F32), (d, F32), (d, F32)]
    return pl.pallas_call(
        _inproj_body,
        grid=(nt // TM,),
        in_specs=[row(d), _full((1, d)), _full(wa.shape), _full(wki2.shape), _full(wwi.shape),
                  _full(wp.shape), _full(wga.shape), _full(wgb.shape)],
        out_specs=[row(wd) for wd, _ in outs],
        out_shape=[jax.ShapeDtypeStruct((nt, wd), dt) for wd, dt in outs],
        compiler_params=pltpu.CompilerParams(dimension_semantics=("arbitrary",), vmem_limit_bytes=VMEM_LIMIT),
        name="inproj",
    )(x_all, g, wa, wki2, wwi, wp, wga, wgb)


def _attn_prompt_body(qs_ref, qib_ref, wif_ref, kb_ref, vb_ref, ki2b_ref, u2_ref, rb_ref,
                      o_ref, qis_scr, wb_scr, key_scr, ntab_scr, qz_scr, m_scr, acc_scr, sum_scr, *, ksel, tail_rows):
    b = pl.program_id(0)
    i = pl.program_id(1)
    nchunk = i + 1
    n_far = jnp.maximum(i - 1, 0)

    @pl.when((b == 0) & (i == 0))
    def _():
        def slab_rows(s, _):
            r0 = pl.multiple_of(s * SUBLANES, SUBLANES)
            r = r0 + lax.broadcasted_iota(I32, (SUBLANES, 2 * TQ), 0)
            x = lax.broadcasted_iota(I32, (SUBLANES, 2 * TQ), 1)
            bucket = _bucket(jnp.maximum(r + TQ - x, 0))
            accs = [jnp.zeros((SUBLANES, 2 * TQ), F32) for _ in range(N_HEADS)]
            for j in range(N_BUCKETS - 1):
                m = bucket == j
                accs = [jnp.where(m, rb_ref[j, h] - rb_ref[N_BUCKETS - 1, h], accs[h]) for h in range(N_HEADS)]
            for h in range(N_HEADS):
                ntab_scr[h, 0, pl.ds(r0, SUBLANES), :] = accs[h][:, 0:TQ]
                ntab_scr[h, 1, pl.ds(r0, SUBLANES), :] = accs[h][:, TQ:2 * TQ]
            return 0
        lax.fori_loop(0, TQ // SUBLANES, slab_rows, 0)

    def two(x):
        return jnp.concatenate([x, x], axis=1)

    def block(nr):
        hr = nr // 2
        lane = lax.broadcasted_iota(I32, (nr, LANES), 1)
        lo_half = lane < HEAD_DIM

        for h in range(N_HEADS):
            cols = slice((h // 2) * LANES, (h // 2 + 1) * LANES)
            keep = lo_half if h % 2 == 0 else jnp.logical_not(lo_half)
            qis_scr[h, 0:nr] = jnp.where(keep, qib_ref[0:nr, cols].astype(F32), 0.0).astype(BF16)
            qz_scr[h // 2, (h % 2) * nr:(h % 2 + 1) * nr] = jnp.where(keep, qs_ref[0:nr, cols].astype(F32), 0.0).astype(BF16)
            wb_scr[h, 0:nr] = jnp.broadcast_to(wif_ref[0:nr, h:h + 1], (nr, LANES))
            m_scr[h, 0:nr] = jnp.full((nr, LANES), NEG, F32)
            acc_scr[h, 0:nr] = jnp.zeros((nr, LANES), F32)
            sum_scr[h, 0:nr] = jnp.zeros((nr, LANES), F32)

        dmat = lax.broadcasted_iota(I32, (nr, TQ), 1) - lax.broadcasted_iota(I32, (nr, TQ), 0)

        def score_chunk(c, _):
            kc = ki2b_ref[pl.ds(pl.multiple_of(c * TQ, TQ), TQ), :]
            s = jnp.zeros((nr, TQ), F32)
            for h in range(N_IDX_HEADS):
                s = s + two(wb_scr[h, 0:nr]) * jnp.maximum(_dot_nt(qis_scr[h, 0:nr], kc), 0.0)
            key_scr[c, 0:nr] = jnp.where(dmat <= (i - c) * TQ, _sort_key(s), INT_MIN)
            return 0
        lax.fori_loop(0, nchunk, score_chunk, 0)

        def count(r0, pred):
            def body(c, acc):
                hit = jnp.where(pred(key_scr[c, r0:r0 + hr, :]), 1.0, 0.0)
                return acc + (hit[:, 0:LANES] + hit[:, LANES:2 * LANES])
            acc = lax.fori_loop(0, nchunk, body, jnp.zeros((hr, LANES), F32))
            return jnp.broadcast_to(jnp.sum(acc, axis=1, keepdims=True), (hr, LANES))

        ts, needs = [], []
        for r0 in (0, hr):
            def bit_step(step, t, r0=r0):
                cand = t + lax.shift_left(jnp.int32(1), 31 - step)
                cand2 = two(cand)
                return jnp.where(count(r0, lambda k: k >= cand2) >= ksel, cand, t)
            t = lax.fori_loop(0, 32, bit_step, jnp.full((hr, LANES), INT_MIN, I32))
            th2 = two(t)
            n_gt = count(r0, lambda k: k > th2)
            ts.append(t)
            needs.append(jnp.where(t == INT_MIN, 0.0, ksel - n_gt))
        t2 = two(jnp.concatenate(ts, axis=0))
        need2 = two(jnp.concatenate(needs, axis=0))

        def mask_chunk(c, carry):
            kc = key_scr[c, 0:nr]
            eq = kc == t2
            pre = _dot(jnp.where(eq, 1.0, 0.0).astype(BF16), u2_ref[...])
            prefix = pre[:, 0:TQ] + two(carry)
            tie_ok = jnp.where(eq, prefix, 3e38) <= need2
            madd = jnp.where(kc > t2, 0.0, jnp.where(tie_ok, 0.0, NEG))
            key_scr[c, 0:nr] = lax.bitcast_convert_type(madd, I32)
            return carry + pre[:, TQ:TQ + LANES]
        lax.fori_loop(0, nchunk, mask_chunk, jnp.zeros((nr, LANES), F32))

        def chunk_logits(c, slab):
            rows = pl.ds(pl.multiple_of(c * TQ, TQ), TQ)
            madd = lax.bitcast_convert_type(key_scr[c, 0:nr], F32)
            out = []
            for hp in range(N_HEADS // 2):
                kc = kb_ref[rows, hp * LANES:(hp + 1) * LANES]
                pair = _dot_nt(qz_scr[hp, 0:2 * nr], kc)
                for sub in range(2):
                    l = pair[sub * nr:(sub + 1) * nr] + madd
                    out.append(l if slab is None else l + ntab_scr[2 * hp + sub, slab, 0:nr])
            return rows, out

        def row_max(c, slab):
            _, ls = chunk_logits(c, slab)
            for h, l in enumerate(ls):
                m_scr[h, 0:nr] = jnp.maximum(m_scr[h, 0:nr], jnp.maximum(l[:, 0:LANES], l[:, LANES:2 * LANES]))

        def accumulate(c, slab):
            rows, ls = chunk_logits(c, slab)
            for hp in range(N_HEADS // 2):
                ps = []
                for h in (2 * hp, 2 * hp + 1):
                    p = jnp.exp(ls[h] - two(m_scr[h, 0:nr]))
                    sum_scr[h, 0:nr] = sum_scr[h, 0:nr] + (p[:, 0:LANES] + p[:, LANES:2 * LANES])
                    ps.append(p.astype(BF16))
                pv = _dot(jnp.concatenate(ps, axis=0), vb_ref[rows, hp * LANES:(hp + 1) * LANES])
                for sub in range(2):
                    h = 2 * hp + sub
                    acc_scr[h, 0:nr] = acc_scr[h, 0:nr] + pv[sub * nr:(sub + 1) * nr]

        def sweep(step):
            def far(c, _):
                step(c, None)
                return 0
            lax.fori_loop(0, n_far, far, 0)

            @pl.when(i >= 1)
            def _():
                step(i - 1, 0)
            step(i, 1)

        sweep(row_max)
        for h in range(N_HEADS):
            m_scr[h, 0:nr] = jnp.broadcast_to(jnp.max(m_scr[h, 0:nr], axis=1, keepdims=True), (nr, LANES))
        sweep(accumulate)
        for hp in range(N_HEADS // 2):
            outs = [acc_scr[h, 0:nr] / jnp.sum(sum_scr[h, 0:nr], axis=1, keepdims=True) for h in (2 * hp, 2 * hp + 1)]
            o_ref[0:nr, hp * LANES:(hp + 1) * LANES] = jnp.where(lo_half, outs[0], outs[1]).astype(BF16)
        if nr < TQ:
            o_ref[nr:TQ, :] = jnp.zeros((TQ - nr, ATTN_WIDTH), BF16)

    if tail_rows == TQ:
        block(TQ)
    else:
        last = pl.num_programs(1) - 1
        pl.when(i < last)(lambda: block(TQ))
        pl.when(i == last)(lambda: block(tail_rows))


def _attn_prompt(qs, qib, wif, kb, vb, ki2b, u2, rel_bias, *, nb, lp, ksel, tail_rows):
    nq = lp // TQ
    rowq = lambda width: pl.BlockSpec((TQ, width), lambda b, i: (b * nq + i, 0))
    seq = lambda width: pl.BlockSpec((lp, width), lambda b, i: (b, 0))
    per_head = lambda dt: pltpu.VMEM((N_HEADS, TQ, LANES), dt)
    return pl.pallas_call(
        functools.partial(_attn_prompt_body, ksel=float(ksel), tail_rows=tail_rows),
        grid=(nb, nq),
        in_specs=[rowq(ATTN_WIDTH), rowq(ATTN_WIDTH), rowq(LANES), seq(ATTN_WIDTH), seq(ATTN_WIDTH), seq(LANES),
                  _full(u2.shape), pl.BlockSpec(memory_space=pltpu.SMEM)],
        out_specs=rowq(ATTN_WIDTH),
        out_shape=jax.ShapeDtypeStruct((nb * lp, ATTN_WIDTH), BF16),
        scratch_shapes=[
            per_head(BF16),
            per_head(F32),
            pltpu.VMEM((nq, TQ, TQ), I32),
            pltpu.VMEM((N_HEADS, 2, TQ, TQ), F32),
            pltpu.VMEM((N_HEADS // 2, 2 * TQ, LANES), BF16),
            per_head(F32), per_head(F32), per_head(F32),
        ],
        compiler_params=pltpu.CompilerParams(dimension_semantics=("arbitrary", "arbitrary"),
                                             vmem_limit_bytes=VMEM_LIMIT),
        name="attn_prompt",
    )(qs, qib, wif, kb, vb, ki2b, u2, rel_bias)


def _pool_prompt_body(p_ref, halo_ref, o_ref, ext_scr, *, tiles_per_seq):
    j = pl.program_id(0) % tiles_per_seq
    halo = 2 * SUBLANES
    p = p_ref[...]
    ext_scr[0:halo, :] = jnp.where(j == 0, 0.0, halo_ref[...])
    ext_scr[halo:halo + TM, :] = p
    pos = j * TM + lax.broadcasted_iota(I32, (TM, 1), 0)
    gw = p.shape[1] // POOL_GROUPS
    for g, w in enumerate(POOL_WINDOWS):
        cols = slice(g * gw, (g + 1) * gw)
        s = p[:, cols]
        for back in range(1, w):
            s = s + ext_scr[halo - back:halo - back + TM, cols]
        cnt = jnp.minimum(pos + 1, w).astype(F32)
        o_ref[:, cols] = (s / cnt - p[:, cols]).astype(BF16)


def _pool_prompt(pf, *, n_rows, lp):
    pw = pf.shape[1]
    halo = 2 * SUBLANES
    return pl.pallas_call(
        functools.partial(_pool_prompt_body, tiles_per_seq=lp // TM),
        grid=(n_rows // TM,),
        in_specs=[pl.BlockSpec((TM, pw), lambda i: (i, 0)),
                  pl.BlockSpec((halo, pw), lambda i: (jnp.maximum(i * (TM // halo) - 1, 0), 0))],
        out_specs=pl.BlockSpec((TM, pw), lambda i: (i, 0)),
        out_shape=jax.ShapeDtypeStruct((n_rows, pw), BF16),
        scratch_shapes=[pltpu.VMEM((halo + TM, pw), F32)],
        compiler_params=pltpu.CompilerParams(dimension_semantics=("arbitrary",)),
        name="pool_prompt",
    )(pf, pf)


def _sample_keys_body(pt_ref, qi_ref, kinew_ref, wi_ref, cache_hbm, key_o, kibuf, sem, *, layer, n_pages):
    s = pl.program_id(0)
    past = n_pages * PAGE_SIZE
    seg_w = past // SUBLANES

    copies = [pltpu.make_async_copy(cache_hbm.at[layer, pt_ref[s * n_pages + pg]],
                                    kibuf.at[:, pg * PAGE_SIZE:(pg + 1) * PAGE_SIZE], sem) for pg in range(n_pages)]
    for cp in copies:
        cp.start()
    for cp in copies:
        cp.wait()

    qi = qi_ref[0]
    wcol = wi_ref[0]
    d = _dot(qi, kibuf[...].astype(BF16))
    sc = jnp.sum(wcol * jnp.maximum(d, 0.0), axis=0, keepdims=True)
    knew = kinew_ref[0].astype(BF16).astype(F32)
    dn = jnp.sum(qi.astype(F32) * knew, axis=1, keepdims=True)
    sn = jnp.sum(wcol * jnp.maximum(dn, 0.0), axis=0, keepdims=True)
    key = _sort_key(sc)
    lane = lax.broadcasted_iota(I32, (1, LANES), 1)
    new_tail = jnp.where(lane == 0, _sort_key(jnp.broadcast_to(sn, (1, LANES))), INT_MIN)
    for g in range(SUBLANES):
        key_o[0, g:g + 1, 0:seg_w] = key[:, g * seg_w:(g + 1) * seg_w]
        key_o[0, g:g + 1, seg_w:seg_w + LANES] = new_tail if g == 0 else jnp.full((1, LANES), INT_MIN, I32)


def _sample_keys(page_table_flat, qi3, kinew3, wi3, cache_ki_t, *, layer, n_pages):
    db = qi3.shape[0]
    past = n_pages * PAGE_SIZE
    kw = past // SUBLANES + LANES
    grid_spec = pltpu.PrefetchScalarGridSpec(
        num_scalar_prefetch=1,
        grid=(db,),
        in_specs=[pl.BlockSpec((1, N_IDX_HEADS, IDX_DIM), lambda s, pt: (s, 0, 0)),
                  pl.BlockSpec((1, 1, IDX_DIM), lambda s, pt: (s, 0, 0)),
                  pl.BlockSpec((1, N_IDX_HEADS, 1), lambda s, pt: (s, 0, 0)),
                  pl.BlockSpec(memory_space=pl.ANY)],
        out_specs=pl.BlockSpec((1, SUBLANES, kw), lambda s, pt: (s, 0, 0)),
        scratch_shapes=[pltpu.VMEM((IDX_DIM, past), F32), pltpu.SemaphoreType.DMA],
    )
    return pl.pallas_call(
        functools.partial(_sample_keys_body, layer=layer, n_pages=n_pages),
        grid_spec=grid_spec,
        out_shape=jax.ShapeDtypeStruct((db, SUBLANES, kw), I32),
        compiler_params=pltpu.CompilerParams(dimension_semantics=("arbitrary",)),
        name="sample_keys",
    )(page_table_flat, qi3, kinew3, wi3, cache_ki_t)


def _sample_select_body(key_ref, sel_o, *, past, ksel):
    keys = key_ref[...]
    kw = keys.shape[2]
    seg_w = kw - LANES
    row = lax.broadcasted_iota(I32, (SUBLANES, kw), 0)
    lane = lax.broadcasted_iota(I32, (SUBLANES, kw), 1)
    pos = jnp.where(lane < seg_w, row * seg_w + lane, past + (lane - seg_w) + row * LANES)[None]

    def cnt(pred):
        x = jnp.where(pred, 1.0, 0.0)
        return jnp.sum(jnp.sum(x, axis=2, keepdims=True), axis=1, keepdims=True)

    def bit_step(step, t):
        cand = t + lax.shift_left(jnp.int32(1), 31 - step)
        return jnp.where(cnt(keys >= cand) >= ksel, cand, t)
    t = lax.fori_loop(0, 32, bit_step, jnp.full((keys.shape[0], 1, 1), INT_MIN, I32))
    gt = keys > t
    eq = keys == t
    need = ksel - cnt(gt)

    nbits = (past + SUBLANES * LANES).bit_length()

    def cut_step(step, c):
        cand = c - lax.shift_left(jnp.int32(1), nbits - 1 - step)
        ok = cnt(jnp.logical_and(eq, pos <= cand)) >= need
        return jnp.where(ok, cand, c)
    cut = lax.fori_loop(0, nbits, cut_step, jnp.full((keys.shape[0], 1, 1), 2 ** nbits - 1, I32))
    sel = jnp.where(gt, 1.0, jnp.where(jnp.logical_and(eq, pos <= cut), 1.0, 0.0))
    sel_o[...] = jnp.where(pos <= past, sel, 0.0)


def _sample_select(keys, *, past, ksel):
    return pl.pallas_call(
        functools.partial(_sample_select_body, past=past, ksel=float(ksel)),
        out_shape=jax.ShapeDtypeStruct(keys.shape, F32),
        name="sample_select",
    )(keys)


def _sample_attend_body(pt_ref, sel_ref, q_ref, knew_ref, vnew_ref, rb_ref, state_ref, pnew_ref, ck_hbm, cv_hbm,
                        attn_o, pool_o, kbuf, vbuf, qb_scr, lg_scr, acc_scr, ksem, vsem, *, layer, n_pages):
    s = pl.program_id(0)
    past = n_pages * PAGE_SIZE
    seg_w = past // SUBLANES
    pages_per_seg = n_pages // SUBLANES
    last_bias = [rb_ref[N_BUCKETS - 1, h] for h in range(N_HEADS)]

    def kcopy(pg, sample=s):
        return pltpu.make_async_copy(ck_hbm.at[layer, pt_ref[sample * n_pages + pg]], kbuf.at[pg], ksem.at[pg])

    def vcopy(pg, sample=s):
        return pltpu.make_async_copy(cv_hbm.at[layer, pt_ref[sample * n_pages + pg]], vbuf.at[pg], vsem.at[pg])

    def request(copy, sample):
        def start(pg, _):
            copy(pg, sample).start()
            return 0
        lax.fori_loop(0, n_pages, start, 0)

    has_next = s + 1 < pl.num_programs(0)

    @pl.when(s == 0)
    def _():
        request(kcopy, s)
        request(vcopy, s)

    for h in range(N_HEADS):
        qb_scr[h] = jnp.broadcast_to(q_ref[0, h], (HEAD_DIM, LANES))
        acc_scr[h] = jnp.zeros((HEAD_DIM, LANES), F32)

    def k_page(pg, _):
        kcopy(pg).wait()
        rows = [jnp.sum(kbuf[pg, h].astype(BF16).astype(F32) * qb_scr[h], axis=0, keepdims=True)
                for h in range(N_HEADS)]
        lg_scr[pg] = jnp.concatenate(rows, axis=0)
        return 0
    lax.fori_loop(0, n_pages, k_page, 0)
    pl.when(has_next)(lambda: request(kcopy, s + 1))

    lane = lax.broadcasted_iota(I32, (1, LANES), 1)
    bucket = _bucket(past - ((n_pages - 1) * PAGE_SIZE + lane))
    near_rows = []
    for h in range(N_HEADS):
        r = jnp.zeros((1, LANES), F32)
        for j in range(N_BUCKETS - 1):
            r = jnp.where(bucket == j, rb_ref[j, h] - last_bias[h], r)
        near_rows.append(r)
    near = jnp.concatenate(near_rows, axis=0)

    sel = sel_ref[0]
    m = jnp.full((N_HEADS, LANES), NEG, F32)
    for pg in range(n_pages):
        g, j = divmod(pg, pages_per_seg)
        keep = sel[g:g + 1, j * PAGE_SIZE:(j + 1) * PAGE_SIZE] > 0.0
        l = lg_scr[pg] + jnp.where(keep, 0.0, NEG)
        if pg == n_pages - 1:
            l = l + near
        lg_scr[pg] = l
        m = jnp.maximum(m, l)
    new_rows = []
    for h in range(N_HEADS):
        kn = knew_ref[0, h].astype(BF16).astype(F32)
        new_rows.append(jnp.sum(q_ref[0, h] * kn, axis=0, keepdims=True) + (rb_ref[0, h] - last_bias[h]))
    l_new = jnp.where(sel[0:1, seg_w:seg_w + 1] > 0.0, jnp.concatenate(new_rows, axis=0), NEG)
    m = jnp.maximum(jnp.max(m, axis=1, keepdims=True), l_new)
    ssum = jnp.zeros((N_HEADS, LANES), F32)
    for pg in range(n_pages):
        p = jnp.exp(lg_scr[pg] - m)
        lg_scr[pg] = p
        ssum = ssum + p
    p_new = jnp.exp(l_new - m)
    den = jnp.sum(ssum, axis=1, keepdims=True) + p_new
    for pg in range(n_pages):
        lg_scr[pg] = (lg_scr[pg] / den).astype(BF16).astype(F32)
    p_new = (p_new / den).astype(BF16).astype(F32)

    def v_page(pg, _):
        vcopy(pg).wait()
        p = lg_scr[pg]
        for h in range(N_HEADS):
            acc_scr[h] = acc_scr[h] + vbuf[pg, h].astype(BF16).astype(F32) * p[h:h + 1, :]
        return 0
    lax.fori_loop(0, n_pages, v_page, 0)
    pl.when(has_next)(lambda: request(vcopy, s + 1))
    for h in range(N_HEADS):
        vn = vnew_ref[0, h].astype(BF16).astype(F32)
        attn_o[0, h] = jnp.sum(acc_scr[h], axis=1, keepdims=True) + p_new[h:h + 1, :] * vn

    pnew = pnew_ref[0]
    st = state_ref[0]
    gw = pnew.shape[1] // POOL_GROUPS
    for g, w in enumerate(POOL_WINDOWS):
        cols = slice(g * gw, (g + 1) * gw)
        acc = pnew[:, cols]
        for back in range(1, w):
            acc = acc + st[POOL_STATE - back:POOL_STATE - back + 1, cols]
        pool_o[0, :, cols] = (acc / float(min(POOL_STATE + 1, w)) - pnew[:, cols]).astype(BF16)


def _sample_attend(page_table_flat, sel, q4, knew4, vnew4, rel_bias, state, pnew3, cache_k_t, cache_v_t, *, layer, n_pages):
    db = q4.shape[0]
    pw = pnew3.shape[2]
    per = lambda shape: pl.BlockSpec((1,) + shape, lambda s, pt: (s,) + (0,) * len(shape))
    col = (N_HEADS, HEAD_DIM, 1)
    page = (n_pages, N_HEADS, HEAD_DIM, LANES)
    grid_spec = pltpu.PrefetchScalarGridSpec(
        num_scalar_prefetch=1,
        grid=(db,),
        in_specs=[per(sel.shape[1:]), per(col), per(col), per(col), pl.BlockSpec(memory_space=pltpu.SMEM),
                  per((POOL_STATE, pw)), per((1, pw)),
                  pl.BlockSpec(memory_space=pl.ANY), pl.BlockSpec(memory_space=pl.ANY)],
        out_specs=[per(col), per((1, pw))],
        scratch_shapes=[pltpu.VMEM(page, F32), pltpu.VMEM(page, F32),
                        pltpu.VMEM((N_HEADS, HEAD_DIM, LANES), F32),
                        pltpu.VMEM((n_pages, N_HEADS, LANES), F32),
                        pltpu.VMEM((N_HEADS, HEAD_DIM, LANES), F32),
                        pltpu.SemaphoreType.DMA((n_pages,)), pltpu.SemaphoreType.DMA((n_pages,))],
    )
    return pl.pallas_call(
        functools.partial(_sample_attend_body, layer=layer, n_pages=n_pages),
        grid_spec=grid_spec,
        out_shape=[jax.ShapeDtypeStruct((db,) + col, F32), jax.ShapeDtypeStruct((db, 1, pw), BF16)],
        compiler_params=pltpu.CompilerParams(dimension_semantics=("arbitrary",), vmem_limit_bytes=VMEM_LIMIT),
        name="sample_attend",
    )(page_table_flat, sel, q4, knew4, vnew4, rel_bias, state, pnew3, cache_k_t, cache_v_t)


def _merge_body(x_ref, attn_ref, pool_ref, ga_ref, gb_ref, wba_ref, wpm_ref, ps_ref, wbp_ref, wout_ref,
                gffn_ref, wr_ref, br_ref, ltri_ref,
                x1_o, h2p_o, ev_o, gv_o, rk_o, cnt_o, carry_scr):
    step = pl.program_id(0)

    @pl.when(step == 0)
    def _():
        carry_scr[...] = jnp.zeros_like(carry_scr)

    a = _dot(attn_ref[...], wba_ref[...])
    pooled = pool_ref[...]
    gw = pooled.shape[1] // POOL_GROUPS
    pm = jnp.concatenate([_dot(pooled[:, g * gw:(g + 1) * gw], wpm_ref[g]) for g in range(POOL_GROUPS)], axis=1)
    bb = _dot((pm * ps_ref[...]).astype(BF16), wbp_ref[...])
    mix = jax.nn.sigmoid(ga_ref[...]) * a + jax.nn.sigmoid(gb_ref[...]) * bb
    x1 = x_ref[...] + _dot(mix.astype(BF16), wout_ref[...])
    x1_o[...] = x1
    h2 = _rms(x1, gffn_ref[...]).astype(BF16)
    half = h2.shape[1] // 2
    lo = lax.shift_right_logical(lax.bitcast_convert_type(h2[:, 0:half].astype(F32), U32), jnp.uint32(16))
    hi = lax.bitcast_convert_type(h2[:, half:].astype(F32), U32) & jnp.uint32(0xFFFF0000)
    h2p_o[...] = hi | lo

    logits = _dot(h2, wr_ref[...]) + br_ref[...]
    lane = lax.broadcasted_iota(I32, (TM, LANES), 1)
    lanef = lane.astype(F32)
    tops, ids = [], []
    l = logits
    for _ in range(TOP_EXPERTS):
        mx = jnp.max(l, axis=1, keepdims=True)
        ix = jnp.min(jnp.where(l == mx, lanef, float(LANES)), axis=1, keepdims=True)
        tops.append(mx)
        ids.append(ix)
        l = jnp.where(lanef == ix, -3e38, l)
    es = [jnp.exp(tv - tops[0]) for tv in tops]
    den = es[0] + es[1] + es[2] + es[3]
    onehot = jnp.zeros((TM, LANES), F32)
    for ix in ids:
        onehot = onehot + jnp.where(lanef == ix, 1.0, 0.0)
    before = _dot(ltri_ref[...], onehot.astype(BF16)) + carry_scr[0:1, :]
    ev = jnp.zeros((TM, LANES), I32)
    gv = jnp.zeros((TM, LANES), F32)
    rk = jnp.zeros((TM, LANES), I32)
    for k in range(TOP_EXPERTS):
        rank = jnp.sum(jnp.where(lanef == ids[k], before, 0.0), axis=1, keepdims=True)
        ev = jnp.where(lane == k, ids[k].astype(I32), ev)
        gv = jnp.where(lane == k, es[k] / den, gv)
        rk = jnp.where(lane == k, rank.astype(I32), rk)
    ev_o[...] = ev
    gv_o[...] = gv
    rk_o[...] = rk
    carry_scr[...] = carry_scr[...] + jnp.sum(onehot, axis=0, keepdims=True)
    cnt_o[...] = carry_scr[...]


def _merge(x_all, attn, pooled, ga, gb, wba, wpm, ps, wbp, wout, gffn, wr, br, ltri):
    nt, d = x_all.shape
    row = lambda width: pl.BlockSpec((TM, width), lambda i: (i, 0))
    consts = [wba, wpm, ps, wbp, wout, gffn, wr, br, ltri]
    return pl.pallas_call(
        _merge_body,
        grid=(nt // TM,),
        in_specs=[row(d), row(attn.shape[1]), row(pooled.shape[1]), row(d), row(d)] + [_full(c.shape) for c in consts],
        out_specs=[row(d), row(d // 2), row(LANES), row(LANES), row(LANES), _full((SUBLANES, LANES))],
        out_shape=[jax.ShapeDtypeStruct((nt, d), F32), jax.ShapeDtypeStruct((nt, d // 2), U32),
                   jax.ShapeDtypeStruct((nt, LANES), I32), jax.ShapeDtypeStruct((nt, LANES), F32),
                   jax.ShapeDtypeStruct((nt, LANES), I32), jax.ShapeDtypeStruct((SUBLANES, LANES), F32)],
        scratch_shapes=[pltpu.VMEM((SUBLANES, LANES), F32)],
        compiler_params=pltpu.CompilerParams(dimension_semantics=("arbitrary",), vmem_limit_bytes=VMEM_LIMIT),
        name="merge_router",
    )(x_all, attn, pooled, ga, gb, *consts)


def _dispatch_body(dest_ref, gend_ref, h2p_ref, xs_hbm, zero_scr, sem, zsem):
    step = pl.program_id(0)

    @pl.when(step == 0)
    def _():
        zero_scr[...] = jnp.zeros_like(zero_scr)

        def fill(e):
            return pltpu.make_async_copy(zero_scr, xs_hbm.at[pl.ds(pl.multiple_of(gend_ref[e] - TS, TS), TS)], zsem)

        def start(e, _):
            @pl.when(gend_ref[e] > gend_ref[e + N_EXPERTS])
            def _():
                fill(e).start()
            return 0
        lax.fori_loop(0, N_EXPERTS, start, 0)

        def wait(e, _):
            @pl.when(gend_ref[e] > gend_ref[e + N_EXPERTS])
            def _():
                fill(e).wait()
            return 0
        lax.fori_loop(0, N_EXPERTS, wait, 0)

        def tail(j):
            return pltpu.make_async_copy(zero_scr, xs_hbm.at[pl.ds(pl.multiple_of(j * TS, TS), TS)], zsem)

        first_unused = gend_ref[N_EXPERTS - 1] // TS
        n_tiles = xs_hbm.shape[0] // TS

        def tail_start(j, _):
            tail(j).start()
            return 0
        lax.fori_loop(first_unused, n_tiles, tail_start, 0)

        def tail_wait(j, _):
            tail(j).wait()
            return 0
        lax.fori_loop(first_unused, n_tiles, tail_wait, 0)

    t0 = step * TM

    def row_copy(r, k):
        d = dest_ref[(t0 + r) * TOP_EXPERTS + k]
        return pltpu.make_async_copy(h2p_ref.at[pl.ds(r, 1)], xs_hbm.at[pl.ds(d, 1)], sem)

    def start(r, _):
        for k in range(TOP_EXPERTS):
            row_copy(r, k).start()
        return 0
    lax.fori_loop(0, TM, start, 0, unroll=DMA_UNROLL)

    for _ in range(TOP_EXPERTS):
        pltpu.make_async_copy(h2p_ref, xs_hbm.at[pl.ds(0, TM)], sem).wait()


def _dispatch(dest_flat, gend, h2p, *, n_slots):
    nt, hw = h2p.shape
    grid_spec = pltpu.PrefetchScalarGridSpec(
        num_scalar_prefetch=2,
        grid=(nt // TM,),
        in_specs=[pl.BlockSpec((TM, hw), lambda i, d, g: (i, 0))],
        out_specs=pl.BlockSpec(memory_space=pl.ANY),
        scratch_shapes=[pltpu.VMEM((TS, hw), U32), pltpu.SemaphoreType.DMA, pltpu.SemaphoreType.DMA],
    )
    return pl.pallas_call(
        _dispatch_body,
        grid_spec=grid_spec,
        out_shape=jax.ShapeDtypeStruct((n_slots, hw), U32),
        compiler_params=pltpu.CompilerParams(dimension_semantics=("arbitrary",)),
        name="dispatch",
    )(dest_flat, gend, h2p)


def _moe_body(te_ref, nu_ref, gi_ref, nx_ref, xs_ref, bgu_ref, bdn_ref, wgu_hbm, wdn_hbm, ys_o,
              wgu_f, wdn_f, wgu_b, wdn_b, sem, *, layer):
    j = pl.program_id(0)
    expert = te_ref[j]
    live = j < nu_ref[0]
    slot = gi_ref[j] % 2

    def weight_copies(e, s):
        return (pltpu.make_async_copy(wgu_hbm.at[layer, e], wgu_f.at[s], sem.at[0, s]),
                pltpu.make_async_copy(wdn_hbm.at[layer, e], wdn_f.at[s], sem.at[1, s]))

    @pl.when(jnp.logical_and(live, jnp.logical_or(j == 0, expert != te_ref[jnp.maximum(j - 1, 0)])))
    def _():
        @pl.when(j == 0)
        def _():
            for cp in weight_copies(expert, slot):
                cp.start()

        @pl.when(nx_ref[j] >= 0)
        def _():
            for cp in weight_copies(nx_ref[j], 1 - slot):
                cp.start()
        for cp in weight_copies(expert, slot):
            cp.wait()
        wgu_b[...] = wgu_f[slot].astype(BF16)
        wdn_b[...] = wdn_f[slot].astype(BF16)

    @pl.when(live)
    def _():
        words = xs_ref[...]
        x_lo = lax.bitcast_convert_type(lax.shift_left(words, jnp.uint32(16)), F32).astype(BF16)
        x_hi = lax.bitcast_convert_type(words & jnp.uint32(0xFFFF0000), F32).astype(BF16)
        half = words.shape[1]
        gu = _dot(x_lo, wgu_b[0:half, :]) + _dot(x_hi, wgu_b[half:, :]) + bgu_ref[0, 0]
        de = gu.shape[1] // 2
        gate = jnp.minimum(gu[:, 0:de], SWIGLU_LIMIT)
        up = jnp.clip(gu[:, de:], -SWIGLU_LIMIT, SWIGLU_LIMIT)
        act = (up + 1.0) * gate * jax.nn.sigmoid(SWIGLU_ALPHA * gate)
        ys_o[...] = _dot(act.astype(BF16), wdn_b[...]) + bdn_ref[0, 0]

    @pl.when(jnp.logical_not(live))
    def _():
        ys_o[...] = jnp.zeros_like(ys_o)


def _moe(tile_expert, n_used, group_index, next_expert, xs, wgu, bgu, wdn, bdn, *, layer):
    n_slots, hw = xs.shape
    _, ne, d, de2 = wgu.shape
    bias = lambda cols: pl.BlockSpec((1, 1, 1, cols), lambda j, te, nu, gi, nx: (layer, te[j], 0, 0))
    grid_spec = pltpu.PrefetchScalarGridSpec(
        num_scalar_prefetch=4,
        grid=(n_slots // TS,),
        in_specs=[pl.BlockSpec((TS, hw), lambda j, te, nu, gi, nx: (jnp.minimum(j, nu[0] - 1), 0)),
                  bias(de2), bias(d), pl.BlockSpec(memory_space=pl.ANY), pl.BlockSpec(memory_space=pl.ANY)],
        out_specs=pl.BlockSpec((TS, d), lambda j, te, nu, gi, nx: (j, 0)),
        scratch_shapes=[pltpu.VMEM((2, d, de2), F32), pltpu.VMEM((2, de2 // 2, d), F32),
                        pltpu.VMEM((d, de2), BF16), pltpu.VMEM((de2 // 2, d), BF16),
                        pltpu.SemaphoreType.DMA((2, 2))],
    )
    return pl.pallas_call(
        functools.partial(_moe_body, layer=layer),
        grid_spec=grid_spec,
        out_shape=jax.ShapeDtypeStruct((n_slots, d), F32),
        compiler_params=pltpu.CompilerParams(dimension_semantics=("arbitrary",), vmem_limit_bytes=VMEM_LIMIT),
        name="moe_experts",
    )(tile_expert, n_used, group_index, next_expert, xs, bgu[:, :, None, :], bdn[:, :, None, :], wgu, wdn)


def _combine_body(dest_ref, x1_ref, gv_ref, gfin_ref, ys_hbm, y_o, ybuf, sem):
    t0 = pl.program_id(0) * TM

    def row_copy(r, k):
        d = dest_ref[(t0 + r) * TOP_EXPERTS + k]
        return pltpu.make_async_copy(ys_hbm.at[pl.ds(d, 1)], ybuf.at[k, pl.ds(r, 1)], sem)

    def start(r, _):
        for k in range(TOP_EXPERTS):
            row_copy(r, k).start()
        return 0
    lax.fori_loop(0, TM, start, 0, unroll=DMA_UNROLL)

    for k in range(TOP_EXPERTS):
        pltpu.make_async_copy(ys_hbm.at[pl.ds(0, TM)], ybuf.at[k], sem).wait()

    gv = gv_ref[...]
    y = jnp.zeros(x1_ref.shape, F32)
    for k in range(TOP_EXPERTS):
        y = y + ybuf[k] * gv[:, k:k + 1]
    y_o[...] = _rms(x1_ref[...] + y, gfin_ref[...])


def _combine(dest_flat, x1, gv, gfin, ys):
    nt, d = x1.shape
    grid_spec = pltpu.PrefetchScalarGridSpec(
        num_scalar_prefetch=1,
        grid=(nt // TM,),
        in_specs=[pl.BlockSpec((TM, d), lambda i, dref: (i, 0)),
                  pl.BlockSpec((TM, LANES), lambda i, dref: (i, 0)),
                  pl.BlockSpec((1, d), lambda i, dref: (0, 0)),
                  pl.BlockSpec(memory_space=pl.ANY)],
        out_specs=pl.BlockSpec((TM, d), lambda i, dref: (i, 0)),
        scratch_shapes=[pltpu.VMEM((TOP_EXPERTS, TM, d), F32), pltpu.SemaphoreType.DMA],
    )
    return pl.pallas_call(
        _combine_body,
        grid_spec=grid_spec,
        out_shape=jax.ShapeDtypeStruct((nt, d), F32),
        compiler_params=pltpu.CompilerParams(dimension_semantics=("arbitrary",)),
        name="combine_norm",
    )(dest_flat, x1, gv, gfin, ys)


def _tri_constants():
    r = lax.broadcasted_iota(I32, (TQ, TQ), 0)
    c = lax.broadcasted_iota(I32, (TQ, TQ), 1)
    incl = (r <= c).astype(BF16)
    u2 = jnp.concatenate([incl, jnp.ones((TQ, LANES), BF16)], axis=1)
    ltri = (c < r).astype(BF16)
    return u2, ltri


def kernel(x_prompt, x_sample, cache_k, cache_v, cache_kidx, state_pool, page_table, meta_tokens, rel_bias, g_mix, w_in, w_pool_mix, pool_scale, w_br_attn, w_br_pool, w_out, g_ffn, w_router, b_router, w_gate_up, b_gate_up, w_down, b_down, g_final):
    nb, seq, d = x_prompt.shape
    db, dec_seq, _ = x_sample.shape
    assert dec_seq == 1, "one new token per sample"
    depth = w_in.shape[0]
    assert depth == 1, "single-layer stack: the combine kernel applies the final norm"
    n_pages = page_table.shape[1]
    assert n_pages % SUBLANES == 0, "the sample top-k lays the cached keys out as eight equal page segments"
    past = n_pages * PAGE_SIZE
    pw = state_pool.shape[-1]
    l_seq = seq + N_META
    lp = _round_up(l_seq, TQ)
    n_prompt = nb * lp
    n_sample = _round_up(db, TM)
    nt = n_prompt + n_sample
    ksel_p = min(TOPK_MAX, seq // 4)
    ksel_s = min(TOPK_MAX, (past + dec_seq) // 4)
    n_tiles = (TOP_EXPERTS * nt) // TS + N_EXPERTS
    n_slots = n_tiles * TS

    meta = jnp.broadcast_to(meta_tokens[None].astype(x_prompt.dtype), (nb, N_META, d))
    xp = jnp.concatenate([meta, x_prompt, jnp.zeros((nb, lp - l_seq, d), x_prompt.dtype)], axis=1)
    xs_rows = jnp.concatenate([x_sample[:, 0, :], jnp.zeros((n_sample - db, d), x_sample.dtype)], axis=0)
    x_all = jnp.concatenate([xp.reshape(n_prompt, d), xs_rows], axis=0)

    u2, ltri = _tri_constants()
    page_flat = page_table.reshape(-1).astype(I32)
    rel_bias = rel_bias.astype(F32)
    cache_k_t = jnp.transpose(cache_k, (0, 1, 3, 4, 2))
    cache_v_t = jnp.transpose(cache_v, (0, 1, 3, 4, 2))
    cache_ki_t = jnp.transpose(cache_kidx, (0, 1, 3, 2))

    outs = {name: [] for name in ("k_p", "v_p", "ki_p", "pool_p", "k_s", "v_s", "ki_s", "pool_s")}
    for l in range(depth):
        wl = w_in[l]
        aw = ATTN_WIDTH
        o_ki = 4 * aw
        o_wi = o_ki + IDX_DIM
        o_p = o_wi + N_IDX_HEADS
        o_ga = o_p + pw
        o_gb = o_ga + d
        wa = wl[:, 0:o_ki].astype(BF16)
        wki = wl[:, o_ki:o_wi]
        wki2 = jnp.concatenate([wki, wki], axis=1).astype(BF16)
        wwi = jnp.pad(wl[:, o_wi:o_p], ((0, 0), (0, LANES - N_IDX_HEADS))).astype(BF16)
        wp = wl[:, o_p:o_ga].astype(BF16)
        wga = wl[:, o_ga:o_gb].astype(BF16)
        wgb = wl[:, o_gb:o_gb + d].astype(BF16)

        (qs, kf, vf, kb, vb, qib, ki2b, kif, wif, pf, ga, gb) = _inproj(
            x_all, g_mix[l][None, :], wa, wki2, wwi, wp, wga, wgb)

        tail_rows = _round_up(l_seq - (lp - TQ), 2 * SUBLANES)
        attn_p = _attn_prompt(qs, qib, wif, kb, vb, ki2b, u2, rel_bias, nb=nb, lp=lp, ksel=ksel_p,
                              tail_rows=tail_rows)
        pooled_p = _pool_prompt(pf, n_rows=n_prompt, lp=lp)

        sl = slice(n_prompt, n_prompt + db)
        keys = _sample_keys(page_flat, qib[sl].reshape(db, N_IDX_HEADS, IDX_DIM), kif[sl, 0:IDX_DIM][:, None, :],
                            wif[sl, 0:N_IDX_HEADS][:, :, None], cache_ki_t, layer=l, n_pages=n_pages)
        sel = _sample_select(keys, past=past, ksel=ksel_s)
        cols = lambda a: a[sl].astype(F32).reshape(db, N_HEADS, HEAD_DIM, 1)
        attn_s, pooled_s = _sample_attend(
            page_flat, sel, cols(qs), cols(kf), cols(vf), rel_bias, state_pool[l], pf[sl][:, None, :],
            cache_k_t, cache_v_t, layer=l, n_pages=n_pages)

        pad_s = lambda a: jnp.concatenate([a, jnp.zeros((n_sample - db, a.shape[1]), a.dtype)], axis=0)
        attn_all = jnp.concatenate([attn_p, pad_s(attn_s.reshape(db, ATTN_WIDTH))], axis=0)
        pooled_all = jnp.concatenate([pooled_p, pad_s(pooled_s[:, 0, :])], axis=0)

        wr = jnp.pad(w_router[l], ((0, 0), (0, LANES - N_EXPERTS))).astype(BF16)
        br = jnp.concatenate([b_router[l].astype(F32), jnp.full((LANES - N_EXPERTS,), NEG, F32)])[None, :]
        x1, h2p, ev, gv, rk, cnt = _merge(
            x_all, attn_all, pooled_all, ga, gb, w_br_attn[l].astype(BF16), w_pool_mix[l].astype(BF16),
            pool_scale[l][None, :], w_br_pool[l].astype(BF16), w_out[l].astype(BF16), g_ffn[l][None, :], wr, br, ltri)

        counts = cnt[0, 0:N_EXPERTS].astype(I32)
        padded = (counts + TS - 1) // TS * TS
        gend = jnp.cumsum(padded)
        gstart = gend - padded
        experts = jnp.arange(N_EXPERTS, dtype=I32)
        ev4 = ev[:, 0:TOP_EXPERTS]
        dest = (jnp.sum(jnp.where(ev4[:, :, None] == experts, gstart, 0), axis=2) + rk[:, 0:TOP_EXPERTS]).reshape(-1)
        tile_first = jnp.arange(n_tiles, dtype=I32)[:, None] * TS
        tile_expert = jnp.minimum(jnp.sum((gend[None, :] <= tile_first).astype(I32), axis=1), N_EXPERTS - 1)
        n_used = (gend[-1] // TS).astype(I32)[None]
        nonempty = counts > 0
        rank_e = jnp.cumsum(nonempty.astype(I32)) - nonempty.astype(I32)
        later = jnp.logical_and(nonempty[None, :], experts[None, :] > experts[:, None])
        next_e = jnp.min(jnp.where(later, experts[None, :], N_EXPERTS), axis=1)
        next_e = jnp.where(next_e == N_EXPERTS, -1, next_e)
        tile_onehot = tile_expert[:, None] == experts[None, :]
        group_index = jnp.sum(jnp.where(tile_onehot, rank_e, 0), axis=1).astype(I32)
        next_expert = jnp.sum(jnp.where(tile_onehot, next_e, 0), axis=1).astype(I32)

        xs_sorted = _dispatch(dest, jnp.concatenate([gend, gstart + counts]).astype(I32), h2p, n_slots=n_slots)
        ys = _moe(tile_expert, n_used, group_index, next_expert, xs_sorted, w_gate_up, b_gate_up, w_down, b_down,
                  layer=l)
        y_all = _combine(dest, x1, gv, g_final[None, :], ys)

        kp = kf[:n_prompt].reshape(nb, lp, N_HEADS, HEAD_DIM)[:, :l_seq]
        vp = vf[:n_prompt].reshape(nb, lp, N_HEADS, HEAD_DIM)[:, :l_seq]
        kip = kif[:n_prompt, 0:IDX_DIM].reshape(nb, lp, IDX_DIM)[:, :l_seq]
        pp = pf[:n_prompt].reshape(nb, lp, pw)
        outs["k_p"].append(kp)
        outs["v_p"].append(vp)
        outs["ki_p"].append(kip)
        outs["pool_p"].append(pp[:, l_seq - POOL_STATE:l_seq])
        outs["k_s"].append(kf[sl].reshape(db, 1, N_HEADS, HEAD_DIM))
        outs["v_s"].append(vf[sl].reshape(db, 1, N_HEADS, HEAD_DIM))
        outs["ki_s"].append(kif[sl, 0:IDX_DIM].reshape(db, 1, IDX_DIM))
        outs["pool_s"].append(jnp.concatenate([state_pool[l][:, 1:], pf[sl][:, None, :]], axis=1))

    y_prompt = y_all[:n_prompt].reshape(nb, lp, d)[:, N_META:l_seq]
    y_sample = y_all[n_prompt:n_prompt + db].reshape(db, 1, d)
    st = lambda name: jnp.stack(outs[name])
    return (y_prompt, y_sample, st("k_p"), st("v_p"), st("ki_p"), st("pool_p"),
            st("k_s"), st("v_s"), st("ki_s"), st("pool_s"))
```

```python
import functools
import math

import jax
import jax.numpy as jnp
from jax import lax
from jax.experimental import pallas as pl
from jax.experimental.pallas import tpu as pltpu

F32 = jnp.float32
BF16 = jnp.bfloat16
I32 = jnp.int32
U32 = jnp.uint32

N_META = 16
N_HEADS = 8
HEAD_DIM = 64
ATTN_WIDTH = N_HEADS * HEAD_DIM
N_IDX_HEADS = 8
IDX_DIM = 64
TOPK_MAX = 256
N_BUCKETS = 32
MAX_DISTANCE = 128
POOL_WINDOWS = (2, 4, 8, 16)
POOL_GROUPS = 4
POOL_STATE = 15
N_EXPERTS = 32
TOP_EXPERTS = 4
SWIGLU_LIMIT = 7.0
SWIGLU_ALPHA = 1.702
PAGE_SIZE = 128
EPS = 1e-6

LANES = 128
SUBLANES = 8
MXU_DIM = 256
TQ = MXU_DIM
TM = MXU_DIM
TS = MXU_DIM
VMEM_LIMIT = 56 * 1024 * 1024
DMA_UNROLL = 8

NEG = -1e30
INT_MIN = -(2 ** 31)

_MAX_EXACT = N_BUCKETS // 2
_BUCKET_THRESHOLDS = tuple(
    math.ceil(_MAX_EXACT * (MAX_DISTANCE / _MAX_EXACT) ** (j / (N_BUCKETS - _MAX_EXACT)))
    for j in range(1, N_BUCKETS - _MAX_EXACT))


def _round_up(a, m):
    return (a + m - 1) // m * m


def _rms(x, g):
    return x * lax.rsqrt(jnp.mean(x * x, axis=-1, keepdims=True) + EPS) * g


def _dot(a, b):
    return jnp.dot(a, b, preferred_element_type=F32)


def _dot_nt(a, b):
    return lax.dot_general(a, b, (((1,), (1,)), ((), ())), preferred_element_type=F32)


def _bucket(dist):
    large = jnp.full(dist.shape, _MAX_EXACT, I32)
    for thr in _BUCKET_THRESHOLDS:
        large = large + jnp.where(dist >= thr, 1, 0)
    return jnp.where(dist < _MAX_EXACT, dist, large)


def _sort_key(s):
    s = jnp.where(s == 0.0, 0.0, s)
    bits = lax.bitcast_convert_type(s, I32)
    return jnp.where(bits >= 0, bits, bits ^ jnp.int32(0x7FFFFFFF))


def _full(shape):
    return pl.BlockSpec(shape, lambda *_: (0,) * len(shape))


def _inproj_body(x_ref, g_ref, wa_ref, wki_ref, wwi_ref, wp_ref, wga_ref, wgb_ref,
                 qs_o, kf_o, vf_o, kb_o, vb_o, qib_o, ki2b_o, kif_o, wif_o, pf_o, ga_o, gb_o, kt_o, vt_o, kit_o):
    h = _rms(x_ref[...], g_ref[...]).astype(BF16)
    za = _dot(h, wa_ref[...])
    w = ATTN_WIDTH
    qs_o[...] = (za[:, 0:w] * (HEAD_DIM ** -0.5)).astype(BF16)
    k = za[:, w:2 * w]
    v = za[:, 2 * w:3 * w]
    kf_o[...] = k
    vf_o[...] = v
    kb_o[...] = k.astype(BF16)
    vb_o[...] = v.astype(BF16)
    qib_o[...] = za[:, 3 * w:4 * w].astype(BF16)
    ki2 = _dot(h, wki_ref[...])
    kif_o[...] = ki2
    ki2b_o[...] = ki2.astype(BF16)
    wif_o[...] = _dot(h, wwi_ref[...])
    pf_o[...] = _dot(h, wp_ref[...])
    ga_o[...] = _dot(h, wga_ref[...])
    gb_o[...] = _dot(h, wgb_ref[...])
    kt_o[0] = k.T.reshape(N_HEADS, HEAD_DIM, TM)
    vt_o[0] = v.T.reshape(N_HEADS, HEAD_DIM, TM)
    kit_o[0] = ki2.T[0:IDX_DIM, :]


def _inproj(x_all, g, wa, wki2, wwi, wp, wga, wgb, *, nb, lp, l_seq):
    nt, d = x_all.shape
    pw = wp.shape[1]
    tps = lp // TM
    row = lambda width: pl.BlockSpec((TM, width), lambda i: (i, 0))
    outs = [
        (ATTN_WIDTH, BF16), (ATTN_WIDTH, F32), (ATTN_WIDTH, F32), (ATTN_WIDTH, BF16), (ATTN_WIDTH, BF16),
        (ATTN_WIDTH, BF16), (LANES, BF16), (LANES, F32), (LANES, F32), (pw, F32), (d, F32), (d, F32)]
    seq_minor_specs = [
        pl.BlockSpec((1, N_HEADS, HEAD_DIM, TM), lambda i: (i // tps, 0, 0, i % tps)),
        pl.BlockSpec((1, N_HEADS, HEAD_DIM, TM), lambda i: (i // tps, 0, 0, i % tps)),
        pl.BlockSpec((1, IDX_DIM, TM), lambda i: (i // tps, 0, i % tps))]
    seq_minor_shapes = [
        jax.ShapeDtypeStruct((nb, N_HEADS, HEAD_DIM, l_seq), F32), jax.ShapeDtypeStruct((nb, N_HEADS, HEAD_DIM, l_seq), F32),
        jax.ShapeDtypeStruct((nb, IDX_DIM, l_seq), F32)]
    return pl.pallas_call(
        _inproj_body,
        grid=(nt // TM,),
        in_specs=[row(d), _full((1, d)), _full(wa.shape), _full(wki2.shape), _full(wwi.shape),
                  _full(wp.shape), _full(wga.shape), _full(wgb.shape)],
        out_specs=[row(wd) for wd, _ in outs] + seq_minor_specs,
        out_shape=[jax.ShapeDtypeStruct((nt, wd), dt) for wd, dt in outs] + seq_minor_shapes,
        compiler_params=pltpu.CompilerParams(dimension_semantics=("arbitrary",), vmem_limit_bytes=VMEM_LIMIT),
        name="inproj",
    )(x_all, g, wa, wki2, wwi, wp, wga, wgb)


def _attn_prompt_body(qs_ref, qib_ref, wif_ref, kb_ref, vb_ref, ki2b_ref, u2_ref, rb_ref,
                      o_ref, qis_scr, wb_scr, key_scr, ntab_scr, qz_scr, m_scr, acc_scr, sum_scr, *, ksel, tail_rows):
    b = pl.program_id(0)
    i = pl.program_id(1)
    nchunk = i + 1
    n_far = jnp.maximum(i - 1, 0)

    @pl.when((b == 0) & (i == 0))
    def _():
        def slab_rows(s, _):
            r0 = pl.multiple_of(s * SUBLANES, SUBLANES)
            r = r0 + lax.broadcasted_iota(I32, (SUBLANES, 2 * TQ), 0)
            x = lax.broadcasted_iota(I32, (SUBLANES, 2 * TQ), 1)
            bucket = _bucket(jnp.maximum(r + TQ - x, 0))
            accs = [jnp.zeros((SUBLANES, 2 * TQ), F32) for _ in range(N_HEADS)]
            for j in range(N_BUCKETS - 1):
                m = bucket == j
                accs = [jnp.where(m, rb_ref[j, h] - rb_ref[N_BUCKETS - 1, h], accs[h]) for h in range(N_HEADS)]
            for h in range(N_HEADS):
                ntab_scr[h, 0, pl.ds(r0, SUBLANES), :] = accs[h][:, 0:TQ]
                ntab_scr[h, 1, pl.ds(r0, SUBLANES), :] = accs[h][:, TQ:2 * TQ]
            return 0
        lax.fori_loop(0, TQ // SUBLANES, slab_rows, 0)

    def two(x):
        return jnp.concatenate([x, x], axis=1)

    def block(nr):
        hr = nr // 2
        lane = lax.broadcasted_iota(I32, (nr, LANES), 1)
        lo_half = lane < HEAD_DIM

        for h in range(N_HEADS):
            cols = slice((h // 2) * LANES, (h // 2 + 1) * LANES)
            keep = lo_half if h % 2 == 0 else jnp.logical_not(lo_half)
            qis_scr[h, 0:nr] = jnp.where(keep, qib_ref[0:nr, cols].astype(F32), 0.0).astype(BF16)
            qz_scr[h // 2, (h % 2) * nr:(h % 2 + 1) * nr] = jnp.where(keep, qs_ref[0:nr, cols].astype(F32), 0.0).astype(BF16)
            wb_scr[h, 0:nr] = jnp.broadcast_to(wif_ref[0:nr, h:h + 1], (nr, LANES))
            m_scr[h, 0:nr] = jnp.full((nr, LANES), NEG, F32)
            acc_scr[h, 0:nr] = jnp.zeros((nr, LANES), F32)
            sum_scr[h, 0:nr] = jnp.zeros((nr, LANES), F32)

        dmat = lax.broadcasted_iota(I32, (nr, TQ), 1) - lax.broadcasted_iota(I32, (nr, TQ), 0)

        def score_chunk(c, _):
            kc = ki2b_ref[pl.ds(pl.multiple_of(c * TQ, TQ), TQ), :]
            s = jnp.zeros((nr, TQ), F32)
            for h in range(N_IDX_HEADS):
                s = s + two(wb_scr[h, 0:nr]) * jnp.maximum(_dot_nt(qis_scr[h, 0:nr], kc), 0.0)
            key_scr[c, 0:nr] = jnp.where(dmat <= (i - c) * TQ, _sort_key(s), INT_MIN)
            return 0
        lax.fori_loop(0, nchunk, score_chunk, 0)

        def count(r0, pred):
            def body(c, acc):
                hit = jnp.where(pred(key_scr[c, r0:r0 + hr, :]), 1.0, 0.0)
                return acc + (hit[:, 0:LANES] + hit[:, LANES:2 * LANES])
            acc = lax.fori_loop(0, nchunk, body, jnp.zeros((hr, LANES), F32))
            return jnp.broadcast_to(jnp.sum(acc, axis=1, keepdims=True), (hr, LANES))

        ts, needs = [], []
        for r0 in (0, hr):
            def bit_step(step, t, r0=r0):
                cand = t + lax.shift_left(jnp.int32(1), 31 - step)
                cand2 = two(cand)
                return jnp.where(count(r0, lambda k: k >= cand2) >= ksel, cand, t)
            t = lax.fori_loop(0, 32, bit_step, jnp.full((hr, LANES), INT_MIN, I32))
            th2 = two(t)
            n_gt = count(r0, lambda k: k > th2)
            ts.append(t)
            needs.append(jnp.where(t == INT_MIN, 0.0, ksel - n_gt))
        t2 = two(jnp.concatenate(ts, axis=0))
        need2 = two(jnp.concatenate(needs, axis=0))

        def mask_chunk(c, carry):
            kc = key_scr[c, 0:nr]
            eq = kc == t2
            pre = _dot(jnp.where(eq, 1.0, 0.0).astype(BF16), u2_ref[...])
            prefix = pre[:, 0:TQ] + two(carry)
            tie_ok = jnp.where(eq, prefix, 3e38) <= need2
            madd = jnp.where(kc > t2, 0.0, jnp.where(tie_ok, 0.0, NEG))
            key_scr[c, 0:nr] = lax.bitcast_convert_type(madd, I32)
            return carry + pre[:, TQ:TQ + LANES]
        lax.fori_loop(0, nchunk, mask_chunk, jnp.zeros((nr, LANES), F32))

        def chunk_logits(c, slab):
            rows = pl.ds(pl.multiple_of(c * TQ, TQ), TQ)
            madd = lax.bitcast_convert_type(key_scr[c, 0:nr], F32)
            out = []
            for hp in range(N_HEADS // 2):
                kc = kb_ref[rows, hp * LANES:(hp + 1) * LANES]
                pair = _dot_nt(qz_scr[hp, 0:2 * nr], kc)
                for sub in range(2):
                    l = pair[sub * nr:(sub + 1) * nr] + madd
                    out.append(l if slab is None else l + ntab_scr[2 * hp + sub, slab, 0:nr])
            return rows, out

        def row_max(c, slab):
            _, ls = chunk_logits(c, slab)
            for h, l in enumerate(ls):
                m_scr[h, 0:nr] = jnp.maximum(m_scr[h, 0:nr], jnp.maximum(l[:, 0:LANES], l[:, LANES:2 * LANES]))

        def accumulate(c, slab):
            rows, ls = chunk_logits(c, slab)
            for hp in range(N_HEADS // 2):
                ps = []
                for h in (2 * hp, 2 * hp + 1):
                    p = jnp.exp(ls[h] - two(m_scr[h, 0:nr]))
                    sum_scr[h, 0:nr] = sum_scr[h, 0:nr] + (p[:, 0:LANES] + p[:, LANES:2 * LANES])
                    ps.append(p.astype(BF16))
                pv = _dot(jnp.concatenate(ps, axis=0), vb_ref[rows, hp * LANES:(hp + 1) * LANES])
                for sub in range(2):
                    h = 2 * hp + sub
                    acc_scr[h, 0:nr] = acc_scr[h, 0:nr] + pv[sub * nr:(sub + 1) * nr]

        def sweep(step):
            def far(c, _):
                step(c, None)
                return 0
            lax.fori_loop(0, n_far, far, 0)

            @pl.when(i >= 1)
            def _():
                step(i - 1, 0)
            step(i, 1)

        sweep(row_max)
        for h in range(N_HEADS):
            m_scr[h, 0:nr] = jnp.broadcast_to(jnp.max(m_scr[h, 0:nr], axis=1, keepdims=True), (nr, LANES))
        sweep(accumulate)
        for hp in range(N_HEADS // 2):
            outs = [acc_scr[h, 0:nr] / jnp.sum(sum_scr[h, 0:nr], axis=1, keepdims=True) for h in (2 * hp, 2 * hp + 1)]
            o_ref[0:nr, hp * LANES:(hp + 1) * LANES] = jnp.where(lo_half, outs[0], outs[1]).astype(BF16)
        if nr < TQ:
            o_ref[nr:TQ, :] = jnp.zeros((TQ - nr, ATTN_WIDTH), BF16)

    if tail_rows == TQ:
        block(TQ)
    else:
        last = pl.num_programs(1) - 1
        pl.when(i < last)(lambda: block(TQ))
        pl.when(i == last)(lambda: block(tail_rows))


def _attn_prompt(qs, qib, wif, kb, vb, ki2b, u2, rel_bias, *, nb, lp, ksel, tail_rows):
    nq = lp // TQ
    rowq = lambda width: pl.BlockSpec((TQ, width), lambda b, i: (b * nq + i, 0))
    seq = lambda width: pl.BlockSpec((lp, width), lambda b, i: (b, 0))
    per_head = lambda dt: pltpu.VMEM((N_HEADS, TQ, LANES), dt)
    return pl.pallas_call(
        functools.partial(_attn_prompt_body, ksel=float(ksel), tail_rows=tail_rows),
        grid=(nb, nq),
        in_specs=[rowq(ATTN_WIDTH), rowq(ATTN_WIDTH), rowq(LANES), seq(ATTN_WIDTH), seq(ATTN_WIDTH), seq(LANES),
                  _full(u2.shape), pl.BlockSpec(memory_space=pltpu.SMEM)],
        out_specs=rowq(ATTN_WIDTH),
        out_shape=jax.ShapeDtypeStruct((nb * lp, ATTN_WIDTH), BF16),
        scratch_shapes=[
            per_head(BF16),
            per_head(F32),
            pltpu.VMEM((nq, TQ, TQ), I32),
            pltpu.VMEM((N_HEADS, 2, TQ, TQ), F32),
            pltpu.VMEM((N_HEADS // 2, 2 * TQ, LANES), BF16),
            per_head(F32), per_head(F32), per_head(F32),
        ],
        compiler_params=pltpu.CompilerParams(dimension_semantics=("arbitrary", "arbitrary"),
                                             vmem_limit_bytes=VMEM_LIMIT),
        name="attn_prompt",
    )(qs, qib, wif, kb, vb, ki2b, u2, rel_bias)


def _pool_prompt_body(p_ref, halo_ref, o_ref, ext_scr, *, tiles_per_seq):
    j = pl.program_id(0) % tiles_per_seq
    halo = 2 * SUBLANES
    p = p_ref[...]
    ext_scr[0:halo, :] = jnp.where(j == 0, 0.0, halo_ref[...])
    ext_scr[halo:halo + TM, :] = p
    pos = j * TM + lax.broadcasted_iota(I32, (TM, 1), 0)
    gw = p.shape[1] // POOL_GROUPS
    for g, w in enumerate(POOL_WINDOWS):
        cols = slice(g * gw, (g + 1) * gw)
        s = p[:, cols]
        for back in range(1, w):
            s = s + ext_scr[halo - back:halo - back + TM, cols]
        cnt = jnp.minimum(pos + 1, w).astype(F32)
        o_ref[:, cols] = (s / cnt - p[:, cols]).astype(BF16)


def _pool_prompt(pf, *, n_rows, lp):
    pw = pf.shape[1]
    halo = 2 * SUBLANES
    return pl.pallas_call(
        functools.partial(_pool_prompt_body, tiles_per_seq=lp // TM),
        grid=(n_rows // TM,),
        in_specs=[pl.BlockSpec((TM, pw), lambda i: (i, 0)),
                  pl.BlockSpec((halo, pw), lambda i: (jnp.maximum(i * (TM // halo) - 1, 0), 0))],
        out_specs=pl.BlockSpec((TM, pw), lambda i: (i, 0)),
        out_shape=jax.ShapeDtypeStruct((n_rows, pw), BF16),
        scratch_shapes=[pltpu.VMEM((halo + TM, pw), F32)],
        compiler_params=pltpu.CompilerParams(dimension_semantics=("arbitrary",)),
        name="pool_prompt",
    )(pf, pf)


def _sample_keys_body(pt_ref, qi_ref, kinew_ref, wi_ref, cache_hbm, key_o, kibuf, sem, *, layer, n_pages):
    s = pl.program_id(0)
    past = n_pages * PAGE_SIZE
    seg_w = past // SUBLANES

    copies = [pltpu.make_async_copy(cache_hbm.at[layer, pt_ref[s * n_pages + pg]],
                                    kibuf.at[:, pg * PAGE_SIZE:(pg + 1) * PAGE_SIZE], sem) for pg in range(n_pages)]
    for cp in copies:
        cp.start()
    for cp in copies:
        cp.wait()

    qi = qi_ref[0]
    wcol = wi_ref[0]
    d = _dot(qi, kibuf[...].astype(BF16))
    sc = jnp.sum(wcol * jnp.maximum(d, 0.0), axis=0, keepdims=True)
    knew = kinew_ref[0].astype(BF16).astype(F32)
    dn = jnp.sum(qi.astype(F32) * knew, axis=1, keepdims=True)
    sn = jnp.sum(wcol * jnp.maximum(dn, 0.0), axis=0, keepdims=True)
    key = _sort_key(sc)
    lane = lax.broadcasted_iota(I32, (1, LANES), 1)
    new_tail = jnp.where(lane == 0, _sort_key(jnp.broadcast_to(sn, (1, LANES))), INT_MIN)
    for g in range(SUBLANES):
        key_o[0, g:g + 1, 0:seg_w] = key[:, g * seg_w:(g + 1) * seg_w]
        key_o[0, g:g + 1, seg_w:seg_w + LANES] = new_tail if g == 0 else jnp.full((1, LANES), INT_MIN, I32)


def _sample_keys(page_table_flat, qi3, kinew3, wi3, cache_ki_t, *, layer, n_pages):
    db = qi3.shape[0]
    past = n_pages * PAGE_SIZE
    kw = past // SUBLANES + LANES
    grid_spec = pltpu.PrefetchScalarGridSpec(
        num_scalar_prefetch=1,
        grid=(db,),
        in_specs=[pl.BlockSpec((1, N_IDX_HEADS, IDX_DIM), lambda s, pt: (s, 0, 0)),
                  pl.BlockSpec((1, 1, IDX_DIM), lambda s, pt: (s, 0, 0)),
                  pl.BlockSpec((1, N_IDX_HEADS, 1), lambda s, pt: (s, 0, 0)),
                  pl.BlockSpec(memory_space=pl.ANY)],
        out_specs=pl.BlockSpec((1, SUBLANES, kw), lambda s, pt: (s, 0, 0)),
        scratch_shapes=[pltpu.VMEM((IDX_DIM, past), F32), pltpu.SemaphoreType.DMA],
    )
    return pl.pallas_call(
        functools.partial(_sample_keys_body, layer=layer, n_pages=n_pages),
        grid_spec=grid_spec,
        out_shape=jax.ShapeDtypeStruct((db, SUBLANES, kw), I32),
        compiler_params=pltpu.CompilerParams(dimension_semantics=("arbitrary",)),
        name="sample_keys",
    )(page_table_flat, qi3, kinew3, wi3, cache_ki_t)


def _sample_select_body(key_ref, sel_o, *, past, ksel):
    keys = key_ref[...]
    kw = keys.shape[2]
    seg_w = kw - LANES
    row = lax.broadcasted_iota(I32, (SUBLANES, kw), 0)
    lane = lax.broadcasted_iota(I32, (SUBLANES, kw), 1)
    pos = jnp.where(lane < seg_w, row * seg_w + lane, past + (lane - seg_w) + row * LANES)[None]

    def cnt(pred):
        x = jnp.where(pred, 1.0, 0.0)
        return jnp.sum(jnp.sum(x, axis=2, keepdims=True), axis=1, keepdims=True)

    def bit_step(step, t):
        cand = t + lax.shift_left(jnp.int32(1), 31 - step)
        return jnp.where(cnt(keys >= cand) >= ksel, cand, t)
    t = lax.fori_loop(0, 32, bit_step, jnp.full((keys.shape[0], 1, 1), INT_MIN, I32))
    gt = keys > t
    eq = keys == t
    need = ksel - cnt(gt)

    nbits = (past + SUBLANES * LANES).bit_length()

    def cut_step(step, c):
        cand = c - lax.shift_left(jnp.int32(1), nbits - 1 - step)
        ok = cnt(jnp.logical_and(eq, pos <= cand)) >= need
        return jnp.where(ok, cand, c)
    cut = lax.fori_loop(0, nbits, cut_step, jnp.full((keys.shape[0], 1, 1), 2 ** nbits - 1, I32))
    sel = jnp.where(gt, 1.0, jnp.where(jnp.logical_and(eq, pos <= cut), 1.0, 0.0))
    sel_o[...] = jnp.where(pos <= past, sel, 0.0)


def _sample_select(keys, *, past, ksel):
    return pl.pallas_call(
        functools.partial(_sample_select_body, past=past, ksel=float(ksel)),
        out_shape=jax.ShapeDtypeStruct(keys.shape, F32),
        name="sample_select",
    )(keys)


def _sample_attend_body(pt_ref, sel_ref, q_ref, knew_ref, vnew_ref, rb_ref, state_ref, pnew_ref, ck_hbm, cv_hbm,
                        attn_o, pool_o, kbuf, vbuf, qb_scr, lg_scr, acc_scr, ksem, vsem, *, layer, n_pages):
    s = pl.program_id(0)
    past = n_pages * PAGE_SIZE
    seg_w = past // SUBLANES
    pages_per_seg = n_pages // SUBLANES
    last_bias = [rb_ref[N_BUCKETS - 1, h] for h in range(N_HEADS)]

    def kcopy(pg, sample=s):
        return pltpu.make_async_copy(ck_hbm.at[layer, pt_ref[sample * n_pages + pg]], kbuf.at[pg], ksem.at[pg])

    def vcopy(pg, sample=s):
        return pltpu.make_async_copy(cv_hbm.at[layer, pt_ref[sample * n_pages + pg]], vbuf.at[pg], vsem.at[pg])

    def request(copy, sample):
        def start(pg, _):
            copy(pg, sample).start()
            return 0
        lax.fori_loop(0, n_pages, start, 0)

    has_next = s + 1 < pl.num_programs(0)

    @pl.when(s == 0)
    def _():
        request(kcopy, s)
        request(vcopy, s)

    for h in range(N_HEADS):
        qb_scr[h] = jnp.broadcast_to(q_ref[0, h], (HEAD_DIM, LANES))
        acc_scr[h] = jnp.zeros((HEAD_DIM, LANES), F32)

    def k_page(pg, _):
        kcopy(pg).wait()
        rows = [jnp.sum(kbuf[pg, h].astype(BF16).astype(F32) * qb_scr[h], axis=0, keepdims=True)
                for h in range(N_HEADS)]
        lg_scr[pg] = jnp.concatenate(rows, axis=0)
        return 0
    lax.fori_loop(0, n_pages, k_page, 0)
    pl.when(has_next)(lambda: request(kcopy, s + 1))

    lane = lax.broadcasted_iota(I32, (1, LANES), 1)
    bucket = _bucket(past - ((n_pages - 1) * PAGE_SIZE + lane))
    near_rows = []
    for h in range(N_HEADS):
        r = jnp.zeros((1, LANES), F32)
        for j in range(N_BUCKETS - 1):
            r = jnp.where(bucket == j, rb_ref[j, h] - last_bias[h], r)
        near_rows.append(r)
    near = jnp.concatenate(near_rows, axis=0)

    sel = sel_ref[0]
    m = jnp.full((N_HEADS, LANES), NEG, F32)
    for pg in range(n_pages):
        g, j = divmod(pg, pages_per_seg)
        keep = sel[g:g + 1, j * PAGE_SIZE:(j + 1) * PAGE_SIZE] > 0.0
        l = lg_scr[pg] + jnp.where(keep, 0.0, NEG)
        if pg == n_pages - 1:
            l = l + near
        lg_scr[pg] = l
        m = jnp.maximum(m, l)
    new_rows = []
    for h in range(N_HEADS):
        kn = knew_ref[0, h].astype(BF16).astype(F32)
        new_rows.append(jnp.sum(q_ref[0, h] * kn, axis=0, keepdims=True) + (rb_ref[0, h] - last_bias[h]))
    l_new = jnp.where(sel[0:1, seg_w:seg_w + 1] > 0.0, jnp.concatenate(new_rows, axis=0), NEG)
    m = jnp.maximum(jnp.max(m, axis=1, keepdims=True), l_new)
    ssum = jnp.zeros((N_HEADS, LANES), F32)
    for pg in range(n_pages):
        p = jnp.exp(lg_scr[pg] - m)
        lg_scr[pg] = p
        ssum = ssum + p
    p_new = jnp.exp(l_new - m)
    den = jnp.sum(ssum, axis=1, keepdims=True) + p_new
    for pg in range(n_pages):
        lg_scr[pg] = (lg_scr[pg] / den).astype(BF16).astype(F32)
    p_new = (p_new / den).astype(BF16).astype(F32)

    def v_page(pg, _):
        vcopy(pg).wait()
        p = lg_scr[pg]
        for h in range(N_HEADS):
            acc_scr[h] = acc_scr[h] + vbuf[pg, h].astype(BF16).astype(F32) * p[h:h + 1, :]
        return 0
    lax.fori_loop(0, n_pages, v_page, 0)
    pl.when(has_next)(lambda: request(vcopy, s + 1))
    for h in range(N_HEADS):
        vn = vnew_ref[0, h].astype(BF16).astype(F32)
        attn_o[0, h] = jnp.sum(acc_scr[h], axis=1, keepdims=True) + p_new[h:h + 1, :] * vn

    pnew = pnew_ref[0]
    st = state_ref[0]
    gw = pnew.shape[1] // POOL_GROUPS
    for g, w in enumerate(POOL_WINDOWS):
        cols = slice(g * gw, (g + 1) * gw)
        acc = pnew[:, cols]
        for back in range(1, w):
            acc = acc + st[POOL_STATE - back:POOL_STATE - back + 1, cols]
        pool_o[0, :, cols] = (acc / float(min(POOL_STATE + 1, w)) - pnew[:, cols]).astype(BF16)


def _sample_attend(page_table_flat, sel, q4, knew4, vnew4, rel_bias, state, pnew3, cache_k_t, cache_v_t, *, layer, n_pages):
    db = q4.shape[0]
    pw = pnew3.shape[2]
    per = lambda shape: pl.BlockSpec((1,) + shape, lambda s, pt: (s,) + (0,) * len(shape))
    col = (N_HEADS, HEAD_DIM, 1)
    page = (n_pages, N_HEADS, HEAD_DIM, LANES)
    grid_spec = pltpu.PrefetchScalarGridSpec(
        num_scalar_prefetch=1,
        grid=(db,),
        in_specs=[per(sel.shape[1:]), per(col), per(col), per(col), pl.BlockSpec(memory_space=pltpu.SMEM),
                  per((POOL_STATE, pw)), per((1, pw)),
                  pl.BlockSpec(memory_space=pl.ANY), pl.BlockSpec(memory_space=pl.ANY)],
        out_specs=[per(col), per((1, pw))],
        scratch_shapes=[pltpu.VMEM(page, F32), pltpu.VMEM(page, F32),
                        pltpu.VMEM((N_HEADS, HEAD_DIM, LANES), F32),
                        pltpu.VMEM((n_pages, N_HEADS, LANES), F32),
                        pltpu.VMEM((N_HEADS, HEAD_DIM, LANES), F32),
                        pltpu.SemaphoreType.DMA((n_pages,)), pltpu.SemaphoreType.DMA((n_pages,))],
    )
    return pl.pallas_call(
        functools.partial(_sample_attend_body, layer=layer, n_pages=n_pages),
        grid_spec=grid_spec,
        out_shape=[jax.ShapeDtypeStruct((db,) + col, F32), jax.ShapeDtypeStruct((db, 1, pw), BF16)],
        compiler_params=pltpu.CompilerParams(dimension_semantics=("arbitrary",), vmem_limit_bytes=VMEM_LIMIT),
        name="sample_attend",
    )(page_table_flat, sel, q4, knew4, vnew4, rel_bias, state, pnew3, cache_k_t, cache_v_t)


def _merge_body(x_ref, attn_ref, pool_ref, ga_ref, gb_ref, wba_ref, wpm_ref, ps_ref, wbp_ref, wout_ref,
                gffn_ref, wr_ref, br_ref, ltri_ref,
                x1_o, h2p_o, ev_o, gv_o, rk_o, cnt_o, carry_scr):
    step = pl.program_id(0)

    @pl.when(step == 0)
    def _():
        carry_scr[...] = jnp.zeros_like(carry_scr)

    a = _dot(attn_ref[...], wba_ref[...])
    pooled = pool_ref[...]
    gw = pooled.shape[1] // POOL_GROUPS
    pm = jnp.concatenate([_dot(pooled[:, g * gw:(g + 1) * gw], wpm_ref[g]) for g in range(POOL_GROUPS)], axis=1)
    bb = _dot((pm * ps_ref[...]).astype(BF16), wbp_ref[...])
    mix = jax.nn.sigmoid(ga_ref[...]) * a + jax.nn.sigmoid(gb_ref[...]) * bb
    x1 = x_ref[...] + _dot(mix.astype(BF16), wout_ref[...])
    x1_o[...] = x1
    h2 = _rms(x1, gffn_ref[...]).astype(BF16)
    half = h2.shape[1] // 2
    lo = lax.shift_right_logical(lax.bitcast_convert_type(h2[:, 0:half].astype(F32), U32), jnp.uint32(16))
    hi = lax.bitcast_convert_type(h2[:, half:].astype(F32), U32) & jnp.uint32(0xFFFF0000)
    h2p_o[...] = hi | lo

    logits = _dot(h2, wr_ref[...]) + br_ref[...]
    lane = lax.broadcasted_iota(I32, (TM, LANES), 1)
    lanef = lane.astype(F32)
    tops, ids = [], []
    l = logits
    for _ in range(TOP_EXPERTS):
        mx = jnp.max(l, axis=1, keepdims=True)
        ix = jnp.min(jnp.where(l == mx, lanef, float(LANES)), axis=1, keepdims=True)
        tops.append(mx)
        ids.append(ix)
        l = jnp.where(lanef == ix, -3e38, l)
    es = [jnp.exp(tv - tops[0]) for tv in tops]
    den = es[0] + es[1] + es[2] + es[3]
    onehot = jnp.zeros((TM, LANES), F32)
    for ix in ids:
        onehot = onehot + jnp.where(lanef == ix, 1.0, 0.0)
    before = _dot(ltri_ref[...], onehot.astype(BF16)) + carry_scr[0:1, :]
    ev = jnp.zeros((TM, LANES), I32)
    gv = jnp.zeros((TM, LANES), F32)
    rk = jnp.zeros((TM, LANES), I32)
    for k in range(TOP_EXPERTS):
        rank = jnp.sum(jnp.where(lanef == ids[k], before, 0.0), axis=1, keepdims=True)
        ev = jnp.where(lane == k, ids[k].astype(I32), ev)
        gv = jnp.where(lane == k, es[k] / den, gv)
        rk = jnp.where(lane == k, rank.astype(I32), rk)
    ev_o[...] = ev
    gv_o[...] = gv
    rk_o[...] = rk
    carry_scr[...] = carry_scr[...] + jnp.sum(onehot, axis=0, keepdims=True)
    cnt_o[...] = carry_scr[...]


def _merge(x_all, attn, pooled, ga, gb, wba, wpm, ps, wbp, wout, gffn, wr, br, ltri):
    nt, d = x_all.shape
    row = lambda width: pl.BlockSpec((TM, width), lambda i: (i, 0))
    consts = [wba, wpm, ps, wbp, wout, gffn, wr, br, ltri]
    return pl.pallas_call(
        _merge_body,
        grid=(nt // TM,),
        in_specs=[row(d), row(attn.shape[1]), row(pooled.shape[1]), row(d), row(d)] + [_full(c.shape) for c in consts],
        out_specs=[row(d), row(d // 2), row(LANES), row(LANES), row(LANES), _full((SUBLANES, LANES))],
        out_shape=[jax.ShapeDtypeStruct((nt, d), F32), jax.ShapeDtypeStruct((nt, d // 2), U32),
                   jax.ShapeDtypeStruct((nt, LANES), I32), jax.ShapeDtypeStruct((nt, LANES), F32),
                   jax.ShapeDtypeStruct((nt, LANES), I32), jax.ShapeDtypeStruct((SUBLANES, LANES), F32)],
        scratch_shapes=[pltpu.VMEM((SUBLANES, LANES), F32)],
        compiler_params=pltpu.CompilerParams(dimension_semantics=("arbitrary",), vmem_limit_bytes=VMEM_LIMIT),
        name="merge_router",
    )(x_all, attn, pooled, ga, gb, *consts)


def _dispatch_body(dest_ref, gend_ref, h2p_ref, xs_hbm, zero_scr, sem, zsem):
    step = pl.program_id(0)

    @pl.when(step == 0)
    def _():
        zero_scr[...] = jnp.zeros_like(zero_scr)

        def fill(e):
            return pltpu.make_async_copy(zero_scr, xs_hbm.at[pl.ds(pl.multiple_of(gend_ref[e] - TS, TS), TS)], zsem)

        def start(e, _):
            @pl.when(gend_ref[e] > gend_ref[e + N_EXPERTS])
            def _():
                fill(e).start()
            return 0
        lax.fori_loop(0, N_EXPERTS, start, 0)

        def wait(e, _):
            @pl.when(gend_ref[e] > gend_ref[e + N_EXPERTS])
            def _():
                fill(e).wait()
            return 0
        lax.fori_loop(0, N_EXPERTS, wait, 0)

        def tail(j):
            return pltpu.make_async_copy(zero_scr, xs_hbm.at[pl.ds(pl.multiple_of(j * TS, TS), TS)], zsem)

        first_unused = gend_ref[N_EXPERTS - 1] // TS
        n_tiles = xs_hbm.shape[0] // TS

        def tail_start(j, _):
            tail(j).start()
            return 0
        lax.fori_loop(first_unused, n_tiles, tail_start, 0)

        def tail_wait(j, _):
            tail(j).wait()
            return 0
        lax.fori_loop(first_unused, n_tiles, tail_wait, 0)

    t0 = step * TM

    def row_copy(r, k):
        d = dest_ref[(t0 + r) * TOP_EXPERTS + k]
        return pltpu.make_async_copy(h2p_ref.at[pl.ds(r, 1)], xs_hbm.at[pl.ds(d, 1)], sem)

    def start(r, _):
        for k in range(TOP_EXPERTS):
            row_copy(r, k).start()
        return 0
    lax.fori_loop(0, TM, start, 0, unroll=DMA_UNROLL)

    for _ in range(TOP_EXPERTS):
        pltpu.make_async_copy(h2p_ref, xs_hbm.at[pl.ds(0, TM)], sem).wait()


def _dispatch(dest_flat, gend, h2p, *, n_slots):
    nt, hw = h2p.shape
    grid_spec = pltpu.PrefetchScalarGridSpec(
        num_scalar_prefetch=2,
        grid=(nt // TM,),
        in_specs=[pl.BlockSpec((TM, hw), lambda i, d, g: (i, 0))],
        out_specs=pl.BlockSpec(memory_space=pl.ANY),
        scratch_shapes=[pltpu.VMEM((TS, hw), U32), pltpu.SemaphoreType.DMA, pltpu.SemaphoreType.DMA],
    )
    return pl.pallas_call(
        _dispatch_body,
        grid_spec=grid_spec,
        out_shape=jax.ShapeDtypeStruct((n_slots, hw), U32),
        compiler_params=pltpu.CompilerParams(dimension_semantics=("arbitrary",)),
        name="dispatch",
    )(dest_flat, gend, h2p)


def _moe_body(te_ref, nu_ref, gi_ref, nx_ref, xs_ref, bgu_ref, bdn_ref, wgu_hbm, wdn_hbm, ys_o,
              wgu_f, wdn_f, wgu_b, wdn_b, sem, *, layer):
    j = pl.program_id(0)
    expert = te_ref[j]
    live = j < nu_ref[0]
    slot = gi_ref[j] % 2

    def weight_copies(e, s):
        return (pltpu.make_async_copy(wgu_hbm.at[layer, e], wgu_f.at[s], sem.at[0, s]),
                pltpu.make_async_copy(wdn_hbm.at[layer, e], wdn_f.at[s], sem.at[1, s]))

    @pl.when(jnp.logical_and(live, jnp.logical_or(j == 0, expert != te_ref[jnp.maximum(j - 1, 0)])))
    def _():
        @pl.when(j == 0)
        def _():
            for cp in weight_copies(expert, slot):
                cp.start()

        @pl.when(nx_ref[j] >= 0)
        def _():
            for cp in weight_copies(nx_ref[j], 1 - slot):
                cp.start()
        for cp in weight_copies(expert, slot):
            cp.wait()
        wgu_b[...] = wgu_f[slot].astype(BF16)
        wdn_b[...] = wdn_f[slot].astype(BF16)

    @pl.when(live)
    def _():
        words = xs_ref[...]
        x_lo = lax.bitcast_convert_type(lax.shift_left(words, jnp.uint32(16)), F32).astype(BF16)
        x_hi = lax.bitcast_convert_type(words & jnp.uint32(0xFFFF0000), F32).astype(BF16)
        half = words.shape[1]
        gu = _dot(x_lo, wgu_b[0:half, :]) + _dot(x_hi, wgu_b[half:, :]) + bgu_ref[0, 0]
        de = gu.shape[1] // 2
        gate = jnp.minimum(gu[:, 0:de], SWIGLU_LIMIT)
        up = jnp.clip(gu[:, de:], -SWIGLU_LIMIT, SWIGLU_LIMIT)
        act = (up + 1.0) * gate * jax.nn.sigmoid(SWIGLU_ALPHA * gate)
        ys_o[...] = _dot(act.astype(BF16), wdn_b[...]) + bdn_ref[0, 0]

    @pl.when(jnp.logical_not(live))
    def _():
        ys_o[...] = jnp.zeros_like(ys_o)


def _moe(tile_expert, n_used, group_index, next_expert, xs, wgu, bgu, wdn, bdn, *, layer):
    n_slots, hw = xs.shape
    _, ne, d, de2 = wgu.shape
    bias = lambda cols: pl.BlockSpec((1, 1, 1, cols), lambda j, te, nu, gi, nx: (layer, te[j], 0, 0))
    grid_spec = pltpu.PrefetchScalarGridSpec(
        num_scalar_prefetch=4,
        grid=(n_slots // TS,),
        in_specs=[pl.BlockSpec((TS, hw), lambda j, te, nu, gi, nx: (jnp.minimum(j, nu[0] - 1), 0)),
                  bias(de2), bias(d), pl.BlockSpec(memory_space=pl.ANY), pl.BlockSpec(memory_space=pl.ANY)],
        out_specs=pl.BlockSpec((TS, d), lambda j, te, nu, gi, nx: (j, 0)),
        scratch_shapes=[pltpu.VMEM((2, d, de2), F32), pltpu.VMEM((2, de2 // 2, d), F32),
                        pltpu.VMEM((d, de2), BF16), pltpu.VMEM((de2 // 2, d), BF16),
                        pltpu.SemaphoreType.DMA((2, 2))],
    )
    return pl.pallas_call(
        functools.partial(_moe_body, layer=layer),
        grid_spec=grid_spec,
        out_shape=jax.ShapeDtypeStruct((n_slots, d), F32),
        compiler_params=pltpu.CompilerParams(dimension_semantics=("arbitrary",), vmem_limit_bytes=VMEM_LIMIT),
        name="moe_experts",
    )(tile_expert, n_used, group_index, next_expert, xs, bgu[:, :, None, :], bdn[:, :, None, :], wgu, wdn)


def _combine_body(dest_ref, x1_ref, gv_ref, gfin_ref, ys_hbm, y_o, ybuf, sem):
    t0 = pl.program_id(0) * TM

    def row_copy(r, k):
        d = dest_ref[(t0 + r) * TOP_EXPERTS + k]
        return pltpu.make_async_copy(ys_hbm.at[pl.ds(d, 1)], ybuf.at[k, pl.ds(r, 1)], sem)

    def start(r, _):
        for k in range(TOP_EXPERTS):
            row_copy(r, k).start()
        return 0
    lax.fori_loop(0, TM, start, 0, unroll=DMA_UNROLL)

    for k in range(TOP_EXPERTS):
        pltpu.make_async_copy(ys_hbm.at[pl.ds(0, TM)], ybuf.at[k], sem).wait()

    gv = gv_ref[...]
    y = jnp.zeros(x1_ref.shape, F32)
    for k in range(TOP_EXPERTS):
        y = y + ybuf[k] * gv[:, k:k + 1]
    y_o[...] = _rms(x1_ref[...] + y, gfin_ref[...])


def _combine(dest_flat, x1, gv, gfin, ys):
    nt, d = x1.shape
    grid_spec = pltpu.PrefetchScalarGridSpec(
        num_scalar_prefetch=1,
        grid=(nt // TM,),
        in_specs=[pl.BlockSpec((TM, d), lambda i, dref: (i, 0)),
                  pl.BlockSpec((TM, LANES), lambda i, dref: (i, 0)),
                  pl.BlockSpec((1, d), lambda i, dref: (0, 0)),
                  pl.BlockSpec(memory_space=pl.ANY)],
        out_specs=pl.BlockSpec((TM, d), lambda i, dref: (i, 0)),
        scratch_shapes=[pltpu.VMEM((TOP_EXPERTS, TM, d), F32), pltpu.SemaphoreType.DMA],
    )
    return pl.pallas_call(
        _combine_body,
        grid_spec=grid_spec,
        out_shape=jax.ShapeDtypeStruct((nt, d), F32),
        compiler_params=pltpu.CompilerParams(dimension_semantics=("arbitrary",)),
        name="combine_norm",
    )(dest_flat, x1, gv, gfin, ys)


def _tri_constants():
    r = lax.broadcasted_iota(I32, (TQ, TQ), 0)
    c = lax.broadcasted_iota(I32, (TQ, TQ), 1)
    incl = (r <= c).astype(BF16)
    u2 = jnp.concatenate([incl, jnp.ones((TQ, LANES), BF16)], axis=1)
    ltri = (c < r).astype(BF16)
    return u2, ltri


def kernel(x_prompt, x_sample, cache_k, cache_v, cache_kidx, state_pool, page_table, meta_tokens, rel_bias, g_mix, w_in, w_pool_mix, pool_scale, w_br_attn, w_br_pool, w_out, g_ffn, w_router, b_router, w_gate_up, b_gate_up, w_down, b_down, g_final):
    nb, seq, d = x_prompt.shape
    db, dec_seq, _ = x_sample.shape
    assert dec_seq == 1, "one new token per sample"
    depth = w_in.shape[0]
    assert depth == 1, "single-layer stack: the combine kernel applies the final norm"
    n_pages = page_table.shape[1]
    assert n_pages % SUBLANES == 0, "the sample top-k lays the cached keys out as eight equal page segments"
    past = n_pages * PAGE_SIZE
    pw = state_pool.shape[-1]
    l_seq = seq + N_META
    lp = _round_up(l_seq, TQ)
    assert l_seq + db <= lp, "sample rows must fit in the padding of the last prompt sequence"
    nt = nb * lp
    s0 = (nb - 1) * lp + l_seq
    ksel_p = min(TOPK_MAX, seq // 4)
    ksel_s = min(TOPK_MAX, (past + dec_seq) // 4)
    n_tiles = (TOP_EXPERTS * nt) // TS + N_EXPERTS
    n_slots = n_tiles * TS

    meta = meta_tokens.astype(x_prompt.dtype)
    pieces = []
    for bi in range(nb):
        pieces += [meta, x_prompt[bi], jnp.zeros((lp - l_seq, d), x_prompt.dtype)]
    pieces[-1] = jnp.concatenate([x_sample[:, 0, :], jnp.zeros((lp - l_seq - db, d), x_prompt.dtype)], axis=0)
    x_all = jnp.concatenate(pieces, axis=0)

    u2, ltri = _tri_constants()
    page_flat = page_table.reshape(-1).astype(I32)
    rel_bias = rel_bias.astype(F32)
    cache_k_t = jnp.transpose(cache_k, (0, 1, 3, 4, 2))
    cache_v_t = jnp.transpose(cache_v, (0, 1, 3, 4, 2))
    cache_ki_t = jnp.transpose(cache_kidx, (0, 1, 3, 2))

    outs = {name: [] for name in ("k_p", "v_p", "ki_p", "pool_p", "k_s", "v_s", "ki_s", "pool_s")}
    for l in range(depth):
        wl = w_in[l]
        aw = ATTN_WIDTH
        o_ki = 4 * aw
        o_wi = o_ki + IDX_DIM
        o_p = o_wi + N_IDX_HEADS
        o_ga = o_p + pw
        o_gb = o_ga + d
        wa = wl[:, 0:o_ki].astype(BF16)
        wki = wl[:, o_ki:o_wi]
        wki2 = jnp.concatenate([wki, wki], axis=1).astype(BF16)
        wwi = jnp.pad(wl[:, o_wi:o_p], ((0, 0), (0, LANES - N_IDX_HEADS))).astype(BF16)
        wp = wl[:, o_p:o_ga].astype(BF16)
        wga = wl[:, o_ga:o_gb].astype(BF16)
        wgb = wl[:, o_gb:o_gb + d].astype(BF16)

        (qs, kf, vf, kb, vb, qib, ki2b, kif, wif, pf, ga, gb, k_t, v_t, ki_t) = _inproj(
            x_all, g_mix[l][None, :], wa, wki2, wwi, wp, wga, wgb, nb=nb, lp=lp, l_seq=l_seq)

        tail_rows = _round_up(l_seq - (lp - TQ), 2 * SUBLANES)
        attn_p = _attn_prompt(qs, qib, wif, kb, vb, ki2b, u2, rel_bias, nb=nb, lp=lp, ksel=ksel_p,
                              tail_rows=tail_rows)
        pooled_p = _pool_prompt(pf, n_rows=nt, lp=lp)

        sl = slice(s0, s0 + db)
        keys = _sample_keys(page_flat, qib[sl].reshape(db, N_IDX_HEADS, IDX_DIM), kif[sl, 0:IDX_DIM][:, None, :],
                            wif[sl, 0:N_IDX_HEADS][:, :, None], cache_ki_t, layer=l, n_pages=n_pages)
        sel = _sample_select(keys, past=past, ksel=ksel_s)
        cols = lambda a: a[sl].astype(F32).reshape(db, N_HEADS, HEAD_DIM, 1)
        attn_s, pooled_s = _sample_attend(
            page_flat, sel, cols(qs), cols(kf), cols(vf), rel_bias, state_pool[l], pf[sl][:, None, :],
            cache_k_t, cache_v_t, layer=l, n_pages=n_pages)

        attn_all = lax.dynamic_update_slice(attn_p, attn_s.reshape(db, ATTN_WIDTH).astype(BF16), (s0, 0))
        pooled_all = lax.dynamic_update_slice(pooled_p, pooled_s[:, 0, :], (s0, 0))

        wr = jnp.pad(w_router[l], ((0, 0), (0, LANES - N_EXPERTS))).astype(BF16)
        br = jnp.concatenate([b_router[l].astype(F32), jnp.full((LANES - N_EXPERTS,), NEG, F32)])[None, :]
        x1, h2p, ev, gv, rk, cnt = _merge(
            x_all, attn_all, pooled_all, ga, gb, w_br_attn[l].astype(BF16), w_pool_mix[l].astype(BF16),
            pool_scale[l][None, :], w_br_pool[l].astype(BF16), w_out[l].astype(BF16), g_ffn[l][None, :], wr, br, ltri)

        counts = cnt[0, 0:N_EXPERTS].astype(I32)
        padded = (counts + TS - 1) // TS * TS
        gend = jnp.cumsum(padded)
        gstart = gend - padded
        experts = jnp.arange(N_EXPERTS, dtype=I32)
        ev4 = ev[:, 0:TOP_EXPERTS]
        dest = (jnp.sum(jnp.where(ev4[:, :, None] == experts, gstart, 0), axis=2) + rk[:, 0:TOP_EXPERTS]).reshape(-1)
        tile_first = jnp.arange(n_tiles, dtype=I32)[:, None] * TS
        tile_expert = jnp.minimum(jnp.sum((gend[None, :] <= tile_first).astype(I32), axis=1), N_EXPERTS - 1)
        n_used = (gend[-1] // TS).astype(I32)[None]
        nonempty = counts > 0
        rank_e = jnp.cumsum(nonempty.astype(I32)) - nonempty.astype(I32)
        later = jnp.logical_and(nonempty[None, :], experts[None, :] > experts[:, None])
        next_e = jnp.min(jnp.where(later, experts[None, :], N_EXPERTS), axis=1)
        next_e = jnp.where(next_e == N_EXPERTS, -1, next_e)
        tile_onehot = tile_expert[:, None] == experts[None, :]
        group_index = jnp.sum(jnp.where(tile_onehot, rank_e, 0), axis=1).astype(I32)
        next_expert = jnp.sum(jnp.where(tile_onehot, next_e, 0), axis=1).astype(I32)

        xs_sorted = _dispatch(dest, jnp.concatenate([gend, gstart + counts]).astype(I32), h2p, n_slots=n_slots)
        ys = _moe(tile_expert, n_used, group_index, next_expert, xs_sorted, w_gate_up, b_gate_up, w_down, b_down,
                  layer=l)
        y_all = _combine(dest, x1, gv, g_final[None, :], ys)

        outs["k_p"].append(jnp.transpose(k_t, (0, 3, 1, 2)))
        outs["v_p"].append(jnp.transpose(v_t, (0, 3, 1, 2)))
        outs["ki_p"].append(jnp.transpose(ki_t, (0, 2, 1)))
        outs["pool_p"].append(pf.reshape(nb, lp, pw)[:, l_seq - POOL_STATE:l_seq])
        outs["k_s"].append(kf[sl].reshape(db, 1, N_HEADS, HEAD_DIM))
        outs["v_s"].append(vf[sl].reshape(db, 1, N_HEADS, HEAD_DIM))
        outs["ki_s"].append(kif[sl, 0:IDX_DIM].reshape(db, 1, IDX_DIM))
        outs["pool_s"].append(jnp.concatenate([state_pool[l][:, 1:], pf[sl][:, None, :]], axis=1))

    y_prompt = y_all.reshape(nb, lp, d)[:, N_META:l_seq]
    y_sample = y_all[s0:s0 + db].reshape(db, 1, d)
    st = lambda name: jnp.stack(outs[name])
    return (y_prompt, y_sample, st("k_p"), st("v_p"), st("ki_p"), st("pool_p"),
            st("k_s"), st("v_s"), st("ki_s"), st("pool_s"))
```

```python
import functools
import math

import jax
import jax.numpy as jnp
from jax import lax
from jax.experimental import pallas as pl
from jax.experimental.pallas import tpu as pltpu

F32 = jnp.float32
BF16 = jnp.bfloat16
I32 = jnp.int32
U32 = jnp.uint32

N_META = 16
N_HEADS = 8
HEAD_DIM = 64
ATTN_WIDTH = N_HEADS * HEAD_DIM
N_IDX_HEADS = 8
IDX_DIM = 64
TOPK_MAX = 256
N_BUCKETS = 32
MAX_DISTANCE = 128
POOL_WINDOWS = (2, 4, 8, 16)
POOL_GROUPS = 4
POOL_STATE = 15
N_EXPERTS = 32
TOP_EXPERTS = 4
SWIGLU_LIMIT = 7.0
SWIGLU_ALPHA = 1.702
PAGE_SIZE = 128
EPS = 1e-6

LANES = 128
SUBLANES = 8
MXU_DIM = 256
TQ = MXU_DIM
TM = MXU_DIM
TS = MXU_DIM
VMEM_LIMIT = 56 * 1024 * 1024
DMA_UNROLL = 8

NEG = -1e30
INT_MIN = -(2 ** 31)

_MAX_EXACT = N_BUCKETS // 2
_BUCKET_THRESHOLDS = tuple(
    math.ceil(_MAX_EXACT * (MAX_DISTANCE / _MAX_EXACT) ** (j / (N_BUCKETS - _MAX_EXACT)))
    for j in range(1, N_BUCKETS - _MAX_EXACT))


def _round_up(a, m):
    return (a + m - 1) // m * m


def _rms(x, g):
    return x * lax.rsqrt(jnp.mean(x * x, axis=-1, keepdims=True) + EPS) * g


def _dot(a, b):
    return jnp.dot(a, b, preferred_element_type=F32)


def _dot_nt(a, b):
    return lax.dot_general(a, b, (((1,), (1,)), ((), ())), preferred_element_type=F32)


def _bucket(dist):
    large = jnp.full(dist.shape, _MAX_EXACT, I32)
    for thr in _BUCKET_THRESHOLDS:
        large = large + jnp.where(dist >= thr, 1, 0)
    return jnp.where(dist < _MAX_EXACT, dist, large)


def _sort_key(s):
    s = jnp.where(s == 0.0, 0.0, s)
    bits = lax.bitcast_convert_type(s, I32)
    return jnp.where(bits >= 0, bits, bits ^ jnp.int32(0x7FFFFFFF))


def _full(shape):
    return pl.BlockSpec(shape, lambda *_: (0,) * len(shape))


def _inproj_body(x_ref, g_ref, wa_ref, wki_ref, wwi_ref, wp_ref, wga_ref, wgb_ref,
                 qs_o, kf_o, vf_o, kb_o, vb_o, qib_o, ki2b_o, kif_o, wif_o, pf_o, ga_o, gb_o, kt_o, vt_o, kit_o):
    h = _rms(x_ref[...], g_ref[...]).astype(BF16)
    za = _dot(h, wa_ref[...])
    w = ATTN_WIDTH
    qs_o[...] = (za[:, 0:w] * (HEAD_DIM ** -0.5)).astype(BF16)
    k = za[:, w:2 * w]
    v = za[:, 2 * w:3 * w]
    kf_o[...] = k
    vf_o[...] = v
    kb_o[...] = k.astype(BF16)
    vb_o[...] = v.astype(BF16)
    qib_o[...] = za[:, 3 * w:4 * w].astype(BF16)
    ki2 = _dot(h, wki_ref[...])
    kif_o[...] = ki2
    ki2b_o[...] = ki2.astype(BF16)
    wif_o[...] = _dot(h, wwi_ref[...])
    pf_o[...] = _dot(h, wp_ref[...])
    ga_o[...] = _dot(h, wga_ref[...])
    gb_o[...] = _dot(h, wgb_ref[...])
    kt_o[0] = k.T.reshape(N_HEADS, HEAD_DIM, TM)
    vt_o[0] = v.T.reshape(N_HEADS, HEAD_DIM, TM)
    kit_o[0] = ki2.T[0:IDX_DIM, :]


def _inproj(x_all, g, wa, wki2, wwi, wp, wga, wgb, *, nb, lp, l_seq):
    nt, d = x_all.shape
    pw = wp.shape[1]
    tps = lp // TM
    row = lambda width: pl.BlockSpec((TM, width), lambda i: (i, 0))
    outs = [
        (ATTN_WIDTH, BF16), (ATTN_WIDTH, F32), (ATTN_WIDTH, F32), (ATTN_WIDTH, BF16), (ATTN_WIDTH, BF16),
        (ATTN_WIDTH, BF16), (LANES, BF16), (LANES, F32), (LANES, F32), (pw, F32), (d, F32), (d, F32)]
    seq_minor_specs = [
        pl.BlockSpec((1, N_HEADS, HEAD_DIM, TM), lambda i: (i // tps, 0, 0, i % tps)),
        pl.BlockSpec((1, N_HEADS, HEAD_DIM, TM), lambda i: (i // tps, 0, 0, i % tps)),
        pl.BlockSpec((1, IDX_DIM, TM), lambda i: (i // tps, 0, i % tps))]
    seq_minor_shapes = [
        jax.ShapeDtypeStruct((nb, N_HEADS, HEAD_DIM, l_seq), F32), jax.ShapeDtypeStruct((nb, N_HEADS, HEAD_DIM, l_seq), F32),
        jax.ShapeDtypeStruct((nb, IDX_DIM, l_seq), F32)]
    return pl.pallas_call(
        _inproj_body,
        grid=(nt // TM,),
        in_specs=[row(d), _full((1, d)), _full(wa.shape), _full(wki2.shape), _full(wwi.shape),
                  _full(wp.shape), _full(wga.shape), _full(wgb.shape)],
        out_specs=[row(wd) for wd, _ in outs] + seq_minor_specs,
        out_shape=[jax.ShapeDtypeStruct((nt, wd), dt) for wd, dt in outs] + seq_minor_shapes,
        compiler_params=pltpu.CompilerParams(dimension_semantics=("arbitrary",), vmem_limit_bytes=VMEM_LIMIT),
        name="inproj",
    )(x_all, g, wa, wki2, wwi, wp, wga, wgb)


def _attn_prompt_body(qs_ref, qib_ref, wif_ref, kb_ref, vb_ref, ki2b_ref, u2_ref, rb_ref,
                      o_ref, qis_scr, wb_scr, key_scr, ntab_scr, qz_scr, m_scr, acc_scr, sum_scr, *, ksel, tail_rows):
    b = pl.program_id(0)
    i = pl.program_id(1)
    nchunk = i + 1
    n_far = jnp.maximum(i - 1, 0)

    @pl.when((b == 0) & (i == 0))
    def _():
        def slab_rows(s, _):
            r0 = pl.multiple_of(s * SUBLANES, SUBLANES)
            r = r0 + lax.broadcasted_iota(I32, (SUBLANES, 2 * TQ), 0)
            x = lax.broadcasted_iota(I32, (SUBLANES, 2 * TQ), 1)
            bucket = _bucket(jnp.maximum(r + TQ - x, 0))
            accs = [jnp.zeros((SUBLANES, 2 * TQ), F32) for _ in range(N_HEADS)]
            for j in range(N_BUCKETS - 1):
                m = bucket == j
                accs = [jnp.where(m, rb_ref[j, h] - rb_ref[N_BUCKETS - 1, h], accs[h]) for h in range(N_HEADS)]
            for h in range(N_HEADS):
                ntab_scr[h, 0, pl.ds(r0, SUBLANES), :] = accs[h][:, 0:TQ]
                ntab_scr[h, 1, pl.ds(r0, SUBLANES), :] = accs[h][:, TQ:2 * TQ]
            return 0
        lax.fori_loop(0, TQ // SUBLANES, slab_rows, 0)

    def two(x):
        return jnp.concatenate([x, x], axis=1)

    def block(nr):
        hr = nr // 2
        lane = lax.broadcasted_iota(I32, (nr, LANES), 1)
        lo_half = lane < HEAD_DIM

        for h in range(N_HEADS):
            cols = slice((h // 2) * LANES, (h // 2 + 1) * LANES)
            keep = lo_half if h % 2 == 0 else jnp.logical_not(lo_half)
            qis_scr[h, 0:nr] = jnp.where(keep, qib_ref[0:nr, cols].astype(F32), 0.0).astype(BF16)
            qz_scr[h // 2, (h % 2) * nr:(h % 2 + 1) * nr] = jnp.where(keep, qs_ref[0:nr, cols].astype(F32), 0.0).astype(BF16)
            wb_scr[h, 0:nr] = jnp.broadcast_to(wif_ref[0:nr, h:h + 1], (nr, LANES))
            m_scr[h, 0:nr] = jnp.full((nr, LANES), NEG, F32)
            acc_scr[h, 0:nr] = jnp.zeros((nr, LANES), F32)
            sum_scr[h, 0:nr] = jnp.zeros((nr, LANES), F32)

        dmat = lax.broadcasted_iota(I32, (nr, TQ), 1) - lax.broadcasted_iota(I32, (nr, TQ), 0)

        def score_chunk(c, _):
            kc = ki2b_ref[pl.ds(pl.multiple_of(c * TQ, TQ), TQ), :]
            s = jnp.zeros((nr, TQ), F32)
            for h in range(N_IDX_HEADS):
                s = s + two(wb_scr[h, 0:nr]) * jnp.maximum(_dot_nt(qis_scr[h, 0:nr], kc), 0.0)
            key_scr[c, 0:nr] = jnp.where(dmat <= (i - c) * TQ, _sort_key(s), INT_MIN)
            return 0
        lax.fori_loop(0, nchunk, score_chunk, 0)
        key_scr[nchunk, 0:nr] = jnp.full((nr, TQ), INT_MIN, I32)

        def count(r0, pred):
            def body(j, acc):
                for c in (2 * j, 2 * j + 1):
                    hit = jnp.where(pred(key_scr[c, r0:r0 + hr, :]), 1.0, 0.0)
                    acc = acc + (hit[:, 0:LANES] + hit[:, LANES:2 * LANES])
                return acc
            acc = lax.fori_loop(0, (nchunk + 1) // 2, body, jnp.zeros((hr, LANES), F32))
            return jnp.broadcast_to(jnp.sum(acc, axis=1, keepdims=True), (hr, LANES))

        ts, needs = [], []
        for r0 in (0, hr):
            def bit_step(step, t, r0=r0):
                cand = t + lax.shift_left(jnp.int32(1), 31 - step)
                cand2 = two(cand)
                return jnp.where(count(r0, lambda k: k >= cand2) >= ksel, cand, t)
            t = lax.fori_loop(0, 32, bit_step, jnp.full((hr, LANES), INT_MIN, I32))
            th2 = two(t)
            n_gt = count(r0, lambda k: k > th2)
            ts.append(t)
            needs.append(jnp.where(t == INT_MIN, 0.0, ksel - n_gt))
        t2 = two(jnp.concatenate(ts, axis=0))
        need2 = two(jnp.concatenate(needs, axis=0))

        def mask_chunk(c, carry):
            kc = key_scr[c, 0:nr]
            eq = kc == t2
            pre = _dot(jnp.where(eq, 1.0, 0.0).astype(BF16), u2_ref[...])
            prefix = pre[:, 0:TQ] + two(carry)
            tie_ok = jnp.where(eq, prefix, 3e38) <= need2
            madd = jnp.where(kc > t2, 0.0, jnp.where(tie_ok, 0.0, NEG))
            key_scr[c, 0:nr] = lax.bitcast_convert_type(madd, I32)
            return carry + pre[:, TQ:TQ + LANES]
        lax.fori_loop(0, nchunk, mask_chunk, jnp.zeros((nr, LANES), F32))

        def chunk_logits(c, slab):
            rows = pl.ds(pl.multiple_of(c * TQ, TQ), TQ)
            madd = lax.bitcast_convert_type(key_scr[c, 0:nr], F32)
            out = []
            for hp in range(N_HEADS // 2):
                kc = kb_ref[rows, hp * LANES:(hp + 1) * LANES]
                pair = _dot_nt(qz_scr[hp, 0:2 * nr], kc)
                for sub in range(2):
                    l = pair[sub * nr:(sub + 1) * nr] + madd
                    out.append(l if slab is None else l + ntab_scr[2 * hp + sub, slab, 0:nr])
            return rows, out

        def accumulate(c, slab):
            rows, ls = chunk_logits(c, slab)
            for hp in range(N_HEADS // 2):
                ps, alphas = [], []
                for h in (2 * hp, 2 * hp + 1):
                    l = ls[h]
                    m_old = m_scr[h, 0:nr]
                    m_chunk = jnp.max(jnp.maximum(l[:, 0:LANES], l[:, LANES:2 * LANES]), axis=1, keepdims=True)
                    m_new = jnp.maximum(m_old, jnp.broadcast_to(m_chunk, (nr, LANES)))
                    alpha = jnp.exp(m_old - m_new)
                    p = jnp.exp(l - two(m_new))
                    m_scr[h, 0:nr] = m_new
                    sum_scr[h, 0:nr] = alpha * sum_scr[h, 0:nr] + (p[:, 0:LANES] + p[:, LANES:2 * LANES])
                    ps.append(p.astype(BF16))
                    alphas.append(alpha)
                pv = _dot(jnp.concatenate(ps, axis=0), vb_ref[rows, hp * LANES:(hp + 1) * LANES])
                for sub in range(2):
                    h = 2 * hp + sub
                    acc_scr[h, 0:nr] = alphas[sub] * acc_scr[h, 0:nr] + pv[sub * nr:(sub + 1) * nr]

        def far(c, _):
            accumulate(c, None)
            return 0
        lax.fori_loop(0, n_far, far, 0)

        @pl.when(i >= 1)
        def _():
            accumulate(i - 1, 0)
        accumulate(i, 1)
        for hp in range(N_HEADS // 2):
            outs = [acc_scr[h, 0:nr] / jnp.sum(sum_scr[h, 0:nr], axis=1, keepdims=True) for h in (2 * hp, 2 * hp + 1)]
            o_ref[0:nr, hp * LANES:(hp + 1) * LANES] = jnp.where(lo_half, outs[0], outs[1]).astype(BF16)
        if nr < TQ:
            o_ref[nr:TQ, :] = jnp.zeros((TQ - nr, ATTN_WIDTH), BF16)

    if tail_rows == TQ:
        block(TQ)
    else:
        last = pl.num_programs(1) - 1
        pl.when(i < last)(lambda: block(TQ))
        pl.when(i == last)(lambda: block(tail_rows))


def _attn_prompt(qs, qib, wif, kb, vb, ki2b, u2, rel_bias, *, nb, lp, ksel, tail_rows):
    nq = lp // TQ
    rowq = lambda width: pl.BlockSpec((TQ, width), lambda b, i: (b * nq + i, 0))
    seq = lambda width: pl.BlockSpec((lp, width), lambda b, i: (b, 0))
    per_head = lambda dt: pltpu.VMEM((N_HEADS, TQ, LANES), dt)
    return pl.pallas_call(
        functools.partial(_attn_prompt_body, ksel=float(ksel), tail_rows=tail_rows),
        grid=(nb, nq),
        in_specs=[rowq(ATTN_WIDTH), rowq(ATTN_WIDTH), rowq(LANES), seq(ATTN_WIDTH), seq(ATTN_WIDTH), seq(LANES),
                  _full(u2.shape), pl.BlockSpec(memory_space=pltpu.SMEM)],
        out_specs=rowq(ATTN_WIDTH),
        out_shape=jax.ShapeDtypeStruct((nb * lp, ATTN_WIDTH), BF16),
        scratch_shapes=[
            per_head(BF16),
            per_head(F32),
            pltpu.VMEM((nq + 1, TQ, TQ), I32),
            pltpu.VMEM((N_HEADS, 2, TQ, TQ), F32),
            pltpu.VMEM((N_HEADS // 2, 2 * TQ, LANES), BF16),
            per_head(F32), per_head(F32), per_head(F32),
        ],
        compiler_params=pltpu.CompilerParams(dimension_semantics=("arbitrary", "arbitrary"),
                                             vmem_limit_bytes=VMEM_LIMIT),
        name="attn_prompt",
    )(qs, qib, wif, kb, vb, ki2b, u2, rel_bias)


def _pool_prompt_body(p_ref, halo_ref, o_ref, ext_scr, *, tiles_per_seq):
    j = pl.program_id(0) % tiles_per_seq
    halo = 2 * SUBLANES
    p = p_ref[...]
    ext_scr[0:halo, :] = jnp.where(j == 0, 0.0, halo_ref[...])
    ext_scr[halo:halo + TM, :] = p
    pos = j * TM + lax.broadcasted_iota(I32, (TM, 1), 0)
    gw = p.shape[1] // POOL_GROUPS
    for g, w in enumerate(POOL_WINDOWS):
        cols = slice(g * gw, (g + 1) * gw)
        s = p[:, cols]
        for back in range(1, w):
            s = s + ext_scr[halo - back:halo - back + TM, cols]
        cnt = jnp.minimum(pos + 1, w).astype(F32)
        o_ref[:, cols] = (s / cnt - p[:, cols]).astype(BF16)


def _pool_prompt(pf, *, n_rows, lp):
    pw = pf.shape[1]
    halo = 2 * SUBLANES
    return pl.pallas_call(
        functools.partial(_pool_prompt_body, tiles_per_seq=lp // TM),
        grid=(n_rows // TM,),
        in_specs=[pl.BlockSpec((TM, pw), lambda i: (i, 0)),
                  pl.BlockSpec((halo, pw), lambda i: (jnp.maximum(i * (TM // halo) - 1, 0), 0))],
        out_specs=pl.BlockSpec((TM, pw), lambda i: (i, 0)),
        out_shape=jax.ShapeDtypeStruct((n_rows, pw), BF16),
        scratch_shapes=[pltpu.VMEM((halo + TM, pw), F32)],
        compiler_params=pltpu.CompilerParams(dimension_semantics=("arbitrary",)),
        name="pool_prompt",
    )(pf, pf)


def _sample_keys_body(pt_ref, qi_ref, kinew_ref, wi_ref, cache_hbm, key_o, kibuf, sem, *, layer, n_pages):
    s = pl.program_id(0)
    past = n_pages * PAGE_SIZE
    seg_w = past // SUBLANES

    copies = [pltpu.make_async_copy(cache_hbm.at[layer, pt_ref[s * n_pages + pg]],
                                    kibuf.at[:, pg * PAGE_SIZE:(pg + 1) * PAGE_SIZE], sem) for pg in range(n_pages)]
    for cp in copies:
        cp.start()
    for cp in copies:
        cp.wait()

    qi = qi_ref[0]
    wcol = wi_ref[0]
    d = _dot(qi, kibuf[...].astype(BF16))
    sc = jnp.sum(wcol * jnp.maximum(d, 0.0), axis=0, keepdims=True)
    knew = kinew_ref[0].astype(BF16).astype(F32)
    dn = jnp.sum(qi.astype(F32) * knew, axis=1, keepdims=True)
    sn = jnp.sum(wcol * jnp.maximum(dn, 0.0), axis=0, keepdims=True)
    key = _sort_key(sc)
    lane = lax.broadcasted_iota(I32, (1, LANES), 1)
    new_tail = jnp.where(lane == 0, _sort_key(jnp.broadcast_to(sn, (1, LANES))), INT_MIN)
    for g in range(SUBLANES):
        key_o[0, g:g + 1, 0:seg_w] = key[:, g * seg_w:(g + 1) * seg_w]
        key_o[0, g:g + 1, seg_w:seg_w + LANES] = new_tail if g == 0 else jnp.full((1, LANES), INT_MIN, I32)


def _sample_keys(page_table_flat, qi3, kinew3, wi3, cache_ki_t, *, layer, n_pages):
    db = qi3.shape[0]
    past = n_pages * PAGE_SIZE
    kw = past // SUBLANES + LANES
    grid_spec = pltpu.PrefetchScalarGridSpec(
        num_scalar_prefetch=1,
        grid=(db,),
        in_specs=[pl.BlockSpec((1, N_IDX_HEADS, IDX_DIM), lambda s, pt: (s, 0, 0)),
                  pl.BlockSpec((1, 1, IDX_DIM), lambda s, pt: (s, 0, 0)),
                  pl.BlockSpec((1, N_IDX_HEADS, 1), lambda s, pt: (s, 0, 0)),
                  pl.BlockSpec(memory_space=pl.ANY)],
        out_specs=pl.BlockSpec((1, SUBLANES, kw), lambda s, pt: (s, 0, 0)),
        scratch_shapes=[pltpu.VMEM((IDX_DIM, past), F32), pltpu.SemaphoreType.DMA],
    )
    return pl.pallas_call(
        functools.partial(_sample_keys_body, layer=layer, n_pages=n_pages),
        grid_spec=grid_spec,
        out_shape=jax.ShapeDtypeStruct((db, SUBLANES, kw), I32),
        compiler_params=pltpu.CompilerParams(dimension_semantics=("arbitrary",)),
        name="sample_keys",
    )(page_table_flat, qi3, kinew3, wi3, cache_ki_t)


def _sample_select_body(key_ref, sel_o, *, past, ksel):
    keys = key_ref[...]
    kw = keys.shape[2]
    seg_w = kw - LANES
    row = lax.broadcasted_iota(I32, (SUBLANES, kw), 0)
    lane = lax.broadcasted_iota(I32, (SUBLANES, kw), 1)
    pos = jnp.where(lane < seg_w, row * seg_w + lane, past + (lane - seg_w) + row * LANES)[None]

    def cnt(pred):
        x = jnp.where(pred, 1.0, 0.0)
        return jnp.sum(jnp.sum(x, axis=2, keepdims=True), axis=1, keepdims=True)

    def bit_step(step, t):
        cand = t + lax.shift_left(jnp.int32(1), 31 - step)
        return jnp.where(cnt(keys >= cand) >= ksel, cand, t)
    t = lax.fori_loop(0, 32, bit_step, jnp.full((keys.shape[0], 1, 1), INT_MIN, I32))
    gt = keys > t
    eq = keys == t
    need = ksel - cnt(gt)

    nbits = (past + SUBLANES * LANES).bit_length()

    def cut_step(step, c):
        cand = c - lax.shift_left(jnp.int32(1), nbits - 1 - step)
        ok = cnt(jnp.logical_and(eq, pos <= cand)) >= need
        return jnp.where(ok, cand, c)
    cut = lax.fori_loop(0, nbits, cut_step, jnp.full((keys.shape[0], 1, 1), 2 ** nbits - 1, I32))
    sel = jnp.where(gt, 1.0, jnp.where(jnp.logical_and(eq, pos <= cut), 1.0, 0.0))
    sel_o[...] = jnp.where(pos <= past, sel, 0.0)


def _sample_select(keys, *, past, ksel):
    return pl.pallas_call(
        functools.partial(_sample_select_body, past=past, ksel=float(ksel)),
        out_shape=jax.ShapeDtypeStruct(keys.shape, F32),
        name="sample_select",
    )(keys)


def _sample_attend_body(pt_ref, sel_ref, q_ref, knew_ref, vnew_ref, rb_ref, state_ref, pnew_ref, ck_hbm, cv_hbm,
                        attn_o, pool_o, kbuf, vbuf, qb_scr, lg_scr, acc_scr, ksem, vsem, *, layer, n_pages):
    s = pl.program_id(0)
    past = n_pages * PAGE_SIZE
    seg_w = past // SUBLANES
    pages_per_seg = n_pages // SUBLANES
    last_bias = [rb_ref[N_BUCKETS - 1, h] for h in range(N_HEADS)]

    def kcopy(pg, sample=s):
        return pltpu.make_async_copy(ck_hbm.at[layer, pt_ref[sample * n_pages + pg]], kbuf.at[pg], ksem.at[pg])

    def vcopy(pg, sample=s):
        return pltpu.make_async_copy(cv_hbm.at[layer, pt_ref[sample * n_pages + pg]], vbuf.at[pg], vsem.at[pg])

    def request(copy, sample):
        def start(pg, _):
            copy(pg, sample).start()
            return 0
        lax.fori_loop(0, n_pages, start, 0)

    has_next = s + 1 < pl.num_programs(0)

    @pl.when(s == 0)
    def _():
        request(kcopy, s)
        request(vcopy, s)

    for h in range(N_HEADS):
        qb_scr[h] = jnp.broadcast_to(q_ref[0, h], (HEAD_DIM, LANES))
        acc_scr[h] = jnp.zeros((HEAD_DIM, LANES), F32)

    def k_page(pg, _):
        kcopy(pg).wait()
        rows = [jnp.sum(kbuf[pg, h].astype(BF16).astype(F32) * qb_scr[h], axis=0, keepdims=True)
                for h in range(N_HEADS)]
        lg_scr[pg] = jnp.concatenate(rows, axis=0)
        return 0
    lax.fori_loop(0, n_pages, k_page, 0)
    pl.when(has_next)(lambda: request(kcopy, s + 1))

    lane = lax.broadcasted_iota(I32, (1, LANES), 1)
    bucket = _bucket(past - ((n_pages - 1) * PAGE_SIZE + lane))
    near_rows = []
    for h in range(N_HEADS):
        r = jnp.zeros((1, LANES), F32)
        for j in range(N_BUCKETS - 1):
            r = jnp.where(bucket == j, rb_ref[j, h] - last_bias[h], r)
        near_rows.append(r)
    near = jnp.concatenate(near_rows, axis=0)

    sel = sel_ref[0]
    m = jnp.full((N_HEADS, LANES), NEG, F32)
    for pg in range(n_pages):
        g, j = divmod(pg, pages_per_seg)
        keep = sel[g:g + 1, j * PAGE_SIZE:(j + 1) * PAGE_SIZE] > 0.0
        l = lg_scr[pg] + jnp.where(keep, 0.0, NEG)
        if pg == n_pages - 1:
            l = l + near
        lg_scr[pg] = l
        m = jnp.maximum(m, l)
    new_rows = []
    for h in range(N_HEADS):
        kn = knew_ref[0, h].astype(BF16).astype(F32)
        new_rows.append(jnp.sum(q_ref[0, h] * kn, axis=0, keepdims=True) + (rb_ref[0, h] - last_bias[h]))
    l_new = jnp.where(sel[0:1, seg_w:seg_w + 1] > 0.0, jnp.concatenate(new_rows, axis=0), NEG)
    m = jnp.maximum(jnp.max(m, axis=1, keepdims=True), l_new)
    ssum = jnp.zeros((N_HEADS, LANES), F32)
    for pg in range(n_pages):
        p = jnp.exp(lg_scr[pg] - m)
        lg_scr[pg] = p
        ssum = ssum + p
    p_new = jnp.exp(l_new - m)
    den = jnp.sum(ssum, axis=1, keepdims=True) + p_new
    for pg in range(n_pages):
        lg_scr[pg] = (lg_scr[pg] / den).astype(BF16).astype(F32)
    p_new = (p_new / den).astype(BF16).astype(F32)

    def v_page(pg, _):
        vcopy(pg).wait()
        p = lg_scr[pg]
        for h in range(N_HEADS):
            acc_scr[h] = acc_scr[h] + vbuf[pg, h].astype(BF16).astype(F32) * p[h:h + 1, :]
        return 0
    lax.fori_loop(0, n_pages, v_page, 0)
    pl.when(has_next)(lambda: request(vcopy, s + 1))
    for h in range(N_HEADS):
        vn = vnew_ref[0, h].astype(BF16).astype(F32)
        attn_o[0, h] = jnp.sum(acc_scr[h], axis=1, keepdims=True) + p_new[h:h + 1, :] * vn

    pnew = pnew_ref[0]
    st = state_ref[0]
    gw = pnew.shape[1] // POOL_GROUPS
    for g, w in enumerate(POOL_WINDOWS):
        cols = slice(g * gw, (g + 1) * gw)
        acc = pnew[:, cols]
        for back in range(1, w):
            acc = acc + st[POOL_STATE - back:POOL_STATE - back + 1, cols]
        pool_o[0, :, cols] = (acc / float(min(POOL_STATE + 1, w)) - pnew[:, cols]).astype(BF16)


def _sample_attend(page_table_flat, sel, q4, knew4, vnew4, rel_bias, state, pnew3, cache_k_t, cache_v_t, *, layer, n_pages):
    db = q4.shape[0]
    pw = pnew3.shape[2]
    per = lambda shape: pl.BlockSpec((1,) + shape, lambda s, pt: (s,) + (0,) * len(shape))
    col = (N_HEADS, HEAD_DIM, 1)
    page = (n_pages, N_HEADS, HEAD_DIM, LANES)
    grid_spec = pltpu.PrefetchScalarGridSpec(
        num_scalar_prefetch=1,
        grid=(db,),
        in_specs=[per(sel.shape[1:]), per(col), per(col), per(col), pl.BlockSpec(memory_space=pltpu.SMEM),
                  per((POOL_STATE, pw)), per((1, pw)),
                  pl.BlockSpec(memory_space=pl.ANY), pl.BlockSpec(memory_space=pl.ANY)],
        out_specs=[per(col), per((1, pw))],
        scratch_shapes=[pltpu.VMEM(page, F32), pltpu.VMEM(page, F32),
                        pltpu.VMEM((N_HEADS, HEAD_DIM, LANES), F32),
                        pltpu.VMEM((n_pages, N_HEADS, LANES), F32),
                        pltpu.VMEM((N_HEADS, HEAD_DIM, LANES), F32),
                        pltpu.SemaphoreType.DMA((n_pages,)), pltpu.SemaphoreType.DMA((n_pages,))],
    )
    return pl.pallas_call(
        functools.partial(_sample_attend_body, layer=layer, n_pages=n_pages),
        grid_spec=grid_spec,
        out_shape=[jax.ShapeDtypeStruct((db,) + col, F32), jax.ShapeDtypeStruct((db, 1, pw), BF16)],
        compiler_params=pltpu.CompilerParams(dimension_semantics=("arbitrary",), vmem_limit_bytes=VMEM_LIMIT),
        name="sample_attend",
    )(page_table_flat, sel, q4, knew4, vnew4, rel_bias, state, pnew3, cache_k_t, cache_v_t)


def _merge_body(x_ref, attn_ref, pool_ref, ga_ref, gb_ref, wba_ref, wpm_ref, ps_ref, wbp_ref, wout_ref,
                gffn_ref, wr_ref, br_ref, ltri_ref,
                x1_o, h2p_o, ev_o, gv_o, rk_o, cnt_o, carry_scr):
    step = pl.program_id(0)

    @pl.when(step == 0)
    def _():
        carry_scr[...] = jnp.zeros_like(carry_scr)

    a = _dot(attn_ref[...], wba_ref[...])
    pooled = pool_ref[...]
    gw = pooled.shape[1] // POOL_GROUPS
    pm = jnp.concatenate([_dot(pooled[:, g * gw:(g + 1) * gw], wpm_ref[g]) for g in range(POOL_GROUPS)], axis=1)
    bb = _dot((pm * ps_ref[...]).astype(BF16), wbp_ref[...])
    mix = jax.nn.sigmoid(ga_ref[...]) * a + jax.nn.sigmoid(gb_ref[...]) * bb
    x1 = x_ref[...] + _dot(mix.astype(BF16), wout_ref[...])
    x1_o[...] = x1
    h2 = _rms(x1, gffn_ref[...]).astype(BF16)
    half = h2.shape[1] // 2
    lo = lax.shift_right_logical(lax.bitcast_convert_type(h2[:, 0:half].astype(F32), U32), jnp.uint32(16))
    hi = lax.bitcast_convert_type(h2[:, half:].astype(F32), U32) & jnp.uint32(0xFFFF0000)
    h2p_o[...] = hi | lo

    logits = _dot(h2, wr_ref[...]) + br_ref[...]
    lane = lax.broadcasted_iota(I32, (TM, LANES), 1)
    lanef = lane.astype(F32)
    tops, ids = [], []
    l = logits
    for _ in range(TOP_EXPERTS):
        mx = jnp.max(l, axis=1, keepdims=True)
        ix = jnp.min(jnp.where(l == mx, lanef, float(LANES)), axis=1, keepdims=True)
        tops.append(mx)
        ids.append(ix)
        l = jnp.where(lanef == ix, -3e38, l)
    es = [jnp.exp(tv - tops[0]) for tv in tops]
    den = es[0] + es[1] + es[2] + es[3]
    onehot = jnp.zeros((TM, LANES), F32)
    for ix in ids:
        onehot = onehot + jnp.where(lanef == ix, 1.0, 0.0)
    before = _dot(ltri_ref[...], onehot.astype(BF16)) + carry_scr[0:1, :]
    ev = jnp.zeros((TM, LANES), I32)
    gv = jnp.zeros((TM, LANES), F32)
    rk = jnp.zeros((TM, LANES), I32)
    for k in range(TOP_EXPERTS):
        rank = jnp.sum(jnp.where(lanef == ids[k], before, 0.0), axis=1, keepdims=True)
        ev = jnp.where(lane == k, ids[k].astype(I32), ev)
        gv = jnp.where(lane == k, es[k] / den, gv)
        rk = jnp.where(lane == k, rank.astype(I32), rk)
    ev_o[...] = ev
    gv_o[...] = gv
    rk_o[...] = rk
    carry_scr[...] = carry_scr[...] + jnp.sum(onehot, axis=0, keepdims=True)
    cnt_o[...] = carry_scr[...]


def _merge(x_all, attn, pooled, ga, gb, wba, wpm, ps, wbp, wout, gffn, wr, br, ltri):
    nt, d = x_all.shape
    row = lambda width: pl.BlockSpec((TM, width), lambda i: (i, 0))
    consts = [wba, wpm, ps, wbp, wout, gffn, wr, br, ltri]
    return pl.pallas_call(
        _merge_body,
        grid=(nt // TM,),
        in_specs=[row(d), row(attn.shape[1]), row(pooled.shape[1]), row(d), row(d)] + [_full(c.shape) for c in consts],
        out_specs=[row(d), row(d // 2), row(LANES), row(LANES), row(LANES), _full((SUBLANES, LANES))],
        out_shape=[jax.ShapeDtypeStruct((nt, d), F32), jax.ShapeDtypeStruct((nt, d // 2), U32),
                   jax.ShapeDtypeStruct((nt, LANES), I32), jax.ShapeDtypeStruct((nt, LANES), F32),
                   jax.ShapeDtypeStruct((nt, LANES), I32), jax.ShapeDtypeStruct((SUBLANES, LANES), F32)],
        scratch_shapes=[pltpu.VMEM((SUBLANES, LANES), F32)],
        compiler_params=pltpu.CompilerParams(dimension_semantics=("arbitrary",), vmem_limit_bytes=VMEM_LIMIT),
        name="merge_router",
    )(x_all, attn, pooled, ga, gb, *consts)


def _dispatch_body(dest_ref, gend_ref, h2p_ref, xs_hbm, zero_scr, sem, zsem):
    step = pl.program_id(0)

    @pl.when(step == 0)
    def _():
        zero_scr[...] = jnp.zeros_like(zero_scr)

        def fill(e):
            return pltpu.make_async_copy(zero_scr, xs_hbm.at[pl.ds(pl.multiple_of(gend_ref[e] - TS, TS), TS)], zsem)

        def start(e, _):
            @pl.when(gend_ref[e] > gend_ref[e + N_EXPERTS])
            def _():
                fill(e).start()
            return 0
        lax.fori_loop(0, N_EXPERTS, start, 0)

        def wait(e, _):
            @pl.when(gend_ref[e] > gend_ref[e + N_EXPERTS])
            def _():
                fill(e).wait()
            return 0
        lax.fori_loop(0, N_EXPERTS, wait, 0)

        def tail(j):
            return pltpu.make_async_copy(zero_scr, xs_hbm.at[pl.ds(pl.multiple_of(j * TS, TS), TS)], zsem)

        first_unused = gend_ref[N_EXPERTS - 1] // TS
        n_tiles = xs_hbm.shape[0] // TS

        def tail_start(j, _):
            tail(j).start()
            return 0
        lax.fori_loop(first_unused, n_tiles, tail_start, 0)

        def tail_wait(j, _):
            tail(j).wait()
            return 0
        lax.fori_loop(first_unused, n_tiles, tail_wait, 0)

    t0 = step * TM

    def row_copy(r, k):
        d = dest_ref[(t0 + r) * TOP_EXPERTS + k]
        return pltpu.make_async_copy(h2p_ref.at[pl.ds(r, 1)], xs_hbm.at[pl.ds(d, 1)], sem)

    def start(r, _):
        for k in range(TOP_EXPERTS):
            row_copy(r, k).start()
        return 0
    lax.fori_loop(0, TM, start, 0, unroll=DMA_UNROLL)

    for _ in range(TOP_EXPERTS):
        pltpu.make_async_copy(h2p_ref, xs_hbm.at[pl.ds(0, TM)], sem).wait()


def _dispatch(dest_flat, gend, h2p, *, n_slots):
    nt, hw = h2p.shape
    grid_spec = pltpu.PrefetchScalarGridSpec(
        num_scalar_prefetch=2,
        grid=(nt // TM,),
        in_specs=[pl.BlockSpec((TM, hw), lambda i, d, g: (i, 0))],
        out_specs=pl.BlockSpec(memory_space=pl.ANY),
        scratch_shapes=[pltpu.VMEM((TS, hw), U32), pltpu.SemaphoreType.DMA, pltpu.SemaphoreType.DMA],
    )
    return pl.pallas_call(
        _dispatch_body,
        grid_spec=grid_spec,
        out_shape=jax.ShapeDtypeStruct((n_slots, hw), U32),
        compiler_params=pltpu.CompilerParams(dimension_semantics=("arbitrary",)),
        name="dispatch",
    )(dest_flat, gend, h2p)


def _moe_body(te_ref, nu_ref, gi_ref, nx_ref, xs_ref, bgu_ref, bdn_ref, wgu_hbm, wdn_hbm, ys_o,
              wgu_f, wdn_f, wgu_b, wdn_b, sem, *, layer):
    j = pl.program_id(0)
    expert = te_ref[j]
    live = j < nu_ref[0]
    slot = gi_ref[j] % 2

    def weight_copies(e, s):
        return (pltpu.make_async_copy(wgu_hbm.at[layer, e], wgu_f.at[s], sem.at[0, s]),
                pltpu.make_async_copy(wdn_hbm.at[layer, e], wdn_f.at[s], sem.at[1, s]))

    @pl.when(jnp.logical_and(live, jnp.logical_or(j == 0, expert != te_ref[jnp.maximum(j - 1, 0)])))
    def _():
        @pl.when(j == 0)
        def _():
            for cp in weight_copies(expert, slot):
                cp.start()

        @pl.when(nx_ref[j] >= 0)
        def _():
            for cp in weight_copies(nx_ref[j], 1 - slot):
                cp.start()
        for cp in weight_copies(expert, slot):
            cp.wait()
        wgu_b[...] = wgu_f[slot].astype(BF16)
        wdn_b[...] = wdn_f[slot].astype(BF16)

    @pl.when(live)
    def _():
        words = xs_ref[...]
        x_lo = lax.bitcast_convert_type(lax.shift_left(words, jnp.uint32(16)), F32).astype(BF16)
        x_hi = lax.bitcast_convert_type(words & jnp.uint32(0xFFFF0000), F32).astype(BF16)
        half = words.shape[1]
        gu = _dot(x_lo, wgu_b[0:half, :]) + _dot(x_hi, wgu_b[half:, :]) + bgu_ref[0, 0]
        de = gu.shape[1] // 2
        gate = jnp.minimum(gu[:, 0:de], SWIGLU_LIMIT)
        up = jnp.clip(gu[:, de:], -SWIGLU_LIMIT, SWIGLU_LIMIT)
        act = (up + 1.0) * gate * jax.nn.sigmoid(SWIGLU_ALPHA * gate)
        ys_o[...] = _dot(act.astype(BF16), wdn_b[...]) + bdn_ref[0, 0]

    @pl.when(jnp.logical_not(live))
    def _():
        ys_o[...] = jnp.zeros_like(ys_o)


def _moe(tile_expert, n_used, group_index, next_expert, xs, wgu, bgu, wdn, bdn, *, layer):
    n_slots, hw = xs.shape
    _, ne, d, de2 = wgu.shape
    bias = lambda cols: pl.BlockSpec((1, 1, 1, cols), lambda j, te, nu, gi, nx: (layer, te[j], 0, 0))
    grid_spec = pltpu.PrefetchScalarGridSpec(
        num_scalar_prefetch=4,
        grid=(n_slots // TS,),
        in_specs=[pl.BlockSpec((TS, hw), lambda j, te, nu, gi, nx: (jnp.minimum(j, nu[0] - 1), 0)),
                  bias(de2), bias(d), pl.BlockSpec(memory_space=pl.ANY), pl.BlockSpec(memory_space=pl.ANY)],
        out_specs=pl.BlockSpec((TS, d), lambda j, te, nu, gi, nx: (j, 0)),
        scratch_shapes=[pltpu.VMEM((2, d, de2), F32), pltpu.VMEM((2, de2 // 2, d), F32),
                        pltpu.VMEM((d, de2), BF16), pltpu.VMEM((de2 // 2, d), BF16),
                        pltpu.SemaphoreType.DMA((2, 2))],
    )
    return pl.pallas_call(
        functools.partial(_moe_body, layer=layer),
        grid_spec=grid_spec,
        out_shape=jax.ShapeDtypeStruct((n_slots, d), F32),
        compiler_params=pltpu.CompilerParams(dimension_semantics=("arbitrary",), vmem_limit_bytes=VMEM_LIMIT),
        name="moe_experts",
    )(tile_expert, n_used, group_index, next_expert, xs, bgu[:, :, None, :], bdn[:, :, None, :], wgu, wdn)


def _combine_body(dest_ref, x1_ref, gv_ref, gfin_ref, ys_hbm, y_o, ybuf, sem):
    t0 = pl.program_id(0) * TM

    def row_copy(r, k):
        d = dest_ref[(t0 + r) * TOP_EXPERTS + k]
        return pltpu.make_async_copy(ys_hbm.at[pl.ds(d, 1)], ybuf.at[k, pl.ds(r, 1)], sem)

    def start(r, _):
        for k in range(TOP_EXPERTS):
            row_copy(r, k).start()
        return 0
    lax.fori_loop(0, TM, start, 0, unroll=DMA_UNROLL)

    for k in range(TOP_EXPERTS):
        pltpu.make_async_copy(ys_hbm.at[pl.ds(0, TM)], ybuf.at[k], sem).wait()

    gv = gv_ref[...]
    y = jnp.zeros(x1_ref.shape, F32)
    for k in range(TOP_EXPERTS):
        y = y + ybuf[k] * gv[:, k:k + 1]
    y_o[...] = _rms(x1_ref[...] + y, gfin_ref[...])


def _combine(dest_flat, x1, gv, gfin, ys):
    nt, d = x1.shape
    grid_spec = pltpu.PrefetchScalarGridSpec(
        num_scalar_prefetch=1,
        grid=(nt // TM,),
        in_specs=[pl.BlockSpec((TM, d), lambda i, dref: (i, 0)),
                  pl.BlockSpec((TM, LANES), lambda i, dref: (i, 0)),
                  pl.BlockSpec((1, d), lambda i, dref: (0, 0)),
                  pl.BlockSpec(memory_space=pl.ANY)],
        out_specs=pl.BlockSpec((TM, d), lambda i, dref: (i, 0)),
        scratch_shapes=[pltpu.VMEM((TOP_EXPERTS, TM, d), F32), pltpu.SemaphoreType.DMA],
    )
    return pl.pallas_call(
        _combine_body,
        grid_spec=grid_spec,
        out_shape=jax.ShapeDtypeStruct((nt, d), F32),
        compiler_params=pltpu.CompilerParams(dimension_semantics=("arbitrary",)),
        name="combine_norm",
    )(dest_flat, x1, gv, gfin, ys)


def _tri_constants():
    r = lax.broadcasted_iota(I32, (TQ, TQ), 0)
    c = lax.broadcasted_iota(I32, (TQ, TQ), 1)
    incl = (r <= c).astype(BF16)
    u2 = jnp.concatenate([incl, jnp.ones((TQ, LANES), BF16)], axis=1)
    ltri = (c < r).astype(BF16)
    return u2, ltri


def kernel(x_prompt, x_sample, cache_k, cache_v, cache_kidx, state_pool, page_table, meta_tokens, rel_bias, g_mix, w_in, w_pool_mix, pool_scale, w_br_attn, w_br_pool, w_out, g_ffn, w_router, b_router, w_gate_up, b_gate_up, w_down, b_down, g_final):
    nb, seq, d = x_prompt.shape
    db, dec_seq, _ = x_sample.shape
    assert dec_seq == 1, "one new token per sample"
    depth = w_in.shape[0]
    assert depth == 1, "single-layer stack: the combine kernel applies the final norm"
    n_pages = page_table.shape[1]
    assert n_pages % SUBLANES == 0, "the sample top-k lays the cached keys out as eight equal page segments"
    past = n_pages * PAGE_SIZE
    pw = state_pool.shape[-1]
    l_seq = seq + N_META
    lp = _round_up(l_seq, TQ)
    assert l_seq + db <= lp, "sample rows must fit in the padding of the last prompt sequence"
    nt = nb * lp
    s0 = (nb - 1) * lp + l_seq
    ksel_p = min(TOPK_MAX, seq // 4)
    ksel_s = min(TOPK_MAX, (past + dec_seq) // 4)
    n_tiles = (TOP_EXPERTS * nt) // TS + N_EXPERTS
    n_slots = n_tiles * TS

    meta = meta_tokens.astype(x_prompt.dtype)
    pieces = []
    for bi in range(nb):
        pieces += [meta, x_prompt[bi], jnp.zeros((lp - l_seq, d), x_prompt.dtype)]
    pieces[-1] = jnp.concatenate([x_sample[:, 0, :], jnp.zeros((lp - l_seq - db, d), x_prompt.dtype)], axis=0)
    x_all = jnp.concatenate(pieces, axis=0)

    u2, ltri = _tri_constants()
    page_flat = page_table.reshape(-1).astype(I32)
    rel_bias = rel_bias.astype(F32)
    cache_k_t = jnp.transpose(cache_k, (0, 1, 3, 4, 2))
    cache_v_t = jnp.transpose(cache_v, (0, 1, 3, 4, 2))
    cache_ki_t = jnp.transpose(cache_kidx, (0, 1, 3, 2))

    outs = {name: [] for name in ("k_p", "v_p", "ki_p", "pool_p", "k_s", "v_s", "ki_s", "pool_s")}
    for l in range(depth):
        wl = w_in[l]
        aw = ATTN_WIDTH
        o_ki = 4 * aw
        o_wi = o_ki + IDX_DIM
        o_p = o_wi + N_IDX_HEADS
        o_ga = o_p + pw
        o_gb = o_ga + d
        wa = wl[:, 0:o_ki].astype(BF16)
        wki = wl[:, o_ki:o_wi]
        wki2 = jnp.concatenate([wki, wki], axis=1).astype(BF16)
        wwi = jnp.pad(wl[:, o_wi:o_p], ((0, 0), (0, LANES - N_IDX_HEADS))).astype(BF16)
        wp = wl[:, o_p:o_ga].astype(BF16)
        wga = wl[:, o_ga:o_gb].astype(BF16)
        wgb = wl[:, o_gb:o_gb + d].astype(BF16)

        (qs, kf, vf, kb, vb, qib, ki2b, kif, wif, pf, ga, gb, k_t, v_t, ki_t) = _inproj(
            x_all, g_mix[l][None, :], wa, wki2, wwi, wp, wga, wgb, nb=nb, lp=lp, l_seq=l_seq)

        tail_rows = _round_up(l_seq - (lp - TQ), 2 * SUBLANES)
        attn_p = _attn_prompt(qs, qib, wif, kb, vb, ki2b, u2, rel_bias, nb=nb, lp=lp, ksel=ksel_p,
                              tail_rows=tail_rows)
        pooled_p = _pool_prompt(pf, n_rows=nt, lp=lp)

        sl = slice(s0, s0 + db)
        keys = _sample_keys(page_flat, qib[sl].reshape(db, N_IDX_HEADS, IDX_DIM), kif[sl, 0:IDX_DIM][:, None, :],
                            wif[sl, 0:N_IDX_HEADS][:, :, None], cache_ki_t, layer=l, n_pages=n_pages)
        sel = _sample_select(keys, past=past, ksel=ksel_s)
        cols = lambda a: a[sl].astype(F32).reshape(db, N_HEADS, HEAD_DIM, 1)
        attn_s, pooled_s = _sample_attend(
            page_flat, sel, cols(qs), cols(kf), cols(vf), rel_bias, state_pool[l], pf[sl][:, None, :],
            cache_k_t, cache_v_t, layer=l, n_pages=n_pages)

        attn_all = lax.dynamic_update_slice(attn_p, attn_s.reshape(db, ATTN_WIDTH).astype(BF16), (s0, 0))
        pooled_all = lax.dynamic_update_slice(pooled_p, pooled_s[:, 0, :], (s0, 0))

        wr = jnp.pad(w_router[l], ((0, 0), (0, LANES - N_EXPERTS))).astype(BF16)
        br = jnp.concatenate([b_router[l].astype(F32), jnp.full((LANES - N_EXPERTS,), NEG, F32)])[None, :]
        x1, h2p, ev, gv, rk, cnt = _merge(
            x_all, attn_all, pooled_all, ga, gb, w_br_attn[l].astype(BF16), w_pool_mix[l].astype(BF16),
            pool_scale[l][None, :], w_br_pool[l].astype(BF16), w_out[l].astype(BF16), g_ffn[l][None, :], wr, br, ltri)

        counts = cnt[0, 0:N_EXPERTS].astype(I32)
        padded = (counts + TS - 1) // TS * TS
        gend = jnp.cumsum(padded)
        gstart = gend - padded
        experts = jnp.arange(N_EXPERTS, dtype=I32)
        ev4 = ev[:, 0:TOP_EXPERTS]
        dest = (jnp.sum(jnp.where(ev4[:, :, None] == experts, gstart, 0), axis=2) + rk[:, 0:TOP_EXPERTS]).reshape(-1)
        tile_first = jnp.arange(n_tiles, dtype=I32)[:, None] * TS
        tile_expert = jnp.minimum(jnp.sum((gend[None, :] <= tile_first).astype(I32), axis=1), N_EXPERTS - 1)
        n_used = (gend[-1] // TS).astype(I32)[None]
        nonempty = counts > 0
        rank_e = jnp.cumsum(nonempty.astype(I32)) - nonempty.astype(I32)
        later = jnp.logical_and(nonempty[None, :], experts[None, :] > experts[:, None])
        next_e = jnp.min(jnp.where(later, experts[None, :], N_EXPERTS), axis=1)
        next_e = jnp.where(next_e == N_EXPERTS, -1, next_e)
        tile_onehot = tile_expert[:, None] == experts[None, :]
        group_index = jnp.sum(jnp.where(tile_onehot, rank_e, 0), axis=1).astype(I32)
        next_expert = jnp.sum(jnp.where(tile_onehot, next_e, 0), axis=1).astype(I32)

        xs_sorted = _dispatch(dest, jnp.concatenate([gend, gstart + counts]).astype(I32), h2p, n_slots=n_slots)
        ys = _moe(tile_expert, n_used, group_index, next_expert, xs_sorted, w_gate_up, b_gate_up, w_down, b_down,
                  layer=l)
        y_all = _combine(dest, x1, gv, g_final[None, :], ys)

        outs["k_p"].append(jnp.transpose(k_t, (0, 3, 1, 2)))
        outs["v_p"].append(jnp.transpose(v_t, (0, 3, 1, 2)))
        outs["ki_p"].append(jnp.transpose(ki_t, (0, 2, 1)))
        outs["pool_p"].append(pf.reshape(nb, lp, pw)[:, l_seq - POOL_STATE:l_seq])
        outs["k_s"].append(kf[sl].reshape(db, 1, N_HEADS, HEAD_DIM))
        outs["v_s"].append(vf[sl].reshape(db, 1, N_HEADS, HEAD_DIM))
        outs["ki_s"].append(kif[sl, 0:IDX_DIM].reshape(db, 1, IDX_DIM))
        outs["pool_s"].append(jnp.concatenate([state_pool[l][:, 1:], pf[sl][:, None, :]], axis=1))

    y_prompt = y_all.reshape(nb, lp, d)[:, N_META:l_seq]
    y_sample = y_all[s0:s0 + db].reshape(db, 1, d)
    st = lambda name: jnp.stack(outs[name])
    return (y_prompt, y_sample, st("k_p"), st("v_p"), st("ki_p"), st("pool_p"),
            st("k_s"), st("v_s"), st("ki_s"), st("pool_s"))
```

```python
import functools
import math

import jax
import jax.numpy as jnp
from jax import lax
from jax.experimental import pallas as pl
from jax.experimental.pallas import tpu as pltpu

F32 = jnp.float32
BF16 = jnp.bfloat16
I32 = jnp.int32
U32 = jnp.uint32

N_META = 16
N_HEADS = 8
HEAD_DIM = 64
ATTN_WIDTH = N_HEADS * HEAD_DIM
N_IDX_HEADS = 8
IDX_DIM = 64
TOPK_MAX = 256
N_BUCKETS = 32
MAX_DISTANCE = 128
POOL_WINDOWS = (2, 4, 8, 16)
POOL_GROUPS = 4
POOL_STATE = 15
N_EXPERTS = 32
TOP_EXPERTS = 4
SWIGLU_LIMIT = 7.0
SWIGLU_ALPHA = 1.702
PAGE_SIZE = 128
EPS = 1e-6

LANES = 128
SUBLANES = 8
MXU_DIM = 256
TQ = MXU_DIM
TM = MXU_DIM
TS = MXU_DIM
VMEM_LIMIT = 56 * 1024 * 1024
DMA_UNROLL = 8

NEG = -1e30
INT_MIN = -(2 ** 31)

_MAX_EXACT = N_BUCKETS // 2
_BUCKET_THRESHOLDS = tuple(
    math.ceil(_MAX_EXACT * (MAX_DISTANCE / _MAX_EXACT) ** (j / (N_BUCKETS - _MAX_EXACT)))
    for j in range(1, N_BUCKETS - _MAX_EXACT))


def _round_up(a, m):
    return (a + m - 1) // m * m


def _rms(x, g):
    return x * lax.rsqrt(jnp.mean(x * x, axis=-1, keepdims=True) + EPS) * g


def _dot(a, b):
    return jnp.dot(a, b, preferred_element_type=F32)


def _dot_nt(a, b):
    return lax.dot_general(a, b, (((1,), (1,)), ((), ())), preferred_element_type=F32)


def _bucket(dist):
    large = jnp.full(dist.shape, _MAX_EXACT, I32)
    for thr in _BUCKET_THRESHOLDS:
        large = large + jnp.where(dist >= thr, 1, 0)
    return jnp.where(dist < _MAX_EXACT, dist, large)


def _sort_key(s):
    s = jnp.where(s == 0.0, 0.0, s)
    bits = lax.bitcast_convert_type(s, I32)
    return jnp.where(bits >= 0, bits, bits ^ jnp.int32(0x7FFFFFFF))


def _full(shape):
    return pl.BlockSpec(shape, lambda *_: (0,) * len(shape))


def _inproj_body(x_ref, g_ref, wa_ref, wki_ref, wwi_ref, wp_ref, wga_ref, wgb_ref,
                 qs_o, kf_o, vf_o, kb_o, vb_o, qib_o, ki2b_o, kif_o, wif_o, pf_o, ga_o, gb_o, kt_o, vt_o, kit_o):
    h = _rms(x_ref[...], g_ref[...]).astype(BF16)
    za = _dot(h, wa_ref[...])
    w = ATTN_WIDTH
    qs_o[...] = (za[:, 0:w] * (HEAD_DIM ** -0.5)).astype(BF16)
    k = za[:, w:2 * w]
    v = za[:, 2 * w:3 * w]
    kf_o[...] = k
    vf_o[...] = v
    kb_o[...] = k.astype(BF16)
    vb_o[...] = v.astype(BF16)
    qib_o[...] = za[:, 3 * w:4 * w].astype(BF16)
    ki2 = _dot(h, wki_ref[...])
    kif_o[...] = ki2
    ki2b_o[...] = ki2.astype(BF16)
    wif_o[...] = _dot(h, wwi_ref[...])
    pf_o[...] = _dot(h, wp_ref[...])
    ga_o[...] = _dot(h, wga_ref[...])
    gb_o[...] = _dot(h, wgb_ref[...])
    kt_o[0] = k.T.reshape(N_HEADS, HEAD_DIM, TM)
    vt_o[0] = v.T.reshape(N_HEADS, HEAD_DIM, TM)
    kit_o[0] = ki2.T[0:IDX_DIM, :]


def _inproj(x_all, g, wa, wki2, wwi, wp, wga, wgb, *, nb, lp, l_seq):
    nt, d = x_all.shape
    pw = wp.shape[1]
    tps = lp // TM
    row = lambda width: pl.BlockSpec((TM, width), lambda i: (i, 0))
    outs = [
        (ATTN_WIDTH, BF16), (ATTN_WIDTH, F32), (ATTN_WIDTH, F32), (ATTN_WIDTH, BF16), (ATTN_WIDTH, BF16),
        (ATTN_WIDTH, BF16), (LANES, BF16), (LANES, F32), (LANES, F32), (pw, F32), (d, F32), (d, F32)]
    seq_minor_specs = [
        pl.BlockSpec((1, N_HEADS, HEAD_DIM, TM), lambda i: (i // tps, 0, 0, i % tps)),
        pl.BlockSpec((1, N_HEADS, HEAD_DIM, TM), lambda i: (i // tps, 0, 0, i % tps)),
        pl.BlockSpec((1, IDX_DIM, TM), lambda i: (i // tps, 0, i % tps))]
    seq_minor_shapes = [
        jax.ShapeDtypeStruct((nb, N_HEADS, HEAD_DIM, l_seq), F32), jax.ShapeDtypeStruct((nb, N_HEADS, HEAD_DIM, l_seq), F32),
        jax.ShapeDtypeStruct((nb, IDX_DIM, l_seq), F32)]
    return pl.pallas_call(
        _inproj_body,
        grid=(nt // TM,),
        in_specs=[row(d), _full((1, d)), _full(wa.shape), _full(wki2.shape), _full(wwi.shape),
                  _full(wp.shape), _full(wga.shape), _full(wgb.shape)],
        out_specs=[row(wd) for wd, _ in outs] + seq_minor_specs,
        out_shape=[jax.ShapeDtypeStruct((nt, wd), dt) for wd, dt in outs] + seq_minor_shapes,
        compiler_params=pltpu.CompilerParams(dimension_semantics=("arbitrary",), vmem_limit_bytes=VMEM_LIMIT),
        name="inproj",
    )(x_all, g, wa, wki2, wwi, wp, wga, wgb)


def _attn_prompt_body(qs_ref, qib_ref, wif_ref, kb_ref, vb_ref, ki2b_ref, u2_ref, rb_ref,
                      o_ref, qis_scr, wb_scr, key_scr, ntab_scr, qz_scr, m_scr, acc_scr, sum_scr, *, ksel, tail_rows):
    b = pl.program_id(0)
    i = pl.program_id(1)
    nchunk = i + 1
    n_far = jnp.maximum(i - 1, 0)

    @pl.when((b == 0) & (i == 0))
    def _():
        def slab_rows(s, _):
            r0 = pl.multiple_of(s * SUBLANES, SUBLANES)
            r = r0 + lax.broadcasted_iota(I32, (SUBLANES, 2 * TQ), 0)
            x = lax.broadcasted_iota(I32, (SUBLANES, 2 * TQ), 1)
            bucket = _bucket(jnp.maximum(r + TQ - x, 0))
            accs = [jnp.zeros((SUBLANES, 2 * TQ), F32) for _ in range(N_HEADS)]
            for j in range(N_BUCKETS - 1):
                m = bucket == j
                accs = [jnp.where(m, rb_ref[j, h] - rb_ref[N_BUCKETS - 1, h], accs[h]) for h in range(N_HEADS)]
            for h in range(N_HEADS):
                ntab_scr[h, 0, pl.ds(r0, SUBLANES), :] = accs[h][:, 0:TQ]
                ntab_scr[h, 1, pl.ds(r0, SUBLANES), :] = accs[h][:, TQ:2 * TQ]
            return 0
        lax.fori_loop(0, TQ // SUBLANES, slab_rows, 0)

    def two(x):
        return jnp.concatenate([x, x], axis=1)

    def block(nr):
        hr = nr // 2
        lane = lax.broadcasted_iota(I32, (nr, LANES), 1)
        lo_half = lane < HEAD_DIM

        for h in range(N_HEADS):
            cols = slice((h // 2) * LANES, (h // 2 + 1) * LANES)
            keep = lo_half if h % 2 == 0 else jnp.logical_not(lo_half)
            qis_scr[h, 0:nr] = jnp.where(keep, qib_ref[0:nr, cols].astype(F32), 0.0).astype(BF16)
            qz_scr[h // 2, (h % 2) * nr:(h % 2 + 1) * nr] = jnp.where(keep, qs_ref[0:nr, cols].astype(F32), 0.0).astype(BF16)
            wb_scr[h, 0:nr] = jnp.broadcast_to(wif_ref[0:nr, h:h + 1], (nr, LANES))
            m_scr[h, 0:nr] = jnp.full((nr, LANES), NEG, F32)
            acc_scr[h, 0:nr] = jnp.zeros((nr, LANES), F32)
            sum_scr[h, 0:nr] = jnp.zeros((nr, LANES), F32)

        dmat = lax.broadcasted_iota(I32, (nr, TQ), 1) - lax.broadcasted_iota(I32, (nr, TQ), 0)

        def score_chunk(c, _):
            kc = ki2b_ref[pl.ds(pl.multiple_of(c * TQ, TQ), TQ), :]
            s = jnp.zeros((nr, TQ), F32)
            for h in range(N_IDX_HEADS):
                s = s + two(wb_scr[h, 0:nr]) * jnp.maximum(_dot_nt(qis_scr[h, 0:nr], kc), 0.0)
            key_scr[c, 0:nr] = jnp.where(dmat <= (i - c) * TQ, _sort_key(s), INT_MIN)
            return 0
        lax.fori_loop(0, nchunk, score_chunk, 0)
        key_scr[nchunk, 0:nr] = jnp.full((nr, TQ), INT_MIN, I32)

        def count(r0, pred):
            def body(j, acc):
                for c in (2 * j, 2 * j + 1):
                    hit = jnp.where(pred(key_scr[c, r0:r0 + hr, :]), 1.0, 0.0)
                    acc = acc + (hit[:, 0:LANES] + hit[:, LANES:2 * LANES])
                return acc
            acc = lax.fori_loop(0, (nchunk + 1) // 2, body, jnp.zeros((hr, LANES), F32))
            return jnp.broadcast_to(jnp.sum(acc, axis=1, keepdims=True), (hr, LANES))

        ts, exact = [], []
        for r0 in (0, hr):
            def bit_step(step, t, r0=r0):
                cand = t + lax.shift_left(jnp.int32(1), 31 - step)
                cand2 = two(cand)
                return jnp.where(count(r0, lambda k: k >= cand2) >= ksel, cand, t)
            t = lax.fori_loop(0, 32, bit_step, jnp.full((hr, LANES), INT_MIN, I32))
            th2 = two(t)
            ts.append(t)
            n_ge = count(r0, lambda k: k >= th2)
            exact.append(jnp.where(jnp.logical_and(n_ge == ksel, t != INT_MIN), 1.0, 0.0))
        t2 = two(jnp.concatenate(ts, axis=0))
        no_ties = jnp.min(jnp.concatenate(exact, axis=0)) > 0.5

        @pl.when(no_ties)
        def _():
            def mask_chunk(c, _):
                madd = jnp.where(key_scr[c, 0:nr] >= t2, 0.0, NEG)
                key_scr[c, 0:nr] = lax.bitcast_convert_type(madd, I32)
                return 0
            lax.fori_loop(0, nchunk, mask_chunk, 0)

        @pl.when(jnp.logical_not(no_ties))
        def _():
            needs = []
            for g, r0 in enumerate((0, hr)):
                th2 = two(ts[g])
                n_gt = count(r0, lambda k: k > th2)
                needs.append(jnp.where(ts[g] == INT_MIN, 0.0, ksel - n_gt))
            need2 = two(jnp.concatenate(needs, axis=0))

            def mask_chunk(c, carry):
                kc = key_scr[c, 0:nr]
                eq = kc == t2
                pre = _dot(jnp.where(eq, 1.0, 0.0).astype(BF16), u2_ref[...])
                prefix = pre[:, 0:TQ] + two(carry)
                tie_ok = jnp.where(eq, prefix, 3e38) <= need2
                madd = jnp.where(kc > t2, 0.0, jnp.where(tie_ok, 0.0, NEG))
                key_scr[c, 0:nr] = lax.bitcast_convert_type(madd, I32)
                return carry + pre[:, TQ:TQ + LANES]
            lax.fori_loop(0, nchunk, mask_chunk, jnp.zeros((nr, LANES), F32))

        def chunk_logits(c, slab):
            rows = pl.ds(pl.multiple_of(c * TQ, TQ), TQ)
            madd = lax.bitcast_convert_type(key_scr[c, 0:nr], F32)
            out = []
            for hp in range(N_HEADS // 2):
                kc = kb_ref[rows, hp * LANES:(hp + 1) * LANES]
                pair = _dot_nt(qz_scr[hp, 0:2 * nr], kc)
                for sub in range(2):
                    l = pair[sub * nr:(sub + 1) * nr] + madd
                    out.append(l if slab is None else l + ntab_scr[2 * hp + sub, slab, 0:nr])
            return rows, out

        def accumulate(c, slab):
            rows, ls = chunk_logits(c, slab)
            for hp in range(N_HEADS // 2):
                ps, alphas = [], []
                for h in (2 * hp, 2 * hp + 1):
                    l = ls[h]
                    m_old = m_scr[h, 0:nr]
                    m_chunk = jnp.max(jnp.maximum(l[:, 0:LANES], l[:, LANES:2 * LANES]), axis=1, keepdims=True)
                    m_new = jnp.maximum(m_old, jnp.broadcast_to(m_chunk, (nr, LANES)))
                    alpha = jnp.exp(m_old - m_new)
                    p = jnp.exp(l - two(m_new))
                    m_scr[h, 0:nr] = m_new
                    sum_scr[h, 0:nr] = alpha * sum_scr[h, 0:nr] + (p[:, 0:LANES] + p[:, LANES:2 * LANES])
                    ps.append(p.astype(BF16))
                    alphas.append(alpha)
                pv = _dot(jnp.concatenate(ps, axis=0), vb_ref[rows, hp * LANES:(hp + 1) * LANES])
                for sub in range(2):
                    h = 2 * hp + sub
                    acc_scr[h, 0:nr] = alphas[sub] * acc_scr[h, 0:nr] + pv[sub * nr:(sub + 1) * nr]

        def far(c, _):
            accumulate(c, None)
            return 0
        lax.fori_loop(0, n_far, far, 0)

        @pl.when(i >= 1)
        def _():
            accumulate(i - 1, 0)
        accumulate(i, 1)
        for hp in range(N_HEADS // 2):
            outs = [acc_scr[h, 0:nr] / jnp.sum(sum_scr[h, 0:nr], axis=1, keepdims=True) for h in (2 * hp, 2 * hp + 1)]
            o_ref[0:nr, hp * LANES:(hp + 1) * LANES] = jnp.where(lo_half, outs[0], outs[1]).astype(BF16)
        if nr < TQ:
            o_ref[nr:TQ, :] = jnp.zeros((TQ - nr, ATTN_WIDTH), BF16)

    if tail_rows == TQ:
        block(TQ)
    else:
        last = pl.num_programs(1) - 1
        pl.when(i < last)(lambda: block(TQ))
        pl.when(i == last)(lambda: block(tail_rows))


def _attn_prompt(qs, qib, wif, kb, vb, ki2b, u2, rel_bias, *, nb, lp, ksel, tail_rows):
    nq = lp // TQ
    rowq = lambda width: pl.BlockSpec((TQ, width), lambda b, i: (b * nq + i, 0))
    seq = lambda width: pl.BlockSpec((lp, width), lambda b, i: (b, 0))
    per_head = lambda dt: pltpu.VMEM((N_HEADS, TQ, LANES), dt)
    return pl.pallas_call(
        functools.partial(_attn_prompt_body, ksel=float(ksel), tail_rows=tail_rows),
        grid=(nb, nq),
        in_specs=[rowq(ATTN_WIDTH), rowq(ATTN_WIDTH), rowq(LANES), seq(ATTN_WIDTH), seq(ATTN_WIDTH), seq(LANES),
                  _full(u2.shape), pl.BlockSpec(memory_space=pltpu.SMEM)],
        out_specs=rowq(ATTN_WIDTH),
        out_shape=jax.ShapeDtypeStruct((nb * lp, ATTN_WIDTH), BF16),
        scratch_shapes=[
            per_head(BF16),
            per_head(F32),
            pltpu.VMEM((nq + 1, TQ, TQ), I32),
            pltpu.VMEM((N_HEADS, 2, TQ, TQ), F32),
            pltpu.VMEM((N_HEADS // 2, 2 * TQ, LANES), BF16),
            per_head(F32), per_head(F32), per_head(F32),
        ],
        compiler_params=pltpu.CompilerParams(dimension_semantics=("arbitrary", "arbitrary"),
                                             vmem_limit_bytes=VMEM_LIMIT),
        name="attn_prompt",
    )(qs, qib, wif, kb, vb, ki2b, u2, rel_bias)


def _pool_prompt_body(p_ref, halo_ref, o_ref, ext_scr, *, tiles_per_seq):
    j = pl.program_id(0) % tiles_per_seq
    halo = 2 * SUBLANES
    p = p_ref[...]
    ext_scr[0:halo, :] = jnp.where(j == 0, 0.0, halo_ref[...])
    ext_scr[halo:halo + TM, :] = p
    pos = j * TM + lax.broadcasted_iota(I32, (TM, 1), 0)
    gw = p.shape[1] // POOL_GROUPS
    for g, w in enumerate(POOL_WINDOWS):
        cols = slice(g * gw, (g + 1) * gw)
        s = p[:, cols]
        for back in range(1, w):
            s = s + ext_scr[halo - back:halo - back + TM, cols]
        cnt = jnp.minimum(pos + 1, w).astype(F32)
        o_ref[:, cols] = (s / cnt - p[:, cols]).astype(BF16)


def _pool_prompt(pf, *, n_rows, lp):
    pw = pf.shape[1]
    halo = 2 * SUBLANES
    return pl.pallas_call(
        functools.partial(_pool_prompt_body, tiles_per_seq=lp // TM),
        grid=(n_rows // TM,),
        in_specs=[pl.BlockSpec((TM, pw), lambda i: (i, 0)),
                  pl.BlockSpec((halo, pw), lambda i: (jnp.maximum(i * (TM // halo) - 1, 0), 0))],
        out_specs=pl.BlockSpec((TM, pw), lambda i: (i, 0)),
        out_shape=jax.ShapeDtypeStruct((n_rows, pw), BF16),
        scratch_shapes=[pltpu.VMEM((halo + TM, pw), F32)],
        compiler_params=pltpu.CompilerParams(dimension_semantics=("arbitrary",)),
        name="pool_prompt",
    )(pf, pf)


def _sample_keys_body(pt_ref, qi_ref, kinew_ref, wi_ref, cache_hbm, key_o, kibuf, sem, *, layer, n_pages):
    s = pl.program_id(0)
    past = n_pages * PAGE_SIZE
    seg_w = past // SUBLANES

    copies = [pltpu.make_async_copy(cache_hbm.at[layer, pt_ref[s * n_pages + pg]],
                                    kibuf.at[:, pg * PAGE_SIZE:(pg + 1) * PAGE_SIZE], sem) for pg in range(n_pages)]
    for cp in copies:
        cp.start()
    for cp in copies:
        cp.wait()

    qi = qi_ref[0]
    wcol = wi_ref[0]
    d = _dot(qi, kibuf[...].astype(BF16))
    sc = jnp.sum(wcol * jnp.maximum(d, 0.0), axis=0, keepdims=True)
    knew = kinew_ref[0].astype(BF16).astype(F32)
    dn = jnp.sum(qi.astype(F32) * knew, axis=1, keepdims=True)
    sn = jnp.sum(wcol * jnp.maximum(dn, 0.0), axis=0, keepdims=True)
    key = _sort_key(sc)
    lane = lax.broadcasted_iota(I32, (1, LANES), 1)
    new_tail = jnp.where(lane == 0, _sort_key(jnp.broadcast_to(sn, (1, LANES))), INT_MIN)
    for g in range(SUBLANES):
        key_o[0, g:g + 1, 0:seg_w] = key[:, g * seg_w:(g + 1) * seg_w]
        key_o[0, g:g + 1, seg_w:seg_w + LANES] = new_tail if g == 0 else jnp.full((1, LANES), INT_MIN, I32)


def _sample_keys(page_table_flat, qi3, kinew3, wi3, cache_ki_t, *, layer, n_pages):
    db = qi3.shape[0]
    past = n_pages * PAGE_SIZE
    kw = past // SUBLANES + LANES
    grid_spec = pltpu.PrefetchScalarGridSpec(
        num_scalar_prefetch=1,
        grid=(db,),
        in_specs=[pl.BlockSpec((1, N_IDX_HEADS, IDX_DIM), lambda s, pt: (s, 0, 0)),
                  pl.BlockSpec((1, 1, IDX_DIM), lambda s, pt: (s, 0, 0)),
                  pl.BlockSpec((1, N_IDX_HEADS, 1), lambda s, pt: (s, 0, 0)),
                  pl.BlockSpec(memory_space=pl.ANY)],
        out_specs=pl.BlockSpec((1, SUBLANES, kw), lambda s, pt: (s, 0, 0)),
        scratch_shapes=[pltpu.VMEM((IDX_DIM, past), F32), pltpu.SemaphoreType.DMA],
    )
    return pl.pallas_call(
        functools.partial(_sample_keys_body, layer=layer, n_pages=n_pages),
        grid_spec=grid_spec,
        out_shape=jax.ShapeDtypeStruct((db, SUBLANES, kw), I32),
        compiler_params=pltpu.CompilerParams(dimension_semantics=("arbitrary",)),
        name="sample_keys",
    )(page_table_flat, qi3, kinew3, wi3, cache_ki_t)


def _sample_select_body(key_ref, sel_o, *, past, ksel):
    keys = key_ref[...]
    kw = keys.shape[2]
    seg_w = kw - LANES
    row = lax.broadcasted_iota(I32, (SUBLANES, kw), 0)
    lane = lax.broadcasted_iota(I32, (SUBLANES, kw), 1)
    pos = jnp.where(lane < seg_w, row * seg_w + lane, past + (lane - seg_w) + row * LANES)[None]

    def cnt(pred):
        x = jnp.where(pred, 1.0, 0.0)
        return jnp.sum(jnp.sum(x, axis=2, keepdims=True), axis=1, keepdims=True)

    def bit_step(step, t):
        cand = t + lax.shift_left(jnp.int32(1), 31 - step)
        return jnp.where(cnt(keys >= cand) >= ksel, cand, t)
    t = lax.fori_loop(0, 32, bit_step, jnp.full((keys.shape[0], 1, 1), INT_MIN, I32))
    gt = keys > t
    eq = keys == t
    need = ksel - cnt(gt)

    nbits = (past + SUBLANES * LANES).bit_length()

    def cut_step(step, c):
        cand = c - lax.shift_left(jnp.int32(1), nbits - 1 - step)
        ok = cnt(jnp.logical_and(eq, pos <= cand)) >= need
        return jnp.where(ok, cand, c)
    cut = lax.fori_loop(0, nbits, cut_step, jnp.full((keys.shape[0], 1, 1), 2 ** nbits - 1, I32))
    sel = jnp.where(gt, 1.0, jnp.where(jnp.logical_and(eq, pos <= cut), 1.0, 0.0))
    sel_o[...] = jnp.where(pos <= past, sel, 0.0)


def _sample_select(keys, *, past, ksel):
    return pl.pallas_call(
        functools.partial(_sample_select_body, past=past, ksel=float(ksel)),
        out_shape=jax.ShapeDtypeStruct(keys.shape, F32),
        name="sample_select",
    )(keys)


def _sample_attend_body(pt_ref, sel_ref, q_ref, knew_ref, vnew_ref, rb_ref, state_ref, pnew_ref, ck_hbm, cv_hbm,
                        attn_o, pool_o, kbuf, vbuf, qb_scr, lg_scr, acc_scr, ksem, vsem, *, layer, n_pages):
    s = pl.program_id(0)
    past = n_pages * PAGE_SIZE
    seg_w = past // SUBLANES
    pages_per_seg = n_pages // SUBLANES
    last_bias = [rb_ref[N_BUCKETS - 1, h] for h in range(N_HEADS)]

    def kcopy(pg, sample=s):
        return pltpu.make_async_copy(ck_hbm.at[layer, pt_ref[sample * n_pages + pg]], kbuf.at[pg], ksem.at[pg])

    def vcopy(pg, sample=s):
        return pltpu.make_async_copy(cv_hbm.at[layer, pt_ref[sample * n_pages + pg]], vbuf.at[pg], vsem.at[pg])

    def request(copy, sample):
        def start(pg, _):
            copy(pg, sample).start()
            return 0
        lax.fori_loop(0, n_pages, start, 0)

    has_next = s + 1 < pl.num_programs(0)

    @pl.when(s == 0)
    def _():
        request(kcopy, s)
        request(vcopy, s)

    for h in range(N_HEADS):
        qb_scr[h] = jnp.broadcast_to(q_ref[0, h], (HEAD_DIM, LANES))
        acc_scr[h] = jnp.zeros((HEAD_DIM, LANES), F32)

    def k_page(pg, _):
        kcopy(pg).wait()
        rows = [jnp.sum(kbuf[pg, h].astype(BF16).astype(F32) * qb_scr[h], axis=0, keepdims=True)
                for h in range(N_HEADS)]
        lg_scr[pg] = jnp.concatenate(rows, axis=0)
        return 0
    lax.fori_loop(0, n_pages, k_page, 0)
    pl.when(has_next)(lambda: request(kcopy, s + 1))

    lane = lax.broadcasted_iota(I32, (1, LANES), 1)
    bucket = _bucket(past - ((n_pages - 1) * PAGE_SIZE + lane))
    near_rows = []
    for h in range(N_HEADS):
        r = jnp.zeros((1, LANES), F32)
        for j in range(N_BUCKETS - 1):
            r = jnp.where(bucket == j, rb_ref[j, h] - last_bias[h], r)
        near_rows.append(r)
    near = jnp.concatenate(near_rows, axis=0)

    sel = sel_ref[0]
    m = jnp.full((N_HEADS, LANES), NEG, F32)
    for pg in range(n_pages):
        g, j = divmod(pg, pages_per_seg)
        keep = sel[g:g + 1, j * PAGE_SIZE:(j + 1) * PAGE_SIZE] > 0.0
        l = lg_scr[pg] + jnp.where(keep, 0.0, NEG)
        if pg == n_pages - 1:
            l = l + near
        lg_scr[pg] = l
        m = jnp.maximum(m, l)
    new_rows = []
    for h in range(N_HEADS):
        kn = knew_ref[0, h].astype(BF16).astype(F32)
        new_rows.append(jnp.sum(q_ref[0, h] * kn, axis=0, keepdims=True) + (rb_ref[0, h] - last_bias[h]))
    l_new = jnp.where(sel[0:1, seg_w:seg_w + 1] > 0.0, jnp.concatenate(new_rows, axis=0), NEG)
    m = jnp.maximum(jnp.max(m, axis=1, keepdims=True), l_new)
    ssum = jnp.zeros((N_HEADS, LANES), F32)
    for pg in range(n_pages):
        p = jnp.exp(lg_scr[pg] - m)
        lg_scr[pg] = p
        ssum = ssum + p
    p_new = jnp.exp(l_new - m)
    den = jnp.sum(ssum, axis=1, keepdims=True) + p_new
    for pg in range(n_pages):
        lg_scr[pg] = (lg_scr[pg] / den).astype(BF16).astype(F32)
    p_new = (p_new / den).astype(BF16).astype(F32)

    def v_page(pg, _):
        vcopy(pg).wait()
        p = lg_scr[pg]
        for h in range(N_HEADS):
            acc_scr[h] = acc_scr[h] + vbuf[pg, h].astype(BF16).astype(F32) * p[h:h + 1, :]
        return 0
    lax.fori_loop(0, n_pages, v_page, 0)
    pl.when(has_next)(lambda: request(vcopy, s + 1))
    for h in range(N_HEADS):
        vn = vnew_ref[0, h].astype(BF16).astype(F32)
        attn_o[0, h] = jnp.sum(acc_scr[h], axis=1, keepdims=True) + p_new[h:h + 1, :] * vn

    pnew = pnew_ref[0]
    st = state_ref[0]
    gw = pnew.shape[1] // POOL_GROUPS
    for g, w in enumerate(POOL_WINDOWS):
        cols = slice(g * gw, (g + 1) * gw)
        acc = pnew[:, cols]
        for back in range(1, w):
            acc = acc + st[POOL_STATE - back:POOL_STATE - back + 1, cols]
        pool_o[0, :, cols] = (acc / float(min(POOL_STATE + 1, w)) - pnew[:, cols]).astype(BF16)


def _sample_attend(page_table_flat, sel, q4, knew4, vnew4, rel_bias, state, pnew3, cache_k_t, cache_v_t, *, layer, n_pages):
    db = q4.shape[0]
    pw = pnew3.shape[2]
    per = lambda shape: pl.BlockSpec((1,) + shape, lambda s, pt: (s,) + (0,) * len(shape))
    col = (N_HEADS, HEAD_DIM, 1)
    page = (n_pages, N_HEADS, HEAD_DIM, LANES)
    grid_spec = pltpu.PrefetchScalarGridSpec(
        num_scalar_prefetch=1,
        grid=(db,),
        in_specs=[per(sel.shape[1:]), per(col), per(col), per(col), pl.BlockSpec(memory_space=pltpu.SMEM),
                  per((POOL_STATE, pw)), per((1, pw)),
                  pl.BlockSpec(memory_space=pl.ANY), pl.BlockSpec(memory_space=pl.ANY)],
        out_specs=[per(col), per((1, pw))],
        scratch_shapes=[pltpu.VMEM(page, F32), pltpu.VMEM(page, F32),
                        pltpu.VMEM((N_HEADS, HEAD_DIM, LANES), F32),
                        pltpu.VMEM((n_pages, N_HEADS, LANES), F32),
                        pltpu.VMEM((N_HEADS, HEAD_DIM, LANES), F32),
                        pltpu.SemaphoreType.DMA((n_pages,)), pltpu.SemaphoreType.DMA((n_pages,))],
    )
    return pl.pallas_call(
        functools.partial(_sample_attend_body, layer=layer, n_pages=n_pages),
        grid_spec=grid_spec,
        out_shape=[jax.ShapeDtypeStruct((db,) + col, F32), jax.ShapeDtypeStruct((db, 1, pw), BF16)],
        compiler_params=pltpu.CompilerParams(dimension_semantics=("arbitrary",), vmem_limit_bytes=VMEM_LIMIT),
        name="sample_attend",
    )(page_table_flat, sel, q4, knew4, vnew4, rel_bias, state, pnew3, cache_k_t, cache_v_t)


def _merge_body(x_ref, attn_ref, pool_ref, ga_ref, gb_ref, wba_ref, wpm_ref, ps_ref, wbp_ref, wout_ref,
                gffn_ref, wr_ref, br_ref, ltri_ref,
                x1_o, h2p_o, ev_o, gv_o, rk_o, cnt_o, carry_scr):
    step = pl.program_id(0)

    @pl.when(step == 0)
    def _():
        carry_scr[...] = jnp.zeros_like(carry_scr)

    a = _dot(attn_ref[...], wba_ref[...])
    pooled = pool_ref[...]
    gw = pooled.shape[1] // POOL_GROUPS
    pm = jnp.concatenate([_dot(pooled[:, g * gw:(g + 1) * gw], wpm_ref[g]) for g in range(POOL_GROUPS)], axis=1)
    bb = _dot((pm * ps_ref[...]).astype(BF16), wbp_ref[...])
    mix = jax.nn.sigmoid(ga_ref[...]) * a + jax.nn.sigmoid(gb_ref[...]) * bb
    x1 = x_ref[...] + _dot(mix.astype(BF16), wout_ref[...])
    x1_o[...] = x1
    h2 = _rms(x1, gffn_ref[...]).astype(BF16)
    half = h2.shape[1] // 2
    lo = lax.shift_right_logical(lax.bitcast_convert_type(h2[:, 0:half].astype(F32), U32), jnp.uint32(16))
    hi = lax.bitcast_convert_type(h2[:, half:].astype(F32), U32) & jnp.uint32(0xFFFF0000)
    h2p_o[...] = hi | lo

    logits = _dot(h2, wr_ref[...]) + br_ref[...]
    lane = lax.broadcasted_iota(I32, (TM, LANES), 1)
    lanef = lane.astype(F32)
    tops, ids = [], []
    l = logits
    for _ in range(TOP_EXPERTS):
        mx = jnp.max(l, axis=1, keepdims=True)
        ix = jnp.min(jnp.where(l == mx, lanef, float(LANES)), axis=1, keepdims=True)
        tops.append(mx)
        ids.append(ix)
        l = jnp.where(lanef == ix, -3e38, l)
    es = [jnp.exp(tv - tops[0]) for tv in tops]
    den = es[0] + es[1] + es[2] + es[3]
    onehot = jnp.zeros((TM, LANES), F32)
    for ix in ids:
        onehot = onehot + jnp.where(lanef == ix, 1.0, 0.0)
    before = _dot(ltri_ref[...], onehot.astype(BF16)) + carry_scr[0:1, :]
    ev = jnp.zeros((TM, LANES), I32)
    gv = jnp.zeros((TM, LANES), F32)
    rk = jnp.zeros((TM, LANES), I32)
    for k in range(TOP_EXPERTS):
        rank = jnp.sum(jnp.where(lanef == ids[k], before, 0.0), axis=1, keepdims=True)
        ev = jnp.where(lane == k, ids[k].astype(I32), ev)
        gv = jnp.where(lane == k, es[k] / den, gv)
        rk = jnp.where(lane == k, rank.astype(I32), rk)
    ev_o[...] = ev
    gv_o[...] = gv
    rk_o[...] = rk
    carry_scr[...] = carry_scr[...] + jnp.sum(onehot, axis=0, keepdims=True)
    cnt_o[...] = carry_scr[...]


def _merge(x_all, attn, pooled, ga, gb, wba, wpm, ps, wbp, wout, gffn, wr, br, ltri):
    nt, d = x_all.shape
    row = lambda width: pl.BlockSpec((TM, width), lambda i: (i, 0))
    consts = [wba, wpm, ps, wbp, wout, gffn, wr, br, ltri]
    return pl.pallas_call(
        _merge_body,
        grid=(nt // TM,),
        in_specs=[row(d), row(attn.shape[1]), row(pooled.shape[1]), row(d), row(d)] + [_full(c.shape) for c in consts],
        out_specs=[row(d), row(d // 2), row(LANES), row(LANES), row(LANES), _full((SUBLANES, LANES))],
        out_shape=[jax.ShapeDtypeStruct((nt, d), F32), jax.ShapeDtypeStruct((nt, d // 2), U32),
                   jax.ShapeDtypeStruct((nt, LANES), I32), jax.ShapeDtypeStruct((nt, LANES), F32),
                   jax.ShapeDtypeStruct((nt, LANES), I32), jax.ShapeDtypeStruct((SUBLANES, LANES), F32)],
        scratch_shapes=[pltpu.VMEM((SUBLANES, LANES), F32)],
        compiler_params=pltpu.CompilerParams(dimension_semantics=("arbitrary",), vmem_limit_bytes=VMEM_LIMIT),
        name="merge_router",
    )(x_all, attn, pooled, ga, gb, *consts)


def _dispatch_body(dest_ref, gend_ref, h2p_ref, xs_hbm, zero_scr, sem, zsem):
    step = pl.program_id(0)

    @pl.when(step == 0)
    def _():
        zero_scr[...] = jnp.zeros_like(zero_scr)

        def fill(e):
            return pltpu.make_async_copy(zero_scr, xs_hbm.at[pl.ds(pl.multiple_of(gend_ref[e] - TS, TS), TS)], zsem)

        def start(e, _):
            @pl.when(gend_ref[e] > gend_ref[e + N_EXPERTS])
            def _():
                fill(e).start()
            return 0
        lax.fori_loop(0, N_EXPERTS, start, 0)

        def wait(e, _):
            @pl.when(gend_ref[e] > gend_ref[e + N_EXPERTS])
            def _():
                fill(e).wait()
            return 0
        lax.fori_loop(0, N_EXPERTS, wait, 0)

        def tail(j):
            return pltpu.make_async_copy(zero_scr, xs_hbm.at[pl.ds(pl.multiple_of(j * TS, TS), TS)], zsem)

        first_unused = gend_ref[N_EXPERTS - 1] // TS
        n_tiles = xs_hbm.shape[0] // TS

        def tail_start(j, _):
            tail(j).start()
            return 0
        lax.fori_loop(first_unused, n_tiles, tail_start, 0)

        def tail_wait(j, _):
            tail(j).wait()
            return 0
        lax.fori_loop(first_unused, n_tiles, tail_wait, 0)

    t0 = step * TM

    def row_copy(r, k):
        d = dest_ref[(t0 + r) * TOP_EXPERTS + k]
        return pltpu.make_async_copy(h2p_ref.at[pl.ds(r, 1)], xs_hbm.at[pl.ds(d, 1)], sem)

    def start(r, _):
        for k in range(TOP_EXPERTS):
            row_copy(r, k).start()
        return 0
    lax.fori_loop(0, TM, start, 0, unroll=DMA_UNROLL)

    for _ in range(TOP_EXPERTS):
        pltpu.make_async_copy(h2p_ref, xs_hbm.at[pl.ds(0, TM)], sem).wait()


def _dispatch(dest_flat, gend, h2p, *, n_slots):
    nt, hw = h2p.shape
    grid_spec = pltpu.PrefetchScalarGridSpec(
        num_scalar_prefetch=2,
        grid=(nt // TM,),
        in_specs=[pl.BlockSpec((TM, hw), lambda i, d, g: (i, 0))],
        out_specs=pl.BlockSpec(memory_space=pl.ANY),
        scratch_shapes=[pltpu.VMEM((TS, hw), U32), pltpu.SemaphoreType.DMA, pltpu.SemaphoreType.DMA],
    )
    return pl.pallas_call(
        _dispatch_body,
        grid_spec=grid_spec,
        out_shape=jax.ShapeDtypeStruct((n_slots, hw), U32),
        compiler_params=pltpu.CompilerParams(dimension_semantics=("arbitrary",)),
        name="dispatch",
    )(dest_flat, gend, h2p)


def _moe_body(te_ref, nu_ref, gi_ref, nx_ref, xs_ref, bgu_ref, bdn_ref, wgu_hbm, wdn_hbm, ys_o,
              wgu_f, wdn_f, wgu_b, wdn_b, sem, *, layer):
    j = pl.program_id(0)
    expert = te_ref[j]
    live = j < nu_ref[0]
    slot = gi_ref[j] % 2

    def weight_copies(e, s):
        return (pltpu.make_async_copy(wgu_hbm.at[layer, e], wgu_f.at[s], sem.at[0, s]),
                pltpu.make_async_copy(wdn_hbm.at[layer, e], wdn_f.at[s], sem.at[1, s]))

    @pl.when(jnp.logical_and(live, jnp.logical_or(j == 0, expert != te_ref[jnp.maximum(j - 1, 0)])))
    def _():
        @pl.when(j == 0)
        def _():
            for cp in weight_copies(expert, slot):
                cp.start()

        @pl.when(nx_ref[j] >= 0)
        def _():
            for cp in weight_copies(nx_ref[j], 1 - slot):
                cp.start()
        for cp in weight_copies(expert, slot):
            cp.wait()
        wgu_b[...] = wgu_f[slot].astype(BF16)
        wdn_b[...] = wdn_f[slot].astype(BF16)

    @pl.when(live)
    def _():
        words = xs_ref[...]
        x_lo = lax.bitcast_convert_type(lax.shift_left(words, jnp.uint32(16)), F32).astype(BF16)
        x_hi = lax.bitcast_convert_type(words & jnp.uint32(0xFFFF0000), F32).astype(BF16)
        half = words.shape[1]
        gu = _dot(x_lo, wgu_b[0:half, :]) + _dot(x_hi, wgu_b[half:, :]) + bgu_ref[0, 0]
        de = gu.shape[1] // 2
        gate = jnp.minimum(gu[:, 0:de], SWIGLU_LIMIT)
        up = jnp.clip(gu[:, de:], -SWIGLU_LIMIT, SWIGLU_LIMIT)
        act = (up + 1.0) * gate * jax.nn.sigmoid(SWIGLU_ALPHA * gate)
        ys_o[...] = _dot(act.astype(BF16), wdn_b[...]) + bdn_ref[0, 0]

    @pl.when(jnp.logical_not(live))
    def _():
        ys_o[...] = jnp.zeros_like(ys_o)


def _moe(tile_expert, n_used, group_index, next_expert, xs, wgu, bgu, wdn, bdn, *, layer):
    n_slots, hw = xs.shape
    _, ne, d, de2 = wgu.shape
    bias = lambda cols: pl.BlockSpec((1, 1, 1, cols), lambda j, te, nu, gi, nx: (layer, te[j], 0, 0))
    grid_spec = pltpu.PrefetchScalarGridSpec(
        num_scalar_prefetch=4,
        grid=(n_slots // TS,),
        in_specs=[pl.BlockSpec((TS, hw), lambda j, te, nu, gi, nx: (jnp.minimum(j, nu[0] - 1), 0)),
                  bias(de2), bias(d), pl.BlockSpec(memory_space=pl.ANY), pl.BlockSpec(memory_space=pl.ANY)],
        out_specs=pl.BlockSpec((TS, d), lambda j, te, nu, gi, nx: (j, 0)),
        scratch_shapes=[pltpu.VMEM((2, d, de2), F32), pltpu.VMEM((2, de2 // 2, d), F32),
                        pltpu.VMEM((d, de2), BF16), pltpu.VMEM((de2 // 2, d), BF16),
                        pltpu.SemaphoreType.DMA((2, 2))],
    )
    return pl.pallas_call(
        functools.partial(_moe_body, layer=layer),
        grid_spec=grid_spec,
        out_shape=jax.ShapeDtypeStruct((n_slots, d), F32),
        compiler_params=pltpu.CompilerParams(dimension_semantics=("arbitrary",), vmem_limit_bytes=VMEM_LIMIT),
        name="moe_experts",
    )(tile_expert, n_used, group_index, next_expert, xs, bgu[:, :, None, :], bdn[:, :, None, :], wgu, wdn)


def _combine_body(dest_ref, x1_ref, gv_ref, gfin_ref, ys_hbm, y_o, ybuf, sem):
    t0 = pl.program_id(0) * TM

    def row_copy(r, k):
        d = dest_ref[(t0 + r) * TOP_EXPERTS + k]
        return pltpu.make_async_copy(ys_hbm.at[pl.ds(d, 1)], ybuf.at[k, pl.ds(r, 1)], sem)

    def start(r, _):
        for k in range(TOP_EXPERTS):
            row_copy(r, k).start()
        return 0
    lax.fori_loop(0, TM, start, 0, unroll=DMA_UNROLL)

    for k in range(TOP_EXPERTS):
        pltpu.make_async_copy(ys_hbm.at[pl.ds(0, TM)], ybuf.at[k], sem).wait()

    gv = gv_ref[...]
    y = jnp.zeros(x1_ref.shape, F32)
    for k in range(TOP_EXPERTS):
        y = y + ybuf[k] * gv[:, k:k + 1]
    y_o[...] = _rms(x1_ref[...] + y, gfin_ref[...])


def _combine(dest_flat, x1, gv, gfin, ys):
    nt, d = x1.shape
    grid_spec = pltpu.PrefetchScalarGridSpec(
        num_scalar_prefetch=1,
        grid=(nt // TM,),
        in_specs=[pl.BlockSpec((TM, d), lambda i, dref: (i, 0)),
                  pl.BlockSpec((TM, LANES), lambda i, dref: (i, 0)),
                  pl.BlockSpec((1, d), lambda i, dref: (0, 0)),
                  pl.BlockSpec(memory_space=pl.ANY)],
        out_specs=pl.BlockSpec((TM, d), lambda i, dref: (i, 0)),
        scratch_shapes=[pltpu.VMEM((TOP_EXPERTS, TM, d), F32), pltpu.SemaphoreType.DMA],
    )
    return pl.pallas_call(
        _combine_body,
        grid_spec=grid_spec,
        out_shape=jax.ShapeDtypeStruct((nt, d), F32),
        compiler_params=pltpu.CompilerParams(dimension_semantics=("arbitrary",)),
        name="combine_norm",
    )(dest_flat, x1, gv, gfin, ys)


def _tri_constants():
    r = lax.broadcasted_iota(I32, (TQ, TQ), 0)
    c = lax.broadcasted_iota(I32, (TQ, TQ), 1)
    incl = (r <= c).astype(BF16)
    u2 = jnp.concatenate([incl, jnp.ones((TQ, LANES), BF16)], axis=1)
    ltri = (c < r).astype(BF16)
    return u2, ltri


def kernel(x_prompt, x_sample, cache_k, cache_v, cache_kidx, state_pool, page_table, meta_tokens, rel_bias, g_mix, w_in, w_pool_mix, pool_scale, w_br_attn, w_br_pool, w_out, g_ffn, w_router, b_router, w_gate_up, b_gate_up, w_down, b_down, g_final):
    nb, seq, d = x_prompt.shape
    db, dec_seq, _ = x_sample.shape
    assert dec_seq == 1, "one new token per sample"
    depth = w_in.shape[0]
    assert depth == 1, "single-layer stack: the combine kernel applies the final norm"
    n_pages = page_table.shape[1]
    assert n_pages % SUBLANES == 0, "the sample top-k lays the cached keys out as eight equal page segments"
    past = n_pages * PAGE_SIZE
    pw = state_pool.shape[-1]
    l_seq = seq + N_META
    lp = _round_up(l_seq, TQ)
    assert l_seq + db <= lp, "sample rows must fit in the padding of the last prompt sequence"
    nt = nb * lp
    s0 = (nb - 1) * lp + l_seq
    ksel_p = min(TOPK_MAX, seq // 4)
    ksel_s = min(TOPK_MAX, (past + dec_seq) // 4)
    n_tiles = (TOP_EXPERTS * nt) // TS + N_EXPERTS
    n_slots = n_tiles * TS

    meta = meta_tokens.astype(x_prompt.dtype)
    pieces = []
    for bi in range(nb):
        pieces += [meta, x_prompt[bi], jnp.zeros((lp - l_seq, d), x_prompt.dtype)]
    pieces[-1] = jnp.concatenate([x_sample[:, 0, :], jnp.zeros((lp - l_seq - db, d), x_prompt.dtype)], axis=0)
    x_all = jnp.concatenate(pieces, axis=0)

    u2, ltri = _tri_constants()
    page_flat = page_table.reshape(-1).astype(I32)
    rel_bias = rel_bias.astype(F32)
    cache_k_t = jnp.transpose(cache_k, (0, 1, 3, 4, 2))
    cache_v_t = jnp.transpose(cache_v, (0, 1, 3, 4, 2))
    cache_ki_t = jnp.transpose(cache_kidx, (0, 1, 3, 2))

    outs = {name: [] for name in ("k_p", "v_p", "ki_p", "pool_p", "k_s", "v_s", "ki_s", "pool_s")}
    for l in range(depth):
        wl = w_in[l]
        aw = ATTN_WIDTH
        o_ki = 4 * aw
        o_wi = o_ki + IDX_DIM
        o_p = o_wi + N_IDX_HEADS
        o_ga = o_p + pw
        o_gb = o_ga + d
        wa = wl[:, 0:o_ki].astype(BF16)
        wki = wl[:, o_ki:o_wi]
        wki2 = jnp.concatenate([wki, wki], axis=1).astype(BF16)
        wwi = jnp.pad(wl[:, o_wi:o_p], ((0, 0), (0, LANES - N_IDX_HEADS))).astype(BF16)
        wp = wl[:, o_p:o_ga].astype(BF16)
        wga = wl[:, o_ga:o_gb].astype(BF16)
        wgb = wl[:, o_gb:o_gb + d].astype(BF16)

        (qs, kf, vf, kb, vb, qib, ki2b, kif, wif, pf, ga, gb, k_t, v_t, ki_t) = _inproj(
            x_all, g_mix[l][None, :], wa, wki2, wwi, wp, wga, wgb, nb=nb, lp=lp, l_seq=l_seq)

        tail_rows = _round_up(l_seq - (lp - TQ), 2 * SUBLANES)
        attn_p = _attn_prompt(qs, qib, wif, kb, vb, ki2b, u2, rel_bias, nb=nb, lp=lp, ksel=ksel_p,
                              tail_rows=tail_rows)
        pooled_p = _pool_prompt(pf, n_rows=nt, lp=lp)

        sl = slice(s0, s0 + db)
        keys = _sample_keys(page_flat, qib[sl].reshape(db, N_IDX_HEADS, IDX_DIM), kif[sl, 0:IDX_DIM][:, None, :],
                            wif[sl, 0:N_IDX_HEADS][:, :, None], cache_ki_t, layer=l, n_pages=n_pages)
        sel = _sample_select(keys, past=past, ksel=ksel_s)
        cols = lambda a: a[sl].astype(F32).reshape(db, N_HEADS, HEAD_DIM, 1)
        attn_s, pooled_s = _sample_attend(
            page_flat, sel, cols(qs), cols(kf), cols(vf), rel_bias, state_pool[l], pf[sl][:, None, :],
            cache_k_t, cache_v_t, layer=l, n_pages=n_pages)

        attn_all = lax.dynamic_update_slice(attn_p, attn_s.reshape(db, ATTN_WIDTH).astype(BF16), (s0, 0))
        pooled_all = lax.dynamic_update_slice(pooled_p, pooled_s[:, 0, :], (s0, 0))

        wr = jnp.pad(w_router[l], ((0, 0), (0, LANES - N_EXPERTS))).astype(BF16)
        br = jnp.concatenate([b_router[l].astype(F32), jnp.full((LANES - N_EXPERTS,), NEG, F32)])[None, :]
        x1, h2p, ev, gv, rk, cnt = _merge(
            x_all, attn_all, pooled_all, ga, gb, w_br_attn[l].astype(BF16), w_pool_mix[l].astype(BF16),
            pool_scale[l][None, :], w_br_pool[l].astype(BF16), w_out[l].astype(BF16), g_ffn[l][None, :], wr, br, ltri)

        counts = cnt[0, 0:N_EXPERTS].astype(I32)
        padded = (counts + TS - 1) // TS * TS
        gend = jnp.cumsum(padded)
        gstart = gend - padded
        experts = jnp.arange(N_EXPERTS, dtype=I32)
        ev4 = ev[:, 0:TOP_EXPERTS]
        dest = (jnp.sum(jnp.where(ev4[:, :, None] == experts, gstart, 0), axis=2) + rk[:, 0:TOP_EXPERTS]).reshape(-1)
        tile_first = jnp.arange(n_tiles, dtype=I32)[:, None] * TS
        tile_expert = jnp.minimum(jnp.sum((gend[None, :] <= tile_first).astype(I32), axis=1), N_EXPERTS - 1)
        n_used = (gend[-1] // TS).astype(I32)[None]
        nonempty = counts > 0
        rank_e = jnp.cumsum(nonempty.astype(I32)) - nonempty.astype(I32)
        later = jnp.logical_and(nonempty[None, :], experts[None, :] > experts[:, None])
        next_e = jnp.min(jnp.where(later, experts[None, :], N_EXPERTS), axis=1)
        next_e = jnp.where(next_e == N_EXPERTS, -1, next_e)
        tile_onehot = tile_expert[:, None] == experts[None, :]
        group_index = jnp.sum(jnp.where(tile_onehot, rank_e, 0), axis=1).astype(I32)
        next_expert = jnp.sum(jnp.where(tile_onehot, next_e, 0), axis=1).astype(I32)

        xs_sorted = _dispatch(dest, jnp.concatenate([gend, gstart + counts]).astype(I32), h2p, n_slots=n_slots)
        ys = _moe(tile_expert, n_used, group_index, next_expert, xs_sorted, w_gate_up, b_gate_up, w_down, b_down,
                  layer=l)
        y_all = _combine(dest, x1, gv, g_final[None, :], ys)

        outs["k_p"].append(jnp.transpose(k_t, (0, 3, 1, 2)))
        outs["v_p"].append(jnp.transpose(v_t, (0, 3, 1, 2)))
        outs["ki_p"].append(jnp.transpose(ki_t, (0, 2, 1)))
        outs["pool_p"].append(pf.reshape(nb, lp, pw)[:, l_seq - POOL_STATE:l_seq])
        outs["k_s"].append(kf[sl].reshape(db, 1, N_HEADS, HEAD_DIM))
        outs["v_s"].append(vf[sl].reshape(db, 1, N_HEADS, HEAD_DIM))
        outs["ki_s"].append(kif[sl, 0:IDX_DIM].reshape(db, 1, IDX_DIM))
        outs["pool_s"].append(jnp.concatenate([state_pool[l][:, 1:], pf[sl][:, None, :]], axis=1))

    y_prompt = y_all.reshape(nb, lp, d)[:, N_META:l_seq]
    y_sample = y_all[s0:s0 + db].reshape(db, 1, d)
    st = lambda name: jnp.stack(outs[name])
    return (y_prompt, y_sample, st("k_p"), st("v_p"), st("ki_p"), st("pool_p"),
            st("k_s"), st("v_s"), st("ki_s"), st("pool_s"))
```

```python
import functools
import math

import jax
import jax.numpy as jnp
from jax import lax
from jax.experimental import pallas as pl
from jax.experimental.pallas import tpu as pltpu

F32 = jnp.float32
BF16 = jnp.bfloat16
I32 = jnp.int32
U32 = jnp.uint32

N_META = 16
N_HEADS = 8
HEAD_DIM = 64
ATTN_WIDTH = N_HEADS * HEAD_DIM
N_IDX_HEADS = 8
IDX_DIM = 64
TOPK_MAX = 256
N_BUCKETS = 32
MAX_DISTANCE = 128
POOL_WINDOWS = (2, 4, 8, 16)
POOL_GROUPS = 4
POOL_STATE = 15
N_EXPERTS = 32
TOP_EXPERTS = 4
SWIGLU_LIMIT = 7.0
SWIGLU_ALPHA = 1.702
PAGE_SIZE = 128
EPS = 1e-6

LANES = 128
SUBLANES = 8
MXU_DIM = 256
TQ = MXU_DIM
TM = MXU_DIM
TS = 2 * MXU_DIM
VMEM_LIMIT = 56 * 1024 * 1024
DMA_UNROLL = 8

NEG = -1e30
INT_MIN = -(2 ** 31)

_MAX_EXACT = N_BUCKETS // 2
_BUCKET_THRESHOLDS = tuple(
    math.ceil(_MAX_EXACT * (MAX_DISTANCE / _MAX_EXACT) ** (j / (N_BUCKETS - _MAX_EXACT)))
    for j in range(1, N_BUCKETS - _MAX_EXACT))


def _round_up(a, m):
    return (a + m - 1) // m * m


def _rms(x, g):
    return x * lax.rsqrt(jnp.mean(x * x, axis=-1, keepdims=True) + EPS) * g


def _dot(a, b):
    return jnp.dot(a, b, preferred_element_type=F32)


def _dot_nt(a, b):
    return lax.dot_general(a, b, (((1,), (1,)), ((), ())), preferred_element_type=F32)


def _bucket(dist):
    large = jnp.full(dist.shape, _MAX_EXACT, I32)
    for thr in _BUCKET_THRESHOLDS:
        large = large + jnp.where(dist >= thr, 1, 0)
    return jnp.where(dist < _MAX_EXACT, dist, large)


def _sort_key(s):
    s = jnp.where(s == 0.0, 0.0, s)
    bits = lax.bitcast_convert_type(s, I32)
    return jnp.where(bits >= 0, bits, bits ^ jnp.int32(0x7FFFFFFF))


def _full(shape):
    return pl.BlockSpec(shape, lambda *_: (0,) * len(shape))


def _inproj_body(x_ref, g_ref, wa_ref, wki_ref, wwi_ref, wp_ref, wga_ref, wgb_ref,
                 qs_o, kf_o, vf_o, kb_o, vb_o, qib_o, ki2b_o, kif_o, wif_o, pf_o, ga_o, gb_o, kt_o, vt_o, kit_o):
    h = _rms(x_ref[...], g_ref[...]).astype(BF16)
    za = _dot(h, wa_ref[...])
    w = ATTN_WIDTH
    qs_o[...] = (za[:, 0:w] * (HEAD_DIM ** -0.5)).astype(BF16)
    k = za[:, w:2 * w]
    v = za[:, 2 * w:3 * w]
    kf_o[...] = k
    vf_o[...] = v
    kb_o[...] = k.astype(BF16)
    vb_o[...] = v.astype(BF16)
    qib_o[...] = za[:, 3 * w:4 * w].astype(BF16)
    ki2 = _dot(h, wki_ref[...])
    kif_o[...] = ki2
    ki2b_o[...] = ki2.astype(BF16)
    wif_o[...] = _dot(h, wwi_ref[...])
    pf_o[...] = _dot(h, wp_ref[...])
    ga_o[...] = _dot(h, wga_ref[...])
    gb_o[...] = _dot(h, wgb_ref[...])
    kt_o[0] = k.T.reshape(N_HEADS, HEAD_DIM, TM)
    vt_o[0] = v.T.reshape(N_HEADS, HEAD_DIM, TM)
    kit_o[0] = ki2.T[0:IDX_DIM, :]


def _inproj(x_all, g, wa, wki2, wwi, wp, wga, wgb, *, nb, lp, l_seq):
    nt, d = x_all.shape
    pw = wp.shape[1]
    tps = lp // TM
    row = lambda width: pl.BlockSpec((TM, width), lambda i: (i, 0))
    outs = [
        (ATTN_WIDTH, BF16), (ATTN_WIDTH, F32), (ATTN_WIDTH, F32), (ATTN_WIDTH, BF16), (ATTN_WIDTH, BF16),
        (ATTN_WIDTH, BF16), (LANES, BF16), (LANES, F32), (LANES, F32), (pw, F32), (d, F32), (d, F32)]
    seq_minor_specs = [
        pl.BlockSpec((1, N_HEADS, HEAD_DIM, TM), lambda i: (i // tps, 0, 0, i % tps)),
        pl.BlockSpec((1, N_HEADS, HEAD_DIM, TM), lambda i: (i // tps, 0, 0, i % tps)),
        pl.BlockSpec((1, IDX_DIM, TM), lambda i: (i // tps, 0, i % tps))]
    seq_minor_shapes = [
        jax.ShapeDtypeStruct((nb, N_HEADS, HEAD_DIM, l_seq), F32), jax.ShapeDtypeStruct((nb, N_HEADS, HEAD_DIM, l_seq), F32),
        jax.ShapeDtypeStruct((nb, IDX_DIM, l_seq), F32)]
    return pl.pallas_call(
        _inproj_body,
        grid=(nt // TM,),
        in_specs=[row(d), _full((1, d)), _full(wa.shape), _full(wki2.shape), _full(wwi.shape),
                  _full(wp.shape), _full(wga.shape), _full(wgb.shape)],
        out_specs=[row(wd) for wd, _ in outs] + seq_minor_specs,
        out_shape=[jax.ShapeDtypeStruct((nt, wd), dt) for wd, dt in outs] + seq_minor_shapes,
        compiler_params=pltpu.CompilerParams(dimension_semantics=("arbitrary",), vmem_limit_bytes=VMEM_LIMIT),
        name="inproj",
    )(x_all, g, wa, wki2, wwi, wp, wga, wgb)


def _attn_prompt_body(qs_ref, qib_ref, wif_ref, kb_ref, vb_ref, ki2b_ref, u2_ref, rb_ref,
                      o_ref, qis_scr, wb_scr, key_scr, ntab_scr, qz_scr, m_scr, acc_scr, sum_scr, *, ksel, tail_rows):
    b = pl.program_id(0)
    i = pl.program_id(1)
    nchunk = i + 1
    n_far = jnp.maximum(i - 1, 0)

    @pl.when((b == 0) & (i == 0))
    def _():
        def slab_rows(s, _):
            r0 = pl.multiple_of(s * SUBLANES, SUBLANES)
            r = r0 + lax.broadcasted_iota(I32, (SUBLANES, 2 * TQ), 0)
            x = lax.broadcasted_iota(I32, (SUBLANES, 2 * TQ), 1)
            bucket = _bucket(jnp.maximum(r + TQ - x, 0))
            accs = [jnp.zeros((SUBLANES, 2 * TQ), F32) for _ in range(N_HEADS)]
            for j in range(N_BUCKETS - 1):
                m = bucket == j
                accs = [jnp.where(m, rb_ref[j, h] - rb_ref[N_BUCKETS - 1, h], accs[h]) for h in range(N_HEADS)]
            for h in range(N_HEADS):
                ntab_scr[h, 0, pl.ds(r0, SUBLANES), :] = accs[h][:, 0:TQ]
                ntab_scr[h, 1, pl.ds(r0, SUBLANES), :] = accs[h][:, TQ:2 * TQ]
            return 0
        lax.fori_loop(0, TQ // SUBLANES, slab_rows, 0)

    def two(x):
        return jnp.concatenate([x, x], axis=1)

    def block(nr):
        hr = nr // 2
        lane = lax.broadcasted_iota(I32, (nr, LANES), 1)
        lo_half = lane < HEAD_DIM

        for h in range(N_HEADS):
            cols = slice((h // 2) * LANES, (h // 2 + 1) * LANES)
            keep = lo_half if h % 2 == 0 else jnp.logical_not(lo_half)
            qis_scr[h, 0:nr] = jnp.where(keep, qib_ref[0:nr, cols].astype(F32), 0.0).astype(BF16)
            qz_scr[h // 2, (h % 2) * nr:(h % 2 + 1) * nr] = jnp.where(keep, qs_ref[0:nr, cols].astype(F32), 0.0).astype(BF16)
            wb_scr[h, 0:nr] = jnp.broadcast_to(wif_ref[0:nr, h:h + 1], (nr, LANES))
            m_scr[h, 0:nr] = jnp.full((nr, LANES), NEG, F32)
            acc_scr[h, 0:nr] = jnp.zeros((nr, LANES), F32)
            sum_scr[h, 0:nr] = jnp.zeros((nr, LANES), F32)

        dmat = lax.broadcasted_iota(I32, (nr, TQ), 1) - lax.broadcasted_iota(I32, (nr, TQ), 0)

        def score_chunk(c, _):
            kc = ki2b_ref[pl.ds(pl.multiple_of(c * TQ, TQ), TQ), :]
            s = jnp.zeros((nr, TQ), F32)
            for h in range(N_IDX_HEADS):
                s = s + two(wb_scr[h, 0:nr]) * jnp.maximum(_dot_nt(qis_scr[h, 0:nr], kc), 0.0)
            key_scr[c, 0:nr] = jnp.where(dmat <= (i - c) * TQ, _sort_key(s), INT_MIN)
            return 0
        lax.fori_loop(0, nchunk, score_chunk, 0)
        key_scr[nchunk, 0:nr] = jnp.full((nr, TQ), INT_MIN, I32)

        def count(r0, pred):
            def body(j, acc):
                for c in (2 * j, 2 * j + 1):
                    hit = jnp.where(pred(key_scr[c, r0:r0 + hr, :]), 1.0, 0.0)
                    acc = acc + (hit[:, 0:LANES] + hit[:, LANES:2 * LANES])
                return acc
            acc = lax.fori_loop(0, (nchunk + 1) // 2, body, jnp.zeros((hr, LANES), F32))
            return jnp.broadcast_to(jnp.sum(acc, axis=1, keepdims=True), (hr, LANES))

        ts, exact = [], []
        for r0 in (0, hr):
            def bit_step(step, t, r0=r0):
                cand = t + lax.shift_left(jnp.int32(1), 31 - step)
                cand2 = two(cand)
                return jnp.where(count(r0, lambda k: k >= cand2) >= ksel, cand, t)
            t = lax.fori_loop(0, 32, bit_step, jnp.full((hr, LANES), INT_MIN, I32))
            th2 = two(t)
            ts.append(t)
            n_ge = count(r0, lambda k: k >= th2)
            exact.append(jnp.where(jnp.logical_and(n_ge == ksel, t != INT_MIN), 1.0, 0.0))
        t2 = two(jnp.concatenate(ts, axis=0))
        no_ties = jnp.min(jnp.concatenate(exact, axis=0)) > 0.5

        @pl.when(no_ties)
        def _():
            def mask_chunk(c, _):
                madd = jnp.where(key_scr[c, 0:nr] >= t2, 0.0, NEG)
                key_scr[c, 0:nr] = lax.bitcast_convert_type(madd, I32)
                return 0
            lax.fori_loop(0, nchunk, mask_chunk, 0)

        @pl.when(jnp.logical_not(no_ties))
        def _():
            needs = []
            for g, r0 in enumerate((0, hr)):
                th2 = two(ts[g])
                n_gt = count(r0, lambda k: k > th2)
                needs.append(jnp.where(ts[g] == INT_MIN, 0.0, ksel - n_gt))
            need2 = two(jnp.concatenate(needs, axis=0))

            def mask_chunk(c, carry):
                kc = key_scr[c, 0:nr]
                eq = kc == t2
                pre = _dot(jnp.where(eq, 1.0, 0.0).astype(BF16), u2_ref[...])
                prefix = pre[:, 0:TQ] + two(carry)
                tie_ok = jnp.where(eq, prefix, 3e38) <= need2
                madd = jnp.where(kc > t2, 0.0, jnp.where(tie_ok, 0.0, NEG))
                key_scr[c, 0:nr] = lax.bitcast_convert_type(madd, I32)
                return carry + pre[:, TQ:TQ + LANES]
            lax.fori_loop(0, nchunk, mask_chunk, jnp.zeros((nr, LANES), F32))

        def chunk_logits(c, slab):
            rows = pl.ds(pl.multiple_of(c * TQ, TQ), TQ)
            madd = lax.bitcast_convert_type(key_scr[c, 0:nr], F32)
            out = []
            for hp in range(N_HEADS // 2):
                kc = kb_ref[rows, hp * LANES:(hp + 1) * LANES]
                pair = _dot_nt(qz_scr[hp, 0:2 * nr], kc)
                for sub in range(2):
                    l = pair[sub * nr:(sub + 1) * nr] + madd
                    out.append(l if slab is None else l + ntab_scr[2 * hp + sub, slab, 0:nr])
            return rows, out

        def accumulate(c, slab):
            rows, ls = chunk_logits(c, slab)
            for hp in range(N_HEADS // 2):
                ps, alphas = [], []
                for h in (2 * hp, 2 * hp + 1):
                    l = ls[h]
                    m_old = m_scr[h, 0:nr]
                    m_chunk = jnp.max(jnp.maximum(l[:, 0:LANES], l[:, LANES:2 * LANES]), axis=1, keepdims=True)
                    m_new = jnp.maximum(m_old, jnp.broadcast_to(m_chunk, (nr, LANES)))
                    alpha = jnp.exp(m_old - m_new)
                    p = jnp.exp(l - two(m_new))
                    m_scr[h, 0:nr] = m_new
                    sum_scr[h, 0:nr] = alpha * sum_scr[h, 0:nr] + (p[:, 0:LANES] + p[:, LANES:2 * LANES])
                    ps.append(p.astype(BF16))
                    alphas.append(alpha)
                pv = _dot(jnp.concatenate(ps, axis=0), vb_ref[rows, hp * LANES:(hp + 1) * LANES])
                for sub in range(2):
                    h = 2 * hp + sub
                    acc_scr[h, 0:nr] = alphas[sub] * acc_scr[h, 0:nr] + pv[sub * nr:(sub + 1) * nr]

        def far(c, _):
            accumulate(c, None)
            return 0
        lax.fori_loop(0, n_far, far, 0)

        @pl.when(i >= 1)
        def _():
            accumulate(i - 1, 0)
        accumulate(i, 1)
        for hp in range(N_HEADS // 2):
            outs = [acc_scr[h, 0:nr] / jnp.sum(sum_scr[h, 0:nr], axis=1, keepdims=True) for h in (2 * hp, 2 * hp + 1)]
            o_ref[0:nr, hp * LANES:(hp + 1) * LANES] = jnp.where(lo_half, outs[0], outs[1]).astype(BF16)
        if nr < TQ:
            o_ref[nr:TQ, :] = jnp.zeros((TQ - nr, ATTN_WIDTH), BF16)

    if tail_rows == TQ:
        block(TQ)
    else:
        last = pl.num_programs(1) - 1
        pl.when(i < last)(lambda: block(TQ))
        pl.when(i == last)(lambda: block(tail_rows))


def _attn_prompt(qs, qib, wif, kb, vb, ki2b, u2, rel_bias, *, nb, lp, ksel, tail_rows):
    nq = lp // TQ
    rowq = lambda width: pl.BlockSpec((TQ, width), lambda b, i: (b * nq + i, 0))
    seq = lambda width: pl.BlockSpec((lp, width), lambda b, i: (b, 0))
    per_head = lambda dt: pltpu.VMEM((N_HEADS, TQ, LANES), dt)
    return pl.pallas_call(
        functools.partial(_attn_prompt_body, ksel=float(ksel), tail_rows=tail_rows),
        grid=(nb, nq),
        in_specs=[rowq(ATTN_WIDTH), rowq(ATTN_WIDTH), rowq(LANES), seq(ATTN_WIDTH), seq(ATTN_WIDTH), seq(LANES),
                  _full(u2.shape), pl.BlockSpec(memory_space=pltpu.SMEM)],
        out_specs=rowq(ATTN_WIDTH),
        out_shape=jax.ShapeDtypeStruct((nb * lp, ATTN_WIDTH), BF16),
        scratch_shapes=[
            per_head(BF16),
            per_head(F32),
            pltpu.VMEM((nq + 1, TQ, TQ), I32),
            pltpu.VMEM((N_HEADS, 2, TQ, TQ), F32),
            pltpu.VMEM((N_HEADS // 2, 2 * TQ, LANES), BF16),
            per_head(F32), per_head(F32), per_head(F32),
        ],
        compiler_params=pltpu.CompilerParams(dimension_semantics=("arbitrary", "arbitrary"),
                                             vmem_limit_bytes=VMEM_LIMIT),
        name="attn_prompt",
    )(qs, qib, wif, kb, vb, ki2b, u2, rel_bias)


def _pool_prompt_body(p_ref, halo_ref, o_ref, ext_scr, *, tiles_per_seq):
    j = pl.program_id(0) % tiles_per_seq
    halo = 2 * SUBLANES
    p = p_ref[...]
    ext_scr[0:halo, :] = jnp.where(j == 0, 0.0, halo_ref[...])
    ext_scr[halo:halo + TM, :] = p
    pos = j * TM + lax.broadcasted_iota(I32, (TM, 1), 0)
    gw = p.shape[1] // POOL_GROUPS
    for g, w in enumerate(POOL_WINDOWS):
        cols = slice(g * gw, (g + 1) * gw)
        s = p[:, cols]
        for back in range(1, w):
            s = s + ext_scr[halo - back:halo - back + TM, cols]
        cnt = jnp.minimum(pos + 1, w).astype(F32)
        o_ref[:, cols] = (s / cnt - p[:, cols]).astype(BF16)


def _pool_prompt(pf, *, n_rows, lp):
    pw = pf.shape[1]
    halo = 2 * SUBLANES
    return pl.pallas_call(
        functools.partial(_pool_prompt_body, tiles_per_seq=lp // TM),
        grid=(n_rows // TM,),
        in_specs=[pl.BlockSpec((TM, pw), lambda i: (i, 0)),
                  pl.BlockSpec((halo, pw), lambda i: (jnp.maximum(i * (TM // halo) - 1, 0), 0))],
        out_specs=pl.BlockSpec((TM, pw), lambda i: (i, 0)),
        out_shape=jax.ShapeDtypeStruct((n_rows, pw), BF16),
        scratch_shapes=[pltpu.VMEM((halo + TM, pw), F32)],
        compiler_params=pltpu.CompilerParams(dimension_semantics=("arbitrary",)),
        name="pool_prompt",
    )(pf, pf)


def _sample_keys_body(pt_ref, qi_ref, kinew_ref, wi_ref, cache_hbm, key_o, kibuf, sem, *, layer, n_pages):
    s = pl.program_id(0)
    past = n_pages * PAGE_SIZE
    seg_w = past // SUBLANES

    copies = [pltpu.make_async_copy(cache_hbm.at[layer, pt_ref[s * n_pages + pg]],
                                    kibuf.at[:, pg * PAGE_SIZE:(pg + 1) * PAGE_SIZE], sem) for pg in range(n_pages)]
    for cp in copies:
        cp.start()
    for cp in copies:
        cp.wait()

    qi = qi_ref[0]
    wcol = wi_ref[0]
    d = _dot(qi, kibuf[...].astype(BF16))
    sc = jnp.sum(wcol * jnp.maximum(d, 0.0), axis=0, keepdims=True)
    knew = kinew_ref[0].astype(BF16).astype(F32)
    dn = jnp.sum(qi.astype(F32) * knew, axis=1, keepdims=True)
    sn = jnp.sum(wcol * jnp.maximum(dn, 0.0), axis=0, keepdims=True)
    key = _sort_key(sc)
    lane = lax.broadcasted_iota(I32, (1, LANES), 1)
    new_tail = jnp.where(lane == 0, _sort_key(jnp.broadcast_to(sn, (1, LANES))), INT_MIN)
    for g in range(SUBLANES):
        key_o[0, g:g + 1, 0:seg_w] = key[:, g * seg_w:(g + 1) * seg_w]
        key_o[0, g:g + 1, seg_w:seg_w + LANES] = new_tail if g == 0 else jnp.full((1, LANES), INT_MIN, I32)


def _sample_keys(page_table_flat, qi3, kinew3, wi3, cache_ki_t, *, layer, n_pages):
    db = qi3.shape[0]
    past = n_pages * PAGE_SIZE
    kw = past // SUBLANES + LANES
    grid_spec = pltpu.PrefetchScalarGridSpec(
        num_scalar_prefetch=1,
        grid=(db,),
        in_specs=[pl.BlockSpec((1, N_IDX_HEADS, IDX_DIM), lambda s, pt: (s, 0, 0)),
                  pl.BlockSpec((1, 1, IDX_DIM), lambda s, pt: (s, 0, 0)),
                  pl.BlockSpec((1, N_IDX_HEADS, 1), lambda s, pt: (s, 0, 0)),
                  pl.BlockSpec(memory_space=pl.ANY)],
        out_specs=pl.BlockSpec((1, SUBLANES, kw), lambda s, pt: (s, 0, 0)),
        scratch_shapes=[pltpu.VMEM((IDX_DIM, past), F32), pltpu.SemaphoreType.DMA],
    )
    return pl.pallas_call(
        functools.partial(_sample_keys_body, layer=layer, n_pages=n_pages),
        grid_spec=grid_spec,
        out_shape=jax.ShapeDtypeStruct((db, SUBLANES, kw), I32),
        compiler_params=pltpu.CompilerParams(dimension_semantics=("arbitrary",)),
        name="sample_keys",
    )(page_table_flat, qi3, kinew3, wi3, cache_ki_t)


def _sample_select_body(key_ref, sel_o, *, past, ksel):
    keys = key_ref[...]
    kw = keys.shape[2]
    seg_w = kw - LANES
    row = lax.broadcasted_iota(I32, (SUBLANES, kw), 0)
    lane = lax.broadcasted_iota(I32, (SUBLANES, kw), 1)
    pos = jnp.where(lane < seg_w, row * seg_w + lane, past + (lane - seg_w) + row * LANES)[None]

    def cnt(pred):
        x = jnp.where(pred, 1.0, 0.0)
        return jnp.sum(jnp.sum(x, axis=2, keepdims=True), axis=1, keepdims=True)

    def bit_step(step, t):
        cand = t + lax.shift_left(jnp.int32(1), 31 - step)
        return jnp.where(cnt(keys >= cand) >= ksel, cand, t)
    t = lax.fori_loop(0, 32, bit_step, jnp.full((keys.shape[0], 1, 1), INT_MIN, I32))
    gt = keys > t
    eq = keys == t
    need = ksel - cnt(gt)

    nbits = (past + SUBLANES * LANES).bit_length()

    def cut_step(step, c):
        cand = c - lax.shift_left(jnp.int32(1), nbits - 1 - step)
        ok = cnt(jnp.logical_and(eq, pos <= cand)) >= need
        return jnp.where(ok, cand, c)
    cut = lax.fori_loop(0, nbits, cut_step, jnp.full((keys.shape[0], 1, 1), 2 ** nbits - 1, I32))
    sel = jnp.where(gt, 1.0, jnp.where(jnp.logical_and(eq, pos <= cut), 1.0, 0.0))
    sel_o[...] = jnp.where(pos <= past, sel, 0.0)


def _sample_select(keys, *, past, ksel):
    return pl.pallas_call(
        functools.partial(_sample_select_body, past=past, ksel=float(ksel)),
        out_shape=jax.ShapeDtypeStruct(keys.shape, F32),
        name="sample_select",
    )(keys)


def _sample_attend_body(pt_ref, sel_ref, q_ref, knew_ref, vnew_ref, rb_ref, state_ref, pnew_ref, ck_hbm, cv_hbm,
                        attn_o, pool_o, kbuf, vbuf, qb_scr, lg_scr, acc_scr, ksem, vsem, *, layer, n_pages):
    s = pl.program_id(0)
    past = n_pages * PAGE_SIZE
    seg_w = past // SUBLANES
    pages_per_seg = n_pages // SUBLANES
    last_bias = [rb_ref[N_BUCKETS - 1, h] for h in range(N_HEADS)]

    def kcopy(pg, sample=s):
        return pltpu.make_async_copy(ck_hbm.at[layer, pt_ref[sample * n_pages + pg]], kbuf.at[pg], ksem.at[pg])

    def vcopy(pg, sample=s):
        return pltpu.make_async_copy(cv_hbm.at[layer, pt_ref[sample * n_pages + pg]], vbuf.at[pg], vsem.at[pg])

    def request(copy, sample):
        def start(pg, _):
            copy(pg, sample).start()
            return 0
        lax.fori_loop(0, n_pages, start, 0)

    has_next = s + 1 < pl.num_programs(0)

    @pl.when(s == 0)
    def _():
        request(kcopy, s)
        request(vcopy, s)

    for h in range(N_HEADS):
        qb_scr[h] = jnp.broadcast_to(q_ref[0, h], (HEAD_DIM, LANES))
        acc_scr[h] = jnp.zeros((HEAD_DIM, LANES), F32)

    def k_page(pg, _):
        kcopy(pg).wait()
        rows = [jnp.sum(kbuf[pg, h].astype(BF16).astype(F32) * qb_scr[h], axis=0, keepdims=True)
                for h in range(N_HEADS)]
        lg_scr[pg] = jnp.concatenate(rows, axis=0)
        return 0
    lax.fori_loop(0, n_pages, k_page, 0)
    pl.when(has_next)(lambda: request(kcopy, s + 1))

    lane = lax.broadcasted_iota(I32, (1, LANES), 1)
    bucket = _bucket(past - ((n_pages - 1) * PAGE_SIZE + lane))
    near_rows = []
    for h in range(N_HEADS):
        r = jnp.zeros((1, LANES), F32)
        for j in range(N_BUCKETS - 1):
            r = jnp.where(bucket == j, rb_ref[j, h] - last_bias[h], r)
        near_rows.append(r)
    near = jnp.concatenate(near_rows, axis=0)

    sel = sel_ref[0]
    m = jnp.full((N_HEADS, LANES), NEG, F32)
    for pg in range(n_pages):
        g, j = divmod(pg, pages_per_seg)
        keep = sel[g:g + 1, j * PAGE_SIZE:(j + 1) * PAGE_SIZE] > 0.0
        l = lg_scr[pg] + jnp.where(keep, 0.0, NEG)
        if pg == n_pages - 1:
            l = l + near
        lg_scr[pg] = l
        m = jnp.maximum(m, l)
    new_rows = []
    for h in range(N_HEADS):
        kn = knew_ref[0, h].astype(BF16).astype(F32)
        new_rows.append(jnp.sum(q_ref[0, h] * kn, axis=0, keepdims=True) + (rb_ref[0, h] - last_bias[h]))
    l_new = jnp.where(sel[0:1, seg_w:seg_w + 1] > 0.0, jnp.concatenate(new_rows, axis=0), NEG)
    m = jnp.maximum(jnp.max(m, axis=1, keepdims=True), l_new)
    ssum = jnp.zeros((N_HEADS, LANES), F32)
    for pg in range(n_pages):
        p = jnp.exp(lg_scr[pg] - m)
        lg_scr[pg] = p
        ssum = ssum + p
    p_new = jnp.exp(l_new - m)
    den = jnp.sum(ssum, axis=1, keepdims=True) + p_new
    for pg in range(n_pages):
        lg_scr[pg] = (lg_scr[pg] / den).astype(BF16).astype(F32)
    p_new = (p_new / den).astype(BF16).astype(F32)

    def v_page(pg, _):
        vcopy(pg).wait()
        p = lg_scr[pg]
        for h in range(N_HEADS):
            acc_scr[h] = acc_scr[h] + vbuf[pg, h].astype(BF16).astype(F32) * p[h:h + 1, :]
        return 0
    lax.fori_loop(0, n_pages, v_page, 0)
    pl.when(has_next)(lambda: request(vcopy, s + 1))
    for h in range(N_HEADS):
        vn = vnew_ref[0, h].astype(BF16).astype(F32)
        attn_o[0, h] = jnp.sum(acc_scr[h], axis=1, keepdims=True) + p_new[h:h + 1, :] * vn

    pnew = pnew_ref[0]
    st = state_ref[0]
    gw = pnew.shape[1] // POOL_GROUPS
    for g, w in enumerate(POOL_WINDOWS):
        cols = slice(g * gw, (g + 1) * gw)
        acc = pnew[:, cols]
        for back in range(1, w):
            acc = acc + st[POOL_STATE - back:POOL_STATE - back + 1, cols]
        pool_o[0, :, cols] = (acc / float(min(POOL_STATE + 1, w)) - pnew[:, cols]).astype(BF16)


def _sample_attend(page_table_flat, sel, q4, knew4, vnew4, rel_bias, state, pnew3, cache_k_t, cache_v_t, *, layer, n_pages):
    db = q4.shape[0]
    pw = pnew3.shape[2]
    per = lambda shape: pl.BlockSpec((1,) + shape, lambda s, pt: (s,) + (0,) * len(shape))
    col = (N_HEADS, HEAD_DIM, 1)
    page = (n_pages, N_HEADS, HEAD_DIM, LANES)
    grid_spec = pltpu.PrefetchScalarGridSpec(
        num_scalar_prefetch=1,
        grid=(db,),
        in_specs=[per(sel.shape[1:]), per(col), per(col), per(col), pl.BlockSpec(memory_space=pltpu.SMEM),
                  per((POOL_STATE, pw)), per((1, pw)),
                  pl.BlockSpec(memory_space=pl.ANY), pl.BlockSpec(memory_space=pl.ANY)],
        out_specs=[per(col), per((1, pw))],
        scratch_shapes=[pltpu.VMEM(page, F32), pltpu.VMEM(page, F32),
                        pltpu.VMEM((N_HEADS, HEAD_DIM, LANES), F32),
                        pltpu.VMEM((n_pages, N_HEADS, LANES), F32),
                        pltpu.VMEM((N_HEADS, HEAD_DIM, LANES), F32),
                        pltpu.SemaphoreType.DMA((n_pages,)), pltpu.SemaphoreType.DMA((n_pages,))],
    )
    return pl.pallas_call(
        functools.partial(_sample_attend_body, layer=layer, n_pages=n_pages),
        grid_spec=grid_spec,
        out_shape=[jax.ShapeDtypeStruct((db,) + col, F32), jax.ShapeDtypeStruct((db, 1, pw), BF16)],
        compiler_params=pltpu.CompilerParams(dimension_semantics=("arbitrary",), vmem_limit_bytes=VMEM_LIMIT),
        name="sample_attend",
    )(page_table_flat, sel, q4, knew4, vnew4, rel_bias, state, pnew3, cache_k_t, cache_v_t)


def _merge_body(x_ref, attn_ref, pool_ref, ga_ref, gb_ref, wba_ref, wpm_ref, ps_ref, wbp_ref, wout_ref,
                gffn_ref, wr_ref, br_ref, ltri_ref,
                x1_o, h2p_o, ev_o, gv_o, rk_o, cnt_o, carry_scr):
    step = pl.program_id(0)

    @pl.when(step == 0)
    def _():
        carry_scr[...] = jnp.zeros_like(carry_scr)

    a = _dot(attn_ref[...], wba_ref[...])
    pooled = pool_ref[...]
    gw = pooled.shape[1] // POOL_GROUPS
    pm = jnp.concatenate([_dot(pooled[:, g * gw:(g + 1) * gw], wpm_ref[g]) for g in range(POOL_GROUPS)], axis=1)
    bb = _dot((pm * ps_ref[...]).astype(BF16), wbp_ref[...])
    mix = jax.nn.sigmoid(ga_ref[...]) * a + jax.nn.sigmoid(gb_ref[...]) * bb
    x1 = x_ref[...] + _dot(mix.astype(BF16), wout_ref[...])
    x1_o[...] = x1
    h2 = _rms(x1, gffn_ref[...]).astype(BF16)
    half = h2.shape[1] // 2
    lo = lax.shift_right_logical(lax.bitcast_convert_type(h2[:, 0:half].astype(F32), U32), jnp.uint32(16))
    hi = lax.bitcast_convert_type(h2[:, half:].astype(F32), U32) & jnp.uint32(0xFFFF0000)
    h2p_o[...] = hi | lo

    logits = _dot(h2, wr_ref[...]) + br_ref[...]
    lane = lax.broadcasted_iota(I32, (TM, LANES), 1)
    lanef = lane.astype(F32)
    tops, ids = [], []
    l = logits
    for _ in range(TOP_EXPERTS):
        mx = jnp.max(l, axis=1, keepdims=True)
        ix = jnp.min(jnp.where(l == mx, lanef, float(LANES)), axis=1, keepdims=True)
        tops.append(mx)
        ids.append(ix)
        l = jnp.where(lanef == ix, -3e38, l)
    es = [jnp.exp(tv - tops[0]) for tv in tops]
    den = es[0] + es[1] + es[2] + es[3]
    onehot = jnp.zeros((TM, LANES), F32)
    for ix in ids:
        onehot = onehot + jnp.where(lanef == ix, 1.0, 0.0)
    before = _dot(ltri_ref[...], onehot.astype(BF16)) + carry_scr[0:1, :]
    ev = jnp.zeros((TM, LANES), I32)
    gv = jnp.zeros((TM, LANES), F32)
    rk = jnp.zeros((TM, LANES), I32)
    for k in range(TOP_EXPERTS):
        rank = jnp.sum(jnp.where(lanef == ids[k], before, 0.0), axis=1, keepdims=True)
        ev = jnp.where(lane == k, ids[k].astype(I32), ev)
        gv = jnp.where(lane == k, es[k] / den, gv)
        rk = jnp.where(lane == k, rank.astype(I32), rk)
    ev_o[...] = ev
    gv_o[...] = gv
    rk_o[...] = rk
    carry_scr[...] = carry_scr[...] + jnp.sum(onehot, axis=0, keepdims=True)
    cnt_o[...] = carry_scr[...]


def _merge(x_all, attn, pooled, ga, gb, wba, wpm, ps, wbp, wout, gffn, wr, br, ltri):
    nt, d = x_all.shape
    row = lambda width: pl.BlockSpec((TM, width), lambda i: (i, 0))
    consts = [wba, wpm, ps, wbp, wout, gffn, wr, br, ltri]
    return pl.pallas_call(
        _merge_body,
        grid=(nt // TM,),
        in_specs=[row(d), row(attn.shape[1]), row(pooled.shape[1]), row(d), row(d)] + [_full(c.shape) for c in consts],
        out_specs=[row(d), row(d // 2), row(LANES), row(LANES), row(LANES), _full((SUBLANES, LANES))],
        out_shape=[jax.ShapeDtypeStruct((nt, d), F32), jax.ShapeDtypeStruct((nt, d // 2), U32),
                   jax.ShapeDtypeStruct((nt, LANES), I32), jax.ShapeDtypeStruct((nt, LANES), F32),
                   jax.ShapeDtypeStruct((nt, LANES), I32), jax.ShapeDtypeStruct((SUBLANES, LANES), F32)],
        scratch_shapes=[pltpu.VMEM((SUBLANES, LANES), F32)],
        compiler_params=pltpu.CompilerParams(dimension_semantics=("arbitrary",), vmem_limit_bytes=VMEM_LIMIT),
        name="merge_router",
    )(x_all, attn, pooled, ga, gb, *consts)


def _dispatch_body(dest_ref, gend_ref, h2p_ref, xs_hbm, zero_scr, sem, zsem):
    step = pl.program_id(0)

    @pl.when(step == 0)
    def _():
        zero_scr[...] = jnp.zeros_like(zero_scr)

        def fill(e):
            return pltpu.make_async_copy(zero_scr, xs_hbm.at[pl.ds(pl.multiple_of(gend_ref[e] - TS, TS), TS)], zsem)

        def start(e, _):
            @pl.when(gend_ref[e] > gend_ref[e + N_EXPERTS])
            def _():
                fill(e).start()
            return 0
        lax.fori_loop(0, N_EXPERTS, start, 0)

        def wait(e, _):
            @pl.when(gend_ref[e] > gend_ref[e + N_EXPERTS])
            def _():
                fill(e).wait()
            return 0
        lax.fori_loop(0, N_EXPERTS, wait, 0)

        def tail(j):
            return pltpu.make_async_copy(zero_scr, xs_hbm.at[pl.ds(pl.multiple_of(j * TS, TS), TS)], zsem)

        first_unused = gend_ref[N_EXPERTS - 1] // TS
        n_tiles = xs_hbm.shape[0] // TS

        def tail_start(j, _):
            tail(j).start()
            return 0
        lax.fori_loop(first_unused, n_tiles, tail_start, 0)

        def tail_wait(j, _):
            tail(j).wait()
            return 0
        lax.fori_loop(first_unused, n_tiles, tail_wait, 0)

    t0 = step * TM

    def row_copy(r, k):
        d = dest_ref[(t0 + r) * TOP_EXPERTS + k]
        return pltpu.make_async_copy(h2p_ref.at[pl.ds(r, 1)], xs_hbm.at[pl.ds(d, 1)], sem)

    def start(r, _):
        for k in range(TOP_EXPERTS):
            row_copy(r, k).start()
        return 0
    lax.fori_loop(0, TM, start, 0, unroll=DMA_UNROLL)

    for _ in range(TOP_EXPERTS):
        pltpu.make_async_copy(h2p_ref, xs_hbm.at[pl.ds(0, TM)], sem).wait()


def _dispatch(dest_flat, gend, h2p, *, n_slots):
    nt, hw = h2p.shape
    grid_spec = pltpu.PrefetchScalarGridSpec(
        num_scalar_prefetch=2,
        grid=(nt // TM,),
        in_specs=[pl.BlockSpec((TM, hw), lambda i, d, g: (i, 0))],
        out_specs=pl.BlockSpec(memory_space=pl.ANY),
        scratch_shapes=[pltpu.VMEM((TS, hw), U32), pltpu.SemaphoreType.DMA, pltpu.SemaphoreType.DMA],
    )
    return pl.pallas_call(
        _dispatch_body,
        grid_spec=grid_spec,
        out_shape=jax.ShapeDtypeStruct((n_slots, hw), U32),
        compiler_params=pltpu.CompilerParams(dimension_semantics=("arbitrary",)),
        name="dispatch",
    )(dest_flat, gend, h2p)


def _moe_body(te_ref, nu_ref, gi_ref, nx_ref, xs_ref, bgu_ref, bdn_ref, wgu_hbm, wdn_hbm, ys_o,
              wgu_f, wdn_f, wgu_b, wdn_b, sem, *, layer):
    j = pl.program_id(0)
    expert = te_ref[j]
    live = j < nu_ref[0]
    slot = gi_ref[j] % 2

    def weight_copies(e, s):
        return (pltpu.make_async_copy(wgu_hbm.at[layer, e], wgu_f.at[s], sem.at[0, s]),
                pltpu.make_async_copy(wdn_hbm.at[layer, e], wdn_f.at[s], sem.at[1, s]))

    @pl.when(jnp.logical_and(live, jnp.logical_or(j == 0, expert != te_ref[jnp.maximum(j - 1, 0)])))
    def _():
        @pl.when(j == 0)
        def _():
            for cp in weight_copies(expert, slot):
                cp.start()

        @pl.when(nx_ref[j] >= 0)
        def _():
            for cp in weight_copies(nx_ref[j], 1 - slot):
                cp.start()
        for cp in weight_copies(expert, slot):
            cp.wait()
        wgu_b[...] = wgu_f[slot].astype(BF16)
        wdn_b[...] = wdn_f[slot].astype(BF16)

    @pl.when(live)
    def _():
        words = xs_ref[...]
        x_lo = lax.bitcast_convert_type(lax.shift_left(words, jnp.uint32(16)), F32).astype(BF16)
        x_hi = lax.bitcast_convert_type(words & jnp.uint32(0xFFFF0000), F32).astype(BF16)
        half = words.shape[1]
        gu = _dot(x_lo, wgu_b[0:half, :]) + _dot(x_hi, wgu_b[half:, :]) + bgu_ref[0, 0]
        de = gu.shape[1] // 2
        gate = jnp.minimum(gu[:, 0:de], SWIGLU_LIMIT)
        up = jnp.clip(gu[:, de:], -SWIGLU_LIMIT, SWIGLU_LIMIT)
        act = (up + 1.0) * gate * jax.nn.sigmoid(SWIGLU_ALPHA * gate)
        ys_o[...] = _dot(act.astype(BF16), wdn_b[...]) + bdn_ref[0, 0]

    @pl.when(jnp.logical_not(live))
    def _():
        ys_o[...] = jnp.zeros_like(ys_o)


def _moe(tile_expert, n_used, group_index, next_expert, xs, wgu, bgu, wdn, bdn, *, layer):
    n_slots, hw = xs.shape
    _, ne, d, de2 = wgu.shape
    bias = lambda cols: pl.BlockSpec((1, 1, 1, cols), lambda j, te, nu, gi, nx: (layer, te[j], 0, 0))
    grid_spec = pltpu.PrefetchScalarGridSpec(
        num_scalar_prefetch=4,
        grid=(n_slots // TS,),
        in_specs=[pl.BlockSpec((TS, hw), lambda j, te, nu, gi, nx: (jnp.minimum(j, nu[0] - 1), 0)),
                  bias(de2), bias(d), pl.BlockSpec(memory_space=pl.ANY), pl.BlockSpec(memory_space=pl.ANY)],
        out_specs=pl.BlockSpec((TS, d), lambda j, te, nu, gi, nx: (j, 0)),
        scratch_shapes=[pltpu.VMEM((2, d, de2), F32), pltpu.VMEM((2, de2 // 2, d), F32),
                        pltpu.VMEM((d, de2), BF16), pltpu.VMEM((de2 // 2, d), BF16),
                        pltpu.SemaphoreType.DMA((2, 2))],
    )
    return pl.pallas_call(
        functools.partial(_moe_body, layer=layer),
        grid_spec=grid_spec,
        out_shape=jax.ShapeDtypeStruct((n_slots, d), F32),
        compiler_params=pltpu.CompilerParams(dimension_semantics=("arbitrary",), vmem_limit_bytes=VMEM_LIMIT),
        name="moe_experts",
    )(tile_expert, n_used, group_index, next_expert, xs, bgu[:, :, None, :], bdn[:, :, None, :], wgu, wdn)


def _combine_body(dest_ref, x1_ref, gv_ref, gfin_ref, ys_hbm, y_o, ybuf, sem):
    t0 = pl.program_id(0) * TM

    def row_copy(r, k):
        d = dest_ref[(t0 + r) * TOP_EXPERTS + k]
        return pltpu.make_async_copy(ys_hbm.at[pl.ds(d, 1)], ybuf.at[k, pl.ds(r, 1)], sem)

    def start(r, _):
        for k in range(TOP_EXPERTS):
            row_copy(r, k).start()
        return 0
    lax.fori_loop(0, TM, start, 0, unroll=DMA_UNROLL)

    for k in range(TOP_EXPERTS):
        pltpu.make_async_copy(ys_hbm.at[pl.ds(0, TM)], ybuf.at[k], sem).wait()

    gv = gv_ref[...]
    y = jnp.zeros(x1_ref.shape, F32)
    for k in range(TOP_EXPERTS):
        y = y + ybuf[k] * gv[:, k:k + 1]
    y_o[...] = _rms(x1_ref[...] + y, gfin_ref[...])


def _combine(dest_flat, x1, gv, gfin, ys):
    nt, d = x1.shape
    grid_spec = pltpu.PrefetchScalarGridSpec(
        num_scalar_prefetch=1,
        grid=(nt // TM,),
        in_specs=[pl.BlockSpec((TM, d), lambda i, dref: (i, 0)),
                  pl.BlockSpec((TM, LANES), lambda i, dref: (i, 0)),
                  pl.BlockSpec((1, d), lambda i, dref: (0, 0)),
                  pl.BlockSpec(memory_space=pl.ANY)],
        out_specs=pl.BlockSpec((TM, d), lambda i, dref: (i, 0)),
        scratch_shapes=[pltpu.VMEM((TOP_EXPERTS, TM, d), F32), pltpu.SemaphoreType.DMA],
    )
    return pl.pallas_call(
        _combine_body,
        grid_spec=grid_spec,
        out_shape=jax.ShapeDtypeStruct((nt, d), F32),
        compiler_params=pltpu.CompilerParams(dimension_semantics=("arbitrary",)),
        name="combine_norm",
    )(dest_flat, x1, gv, gfin, ys)


def _tri_constants():
    r = lax.broadcasted_iota(I32, (TQ, TQ), 0)
    c = lax.broadcasted_iota(I32, (TQ, TQ), 1)
    incl = (r <= c).astype(BF16)
    u2 = jnp.concatenate([incl, jnp.ones((TQ, LANES), BF16)], axis=1)
    ltri = (c < r).astype(BF16)
    return u2, ltri


def kernel(x_prompt, x_sample, cache_k, cache_v, cache_kidx, state_pool, page_table, meta_tokens, rel_bias, g_mix, w_in, w_pool_mix, pool_scale, w_br_attn, w_br_pool, w_out, g_ffn, w_router, b_router, w_gate_up, b_gate_up, w_down, b_down, g_final):
    nb, seq, d = x_prompt.shape
    db, dec_seq, _ = x_sample.shape
    assert dec_seq == 1, "one new token per sample"
    depth = w_in.shape[0]
    assert depth == 1, "single-layer stack: the combine kernel applies the final norm"
    n_pages = page_table.shape[1]
    assert n_pages % SUBLANES == 0, "the sample top-k lays the cached keys out as eight equal page segments"
    past = n_pages * PAGE_SIZE
    pw = state_pool.shape[-1]
    l_seq = seq + N_META
    lp = _round_up(l_seq, TQ)
    assert l_seq + db <= lp, "sample rows must fit in the padding of the last prompt sequence"
    nt = nb * lp
    s0 = (nb - 1) * lp + l_seq
    ksel_p = min(TOPK_MAX, seq // 4)
    ksel_s = min(TOPK_MAX, (past + dec_seq) // 4)
    n_tiles = (TOP_EXPERTS * nt) // TS + N_EXPERTS
    n_slots = n_tiles * TS

    meta = meta_tokens.astype(x_prompt.dtype)
    pieces = []
    for bi in range(nb):
        pieces += [meta, x_prompt[bi], jnp.zeros((lp - l_seq, d), x_prompt.dtype)]
    pieces[-1] = jnp.concatenate([x_sample[:, 0, :], jnp.zeros((lp - l_seq - db, d), x_prompt.dtype)], axis=0)
    x_all = jnp.concatenate(pieces, axis=0)

    u2, ltri = _tri_constants()
    page_flat = page_table.reshape(-1).astype(I32)
    rel_bias = rel_bias.astype(F32)
    cache_k_t = jnp.transpose(cache_k, (0, 1, 3, 4, 2))
    cache_v_t = jnp.transpose(cache_v, (0, 1, 3, 4, 2))
    cache_ki_t = jnp.transpose(cache_kidx, (0, 1, 3, 2))

    outs = {name: [] for name in ("k_p", "v_p", "ki_p", "pool_p", "k_s", "v_s", "ki_s", "pool_s")}
    for l in range(depth):
        wl = w_in[l]
        aw = ATTN_WIDTH
        o_ki = 4 * aw
        o_wi = o_ki + IDX_DIM
        o_p = o_wi + N_IDX_HEADS
        o_ga = o_p + pw
        o_gb = o_ga + d
        wa = wl[:, 0:o_ki].astype(BF16)
        wki = wl[:, o_ki:o_wi]
        wki2 = jnp.concatenate([wki, wki], axis=1).astype(BF16)
        wwi = jnp.pad(wl[:, o_wi:o_p], ((0, 0), (0, LANES - N_IDX_HEADS))).astype(BF16)
        wp = wl[:, o_p:o_ga].astype(BF16)
        wga = wl[:, o_ga:o_gb].astype(BF16)
        wgb = wl[:, o_gb:o_gb + d].astype(BF16)

        (qs, kf, vf, kb, vb, qib, ki2b, kif, wif, pf, ga, gb, k_t, v_t, ki_t) = _inproj(
            x_all, g_mix[l][None, :], wa, wki2, wwi, wp, wga, wgb, nb=nb, lp=lp, l_seq=l_seq)

        tail_rows = _round_up(l_seq - (lp - TQ), 2 * SUBLANES)
        attn_p = _attn_prompt(qs, qib, wif, kb, vb, ki2b, u2, rel_bias, nb=nb, lp=lp, ksel=ksel_p,
                              tail_rows=tail_rows)
        pooled_p = _pool_prompt(pf, n_rows=nt, lp=lp)

        sl = slice(s0, s0 + db)
        keys = _sample_keys(page_flat, qib[sl].reshape(db, N_IDX_HEADS, IDX_DIM), kif[sl, 0:IDX_DIM][:, None, :],
                            wif[sl, 0:N_IDX_HEADS][:, :, None], cache_ki_t, layer=l, n_pages=n_pages)
        sel = _sample_select(keys, past=past, ksel=ksel_s)
        cols = lambda a: a[sl].astype(F32).reshape(db, N_HEADS, HEAD_DIM, 1)
        attn_s, pooled_s = _sample_attend(
            page_flat, sel, cols(qs), cols(kf), cols(vf), rel_bias, state_pool[l], pf[sl][:, None, :],
            cache_k_t, cache_v_t, layer=l, n_pages=n_pages)

        attn_all = lax.dynamic_update_slice(attn_p, attn_s.reshape(db, ATTN_WIDTH).astype(BF16), (s0, 0))
        pooled_all = lax.dynamic_update_slice(pooled_p, pooled_s[:, 0, :], (s0, 0))

        wr = jnp.pad(w_router[l], ((0, 0), (0, LANES - N_EXPERTS))).astype(BF16)
        br = jnp.concatenate([b_router[l].astype(F32), jnp.full((LANES - N_EXPERTS,), NEG, F32)])[None, :]
        x1, h2p, ev, gv, rk, cnt = _merge(
            x_all, attn_all, pooled_all, ga, gb, w_br_attn[l].astype(BF16), w_pool_mix[l].astype(BF16),
            pool_scale[l][None, :], w_br_pool[l].astype(BF16), w_out[l].astype(BF16), g_ffn[l][None, :], wr, br, ltri)

        counts = cnt[0, 0:N_EXPERTS].astype(I32)
        padded = (counts + TS - 1) // TS * TS
        gend = jnp.cumsum(padded)
        gstart = gend - padded
        experts = jnp.arange(N_EXPERTS, dtype=I32)
        ev4 = ev[:, 0:TOP_EXPERTS]
        dest = (jnp.sum(jnp.where(ev4[:, :, None] == experts, gstart, 0), axis=2) + rk[:, 0:TOP_EXPERTS]).reshape(-1)
        tile_first = jnp.arange(n_tiles, dtype=I32)[:, None] * TS
        tile_expert = jnp.minimum(jnp.sum((gend[None, :] <= tile_first).astype(I32), axis=1), N_EXPERTS - 1)
        n_used = (gend[-1] // TS).astype(I32)[None]
        nonempty = counts > 0
        rank_e = jnp.cumsum(nonempty.astype(I32)) - nonempty.astype(I32)
        later = jnp.logical_and(nonempty[None, :], experts[None, :] > experts[:, None])
        next_e = jnp.min(jnp.where(later, experts[None, :], N_EXPERTS), axis=1)
        next_e = jnp.where(next_e == N_EXPERTS, -1, next_e)
        tile_onehot = tile_expert[:, None] == experts[None, :]
        group_index = jnp.sum(jnp.where(tile_onehot, rank_e, 0), axis=1).astype(I32)
        next_expert = jnp.sum(jnp.where(tile_onehot, next_e, 0), axis=1).astype(I32)

        xs_sorted = _dispatch(dest, jnp.concatenate([gend, gstart + counts]).astype(I32), h2p, n_slots=n_slots)
        ys = _moe(tile_expert, n_used, group_index, next_expert, xs_sorted, w_gate_up, b_gate_up, w_down, b_down,
                  layer=l)
        y_all = _combine(dest, x1, gv, g_final[None, :], ys)

        outs["k_p"].append(jnp.transpose(k_t, (0, 3, 1, 2)))
        outs["v_p"].append(jnp.transpose(v_t, (0, 3, 1, 2)))
        outs["ki_p"].append(jnp.transpose(ki_t, (0, 2, 1)))
        outs["pool_p"].append(pf.reshape(nb, lp, pw)[:, l_seq - POOL_STATE:l_seq])
        outs["k_s"].append(kf[sl].reshape(db, 1, N_HEADS, HEAD_DIM))
        outs["v_s"].append(vf[sl].reshape(db, 1, N_HEADS, HEAD_DIM))
        outs["ki_s"].append(kif[sl, 0:IDX_DIM].reshape(db, 1, IDX_DIM))
        outs["pool_s"].append(jnp.concatenate([state_pool[l][:, 1:], pf[sl][:, None, :]], axis=1))

    y_prompt = y_all.reshape(nb, lp, d)[:, N_META:l_seq]
    y_sample = y_all[s0:s0 + db].reshape(db, 1, d)
    st = lambda name: jnp.stack(outs[name])
    return (y_prompt, y_sample, st("k_p"), st("v_p"), st("ki_p"), st("pool_p"),
            st("k_s"), st("v_s"), st("ki_s"), st("pool_s"))
```

```python
import functools
import math

import jax
import jax.numpy as jnp
from jax import lax
from jax.experimental import pallas as pl
from jax.experimental.pallas import tpu as pltpu

F32 = jnp.float32
BF16 = jnp.bfloat16
I32 = jnp.int32
U32 = jnp.uint32

N_META = 16
N_HEADS = 8
HEAD_DIM = 64
ATTN_WIDTH = N_HEADS * HEAD_DIM
N_IDX_HEADS = 8
IDX_DIM = 64
TOPK_MAX = 256
N_BUCKETS = 32
MAX_DISTANCE = 128
POOL_WINDOWS = (2, 4, 8, 16)
POOL_GROUPS = 4
POOL_STATE = 15
N_EXPERTS = 32
TOP_EXPERTS = 4
SWIGLU_LIMIT = 7.0
SWIGLU_ALPHA = 1.702
PAGE_SIZE = 128
EPS = 1e-6

LANES = 128
SUBLANES = 8
MXU_DIM = 256
TQ = MXU_DIM
TM = MXU_DIM
TS = 2 * MXU_DIM
VMEM_LIMIT = 56 * 1024 * 1024
DMA_UNROLL = 8

NEG = -1e30
F32_BIG = 3e38
BF16_BITS = 16
HI_HALF = 0xFFFF0000
INT_MIN = -(2 ** 31)

_MAX_EXACT = N_BUCKETS // 2
_BUCKET_THRESHOLDS = tuple(
    math.ceil(_MAX_EXACT * (MAX_DISTANCE / _MAX_EXACT) ** (j / (N_BUCKETS - _MAX_EXACT)))
    for j in range(1, N_BUCKETS - _MAX_EXACT))


def _round_up(a, m):
    return (a + m - 1) // m * m


def _rms(x, g):
    return x * lax.rsqrt(jnp.mean(x * x, axis=-1, keepdims=True) + EPS) * g


def _dot(a, b):
    return jnp.dot(a, b, preferred_element_type=F32)


def _dot_nt(a, b):
    return lax.dot_general(a, b, (((1,), (1,)), ((), ())), preferred_element_type=F32)


def _bucket(dist):
    large = jnp.full(dist.shape, _MAX_EXACT, I32)
    for thr in _BUCKET_THRESHOLDS:
        large = large + jnp.where(dist >= thr, 1, 0)
    return jnp.where(dist < _MAX_EXACT, dist, large)


def _sort_key(s):
    s = jnp.where(s == 0.0, 0.0, s)
    bits = lax.bitcast_convert_type(s, I32)
    return jnp.where(bits >= 0, bits, bits ^ jnp.int32(0x7FFFFFFF))


def _full(shape):
    return pl.BlockSpec(shape, lambda *_: (0,) * len(shape))


def _inproj_body(x_ref, g_ref, wa_ref, wki_ref, wwi_ref, wp_ref, wga_ref, wgb_ref,
                 qs_o, kf_o, vf_o, kb_o, vb_o, qib_o, ki2b_o, kif_o, wif_o, pf_o, ga_o, gb_o, kt_o, vt_o, kit_o):
    h = _rms(x_ref[...], g_ref[...]).astype(BF16)
    za = _dot(h, wa_ref[...])
    w = ATTN_WIDTH
    qs_o[...] = (za[:, 0:w] * (HEAD_DIM ** -0.5)).astype(BF16)
    k = za[:, w:2 * w]
    v = za[:, 2 * w:3 * w]
    kf_o[...] = k
    vf_o[...] = v
    kb_o[...] = k.astype(BF16)
    vb_o[...] = v.astype(BF16)
    qib_o[...] = za[:, 3 * w:4 * w].astype(BF16)
    ki2 = _dot(h, wki_ref[...])
    kif_o[...] = ki2
    ki2b_o[...] = ki2.astype(BF16)
    wif_o[...] = _dot(h, wwi_ref[...])
    pf_o[...] = _dot(h, wp_ref[...])
    ga_o[...] = _dot(h, wga_ref[...])
    gb_o[...] = _dot(h, wgb_ref[...])
    kt_o[0] = k.T.reshape(N_HEADS, HEAD_DIM, TM)
    vt_o[0] = v.T.reshape(N_HEADS, HEAD_DIM, TM)
    kit_o[0] = ki2.T[0:IDX_DIM, :]


def _inproj(x_all, g, wa, wki2, wwi, wp, wga, wgb, *, nb, lp, l_seq):
    nt, d = x_all.shape
    pw = wp.shape[1]
    tps = lp // TM
    row = lambda width: pl.BlockSpec((TM, width), lambda i: (i, 0))
    outs = [
        (ATTN_WIDTH, BF16), (ATTN_WIDTH, F32), (ATTN_WIDTH, F32), (ATTN_WIDTH, BF16), (ATTN_WIDTH, BF16),
        (ATTN_WIDTH, BF16), (LANES, BF16), (LANES, F32), (LANES, F32), (pw, F32), (d, F32), (d, F32)]
    seq_minor_specs = [
        pl.BlockSpec((1, N_HEADS, HEAD_DIM, TM), lambda i: (i // tps, 0, 0, i % tps)),
        pl.BlockSpec((1, N_HEADS, HEAD_DIM, TM), lambda i: (i // tps, 0, 0, i % tps)),
        pl.BlockSpec((1, IDX_DIM, TM), lambda i: (i // tps, 0, i % tps))]
    seq_minor_shapes = [
        jax.ShapeDtypeStruct((nb, N_HEADS, HEAD_DIM, l_seq), F32), jax.ShapeDtypeStruct((nb, N_HEADS, HEAD_DIM, l_seq), F32),
        jax.ShapeDtypeStruct((nb, IDX_DIM, l_seq), F32)]
    return pl.pallas_call(
        _inproj_body,
        grid=(nt // TM,),
        in_specs=[row(d), _full((1, d)), _full(wa.shape), _full(wki2.shape), _full(wwi.shape),
                  _full(wp.shape), _full(wga.shape), _full(wgb.shape)],
        out_specs=[row(wd) for wd, _ in outs] + seq_minor_specs,
        out_shape=[jax.ShapeDtypeStruct((nt, wd), dt) for wd, dt in outs] + seq_minor_shapes,
        compiler_params=pltpu.CompilerParams(dimension_semantics=("arbitrary",), vmem_limit_bytes=VMEM_LIMIT),
        name="inproj",
    )(x_all, g, wa, wki2, wwi, wp, wga, wgb)


def _attn_prompt_body(qs_ref, qib_ref, wif_ref, kb_ref, vb_ref, ki2b_ref, u2_ref, rb_ref,
                      o_ref, qis_scr, wb_scr, key_scr, ntab_scr, qz_scr, m_scr, acc_scr, sum_scr, *, ksel, tail_rows):
    b = pl.program_id(0)
    i = pl.program_id(1)
    nchunk = i + 1
    n_far = jnp.maximum(i - 1, 0)

    @pl.when((b == 0) & (i == 0))
    def _():
        def slab_rows(s, _):
            r0 = pl.multiple_of(s * SUBLANES, SUBLANES)
            r = r0 + lax.broadcasted_iota(I32, (SUBLANES, 2 * TQ), 0)
            x = lax.broadcasted_iota(I32, (SUBLANES, 2 * TQ), 1)
            bucket = _bucket(jnp.maximum(r + TQ - x, 0))
            accs = [jnp.zeros((SUBLANES, 2 * TQ), F32) for _ in range(N_HEADS)]
            for j in range(N_BUCKETS - 1):
                m = bucket == j
                accs = [jnp.where(m, rb_ref[j, h] - rb_ref[N_BUCKETS - 1, h], accs[h]) for h in range(N_HEADS)]
            for h in range(N_HEADS):
                ntab_scr[h, 0, pl.ds(r0, SUBLANES), :] = accs[h][:, 0:TQ]
                ntab_scr[h, 1, pl.ds(r0, SUBLANES), :] = accs[h][:, TQ:2 * TQ]
            return 0
        lax.fori_loop(0, TQ // SUBLANES, slab_rows, 0)

    def two(x):
        return jnp.concatenate([x, x], axis=1)

    def block(nr):
        hr = nr // 2
        lane = lax.broadcasted_iota(I32, (nr, LANES), 1)
        lo_half = lane < HEAD_DIM

        for h in range(N_HEADS):
            cols = slice((h // 2) * LANES, (h // 2 + 1) * LANES)
            keep = lo_half if h % 2 == 0 else jnp.logical_not(lo_half)
            qis_scr[h, 0:nr] = jnp.where(keep, qib_ref[0:nr, cols].astype(F32), 0.0).astype(BF16)
            qz_scr[h // 2, (h % 2) * nr:(h % 2 + 1) * nr] = jnp.where(keep, qs_ref[0:nr, cols].astype(F32), 0.0).astype(BF16)
            wb_scr[h, 0:nr] = jnp.broadcast_to(wif_ref[0:nr, h:h + 1], (nr, LANES))
            m_scr[h, 0:nr] = jnp.full((nr, LANES), NEG, F32)
            acc_scr[h, 0:nr] = jnp.zeros((nr, LANES), F32)
            sum_scr[h, 0:nr] = jnp.zeros((nr, LANES), F32)

        dmat = lax.broadcasted_iota(I32, (nr, TQ), 1) - lax.broadcasted_iota(I32, (nr, TQ), 0)

        def score_chunk(c, _):
            kc = ki2b_ref[pl.ds(pl.multiple_of(c * TQ, TQ), TQ), :]
            s = jnp.zeros((nr, TQ), F32)
            for h in range(N_IDX_HEADS):
                s = s + two(wb_scr[h, 0:nr]) * jnp.maximum(_dot_nt(qis_scr[h, 0:nr], kc), 0.0)
            key_scr[c, 0:nr] = jnp.where(dmat <= (i - c) * TQ, _sort_key(s), INT_MIN)
            return 0
        lax.fori_loop(0, nchunk, score_chunk, 0)
        key_scr[nchunk, 0:nr] = jnp.full((nr, TQ), INT_MIN, I32)

        def count(r0, pred):
            def body(j, acc):
                for c in (2 * j, 2 * j + 1):
                    hit = jnp.where(pred(key_scr[c, r0:r0 + hr, :]), 1.0, 0.0)
                    acc = acc + (hit[:, 0:LANES] + hit[:, LANES:2 * LANES])
                return acc
            acc = lax.fori_loop(0, (nchunk + 1) // 2, body, jnp.zeros((hr, LANES), F32))
            return jnp.broadcast_to(jnp.sum(acc, axis=1, keepdims=True), (hr, LANES))

        ts, exact = [], []
        for r0 in (0, hr):
            def bit_step(step, t, r0=r0):
                cand = t + lax.shift_left(jnp.int32(1), 31 - step)
                cand2 = two(cand)
                return jnp.where(count(r0, lambda k: k >= cand2) >= ksel, cand, t)
            t = lax.fori_loop(0, 32, bit_step, jnp.full((hr, LANES), INT_MIN, I32))
            th2 = two(t)
            ts.append(t)
            n_ge = count(r0, lambda k: k >= th2)
            exact.append(jnp.where(jnp.logical_and(n_ge == ksel, t != INT_MIN), 1.0, 0.0))
        t2 = two(jnp.concatenate(ts, axis=0))
        no_ties = jnp.min(jnp.concatenate(exact, axis=0)) > 0.5

        @pl.when(no_ties)
        def _():
            def mask_chunk(c, _):
                madd = jnp.where(key_scr[c, 0:nr] >= t2, 0.0, NEG)
                key_scr[c, 0:nr] = lax.bitcast_convert_type(madd, I32)
                return 0
            lax.fori_loop(0, nchunk, mask_chunk, 0)

        @pl.when(jnp.logical_not(no_ties))
        def _():
            needs = []
            for g, r0 in enumerate((0, hr)):
                th2 = two(ts[g])
                n_gt = count(r0, lambda k: k > th2)
                needs.append(jnp.where(ts[g] == INT_MIN, 0.0, ksel - n_gt))
            need2 = two(jnp.concatenate(needs, axis=0))

            def mask_chunk(c, carry):
                kc = key_scr[c, 0:nr]
                eq = kc == t2
                pre = _dot(jnp.where(eq, 1.0, 0.0).astype(BF16), u2_ref[...])
                prefix = pre[:, 0:TQ] + two(carry)
                tie_ok = jnp.where(eq, prefix, F32_BIG) <= need2
                madd = jnp.where(kc > t2, 0.0, jnp.where(tie_ok, 0.0, NEG))
                key_scr[c, 0:nr] = lax.bitcast_convert_type(madd, I32)
                return carry + pre[:, TQ:TQ + LANES]
            lax.fori_loop(0, nchunk, mask_chunk, jnp.zeros((nr, LANES), F32))

        def chunk_logits(c, slab):
            rows = pl.ds(pl.multiple_of(c * TQ, TQ), TQ)
            madd = lax.bitcast_convert_type(key_scr[c, 0:nr], F32)
            out = []
            for hp in range(N_HEADS // 2):
                kc = kb_ref[rows, hp * LANES:(hp + 1) * LANES]
                pair = _dot_nt(qz_scr[hp, 0:2 * nr], kc)
                for sub in range(2):
                    l = pair[sub * nr:(sub + 1) * nr] + madd
                    out.append(l if slab is None else l + ntab_scr[2 * hp + sub, slab, 0:nr])
            return rows, out

        def accumulate(c, slab):
            rows, ls = chunk_logits(c, slab)
            for hp in range(N_HEADS // 2):
                ps, alphas = [], []
                for h in (2 * hp, 2 * hp + 1):
                    l = ls[h]
                    m_old = m_scr[h, 0:nr]
                    m_chunk = jnp.max(jnp.maximum(l[:, 0:LANES], l[:, LANES:2 * LANES]), axis=1, keepdims=True)
                    m_new = jnp.maximum(m_old, jnp.broadcast_to(m_chunk, (nr, LANES)))
                    alpha = jnp.exp(m_old - m_new)
                    p = jnp.exp(l - two(m_new))
                    m_scr[h, 0:nr] = m_new
                    sum_scr[h, 0:nr] = alpha * sum_scr[h, 0:nr] + (p[:, 0:LANES] + p[:, LANES:2 * LANES])
                    ps.append(p.astype(BF16))
                    alphas.append(alpha)
                pv = _dot(jnp.concatenate(ps, axis=0), vb_ref[rows, hp * LANES:(hp + 1) * LANES])
                for sub in range(2):
                    h = 2 * hp + sub
                    acc_scr[h, 0:nr] = alphas[sub] * acc_scr[h, 0:nr] + pv[sub * nr:(sub + 1) * nr]

        def far(c, _):
            accumulate(c, None)
            return 0
        lax.fori_loop(0, n_far, far, 0)

        @pl.when(i >= 1)
        def _():
            accumulate(i - 1, 0)
        accumulate(i, 1)
        for hp in range(N_HEADS // 2):
            outs = [acc_scr[h, 0:nr] / jnp.sum(sum_scr[h, 0:nr], axis=1, keepdims=True) for h in (2 * hp, 2 * hp + 1)]
            o_ref[0:nr, hp * LANES:(hp + 1) * LANES] = jnp.where(lo_half, outs[0], outs[1]).astype(BF16)
        if nr < TQ:
            o_ref[nr:TQ, :] = jnp.zeros((TQ - nr, ATTN_WIDTH), BF16)

    if tail_rows == TQ:
        block(TQ)
    else:
        last = pl.num_programs(1) - 1
        pl.when(i < last)(lambda: block(TQ))
        pl.when(i == last)(lambda: block(tail_rows))


def _attn_prompt(qs, qib, wif, kb, vb, ki2b, u2, rel_bias, *, nb, lp, ksel, tail_rows):
    nq = lp // TQ
    rowq = lambda width: pl.BlockSpec((TQ, width), lambda b, i: (b * nq + i, 0))
    seq = lambda width: pl.BlockSpec((lp, width), lambda b, i: (b, 0))
    per_head = lambda dt: pltpu.VMEM((N_HEADS, TQ, LANES), dt)
    return pl.pallas_call(
        functools.partial(_attn_prompt_body, ksel=float(ksel), tail_rows=tail_rows),
        grid=(nb, nq),
        in_specs=[rowq(ATTN_WIDTH), rowq(ATTN_WIDTH), rowq(LANES), seq(ATTN_WIDTH), seq(ATTN_WIDTH), seq(LANES),
                  _full(u2.shape), pl.BlockSpec(memory_space=pltpu.SMEM)],
        out_specs=rowq(ATTN_WIDTH),
        out_shape=jax.ShapeDtypeStruct((nb * lp, ATTN_WIDTH), BF16),
        scratch_shapes=[
            per_head(BF16),
            per_head(F32),
            pltpu.VMEM((nq + 1, TQ, TQ), I32),
            pltpu.VMEM((N_HEADS, 2, TQ, TQ), F32),
            pltpu.VMEM((N_HEADS // 2, 2 * TQ, LANES), BF16),
            per_head(F32), per_head(F32), per_head(F32),
        ],
        compiler_params=pltpu.CompilerParams(dimension_semantics=("arbitrary", "arbitrary"),
                                             vmem_limit_bytes=VMEM_LIMIT),
        name="attn_prompt",
    )(qs, qib, wif, kb, vb, ki2b, u2, rel_bias)


def _pool_prompt_body(p_ref, halo_ref, o_ref, ext_scr, *, tiles_per_seq):
    j = pl.program_id(0) % tiles_per_seq
    halo = 2 * SUBLANES
    p = p_ref[...]
    ext_scr[0:halo, :] = jnp.where(j == 0, 0.0, halo_ref[...])
    ext_scr[halo:halo + TM, :] = p
    pos = j * TM + lax.broadcasted_iota(I32, (TM, 1), 0)
    gw = p.shape[1] // POOL_GROUPS
    for g, w in enumerate(POOL_WINDOWS):
        cols = slice(g * gw, (g + 1) * gw)
        s = p[:, cols]
        for back in range(1, w):
            s = s + ext_scr[halo - back:halo - back + TM, cols]
        cnt = jnp.minimum(pos + 1, w).astype(F32)
        o_ref[:, cols] = (s / cnt - p[:, cols]).astype(BF16)


def _pool_prompt(pf, *, n_rows, lp):
    pw = pf.shape[1]
    halo = 2 * SUBLANES
    return pl.pallas_call(
        functools.partial(_pool_prompt_body, tiles_per_seq=lp // TM),
        grid=(n_rows // TM,),
        in_specs=[pl.BlockSpec((TM, pw), lambda i: (i, 0)),
                  pl.BlockSpec((halo, pw), lambda i: (jnp.maximum(i * (TM // halo) - 1, 0), 0))],
        out_specs=pl.BlockSpec((TM, pw), lambda i: (i, 0)),
        out_shape=jax.ShapeDtypeStruct((n_rows, pw), BF16),
        scratch_shapes=[pltpu.VMEM((halo + TM, pw), F32)],
        compiler_params=pltpu.CompilerParams(dimension_semantics=("arbitrary",)),
        name="pool_prompt",
    )(pf, pf)


def _sample_keys_body(pt_ref, qi_ref, kinew_ref, wi_ref, cache_hbm, key_o, kibuf, sem, *, layer, n_pages):
    s = pl.program_id(0)
    past = n_pages * PAGE_SIZE
    seg_w = past // SUBLANES

    copies = [pltpu.make_async_copy(cache_hbm.at[layer, pt_ref[s * n_pages + pg]],
                                    kibuf.at[:, pg * PAGE_SIZE:(pg + 1) * PAGE_SIZE], sem) for pg in range(n_pages)]
    for cp in copies:
        cp.start()
    for cp in copies:
        cp.wait()

    qi = qi_ref[0]
    wcol = wi_ref[0]
    d = _dot(qi, kibuf[...].astype(BF16))
    sc = jnp.sum(wcol * jnp.maximum(d, 0.0), axis=0, keepdims=True)
    knew = kinew_ref[0].astype(BF16).astype(F32)
    dn = jnp.sum(qi.astype(F32) * knew, axis=1, keepdims=True)
    sn = jnp.sum(wcol * jnp.maximum(dn, 0.0), axis=0, keepdims=True)
    key = _sort_key(sc)
    lane = lax.broadcasted_iota(I32, (1, LANES), 1)
    new_tail = jnp.where(lane == 0, _sort_key(jnp.broadcast_to(sn, (1, LANES))), INT_MIN)
    for g in range(SUBLANES):
        key_o[0, g:g + 1, 0:seg_w] = key[:, g * seg_w:(g + 1) * seg_w]
        key_o[0, g:g + 1, seg_w:seg_w + LANES] = new_tail if g == 0 else jnp.full((1, LANES), INT_MIN, I32)


def _sample_keys(page_table_flat, qi3, kinew3, wi3, cache_ki_t, *, layer, n_pages):
    db = qi3.shape[0]
    past = n_pages * PAGE_SIZE
    kw = past // SUBLANES + LANES
    grid_spec = pltpu.PrefetchScalarGridSpec(
        num_scalar_prefetch=1,
        grid=(db,),
        in_specs=[pl.BlockSpec((1, N_IDX_HEADS, IDX_DIM), lambda s, pt: (s, 0, 0)),
                  pl.BlockSpec((1, 1, IDX_DIM), lambda s, pt: (s, 0, 0)),
                  pl.BlockSpec((1, N_IDX_HEADS, 1), lambda s, pt: (s, 0, 0)),
                  pl.BlockSpec(memory_space=pl.ANY)],
        out_specs=pl.BlockSpec((1, SUBLANES, kw), lambda s, pt: (s, 0, 0)),
        scratch_shapes=[pltpu.VMEM((IDX_DIM, past), F32), pltpu.SemaphoreType.DMA],
    )
    return pl.pallas_call(
        functools.partial(_sample_keys_body, layer=layer, n_pages=n_pages),
        grid_spec=grid_spec,
        out_shape=jax.ShapeDtypeStruct((db, SUBLANES, kw), I32),
        compiler_params=pltpu.CompilerParams(dimension_semantics=("arbitrary",)),
        name="sample_keys",
    )(page_table_flat, qi3, kinew3, wi3, cache_ki_t)


def _sample_select_body(key_ref, sel_o, *, past, ksel):
    keys = key_ref[...]
    kw = keys.shape[2]
    seg_w = kw - LANES
    row = lax.broadcasted_iota(I32, (SUBLANES, kw), 0)
    lane = lax.broadcasted_iota(I32, (SUBLANES, kw), 1)
    pos = jnp.where(lane < seg_w, row * seg_w + lane, past + (lane - seg_w) + row * LANES)[None]

    def cnt(pred):
        x = jnp.where(pred, 1.0, 0.0)
        return jnp.sum(jnp.sum(x, axis=2, keepdims=True), axis=1, keepdims=True)

    def bit_step(step, t):
        cand = t + lax.shift_left(jnp.int32(1), 31 - step)
        return jnp.where(cnt(keys >= cand) >= ksel, cand, t)
    t = lax.fori_loop(0, 32, bit_step, jnp.full((keys.shape[0], 1, 1), INT_MIN, I32))
    gt = keys > t
    eq = keys == t
    need = ksel - cnt(gt)

    nbits = (past + SUBLANES * LANES).bit_length()

    def cut_step(step, c):
        cand = c - lax.shift_left(jnp.int32(1), nbits - 1 - step)
        ok = cnt(jnp.logical_and(eq, pos <= cand)) >= need
        return jnp.where(ok, cand, c)
    cut = lax.fori_loop(0, nbits, cut_step, jnp.full((keys.shape[0], 1, 1), 2 ** nbits - 1, I32))
    sel = jnp.where(gt, 1.0, jnp.where(jnp.logical_and(eq, pos <= cut), 1.0, 0.0))
    sel_o[...] = jnp.where(pos <= past, sel, 0.0)


def _sample_select(keys, *, past, ksel):
    return pl.pallas_call(
        functools.partial(_sample_select_body, past=past, ksel=float(ksel)),
        out_shape=jax.ShapeDtypeStruct(keys.shape, F32),
        name="sample_select",
    )(keys)


def _sample_attend_body(pt_ref, sel_ref, q_ref, knew_ref, vnew_ref, rb_ref, state_ref, pnew_ref, ck_hbm, cv_hbm,
                        attn_o, pool_o, kbuf, vbuf, qb_scr, lg_scr, acc_scr, ksem, vsem, *, layer, n_pages):
    s = pl.program_id(0)
    past = n_pages * PAGE_SIZE
    seg_w = past // SUBLANES
    pages_per_seg = n_pages // SUBLANES
    last_bias = [rb_ref[N_BUCKETS - 1, h] for h in range(N_HEADS)]

    def kcopy(pg, sample=s):
        return pltpu.make_async_copy(ck_hbm.at[layer, pt_ref[sample * n_pages + pg]], kbuf.at[pg], ksem.at[pg])

    def vcopy(pg, sample=s):
        return pltpu.make_async_copy(cv_hbm.at[layer, pt_ref[sample * n_pages + pg]], vbuf.at[pg], vsem.at[pg])

    def request(copy, sample):
        def start(pg, _):
            copy(pg, sample).start()
            return 0
        lax.fori_loop(0, n_pages, start, 0)

    has_next = s + 1 < pl.num_programs(0)

    @pl.when(s == 0)
    def _():
        request(kcopy, s)
        request(vcopy, s)

    for h in range(N_HEADS):
        qb_scr[h] = jnp.broadcast_to(q_ref[0, h], (HEAD_DIM, LANES))
        acc_scr[h] = jnp.zeros((HEAD_DIM, LANES), F32)

    def k_page(pg, _):
        kcopy(pg).wait()
        rows = [jnp.sum(kbuf[pg, h].astype(BF16).astype(F32) * qb_scr[h], axis=0, keepdims=True)
                for h in range(N_HEADS)]
        lg_scr[pg] = jnp.concatenate(rows, axis=0)
        return 0
    lax.fori_loop(0, n_pages, k_page, 0)
    pl.when(has_next)(lambda: request(kcopy, s + 1))

    lane = lax.broadcasted_iota(I32, (1, LANES), 1)
    bucket = _bucket(past - ((n_pages - 1) * PAGE_SIZE + lane))
    near_rows = []
    for h in range(N_HEADS):
        r = jnp.zeros((1, LANES), F32)
        for j in range(N_BUCKETS - 1):
            r = jnp.where(bucket == j, rb_ref[j, h] - last_bias[h], r)
        near_rows.append(r)
    near = jnp.concatenate(near_rows, axis=0)

    sel = sel_ref[0]
    m = jnp.full((N_HEADS, LANES), NEG, F32)
    for pg in range(n_pages):
        g, j = divmod(pg, pages_per_seg)
        keep = sel[g:g + 1, j * PAGE_SIZE:(j + 1) * PAGE_SIZE] > 0.0
        l = lg_scr[pg] + jnp.where(keep, 0.0, NEG)
        if pg == n_pages - 1:
            l = l + near
        lg_scr[pg] = l
        m = jnp.maximum(m, l)
    new_rows = []
    for h in range(N_HEADS):
        kn = knew_ref[0, h].astype(BF16).astype(F32)
        new_rows.append(jnp.sum(q_ref[0, h] * kn, axis=0, keepdims=True) + (rb_ref[0, h] - last_bias[h]))
    l_new = jnp.where(sel[0:1, seg_w:seg_w + 1] > 0.0, jnp.concatenate(new_rows, axis=0), NEG)
    m = jnp.maximum(jnp.max(m, axis=1, keepdims=True), l_new)
    ssum = jnp.zeros((N_HEADS, LANES), F32)
    for pg in range(n_pages):
        p = jnp.exp(lg_scr[pg] - m)
        lg_scr[pg] = p
        ssum = ssum + p
    p_new = jnp.exp(l_new - m)
    den = jnp.sum(ssum, axis=1, keepdims=True) + p_new
    for pg in range(n_pages):
        lg_scr[pg] = (lg_scr[pg] / den).astype(BF16).astype(F32)
    p_new = (p_new / den).astype(BF16).astype(F32)

    def v_page(pg, _):
        vcopy(pg).wait()
        p = lg_scr[pg]
        for h in range(N_HEADS):
            acc_scr[h] = acc_scr[h] + vbuf[pg, h].astype(BF16).astype(F32) * p[h:h + 1, :]
        return 0
    lax.fori_loop(0, n_pages, v_page, 0)
    pl.when(has_next)(lambda: request(vcopy, s + 1))
    for h in range(N_HEADS):
        vn = vnew_ref[0, h].astype(BF16).astype(F32)
        attn_o[0, h] = jnp.sum(acc_scr[h], axis=1, keepdims=True) + p_new[h:h + 1, :] * vn

    pnew = pnew_ref[0]
    st = state_ref[0]
    gw = pnew.shape[1] // POOL_GROUPS
    for g, w in enumerate(POOL_WINDOWS):
        cols = slice(g * gw, (g + 1) * gw)
        acc = pnew[:, cols]
        for back in range(1, w):
            acc = acc + st[POOL_STATE - back:POOL_STATE - back + 1, cols]
        pool_o[0, :, cols] = (acc / float(min(POOL_STATE + 1, w)) - pnew[:, cols]).astype(BF16)


def _sample_attend(page_table_flat, sel, q4, knew4, vnew4, rel_bias, state, pnew3, cache_k_t, cache_v_t, *, layer, n_pages):
    db = q4.shape[0]
    pw = pnew3.shape[2]
    per = lambda shape: pl.BlockSpec((1,) + shape, lambda s, pt: (s,) + (0,) * len(shape))
    col = (N_HEADS, HEAD_DIM, 1)
    page = (n_pages, N_HEADS, HEAD_DIM, LANES)
    grid_spec = pltpu.PrefetchScalarGridSpec(
        num_scalar_prefetch=1,
        grid=(db,),
        in_specs=[per(sel.shape[1:]), per(col), per(col), per(col), pl.BlockSpec(memory_space=pltpu.SMEM),
                  per((POOL_STATE, pw)), per((1, pw)),
                  pl.BlockSpec(memory_space=pl.ANY), pl.BlockSpec(memory_space=pl.ANY)],
        out_specs=[per(col), per((1, pw))],
        scratch_shapes=[pltpu.VMEM(page, F32), pltpu.VMEM(page, F32),
                        pltpu.VMEM((N_HEADS, HEAD_DIM, LANES), F32),
                        pltpu.VMEM((n_pages, N_HEADS, LANES), F32),
                        pltpu.VMEM((N_HEADS, HEAD_DIM, LANES), F32),
                        pltpu.SemaphoreType.DMA((n_pages,)), pltpu.SemaphoreType.DMA((n_pages,))],
    )
    return pl.pallas_call(
        functools.partial(_sample_attend_body, layer=layer, n_pages=n_pages),
        grid_spec=grid_spec,
        out_shape=[jax.ShapeDtypeStruct((db,) + col, F32), jax.ShapeDtypeStruct((db, 1, pw), BF16)],
        compiler_params=pltpu.CompilerParams(dimension_semantics=("arbitrary",), vmem_limit_bytes=VMEM_LIMIT),
        name="sample_attend",
    )(page_table_flat, sel, q4, knew4, vnew4, rel_bias, state, pnew3, cache_k_t, cache_v_t)


def _merge_body(x_ref, attn_ref, pool_ref, ga_ref, gb_ref, wba_ref, wpm_ref, ps_ref, wbp_ref, wout_ref,
                gffn_ref, wr_ref, br_ref, ltri_ref,
                x1_o, h2p_o, ev_o, gv_o, rk_o, cnt_o, carry_scr):
    step = pl.program_id(0)

    @pl.when(step == 0)
    def _():
        carry_scr[...] = jnp.zeros_like(carry_scr)

    a = _dot(attn_ref[...], wba_ref[...])
    pooled = pool_ref[...]
    gw = pooled.shape[1] // POOL_GROUPS
    pm = jnp.concatenate([_dot(pooled[:, g * gw:(g + 1) * gw], wpm_ref[g]) for g in range(POOL_GROUPS)], axis=1)
    bb = _dot((pm * ps_ref[...]).astype(BF16), wbp_ref[...])
    mix = jax.nn.sigmoid(ga_ref[...]) * a + jax.nn.sigmoid(gb_ref[...]) * bb
    x1 = x_ref[...] + _dot(mix.astype(BF16), wout_ref[...])
    x1_o[...] = x1
    h2 = _rms(x1, gffn_ref[...]).astype(BF16)
    half = h2.shape[1] // 2
    lo = lax.shift_right_logical(lax.bitcast_convert_type(h2[:, 0:half].astype(F32), U32), jnp.uint32(BF16_BITS))
    hi = lax.bitcast_convert_type(h2[:, half:].astype(F32), U32) & jnp.uint32(HI_HALF)
    h2p_o[...] = hi | lo

    logits = _dot(h2, wr_ref[...]) + br_ref[...]
    lane = lax.broadcasted_iota(I32, (TM, LANES), 1)
    lanef = lane.astype(F32)
    tops, ids = [], []
    l = logits
    for _ in range(TOP_EXPERTS):
        mx = jnp.max(l, axis=1, keepdims=True)
        ix = jnp.min(jnp.where(l == mx, lanef, float(LANES)), axis=1, keepdims=True)
        tops.append(mx)
        ids.append(ix)
        l = jnp.where(lanef == ix, -F32_BIG, l)
    es = [jnp.exp(tv - tops[0]) for tv in tops]
    den = es[0] + es[1] + es[2] + es[3]
    onehot = jnp.zeros((TM, LANES), F32)
    for ix in ids:
        onehot = onehot + jnp.where(lanef == ix, 1.0, 0.0)
    before = _dot(ltri_ref[...], onehot.astype(BF16)) + carry_scr[0:1, :]
    ev = jnp.zeros((TM, LANES), I32)
    gv = jnp.zeros((TM, LANES), F32)
    rk = jnp.zeros((TM, LANES), I32)
    for k in range(TOP_EXPERTS):
        rank = jnp.sum(jnp.where(lanef == ids[k], before, 0.0), axis=1, keepdims=True)
        ev = jnp.where(lane == k, ids[k].astype(I32), ev)
        gv = jnp.where(lane == k, es[k] / den, gv)
        rk = jnp.where(lane == k, rank.astype(I32), rk)
    ev_o[...] = ev
    gv_o[...] = gv
    rk_o[...] = rk
    carry_scr[...] = carry_scr[...] + jnp.sum(onehot, axis=0, keepdims=True)
    cnt_o[...] = carry_scr[...]


def _merge(x_all, attn, pooled, ga, gb, wba, wpm, ps, wbp, wout, gffn, wr, br, ltri):
    nt, d = x_all.shape
    row = lambda width: pl.BlockSpec((TM, width), lambda i: (i, 0))
    consts = [wba, wpm, ps, wbp, wout, gffn, wr, br, ltri]
    return pl.pallas_call(
        _merge_body,
        grid=(nt // TM,),
        in_specs=[row(d), row(attn.shape[1]), row(pooled.shape[1]), row(d), row(d)] + [_full(c.shape) for c in consts],
        out_specs=[row(d), row(d // 2), row(LANES), row(LANES), row(LANES), _full((SUBLANES, LANES))],
        out_shape=[jax.ShapeDtypeStruct((nt, d), F32), jax.ShapeDtypeStruct((nt, d // 2), U32),
                   jax.ShapeDtypeStruct((nt, LANES), I32), jax.ShapeDtypeStruct((nt, LANES), F32),
                   jax.ShapeDtypeStruct((nt, LANES), I32), jax.ShapeDtypeStruct((SUBLANES, LANES), F32)],
        scratch_shapes=[pltpu.VMEM((SUBLANES, LANES), F32)],
        compiler_params=pltpu.CompilerParams(dimension_semantics=("arbitrary",), vmem_limit_bytes=VMEM_LIMIT),
        name="merge_router",
    )(x_all, attn, pooled, ga, gb, *consts)


def _dispatch_body(dest_ref, gend_ref, h2p_ref, xs_hbm, zero_scr, sem, zsem):
    step = pl.program_id(0)

    @pl.when(step == 0)
    def _():
        zero_scr[...] = jnp.zeros_like(zero_scr)

        def fill(e):
            return pltpu.make_async_copy(zero_scr, xs_hbm.at[pl.ds(pl.multiple_of(gend_ref[e] - TS, TS), TS)], zsem)

        def start(e, _):
            @pl.when(gend_ref[e] > gend_ref[e + N_EXPERTS])
            def _():
                fill(e).start()
            return 0
        lax.fori_loop(0, N_EXPERTS, start, 0)

        def wait(e, _):
            @pl.when(gend_ref[e] > gend_ref[e + N_EXPERTS])
            def _():
                fill(e).wait()
            return 0
        lax.fori_loop(0, N_EXPERTS, wait, 0)

        def tail(j):
            return pltpu.make_async_copy(zero_scr, xs_hbm.at[pl.ds(pl.multiple_of(j * TS, TS), TS)], zsem)

        first_unused = gend_ref[N_EXPERTS - 1] // TS
        n_tiles = xs_hbm.shape[0] // TS

        def tail_start(j, _):
            tail(j).start()
            return 0
        lax.fori_loop(first_unused, n_tiles, tail_start, 0)

        def tail_wait(j, _):
            tail(j).wait()
            return 0
        lax.fori_loop(first_unused, n_tiles, tail_wait, 0)

    t0 = step * TM

    def row_copy(r, k):
        d = dest_ref[(t0 + r) * TOP_EXPERTS + k]
        return pltpu.make_async_copy(h2p_ref.at[pl.ds(r, 1)], xs_hbm.at[pl.ds(d, 1)], sem)

    def start(r, _):
        for k in range(TOP_EXPERTS):
            row_copy(r, k).start()
        return 0
    lax.fori_loop(0, TM, start, 0, unroll=DMA_UNROLL)

    for _ in range(TOP_EXPERTS):
        pltpu.make_async_copy(h2p_ref, xs_hbm.at[pl.ds(0, TM)], sem).wait()


def _dispatch(dest_flat, gend, h2p, *, n_slots):
    nt, hw = h2p.shape
    grid_spec = pltpu.PrefetchScalarGridSpec(
        num_scalar_prefetch=2,
        grid=(nt // TM,),
        in_specs=[pl.BlockSpec((TM, hw), lambda i, d, g: (i, 0))],
        out_specs=pl.BlockSpec(memory_space=pl.ANY),
        scratch_shapes=[pltpu.VMEM((TS, hw), U32), pltpu.SemaphoreType.DMA, pltpu.SemaphoreType.DMA],
    )
    return pl.pallas_call(
        _dispatch_body,
        grid_spec=grid_spec,
        out_shape=jax.ShapeDtypeStruct((n_slots, hw), U32),
        compiler_params=pltpu.CompilerParams(dimension_semantics=("arbitrary",)),
        name="dispatch",
    )(dest_flat, gend, h2p)


def _moe_body(te_ref, nu_ref, gi_ref, nx_ref, xs_ref, bgu_ref, bdn_ref, wgu_hbm, wdn_hbm, ys_o,
              wgu_f, wdn_f, wgu_b, wdn_b, sem, *, layer):
    j = pl.program_id(0)
    expert = te_ref[j]
    live = j < nu_ref[0]
    slot = gi_ref[j] % 2

    def weight_copies(e, s):
        return (pltpu.make_async_copy(wgu_hbm.at[layer, e], wgu_f.at[s], sem.at[0, s]),
                pltpu.make_async_copy(wdn_hbm.at[layer, e], wdn_f.at[s], sem.at[1, s]))

    @pl.when(jnp.logical_and(live, jnp.logical_or(j == 0, expert != te_ref[jnp.maximum(j - 1, 0)])))
    def _():
        @pl.when(j == 0)
        def _():
            for cp in weight_copies(expert, slot):
                cp.start()

        @pl.when(nx_ref[j] >= 0)
        def _():
            for cp in weight_copies(nx_ref[j], 1 - slot):
                cp.start()
        for cp in weight_copies(expert, slot):
            cp.wait()
        wgu_b[...] = wgu_f[slot].astype(BF16)
        wdn_b[...] = wdn_f[slot].astype(BF16)

    @pl.when(live)
    def _():
        words = xs_ref[...]
        x_lo = lax.bitcast_convert_type(lax.shift_left(words, jnp.uint32(BF16_BITS)), F32).astype(BF16)
        x_hi = lax.bitcast_convert_type(words & jnp.uint32(HI_HALF), F32).astype(BF16)
        half = words.shape[1]
        gu = _dot(x_lo, wgu_b[0:half, :]) + _dot(x_hi, wgu_b[half:, :]) + bgu_ref[0, 0]
        de = gu.shape[1] // 2
        gate = jnp.minimum(gu[:, 0:de], SWIGLU_LIMIT)
        up = jnp.clip(gu[:, de:], -SWIGLU_LIMIT, SWIGLU_LIMIT)
        act = (up + 1.0) * gate * jax.nn.sigmoid(SWIGLU_ALPHA * gate)
        ys_o[...] = _dot(act.astype(BF16), wdn_b[...]) + bdn_ref[0, 0]

    @pl.when(jnp.logical_not(live))
    def _():
        ys_o[...] = jnp.zeros_like(ys_o)


def _moe(tile_expert, n_used, group_index, next_expert, xs, wgu, bgu, wdn, bdn, *, layer):
    n_slots, hw = xs.shape
    _, _, d, de2 = wgu.shape
    bias = lambda cols: pl.BlockSpec((1, 1, 1, cols), lambda j, te, nu, gi, nx: (layer, te[j], 0, 0))
    grid_spec = pltpu.PrefetchScalarGridSpec(
        num_scalar_prefetch=4,
        grid=(n_slots // TS,),
        in_specs=[pl.BlockSpec((TS, hw), lambda j, te, nu, gi, nx: (jnp.minimum(j, nu[0] - 1), 0)),
                  bias(de2), bias(d), pl.BlockSpec(memory_space=pl.ANY), pl.BlockSpec(memory_space=pl.ANY)],
        out_specs=pl.BlockSpec((TS, d), lambda j, te, nu, gi, nx: (j, 0)),
        scratch_shapes=[pltpu.VMEM((2, d, de2), F32), pltpu.VMEM((2, de2 // 2, d), F32),
                        pltpu.VMEM((d, de2), BF16), pltpu.VMEM((de2 // 2, d), BF16),
                        pltpu.SemaphoreType.DMA((2, 2))],
    )
    return pl.pallas_call(
        functools.partial(_moe_body, layer=layer),
        grid_spec=grid_spec,
        out_shape=jax.ShapeDtypeStruct((n_slots, d), F32),
        compiler_params=pltpu.CompilerParams(dimension_semantics=("arbitrary",), vmem_limit_bytes=VMEM_LIMIT),
        name="moe_experts",
    )(tile_expert, n_used, group_index, next_expert, xs, bgu[:, :, None, :], bdn[:, :, None, :], wgu, wdn)


def _combine_body(dest_ref, x1_ref, gv_ref, gfin_ref, ys_hbm, y_o, ybuf, sem):
    t0 = pl.program_id(0) * TM

    def row_copy(r, k):
        d = dest_ref[(t0 + r) * TOP_EXPERTS + k]
        return pltpu.make_async_copy(ys_hbm.at[pl.ds(d, 1)], ybuf.at[k, pl.ds(r, 1)], sem)

    def start(r, _):
        for k in range(TOP_EXPERTS):
            row_copy(r, k).start()
        return 0
    lax.fori_loop(0, TM, start, 0, unroll=DMA_UNROLL)

    for k in range(TOP_EXPERTS):
        pltpu.make_async_copy(ys_hbm.at[pl.ds(0, TM)], ybuf.at[k], sem).wait()

    gv = gv_ref[...]
    y = jnp.zeros(x1_ref.shape, F32)
    for k in range(TOP_EXPERTS):
        y = y + ybuf[k] * gv[:, k:k + 1]
    y_o[...] = _rms(x1_ref[...] + y, gfin_ref[...])


def _combine(dest_flat, x1, gv, gfin, ys):
    nt, d = x1.shape
    grid_spec = pltpu.PrefetchScalarGridSpec(
        num_scalar_prefetch=1,
        grid=(nt // TM,),
        in_specs=[pl.BlockSpec((TM, d), lambda i, dref: (i, 0)),
                  pl.BlockSpec((TM, LANES), lambda i, dref: (i, 0)),
                  pl.BlockSpec((1, d), lambda i, dref: (0, 0)),
                  pl.BlockSpec(memory_space=pl.ANY)],
        out_specs=pl.BlockSpec((TM, d), lambda i, dref: (i, 0)),
        scratch_shapes=[pltpu.VMEM((TOP_EXPERTS, TM, d), F32), pltpu.SemaphoreType.DMA],
    )
    return pl.pallas_call(
        _combine_body,
        grid_spec=grid_spec,
        out_shape=jax.ShapeDtypeStruct((nt, d), F32),
        compiler_params=pltpu.CompilerParams(dimension_semantics=("arbitrary",)),
        name="combine_norm",
    )(dest_flat, x1, gv, gfin, ys)


def _tri_constants():
    r = lax.broadcasted_iota(I32, (TQ, TQ), 0)
    c = lax.broadcasted_iota(I32, (TQ, TQ), 1)
    incl = (r <= c).astype(BF16)
    u2 = jnp.concatenate([incl, jnp.ones((TQ, LANES), BF16)], axis=1)
    ltri = (c < r).astype(BF16)
    return u2, ltri


def kernel(x_prompt, x_sample, cache_k, cache_v, cache_kidx, state_pool, page_table, meta_tokens, rel_bias, g_mix, w_in, w_pool_mix, pool_scale, w_br_attn, w_br_pool, w_out, g_ffn, w_router, b_router, w_gate_up, b_gate_up, w_down, b_down, g_final):
    nb, seq, d = x_prompt.shape
    db, dec_seq, _ = x_sample.shape
    assert dec_seq == 1, "one new token per sample"
    depth = w_in.shape[0]
    assert depth == 1, "single-layer stack: the combine kernel applies the final norm"
    n_pages = page_table.shape[1]
    assert n_pages % SUBLANES == 0, "the sample top-k lays the cached keys out as eight equal page segments"
    past = n_pages * PAGE_SIZE
    pw = state_pool.shape[-1]
    l_seq = seq + N_META
    lp = _round_up(l_seq, TQ)
    assert l_seq + db <= lp, "sample rows must fit in the padding of the last prompt sequence"
    nt = nb * lp
    s0 = (nb - 1) * lp + l_seq
    ksel_p = min(TOPK_MAX, seq // 4)
    ksel_s = min(TOPK_MAX, (past + dec_seq) // 4)
    n_tiles = (TOP_EXPERTS * nt) // TS + N_EXPERTS
    n_slots = n_tiles * TS

    meta = meta_tokens.astype(x_prompt.dtype)
    pieces = []
    for bi in range(nb):
        pieces += [meta, x_prompt[bi], jnp.zeros((lp - l_seq, d), x_prompt.dtype)]
    pieces[-1] = jnp.concatenate([x_sample[:, 0, :], jnp.zeros((lp - l_seq - db, d), x_prompt.dtype)], axis=0)
    x_all = jnp.concatenate(pieces, axis=0)

    u2, ltri = _tri_constants()
    page_flat = page_table.reshape(-1).astype(I32)
    rel_bias = rel_bias.astype(F32)
    cache_k_t = jnp.transpose(cache_k, (0, 1, 3, 4, 2))
    cache_v_t = jnp.transpose(cache_v, (0, 1, 3, 4, 2))
    cache_ki_t = jnp.transpose(cache_kidx, (0, 1, 3, 2))

    outs = {name: [] for name in ("k_p", "v_p", "ki_p", "pool_p", "k_s", "v_s", "ki_s", "pool_s")}
    for l in range(depth):
        wl = w_in[l]
        aw = ATTN_WIDTH
        o_ki = 4 * aw
        o_wi = o_ki + IDX_DIM
        o_p = o_wi + N_IDX_HEADS
        o_ga = o_p + pw
        o_gb = o_ga + d
        wa = wl[:, 0:o_ki].astype(BF16)
        wki = wl[:, o_ki:o_wi]
        wki2 = jnp.concatenate([wki, wki], axis=1).astype(BF16)
        wwi = jnp.pad(wl[:, o_wi:o_p], ((0, 0), (0, LANES - N_IDX_HEADS))).astype(BF16)
        wp = wl[:, o_p:o_ga].astype(BF16)
        wga = wl[:, o_ga:o_gb].astype(BF16)
        wgb = wl[:, o_gb:o_gb + d].astype(BF16)

        (qs, kf, vf, kb, vb, qib, ki2b, kif, wif, pf, ga, gb, k_t, v_t, ki_t) = _inproj(
            x_all, g_mix[l][None, :], wa, wki2, wwi, wp, wga, wgb, nb=nb, lp=lp, l_seq=l_seq)

        tail_rows = _round_up(l_seq - (lp - TQ), 2 * SUBLANES)
        attn_p = _attn_prompt(qs, qib, wif, kb, vb, ki2b, u2, rel_bias, nb=nb, lp=lp, ksel=ksel_p,
                              tail_rows=tail_rows)
        pooled_p = _pool_prompt(pf, n_rows=nt, lp=lp)

        sl = slice(s0, s0 + db)
        keys = _sample_keys(page_flat, qib[sl].reshape(db, N_IDX_HEADS, IDX_DIM), kif[sl, 0:IDX_DIM][:, None, :],
                            wif[sl, 0:N_IDX_HEADS][:, :, None], cache_ki_t, layer=l, n_pages=n_pages)
        sel = _sample_select(keys, past=past, ksel=ksel_s)
        cols = lambda a: a[sl].astype(F32).reshape(db, N_HEADS, HEAD_DIM, 1)
        attn_s, pooled_s = _sample_attend(
            page_flat, sel, cols(qs), cols(kf), cols(vf), rel_bias, state_pool[l], pf[sl][:, None, :],
            cache_k_t, cache_v_t, layer=l, n_pages=n_pages)

        attn_all = lax.dynamic_update_slice(attn_p, attn_s.reshape(db, ATTN_WIDTH).astype(BF16), (s0, 0))
        pooled_all = lax.dynamic_update_slice(pooled_p, pooled_s[:, 0, :], (s0, 0))

        wr = jnp.pad(w_router[l], ((0, 0), (0, LANES - N_EXPERTS))).astype(BF16)
        br = jnp.concatenate([b_router[l].astype(F32), jnp.full((LANES - N_EXPERTS,), NEG, F32)])[None, :]
        x1, h2p, ev, gv, rk, cnt = _merge(
            x_all, attn_all, pooled_all, ga, gb, w_br_attn[l].astype(BF16), w_pool_mix[l].astype(BF16),
            pool_scale[l][None, :], w_br_pool[l].astype(BF16), w_out[l].astype(BF16), g_ffn[l][None, :], wr, br, ltri)

        counts = cnt[0, 0:N_EXPERTS].astype(I32)
        padded = (counts + TS - 1) // TS * TS
        gend = jnp.cumsum(padded)
        gstart = gend - padded
        experts = jnp.arange(N_EXPERTS, dtype=I32)
        ev4 = ev[:, 0:TOP_EXPERTS]
        dest = (jnp.sum(jnp.where(ev4[:, :, None] == experts, gstart, 0), axis=2) + rk[:, 0:TOP_EXPERTS]).reshape(-1)
        tile_first = jnp.arange(n_tiles, dtype=I32)[:, None] * TS
        tile_expert = jnp.minimum(jnp.sum((gend[None, :] <= tile_first).astype(I32), axis=1), N_EXPERTS - 1)
        n_used = (gend[-1] // TS).astype(I32)[None]
        nonempty = counts > 0
        rank_e = jnp.cumsum(nonempty.astype(I32)) - nonempty.astype(I32)
        later = jnp.logical_and(nonempty[None, :], experts[None, :] > experts[:, None])
        next_e = jnp.min(jnp.where(later, experts[None, :], N_EXPERTS), axis=1)
        next_e = jnp.where(next_e == N_EXPERTS, -1, next_e)
        tile_onehot = tile_expert[:, None] == experts[None, :]
        group_index = jnp.sum(jnp.where(tile_onehot, rank_e, 0), axis=1).astype(I32)
        next_expert = jnp.sum(jnp.where(tile_onehot, next_e, 0), axis=1).astype(I32)

        xs_sorted = _dispatch(dest, jnp.concatenate([gend, gstart + counts]).astype(I32), h2p, n_slots=n_slots)
        ys = _moe(tile_expert, n_used, group_index, next_expert, xs_sorted, w_gate_up, b_gate_up, w_down, b_down,
                  layer=l)
        y_all = _combine(dest, x1, gv, g_final[None, :], ys)

        outs["k_p"].append(jnp.transpose(k_t, (0, 3, 1, 2)))
        outs["v_p"].append(jnp.transpose(v_t, (0, 3, 1, 2)))
        outs["ki_p"].append(jnp.transpose(ki_t, (0, 2, 1)))
        outs["pool_p"].append(pf.reshape(nb, lp, pw)[:, l_seq - POOL_STATE:l_seq])
        outs["k_s"].append(kf[sl].reshape(db, 1, N_HEADS, HEAD_DIM))
        outs["v_s"].append(vf[sl].reshape(db, 1, N_HEADS, HEAD_DIM))
        outs["ki_s"].append(kif[sl, 0:IDX_DIM].reshape(db, 1, IDX_DIM))
        outs["pool_s"].append(jnp.concatenate([state_pool[l][:, 1:], pf[sl][:, None, :]], axis=1))

    y_prompt = y_all.reshape(nb, lp, d)[:, N_META:l_seq]
    y_sample = y_all[s0:s0 + db].reshape(db, 1, d)
    st = lambda name: jnp.stack(outs[name])
    return (y_prompt, y_sample, st("k_p"), st("v_p"), st("ki_p"), st("pool_p"),
            st("k_s"), st("v_s"), st("ki_s"), st("pool_s"))
```

```python
import functools
import math

import jax
import jax.numpy as jnp
from jax import lax
from jax.experimental import pallas as pl
from jax.experimental.pallas import tpu as pltpu

F32 = jnp.float32
BF16 = jnp.bfloat16
I32 = jnp.int32
U32 = jnp.uint32

N_META = 16
N_HEADS = 8
HEAD_DIM = 64
ATTN_WIDTH = N_HEADS * HEAD_DIM
N_IDX_HEADS = 8
IDX_DIM = 64
TOPK_MAX = 256
N_BUCKETS = 32
MAX_DISTANCE = 128
POOL_WINDOWS = (2, 4, 8, 16)
POOL_GROUPS = 4
POOL_STATE = 15
N_EXPERTS = 32
TOP_EXPERTS = 4
SWIGLU_LIMIT = 7.0
SWIGLU_ALPHA = 1.702
PAGE_SIZE = 128
EPS = 1e-6

LANES = 128
SUBLANES = 8
MXU_DIM = 256
TQ = MXU_DIM
TM = MXU_DIM
TS = 2 * MXU_DIM
VMEM_LIMIT = 56 * 1024 * 1024
DMA_UNROLL = 8
N_DMA_THREADS = 2

NEG = -1e30
F32_BIG = 3e38
BF16_BITS = 16
HI_HALF = 0xFFFF0000
INT_MIN = -(2 ** 31)

_MAX_EXACT = N_BUCKETS // 2
_BUCKET_THRESHOLDS = tuple(
    math.ceil(_MAX_EXACT * (MAX_DISTANCE / _MAX_EXACT) ** (j / (N_BUCKETS - _MAX_EXACT)))
    for j in range(1, N_BUCKETS - _MAX_EXACT))


def _round_up(a, m):
    return (a + m - 1) // m * m


def _rms(x, g):
    return x * lax.rsqrt(jnp.mean(x * x, axis=-1, keepdims=True) + EPS) * g


def _dot(a, b):
    return jnp.dot(a, b, preferred_element_type=F32)


def _dot_nt(a, b):
    return lax.dot_general(a, b, (((1,), (1,)), ((), ())), preferred_element_type=F32)


def _bucket(dist):
    large = jnp.full(dist.shape, _MAX_EXACT, I32)
    for thr in _BUCKET_THRESHOLDS:
        large = large + jnp.where(dist >= thr, 1, 0)
    return jnp.where(dist < _MAX_EXACT, dist, large)


def _sort_key(s):
    s = jnp.where(s == 0.0, 0.0, s)
    bits = lax.bitcast_convert_type(s, I32)
    return jnp.where(bits >= 0, bits, bits ^ jnp.int32(0x7FFFFFFF))


def _full(shape):
    return pl.BlockSpec(shape, lambda *_: (0,) * len(shape))


def _inproj_body(x_ref, g_ref, wa_ref, wki_ref, wwi_ref, wp_ref, wga_ref, wgb_ref,
                 qs_o, kf_o, vf_o, kb_o, vb_o, qib_o, ki2b_o, kif_o, wif_o, pf_o, ga_o, gb_o, kt_o, vt_o, kit_o):
    h = _rms(x_ref[...], g_ref[...]).astype(BF16)
    za = _dot(h, wa_ref[...])
    w = ATTN_WIDTH
    qs_o[...] = (za[:, 0:w] * (HEAD_DIM ** -0.5)).astype(BF16)
    k = za[:, w:2 * w]
    v = za[:, 2 * w:3 * w]
    kf_o[...] = k
    vf_o[...] = v
    kb_o[...] = k.astype(BF16)
    vb_o[...] = v.astype(BF16)
    qib_o[...] = za[:, 3 * w:4 * w].astype(BF16)
    ki2 = _dot(h, wki_ref[...])
    kif_o[...] = ki2
    ki2b_o[...] = ki2.astype(BF16)
    wif_o[...] = _dot(h, wwi_ref[...])
    pf_o[...] = _dot(h, wp_ref[...])
    ga_o[...] = _dot(h, wga_ref[...])
    gb_o[...] = _dot(h, wgb_ref[...])
    kt_o[0] = k.T.reshape(N_HEADS, HEAD_DIM, TM)
    vt_o[0] = v.T.reshape(N_HEADS, HEAD_DIM, TM)
    kit_o[0] = ki2.T[0:IDX_DIM, :]


def _inproj(x_all, g, wa, wki2, wwi, wp, wga, wgb, *, nb, lp, l_seq):
    nt, d = x_all.shape
    pw = wp.shape[1]
    tps = lp // TM
    row = lambda width: pl.BlockSpec((TM, width), lambda i: (i, 0))
    outs = [
        (ATTN_WIDTH, BF16), (ATTN_WIDTH, F32), (ATTN_WIDTH, F32), (ATTN_WIDTH, BF16), (ATTN_WIDTH, BF16),
        (ATTN_WIDTH, BF16), (LANES, BF16), (LANES, F32), (LANES, F32), (pw, F32), (d, F32), (d, F32)]
    seq_minor_specs = [
        pl.BlockSpec((1, N_HEADS, HEAD_DIM, TM), lambda i: (i // tps, 0, 0, i % tps)),
        pl.BlockSpec((1, N_HEADS, HEAD_DIM, TM), lambda i: (i // tps, 0, 0, i % tps)),
        pl.BlockSpec((1, IDX_DIM, TM), lambda i: (i // tps, 0, i % tps))]
    seq_minor_shapes = [
        jax.ShapeDtypeStruct((nb, N_HEADS, HEAD_DIM, l_seq), F32), jax.ShapeDtypeStruct((nb, N_HEADS, HEAD_DIM, l_seq), F32),
        jax.ShapeDtypeStruct((nb, IDX_DIM, l_seq), F32)]
    return pl.pallas_call(
        _inproj_body,
        grid=(nt // TM,),
        in_specs=[row(d), _full((1, d)), _full(wa.shape), _full(wki2.shape), _full(wwi.shape),
                  _full(wp.shape), _full(wga.shape), _full(wgb.shape)],
        out_specs=[row(wd) for wd, _ in outs] + seq_minor_specs,
        out_shape=[jax.ShapeDtypeStruct((nt, wd), dt) for wd, dt in outs] + seq_minor_shapes,
        compiler_params=pltpu.CompilerParams(dimension_semantics=("arbitrary",), vmem_limit_bytes=VMEM_LIMIT),
        name="inproj",
    )(x_all, g, wa, wki2, wwi, wp, wga, wgb)


def _attn_prompt_body(qs_ref, qib_ref, wif_ref, kb_ref, vb_ref, ki2b_ref, u2_ref, rb_ref,
                      o_ref, qis_scr, wb_scr, key_scr, ntab_scr, qz_scr, m_scr, acc_scr, sum_scr, *, ksel, tail_rows):
    b = pl.program_id(0)
    i = pl.program_id(1)
    nchunk = i + 1
    n_far = jnp.maximum(i - 1, 0)

    @pl.when((b == 0) & (i == 0))
    def _():
        def slab_rows(s, _):
            r0 = pl.multiple_of(s * SUBLANES, SUBLANES)
            r = r0 + lax.broadcasted_iota(I32, (SUBLANES, 2 * TQ), 0)
            x = lax.broadcasted_iota(I32, (SUBLANES, 2 * TQ), 1)
            bucket = _bucket(jnp.maximum(r + TQ - x, 0))
            accs = [jnp.zeros((SUBLANES, 2 * TQ), F32) for _ in range(N_HEADS)]
            for j in range(N_BUCKETS - 1):
                m = bucket == j
                accs = [jnp.where(m, rb_ref[j, h] - rb_ref[N_BUCKETS - 1, h], accs[h]) for h in range(N_HEADS)]
            for h in range(N_HEADS):
                ntab_scr[h, 0, pl.ds(r0, SUBLANES), :] = accs[h][:, 0:TQ]
                ntab_scr[h, 1, pl.ds(r0, SUBLANES), :] = accs[h][:, TQ:2 * TQ]
            return 0
        lax.fori_loop(0, TQ // SUBLANES, slab_rows, 0)

    def two(x):
        return jnp.concatenate([x, x], axis=1)

    def block(nr):
        hr = nr // 2
        lane = lax.broadcasted_iota(I32, (nr, LANES), 1)
        lo_half = lane < HEAD_DIM

        for h in range(N_HEADS):
            cols = slice((h // 2) * LANES, (h // 2 + 1) * LANES)
            keep = lo_half if h % 2 == 0 else jnp.logical_not(lo_half)
            qis_scr[h, 0:nr] = jnp.where(keep, qib_ref[0:nr, cols].astype(F32), 0.0).astype(BF16)
            qz_scr[h // 2, (h % 2) * nr:(h % 2 + 1) * nr] = jnp.where(keep, qs_ref[0:nr, cols].astype(F32), 0.0).astype(BF16)
            wb_scr[h, 0:nr] = jnp.broadcast_to(wif_ref[0:nr, h:h + 1], (nr, LANES))
            m_scr[h, 0:nr] = jnp.full((nr, LANES), NEG, F32)
            acc_scr[h, 0:nr] = jnp.zeros((nr, LANES), F32)
            sum_scr[h, 0:nr] = jnp.zeros((nr, LANES), F32)

        dmat = lax.broadcasted_iota(I32, (nr, TQ), 1) - lax.broadcasted_iota(I32, (nr, TQ), 0)

        def score_chunk(c, _):
            kc = ki2b_ref[pl.ds(pl.multiple_of(c * TQ, TQ), TQ), :]
            s = jnp.zeros((nr, TQ), F32)
            for h in range(N_IDX_HEADS):
                s = s + two(wb_scr[h, 0:nr]) * jnp.maximum(_dot_nt(qis_scr[h, 0:nr], kc), 0.0)
            key_scr[c, 0:nr] = jnp.where(dmat <= (i - c) * TQ, _sort_key(s), INT_MIN)
            return 0
        lax.fori_loop(0, nchunk, score_chunk, 0)
        key_scr[nchunk, 0:nr] = jnp.full((nr, TQ), INT_MIN, I32)

        def count(r0, pred):
            def body(j, acc):
                for c in (2 * j, 2 * j + 1):
                    hit = jnp.where(pred(key_scr[c, r0:r0 + hr, :]), 1.0, 0.0)
                    acc = acc + (hit[:, 0:LANES] + hit[:, LANES:2 * LANES])
                return acc
            acc = lax.fori_loop(0, (nchunk + 1) // 2, body, jnp.zeros((hr, LANES), F32))
            return jnp.broadcast_to(jnp.sum(acc, axis=1, keepdims=True), (hr, LANES))

        ts, exact = [], []
        for r0 in (0, hr):
            def bit_step(step, t, r0=r0):
                cand = t + lax.shift_left(jnp.int32(1), 31 - step)
                cand2 = two(cand)
                return jnp.where(count(r0, lambda k: k >= cand2) >= ksel, cand, t)
            t = lax.fori_loop(0, 32, bit_step, jnp.full((hr, LANES), INT_MIN, I32))
            th2 = two(t)
            ts.append(t)
            n_ge = count(r0, lambda k: k >= th2)
            exact.append(jnp.where(jnp.logical_and(n_ge == ksel, t != INT_MIN), 1.0, 0.0))
        t2 = two(jnp.concatenate(ts, axis=0))
        no_ties = jnp.min(jnp.concatenate(exact, axis=0)) > 0.5

        @pl.when(no_ties)
        def _():
            def mask_chunk(c, _):
                madd = jnp.where(key_scr[c, 0:nr] >= t2, 0.0, NEG)
                key_scr[c, 0:nr] = lax.bitcast_convert_type(madd, I32)
                return 0
            lax.fori_loop(0, nchunk, mask_chunk, 0)

        @pl.when(jnp.logical_not(no_ties))
        def _():
            needs = []
            for g, r0 in enumerate((0, hr)):
                th2 = two(ts[g])
                n_gt = count(r0, lambda k: k > th2)
                needs.append(jnp.where(ts[g] == INT_MIN, 0.0, ksel - n_gt))
            need2 = two(jnp.concatenate(needs, axis=0))

            def mask_chunk(c, carry):
                kc = key_scr[c, 0:nr]
                eq = kc == t2
                pre = _dot(jnp.where(eq, 1.0, 0.0).astype(BF16), u2_ref[...])
                prefix = pre[:, 0:TQ] + two(carry)
                tie_ok = jnp.where(eq, prefix, F32_BIG) <= need2
                madd = jnp.where(kc > t2, 0.0, jnp.where(tie_ok, 0.0, NEG))
                key_scr[c, 0:nr] = lax.bitcast_convert_type(madd, I32)
                return carry + pre[:, TQ:TQ + LANES]
            lax.fori_loop(0, nchunk, mask_chunk, jnp.zeros((nr, LANES), F32))

        def chunk_logits(c, slab):
            rows = pl.ds(pl.multiple_of(c * TQ, TQ), TQ)
            madd = lax.bitcast_convert_type(key_scr[c, 0:nr], F32)
            out = []
            for hp in range(N_HEADS // 2):
                kc = kb_ref[rows, hp * LANES:(hp + 1) * LANES]
                pair = _dot_nt(qz_scr[hp, 0:2 * nr], kc)
                for sub in range(2):
                    l = pair[sub * nr:(sub + 1) * nr] + madd
                    out.append(l if slab is None else l + ntab_scr[2 * hp + sub, slab, 0:nr])
            return rows, out

        def accumulate(c, slab):
            rows, ls = chunk_logits(c, slab)
            for hp in range(N_HEADS // 2):
                ps, alphas = [], []
                for h in (2 * hp, 2 * hp + 1):
                    l = ls[h]
                    m_old = m_scr[h, 0:nr]
                    m_chunk = jnp.max(jnp.maximum(l[:, 0:LANES], l[:, LANES:2 * LANES]), axis=1, keepdims=True)
                    m_new = jnp.maximum(m_old, jnp.broadcast_to(m_chunk, (nr, LANES)))
                    alpha = jnp.exp(m_old - m_new)
                    p = jnp.exp(l - two(m_new))
                    m_scr[h, 0:nr] = m_new
                    sum_scr[h, 0:nr] = alpha * sum_scr[h, 0:nr] + (p[:, 0:LANES] + p[:, LANES:2 * LANES])
                    ps.append(p.astype(BF16))
                    alphas.append(alpha)
                pv = _dot(jnp.concatenate(ps, axis=0), vb_ref[rows, hp * LANES:(hp + 1) * LANES])
                for sub in range(2):
                    h = 2 * hp + sub
                    acc_scr[h, 0:nr] = alphas[sub] * acc_scr[h, 0:nr] + pv[sub * nr:(sub + 1) * nr]

        def far(c, _):
            accumulate(c, None)
            return 0
        lax.fori_loop(0, n_far, far, 0)

        @pl.when(i >= 1)
        def _():
            accumulate(i - 1, 0)
        accumulate(i, 1)
        for hp in range(N_HEADS // 2):
            outs = [acc_scr[h, 0:nr] / jnp.sum(sum_scr[h, 0:nr], axis=1, keepdims=True) for h in (2 * hp, 2 * hp + 1)]
            o_ref[0:nr, hp * LANES:(hp + 1) * LANES] = jnp.where(lo_half, outs[0], outs[1]).astype(BF16)
        if nr < TQ:
            o_ref[nr:TQ, :] = jnp.zeros((TQ - nr, ATTN_WIDTH), BF16)

    if tail_rows == TQ:
        block(TQ)
    else:
        last = pl.num_programs(1) - 1
        pl.when(i < last)(lambda: block(TQ))
        pl.when(i == last)(lambda: block(tail_rows))


def _attn_prompt(qs, qib, wif, kb, vb, ki2b, u2, rel_bias, *, nb, lp, ksel, tail_rows):
    nq = lp // TQ
    rowq = lambda width: pl.BlockSpec((TQ, width), lambda b, i: (b * nq + i, 0))
    seq = lambda width: pl.BlockSpec((lp, width), lambda b, i: (b, 0))
    per_head = lambda dt: pltpu.VMEM((N_HEADS, TQ, LANES), dt)
    return pl.pallas_call(
        functools.partial(_attn_prompt_body, ksel=float(ksel), tail_rows=tail_rows),
        grid=(nb, nq),
        in_specs=[rowq(ATTN_WIDTH), rowq(ATTN_WIDTH), rowq(LANES), seq(ATTN_WIDTH), seq(ATTN_WIDTH), seq(LANES),
                  _full(u2.shape), pl.BlockSpec(memory_space=pltpu.SMEM)],
        out_specs=rowq(ATTN_WIDTH),
        out_shape=jax.ShapeDtypeStruct((nb * lp, ATTN_WIDTH), BF16),
        scratch_shapes=[
            per_head(BF16),
            per_head(F32),
            pltpu.VMEM((nq + 1, TQ, TQ), I32),
            pltpu.VMEM((N_HEADS, 2, TQ, TQ), F32),
            pltpu.VMEM((N_HEADS // 2, 2 * TQ, LANES), BF16),
            per_head(F32), per_head(F32), per_head(F32),
        ],
        compiler_params=pltpu.CompilerParams(dimension_semantics=("arbitrary", "arbitrary"),
                                             vmem_limit_bytes=VMEM_LIMIT),
        name="attn_prompt",
    )(qs, qib, wif, kb, vb, ki2b, u2, rel_bias)


def _pool_prompt_body(p_ref, halo_ref, o_ref, ext_scr, *, tiles_per_seq):
    j = pl.program_id(0) % tiles_per_seq
    halo = 2 * SUBLANES
    p = p_ref[...]
    ext_scr[0:halo, :] = jnp.where(j == 0, 0.0, halo_ref[...])
    ext_scr[halo:halo + TM, :] = p
    pos = j * TM + lax.broadcasted_iota(I32, (TM, 1), 0)
    gw = p.shape[1] // POOL_GROUPS
    for g, w in enumerate(POOL_WINDOWS):
        cols = slice(g * gw, (g + 1) * gw)
        s = p[:, cols]
        for back in range(1, w):
            s = s + ext_scr[halo - back:halo - back + TM, cols]
        cnt = jnp.minimum(pos + 1, w).astype(F32)
        o_ref[:, cols] = (s / cnt - p[:, cols]).astype(BF16)


def _pool_prompt(pf, *, n_rows, lp):
    pw = pf.shape[1]
    halo = 2 * SUBLANES
    return pl.pallas_call(
        functools.partial(_pool_prompt_body, tiles_per_seq=lp // TM),
        grid=(n_rows // TM,),
        in_specs=[pl.BlockSpec((TM, pw), lambda i: (i, 0)),
                  pl.BlockSpec((halo, pw), lambda i: (jnp.maximum(i * (TM // halo) - 1, 0), 0))],
        out_specs=pl.BlockSpec((TM, pw), lambda i: (i, 0)),
        out_shape=jax.ShapeDtypeStruct((n_rows, pw), BF16),
        scratch_shapes=[pltpu.VMEM((halo + TM, pw), F32)],
        compiler_params=pltpu.CompilerParams(dimension_semantics=("arbitrary",)),
        name="pool_prompt",
    )(pf, pf)


def _sample_keys_body(pt_ref, qi_ref, kinew_ref, wi_ref, cache_hbm, key_o, kibuf, sem, *, layer, n_pages):
    s = pl.program_id(0)
    past = n_pages * PAGE_SIZE
    seg_w = past // SUBLANES

    copies = [pltpu.make_async_copy(cache_hbm.at[layer, pt_ref[s * n_pages + pg]],
                                    kibuf.at[:, pg * PAGE_SIZE:(pg + 1) * PAGE_SIZE], sem) for pg in range(n_pages)]
    for cp in copies:
        cp.start()
    for cp in copies:
        cp.wait()

    qi = qi_ref[0]
    wcol = wi_ref[0]
    d = _dot(qi, kibuf[...].astype(BF16))
    sc = jnp.sum(wcol * jnp.maximum(d, 0.0), axis=0, keepdims=True)
    knew = kinew_ref[0].astype(BF16).astype(F32)
    dn = jnp.sum(qi.astype(F32) * knew, axis=1, keepdims=True)
    sn = jnp.sum(wcol * jnp.maximum(dn, 0.0), axis=0, keepdims=True)
    key = _sort_key(sc)
    lane = lax.broadcasted_iota(I32, (1, LANES), 1)
    new_tail = jnp.where(lane == 0, _sort_key(jnp.broadcast_to(sn, (1, LANES))), INT_MIN)
    for g in range(SUBLANES):
        key_o[0, g:g + 1, 0:seg_w] = key[:, g * seg_w:(g + 1) * seg_w]
        key_o[0, g:g + 1, seg_w:seg_w + LANES] = new_tail if g == 0 else jnp.full((1, LANES), INT_MIN, I32)


def _sample_keys(page_table_flat, qi3, kinew3, wi3, cache_ki_t, *, layer, n_pages):
    db = qi3.shape[0]
    past = n_pages * PAGE_SIZE
    kw = past // SUBLANES + LANES
    grid_spec = pltpu.PrefetchScalarGridSpec(
        num_scalar_prefetch=1,
        grid=(db,),
        in_specs=[pl.BlockSpec((1, N_IDX_HEADS, IDX_DIM), lambda s, pt: (s, 0, 0)),
                  pl.BlockSpec((1, 1, IDX_DIM), lambda s, pt: (s, 0, 0)),
                  pl.BlockSpec((1, N_IDX_HEADS, 1), lambda s, pt: (s, 0, 0)),
                  pl.BlockSpec(memory_space=pl.ANY)],
        out_specs=pl.BlockSpec((1, SUBLANES, kw), lambda s, pt: (s, 0, 0)),
        scratch_shapes=[pltpu.VMEM((IDX_DIM, past), F32), pltpu.SemaphoreType.DMA],
    )
    return pl.pallas_call(
        functools.partial(_sample_keys_body, layer=layer, n_pages=n_pages),
        grid_spec=grid_spec,
        out_shape=jax.ShapeDtypeStruct((db, SUBLANES, kw), I32),
        compiler_params=pltpu.CompilerParams(dimension_semantics=("arbitrary",)),
        name="sample_keys",
    )(page_table_flat, qi3, kinew3, wi3, cache_ki_t)


def _sample_select_body(key_ref, sel_o, *, past, ksel):
    keys = key_ref[...]
    kw = keys.shape[2]
    seg_w = kw - LANES
    row = lax.broadcasted_iota(I32, (SUBLANES, kw), 0)
    lane = lax.broadcasted_iota(I32, (SUBLANES, kw), 1)
    pos = jnp.where(lane < seg_w, row * seg_w + lane, past + (lane - seg_w) + row * LANES)[None]

    def cnt(pred):
        x = jnp.where(pred, 1.0, 0.0)
        return jnp.sum(jnp.sum(x, axis=2, keepdims=True), axis=1, keepdims=True)

    def bit_step(step, t):
        cand = t + lax.shift_left(jnp.int32(1), 31 - step)
        return jnp.where(cnt(keys >= cand) >= ksel, cand, t)
    t = lax.fori_loop(0, 32, bit_step, jnp.full((keys.shape[0], 1, 1), INT_MIN, I32))
    gt = keys > t
    eq = keys == t
    need = ksel - cnt(gt)

    nbits = (past + SUBLANES * LANES).bit_length()

    def cut_step(step, c):
        cand = c - lax.shift_left(jnp.int32(1), nbits - 1 - step)
        ok = cnt(jnp.logical_and(eq, pos <= cand)) >= need
        return jnp.where(ok, cand, c)
    cut = lax.fori_loop(0, nbits, cut_step, jnp.full((keys.shape[0], 1, 1), 2 ** nbits - 1, I32))
    sel = jnp.where(gt, 1.0, jnp.where(jnp.logical_and(eq, pos <= cut), 1.0, 0.0))
    sel_o[...] = jnp.where(pos <= past, sel, 0.0)


def _sample_select(keys, *, past, ksel):
    return pl.pallas_call(
        functools.partial(_sample_select_body, past=past, ksel=float(ksel)),
        out_shape=jax.ShapeDtypeStruct(keys.shape, F32),
        name="sample_select",
    )(keys)


def _sample_attend_body(pt_ref, sel_ref, q_ref, knew_ref, vnew_ref, rb_ref, state_ref, pnew_ref, ck_hbm, cv_hbm,
                        attn_o, pool_o, kbuf, vbuf, qb_scr, lg_scr, acc_scr, ksem, vsem, *, layer, n_pages):
    s = pl.program_id(0)
    past = n_pages * PAGE_SIZE
    seg_w = past // SUBLANES
    pages_per_seg = n_pages // SUBLANES
    last_bias = [rb_ref[N_BUCKETS - 1, h] for h in range(N_HEADS)]

    def kcopy(pg, sample=s):
        return pltpu.make_async_copy(ck_hbm.at[layer, pt_ref[sample * n_pages + pg]], kbuf.at[pg], ksem.at[pg])

    def vcopy(pg, sample=s):
        return pltpu.make_async_copy(cv_hbm.at[layer, pt_ref[sample * n_pages + pg]], vbuf.at[pg], vsem.at[pg])

    def request(copy, sample):
        def start(pg, _):
            copy(pg, sample).start()
            return 0
        lax.fori_loop(0, n_pages, start, 0)

    has_next = s + 1 < pl.num_programs(0)

    @pl.when(s == 0)
    def _():
        request(kcopy, s)
        request(vcopy, s)

    for h in range(N_HEADS):
        qb_scr[h] = jnp.broadcast_to(q_ref[0, h], (HEAD_DIM, LANES))
        acc_scr[h] = jnp.zeros((HEAD_DIM, LANES), F32)

    def k_page(pg, _):
        kcopy(pg).wait()
        rows = [jnp.sum(kbuf[pg, h].astype(BF16).astype(F32) * qb_scr[h], axis=0, keepdims=True)
                for h in range(N_HEADS)]
        lg_scr[pg] = jnp.concatenate(rows, axis=0)
        return 0
    lax.fori_loop(0, n_pages, k_page, 0)
    pl.when(has_next)(lambda: request(kcopy, s + 1))

    lane = lax.broadcasted_iota(I32, (1, LANES), 1)
    bucket = _bucket(past - ((n_pages - 1) * PAGE_SIZE + lane))
    near_rows = []
    for h in range(N_HEADS):
        r = jnp.zeros((1, LANES), F32)
        for j in range(N_BUCKETS - 1):
            r = jnp.where(bucket == j, rb_ref[j, h] - last_bias[h], r)
        near_rows.append(r)
    near = jnp.concatenate(near_rows, axis=0)

    sel = sel_ref[0]
    m = jnp.full((N_HEADS, LANES), NEG, F32)
    for pg in range(n_pages):
        g, j = divmod(pg, pages_per_seg)
        keep = sel[g:g + 1, j * PAGE_SIZE:(j + 1) * PAGE_SIZE] > 0.0
        l = lg_scr[pg] + jnp.where(keep, 0.0, NEG)
        if pg == n_pages - 1:
            l = l + near
        lg_scr[pg] = l
        m = jnp.maximum(m, l)
    new_rows = []
    for h in range(N_HEADS):
        kn = knew_ref[0, h].astype(BF16).astype(F32)
        new_rows.append(jnp.sum(q_ref[0, h] * kn, axis=0, keepdims=True) + (rb_ref[0, h] - last_bias[h]))
    l_new = jnp.where(sel[0:1, seg_w:seg_w + 1] > 0.0, jnp.concatenate(new_rows, axis=0), NEG)
    m = jnp.maximum(jnp.max(m, axis=1, keepdims=True), l_new)
    ssum = jnp.zeros((N_HEADS, LANES), F32)
    for pg in range(n_pages):
        p = jnp.exp(lg_scr[pg] - m)
        lg_scr[pg] = p
        ssum = ssum + p
    p_new = jnp.exp(l_new - m)
    den = jnp.sum(ssum, axis=1, keepdims=True) + p_new
    for pg in range(n_pages):
        lg_scr[pg] = (lg_scr[pg] / den).astype(BF16).astype(F32)
    p_new = (p_new / den).astype(BF16).astype(F32)

    def v_page(pg, _):
        vcopy(pg).wait()
        p = lg_scr[pg]
        for h in range(N_HEADS):
            acc_scr[h] = acc_scr[h] + vbuf[pg, h].astype(BF16).astype(F32) * p[h:h + 1, :]
        return 0
    lax.fori_loop(0, n_pages, v_page, 0)
    pl.when(has_next)(lambda: request(vcopy, s + 1))
    for h in range(N_HEADS):
        vn = vnew_ref[0, h].astype(BF16).astype(F32)
        attn_o[0, h] = jnp.sum(acc_scr[h], axis=1, keepdims=True) + p_new[h:h + 1, :] * vn

    pnew = pnew_ref[0]
    st = state_ref[0]
    gw = pnew.shape[1] // POOL_GROUPS
    for g, w in enumerate(POOL_WINDOWS):
        cols = slice(g * gw, (g + 1) * gw)
        acc = pnew[:, cols]
        for back in range(1, w):
            acc = acc + st[POOL_STATE - back:POOL_STATE - back + 1, cols]
        pool_o[0, :, cols] = (acc / float(min(POOL_STATE + 1, w)) - pnew[:, cols]).astype(BF16)


def _sample_attend(page_table_flat, sel, q4, knew4, vnew4, rel_bias, state, pnew3, cache_k_t, cache_v_t, *, layer, n_pages):
    db = q4.shape[0]
    pw = pnew3.shape[2]
    per = lambda shape: pl.BlockSpec((1,) + shape, lambda s, pt: (s,) + (0,) * len(shape))
    col = (N_HEADS, HEAD_DIM, 1)
    page = (n_pages, N_HEADS, HEAD_DIM, LANES)
    grid_spec = pltpu.PrefetchScalarGridSpec(
        num_scalar_prefetch=1,
        grid=(db,),
        in_specs=[per(sel.shape[1:]), per(col), per(col), per(col), pl.BlockSpec(memory_space=pltpu.SMEM),
                  per((POOL_STATE, pw)), per((1, pw)),
                  pl.BlockSpec(memory_space=pl.ANY), pl.BlockSpec(memory_space=pl.ANY)],
        out_specs=[per(col), per((1, pw))],
        scratch_shapes=[pltpu.VMEM(page, F32), pltpu.VMEM(page, F32),
                        pltpu.VMEM((N_HEADS, HEAD_DIM, LANES), F32),
                        pltpu.VMEM((n_pages, N_HEADS, LANES), F32),
                        pltpu.VMEM((N_HEADS, HEAD_DIM, LANES), F32),
                        pltpu.SemaphoreType.DMA((n_pages,)), pltpu.SemaphoreType.DMA((n_pages,))],
    )
    return pl.pallas_call(
        functools.partial(_sample_attend_body, layer=layer, n_pages=n_pages),
        grid_spec=grid_spec,
        out_shape=[jax.ShapeDtypeStruct((db,) + col, F32), jax.ShapeDtypeStruct((db, 1, pw), BF16)],
        compiler_params=pltpu.CompilerParams(dimension_semantics=("arbitrary",), vmem_limit_bytes=VMEM_LIMIT),
        name="sample_attend",
    )(page_table_flat, sel, q4, knew4, vnew4, rel_bias, state, pnew3, cache_k_t, cache_v_t)


def _merge_body(x_ref, attn_ref, pool_ref, ga_ref, gb_ref, wba_ref, wpm_ref, ps_ref, wbp_ref, wout_ref,
                gffn_ref, wr_ref, br_ref, ltri_ref,
                x1_o, h2p_o, ev_o, gv_o, rk_o, cnt_o, carry_scr):
    step = pl.program_id(0)

    @pl.when(step == 0)
    def _():
        carry_scr[...] = jnp.zeros_like(carry_scr)

    a = _dot(attn_ref[...], wba_ref[...])
    pooled = pool_ref[...]
    gw = pooled.shape[1] // POOL_GROUPS
    pm = jnp.concatenate([_dot(pooled[:, g * gw:(g + 1) * gw], wpm_ref[g]) for g in range(POOL_GROUPS)], axis=1)
    bb = _dot((pm * ps_ref[...]).astype(BF16), wbp_ref[...])
    mix = jax.nn.sigmoid(ga_ref[...]) * a + jax.nn.sigmoid(gb_ref[...]) * bb
    x1 = x_ref[...] + _dot(mix.astype(BF16), wout_ref[...])
    x1_o[...] = x1
    h2 = _rms(x1, gffn_ref[...]).astype(BF16)
    half = h2.shape[1] // 2
    lo = lax.shift_right_logical(lax.bitcast_convert_type(h2[:, 0:half].astype(F32), U32), jnp.uint32(BF16_BITS))
    hi = lax.bitcast_convert_type(h2[:, half:].astype(F32), U32) & jnp.uint32(HI_HALF)
    h2p_o[...] = hi | lo

    logits = _dot(h2, wr_ref[...]) + br_ref[...]
    lane = lax.broadcasted_iota(I32, (TM, LANES), 1)
    lanef = lane.astype(F32)
    tops, ids = [], []
    l = logits
    for _ in range(TOP_EXPERTS):
        mx = jnp.max(l, axis=1, keepdims=True)
        ix = jnp.min(jnp.where(l == mx, lanef, float(LANES)), axis=1, keepdims=True)
        tops.append(mx)
        ids.append(ix)
        l = jnp.where(lanef == ix, -F32_BIG, l)
    es = [jnp.exp(tv - tops[0]) for tv in tops]
    den = es[0] + es[1] + es[2] + es[3]
    onehot = jnp.zeros((TM, LANES), F32)
    for ix in ids:
        onehot = onehot + jnp.where(lanef == ix, 1.0, 0.0)
    before = _dot(ltri_ref[...], onehot.astype(BF16)) + carry_scr[0:1, :]
    ev = jnp.zeros((TM, LANES), I32)
    gv = jnp.zeros((TM, LANES), F32)
    rk = jnp.zeros((TM, LANES), I32)
    for k in range(TOP_EXPERTS):
        rank = jnp.sum(jnp.where(lanef == ids[k], before, 0.0), axis=1, keepdims=True)
        ev = jnp.where(lane == k, ids[k].astype(I32), ev)
        gv = jnp.where(lane == k, es[k] / den, gv)
        rk = jnp.where(lane == k, rank.astype(I32), rk)
    ev_o[...] = ev
    gv_o[...] = gv
    rk_o[...] = rk
    carry_scr[...] = carry_scr[...] + jnp.sum(onehot, axis=0, keepdims=True)
    cnt_o[...] = carry_scr[...]


def _merge(x_all, attn, pooled, ga, gb, wba, wpm, ps, wbp, wout, gffn, wr, br, ltri):
    nt, d = x_all.shape
    row = lambda width: pl.BlockSpec((TM, width), lambda i: (i, 0))
    consts = [wba, wpm, ps, wbp, wout, gffn, wr, br, ltri]
    return pl.pallas_call(
        _merge_body,
        grid=(nt // TM,),
        in_specs=[row(d), row(attn.shape[1]), row(pooled.shape[1]), row(d), row(d)] + [_full(c.shape) for c in consts],
        out_specs=[row(d), row(d // 2), row(LANES), row(LANES), row(LANES), _full((SUBLANES, LANES))],
        out_shape=[jax.ShapeDtypeStruct((nt, d), F32), jax.ShapeDtypeStruct((nt, d // 2), U32),
                   jax.ShapeDtypeStruct((nt, LANES), I32), jax.ShapeDtypeStruct((nt, LANES), F32),
                   jax.ShapeDtypeStruct((nt, LANES), I32), jax.ShapeDtypeStruct((SUBLANES, LANES), F32)],
        scratch_shapes=[pltpu.VMEM((SUBLANES, LANES), F32)],
        compiler_params=pltpu.CompilerParams(dimension_semantics=("arbitrary",), vmem_limit_bytes=VMEM_LIMIT),
        name="merge_router",
    )(x_all, attn, pooled, ga, gb, *consts)


def _dispatch_body(dest_ref, gend_ref, h2p_ref, xs_hbm, zero_scr, sem, zsem):
    step = pl.program_id(0)

    @pl.when(step == 0)
    def _():
        zero_scr[...] = jnp.zeros_like(zero_scr)

        def fill(e):
            return pltpu.make_async_copy(zero_scr, xs_hbm.at[pl.ds(pl.multiple_of(gend_ref[e] - TS, TS), TS)], zsem)

        def start(e, _):
            @pl.when(gend_ref[e] > gend_ref[e + N_EXPERTS])
            def _():
                fill(e).start()
            return 0
        lax.fori_loop(0, N_EXPERTS, start, 0)

        def wait(e, _):
            @pl.when(gend_ref[e] > gend_ref[e + N_EXPERTS])
            def _():
                fill(e).wait()
            return 0
        lax.fori_loop(0, N_EXPERTS, wait, 0)

        def tail(j):
            return pltpu.make_async_copy(zero_scr, xs_hbm.at[pl.ds(pl.multiple_of(j * TS, TS), TS)], zsem)

        first_unused = gend_ref[N_EXPERTS - 1] // TS
        n_tiles = xs_hbm.shape[0] // TS

        def tail_start(j, _):
            tail(j).start()
            return 0
        lax.fori_loop(first_unused, n_tiles, tail_start, 0)

        def tail_wait(j, _):
            tail(j).wait()
            return 0
        lax.fori_loop(first_unused, n_tiles, tail_wait, 0)

    t0 = step * TM

    def row_copy(r, k):
        d = dest_ref[(t0 + r) * TOP_EXPERTS + k]
        return pltpu.make_async_copy(h2p_ref.at[pl.ds(r, 1)], xs_hbm.at[pl.ds(d, 1)], sem)

    def start(r, _):
        for k in range(TOP_EXPERTS):
            row_copy(r, k).start(priority=k % N_DMA_THREADS)
        return 0
    lax.fori_loop(0, TM, start, 0, unroll=DMA_UNROLL)

    for _ in range(TOP_EXPERTS):
        pltpu.make_async_copy(h2p_ref, xs_hbm.at[pl.ds(0, TM)], sem).wait()


def _dispatch(dest_flat, gend, h2p, *, n_slots):
    nt, hw = h2p.shape
    grid_spec = pltpu.PrefetchScalarGridSpec(
        num_scalar_prefetch=2,
        grid=(nt // TM,),
        in_specs=[pl.BlockSpec((TM, hw), lambda i, d, g: (i, 0))],
        out_specs=pl.BlockSpec(memory_space=pl.ANY),
        scratch_shapes=[pltpu.VMEM((TS, hw), U32), pltpu.SemaphoreType.DMA, pltpu.SemaphoreType.DMA],
    )
    return pl.pallas_call(
        _dispatch_body,
        grid_spec=grid_spec,
        out_shape=jax.ShapeDtypeStruct((n_slots, hw), U32),
        compiler_params=pltpu.CompilerParams(dimension_semantics=("arbitrary",)),
        name="dispatch",
    )(dest_flat, gend, h2p)


def _moe_body(te_ref, nu_ref, gi_ref, nx_ref, xs_ref, bgu_ref, bdn_ref, wgu_hbm, wdn_hbm, ys_o,
              wgu_f, wdn_f, wgu_b, wdn_b, sem, *, layer):
    j = pl.program_id(0)
    expert = te_ref[j]
    live = j < nu_ref[0]
    slot = gi_ref[j] % 2

    def weight_copies(e, s):
        return (pltpu.make_async_copy(wgu_hbm.at[layer, e], wgu_f.at[s], sem.at[0, s]),
                pltpu.make_async_copy(wdn_hbm.at[layer, e], wdn_f.at[s], sem.at[1, s]))

    @pl.when(jnp.logical_and(live, jnp.logical_or(j == 0, expert != te_ref[jnp.maximum(j - 1, 0)])))
    def _():
        @pl.when(j == 0)
        def _():
            for cp in weight_copies(expert, slot):
                cp.start()

        @pl.when(nx_ref[j] >= 0)
        def _():
            for cp in weight_copies(nx_ref[j], 1 - slot):
                cp.start()
        for cp in weight_copies(expert, slot):
            cp.wait()
        wgu_b[...] = wgu_f[slot].astype(BF16)
        wdn_b[...] = wdn_f[slot].astype(BF16)

    @pl.when(live)
    def _():
        words = xs_ref[...]
        x_lo = lax.bitcast_convert_type(lax.shift_left(words, jnp.uint32(BF16_BITS)), F32).astype(BF16)
        x_hi = lax.bitcast_convert_type(words & jnp.uint32(HI_HALF), F32).astype(BF16)
        half = words.shape[1]
        gu = _dot(x_lo, wgu_b[0:half, :]) + _dot(x_hi, wgu_b[half:, :]) + bgu_ref[0, 0]
        de = gu.shape[1] // 2
        gate = jnp.minimum(gu[:, 0:de], SWIGLU_LIMIT)
        up = jnp.clip(gu[:, de:], -SWIGLU_LIMIT, SWIGLU_LIMIT)
        act = (up + 1.0) * gate * jax.nn.sigmoid(SWIGLU_ALPHA * gate)
        ys_o[...] = _dot(act.astype(BF16), wdn_b[...]) + bdn_ref[0, 0]

    @pl.when(jnp.logical_not(live))
    def _():
        ys_o[...] = jnp.zeros_like(ys_o)


def _moe(tile_expert, n_used, group_index, next_expert, xs, wgu, bgu, wdn, bdn, *, layer):
    n_slots, hw = xs.shape
    _, _, d, de2 = wgu.shape
    bias = lambda cols: pl.BlockSpec((1, 1, 1, cols), lambda j, te, nu, gi, nx: (layer, te[j], 0, 0))
    grid_spec = pltpu.PrefetchScalarGridSpec(
        num_scalar_prefetch=4,
        grid=(n_slots // TS,),
        in_specs=[pl.BlockSpec((TS, hw), lambda j, te, nu, gi, nx: (jnp.minimum(j, nu[0] - 1), 0)),
                  bias(de2), bias(d), pl.BlockSpec(memory_space=pl.ANY), pl.BlockSpec(memory_space=pl.ANY)],
        out_specs=pl.BlockSpec((TS, d), lambda j, te, nu, gi, nx: (j, 0)),
        scratch_shapes=[pltpu.VMEM((2, d, de2), F32), pltpu.VMEM((2, de2 // 2, d), F32),
                        pltpu.VMEM((d, de2), BF16), pltpu.VMEM((de2 // 2, d), BF16),
                        pltpu.SemaphoreType.DMA((2, 2))],
    )
    return pl.pallas_call(
        functools.partial(_moe_body, layer=layer),
        grid_spec=grid_spec,
        out_shape=jax.ShapeDtypeStruct((n_slots, d), F32),
        compiler_params=pltpu.CompilerParams(dimension_semantics=("arbitrary",), vmem_limit_bytes=VMEM_LIMIT),
        name="moe_experts",
    )(tile_expert, n_used, group_index, next_expert, xs, bgu[:, :, None, :], bdn[:, :, None, :], wgu, wdn)


def _combine_body(dest_ref, x1_ref, gv_ref, gfin_ref, ys_hbm, y_o, ybuf, sem):
    t0 = pl.program_id(0) * TM

    def row_copy(r, k):
        d = dest_ref[(t0 + r) * TOP_EXPERTS + k]
        return pltpu.make_async_copy(ys_hbm.at[pl.ds(d, 1)], ybuf.at[k, pl.ds(r, 1)], sem)

    def start(r, _):
        for k in range(TOP_EXPERTS):
            row_copy(r, k).start(priority=k % N_DMA_THREADS)
        return 0
    lax.fori_loop(0, TM, start, 0, unroll=DMA_UNROLL)

    for k in range(TOP_EXPERTS):
        pltpu.make_async_copy(ys_hbm.at[pl.ds(0, TM)], ybuf.at[k], sem).wait()

    gv = gv_ref[...]
    y = jnp.zeros(x1_ref.shape, F32)
    for k in range(TOP_EXPERTS):
        y = y + ybuf[k] * gv[:, k:k + 1]
    y_o[...] = _rms(x1_ref[...] + y, gfin_ref[...])


def _combine(dest_flat, x1, gv, gfin, ys):
    nt, d = x1.shape
    grid_spec = pltpu.PrefetchScalarGridSpec(
        num_scalar_prefetch=1,
        grid=(nt // TM,),
        in_specs=[pl.BlockSpec((TM, d), lambda i, dref: (i, 0)),
                  pl.BlockSpec((TM, LANES), lambda i, dref: (i, 0)),
                  pl.BlockSpec((1, d), lambda i, dref: (0, 0)),
                  pl.BlockSpec(memory_space=pl.ANY)],
        out_specs=pl.BlockSpec((TM, d), lambda i, dref: (i, 0)),
        scratch_shapes=[pltpu.VMEM((TOP_EXPERTS, TM, d), F32), pltpu.SemaphoreType.DMA],
    )
    return pl.pallas_call(
        _combine_body,
        grid_spec=grid_spec,
        out_shape=jax.ShapeDtypeStruct((nt, d), F32),
        compiler_params=pltpu.CompilerParams(dimension_semantics=("arbitrary",)),
        name="combine_norm",
    )(dest_flat, x1, gv, gfin, ys)


def _tri_constants():
    r = lax.broadcasted_iota(I32, (TQ, TQ), 0)
    c = lax.broadcasted_iota(I32, (TQ, TQ), 1)
    incl = (r <= c).astype(BF16)
    u2 = jnp.concatenate([incl, jnp.ones((TQ, LANES), BF16)], axis=1)
    ltri = (c < r).astype(BF16)
    return u2, ltri


def kernel(x_prompt, x_sample, cache_k, cache_v, cache_kidx, state_pool, page_table, meta_tokens, rel_bias, g_mix, w_in, w_pool_mix, pool_scale, w_br_attn, w_br_pool, w_out, g_ffn, w_router, b_router, w_gate_up, b_gate_up, w_down, b_down, g_final):
    nb, seq, d = x_prompt.shape
    db, dec_seq, _ = x_sample.shape
    assert dec_seq == 1, "one new token per sample"
    depth = w_in.shape[0]
    assert depth == 1, "single-layer stack: the combine kernel applies the final norm"
    n_pages = page_table.shape[1]
    assert n_pages % SUBLANES == 0, "the sample top-k lays the cached keys out as eight equal page segments"
    past = n_pages * PAGE_SIZE
    pw = state_pool.shape[-1]
    l_seq = seq + N_META
    lp = _round_up(l_seq, TQ)
    assert l_seq + db <= lp, "sample rows must fit in the padding of the last prompt sequence"
    nt = nb * lp
    s0 = (nb - 1) * lp + l_seq
    ksel_p = min(TOPK_MAX, seq // 4)
    ksel_s = min(TOPK_MAX, (past + dec_seq) // 4)
    n_tiles = (TOP_EXPERTS * nt) // TS + N_EXPERTS
    n_slots = n_tiles * TS

    meta = meta_tokens.astype(x_prompt.dtype)
    pieces = []
    for bi in range(nb):
        pieces += [meta, x_prompt[bi], jnp.zeros((lp - l_seq, d), x_prompt.dtype)]
    pieces[-1] = jnp.concatenate([x_sample[:, 0, :], jnp.zeros((lp - l_seq - db, d), x_prompt.dtype)], axis=0)
    x_all = jnp.concatenate(pieces, axis=0)

    u2, ltri = _tri_constants()
    page_flat = page_table.reshape(-1).astype(I32)
    rel_bias = rel_bias.astype(F32)
    cache_k_t = jnp.transpose(cache_k, (0, 1, 3, 4, 2))
    cache_v_t = jnp.transpose(cache_v, (0, 1, 3, 4, 2))
    cache_ki_t = jnp.transpose(cache_kidx, (0, 1, 3, 2))

    outs = {name: [] for name in ("k_p", "v_p", "ki_p", "pool_p", "k_s", "v_s", "ki_s", "pool_s")}
    for l in range(depth):
        wl = w_in[l]
        aw = ATTN_WIDTH
        o_ki = 4 * aw
        o_wi = o_ki + IDX_DIM
        o_p = o_wi + N_IDX_HEADS
        o_ga = o_p + pw
        o_gb = o_ga + d
        wa = wl[:, 0:o_ki].astype(BF16)
        wki = wl[:, o_ki:o_wi]
        wki2 = jnp.concatenate([wki, wki], axis=1).astype(BF16)
        wwi = jnp.pad(wl[:, o_wi:o_p], ((0, 0), (0, LANES - N_IDX_HEADS))).astype(BF16)
        wp = wl[:, o_p:o_ga].astype(BF16)
        wga = wl[:, o_ga:o_gb].astype(BF16)
        wgb = wl[:, o_gb:o_gb + d].astype(BF16)

        (qs, kf, vf, kb, vb, qib, ki2b, kif, wif, pf, ga, gb, k_t, v_t, ki_t) = _inproj(
            x_all, g_mix[l][None, :], wa, wki2, wwi, wp, wga, wgb, nb=nb, lp=lp, l_seq=l_seq)

        tail_rows = _round_up(l_seq - (lp - TQ), 2 * SUBLANES)
        attn_p = _attn_prompt(qs, qib, wif, kb, vb, ki2b, u2, rel_bias, nb=nb, lp=lp, ksel=ksel_p,
                              tail_rows=tail_rows)
        pooled_p = _pool_prompt(pf, n_rows=nt, lp=lp)

        sl = slice(s0, s0 + db)
        keys = _sample_keys(page_flat, qib[sl].reshape(db, N_IDX_HEADS, IDX_DIM), kif[sl, 0:IDX_DIM][:, None, :],
                            wif[sl, 0:N_IDX_HEADS][:, :, None], cache_ki_t, layer=l, n_pages=n_pages)
        sel = _sample_select(keys, past=past, ksel=ksel_s)
        cols = lambda a: a[sl].astype(F32).reshape(db, N_HEADS, HEAD_DIM, 1)
        attn_s, pooled_s = _sample_attend(
            page_flat, sel, cols(qs), cols(kf), cols(vf), rel_bias, state_pool[l], pf[sl][:, None, :],
            cache_k_t, cache_v_t, layer=l, n_pages=n_pages)

        attn_all = lax.dynamic_update_slice(attn_p, attn_s.reshape(db, ATTN_WIDTH).astype(BF16), (s0, 0))
        pooled_all = lax.dynamic_update_slice(pooled_p, pooled_s[:, 0, :], (s0, 0))

        wr = jnp.pad(w_router[l], ((0, 0), (0, LANES - N_EXPERTS))).astype(BF16)
        br = jnp.concatenate([b_router[l].astype(F32), jnp.full((LANES - N_EXPERTS,), NEG, F32)])[None, :]
        x1, h2p, ev, gv, rk, cnt = _merge(
            x_all, attn_all, pooled_all, ga, gb, w_br_attn[l].astype(BF16), w_pool_mix[l].astype(BF16),
            pool_scale[l][None, :], w_br_pool[l].astype(BF16), w_out[l].astype(BF16), g_ffn[l][None, :], wr, br, ltri)

        counts = cnt[0, 0:N_EXPERTS].astype(I32)
        padded = (counts + TS - 1) // TS * TS
        gend = jnp.cumsum(padded)
        gstart = gend - padded
        experts = jnp.arange(N_EXPERTS, dtype=I32)
        ev4 = ev[:, 0:TOP_EXPERTS]
        dest = (jnp.sum(jnp.where(ev4[:, :, None] == experts, gstart, 0), axis=2) + rk[:, 0:TOP_EXPERTS]).reshape(-1)
        tile_first = jnp.arange(n_tiles, dtype=I32)[:, None] * TS
        tile_expert = jnp.minimum(jnp.sum((gend[None, :] <= tile_first).astype(I32), axis=1), N_EXPERTS - 1)
        n_used = (gend[-1] // TS).astype(I32)[None]
        nonempty = counts > 0
        rank_e = jnp.cumsum(nonempty.astype(I32)) - nonempty.astype(I32)
        later = jnp.logical_and(nonempty[None, :], experts[None, :] > experts[:, None])
        next_e = jnp.min(jnp.where(later, experts[None, :], N_EXPERTS), axis=1)
        next_e = jnp.where(next_e == N_EXPERTS, -1, next_e)
        tile_onehot = tile_expert[:, None] == experts[None, :]
        group_index = jnp.sum(jnp.where(tile_onehot, rank_e, 0), axis=1).astype(I32)
        next_expert = jnp.sum(jnp.where(tile_onehot, next_e, 0), axis=1).astype(I32)

        xs_sorted = _dispatch(dest, jnp.concatenate([gend, gstart + counts]).astype(I32), h2p, n_slots=n_slots)
        ys = _moe(tile_expert, n_used, group_index, next_expert, xs_sorted, w_gate_up, b_gate_up, w_down, b_down,
                  layer=l)
        y_all = _combine(dest, x1, gv, g_final[None, :], ys)

        outs["k_p"].append(jnp.transpose(k_t, (0, 3, 1, 2)))
        outs["v_p"].append(jnp.transpose(v_t, (0, 3, 1, 2)))
        outs["ki_p"].append(jnp.transpose(ki_t, (0, 2, 1)))
        outs["pool_p"].append(pf.reshape(nb, lp, pw)[:, l_seq - POOL_STATE:l_seq])
        outs["k_s"].append(kf[sl].reshape(db, 1, N_HEADS, HEAD_DIM))
        outs["v_s"].append(vf[sl].reshape(db, 1, N_HEADS, HEAD_DIM))
        outs["ki_s"].append(kif[sl, 0:IDX_DIM].reshape(db, 1, IDX_DIM))
        outs["pool_s"].append(jnp.concatenate([state_pool[l][:, 1:], pf[sl][:, None, :]], axis=1))

    y_prompt = y_all.reshape(nb, lp, d)[:, N_META:l_seq]
    y_sample = y_all[s0:s0 + db].reshape(db, 1, d)
    st = lambda name: jnp.stack(outs[name])
    return (y_prompt, y_sample, st("k_p"), st("v_p"), st("ki_p"), st("pool_p"),
            st("k_s"), st("v_s"), st("ki_s"), st("pool_s"))
```
